```python
import math
import jax
import jax.numpy as jnp
from jax import lax
import numpy as np

D_MODEL = 1024
BATCH = 1
SEQ = 16384
DEPTH = 4

GRID_W = 64
CTX_LEN = 256
N_BRANCH = 4
CONF_D = 512
CONF_K = 31
SSD_D = 768
SSD_HEADS = 12
SSD_HEAD_DIM = 64
SSD_GROUPS = 4
SSD_HPG = SSD_HEADS // SSD_GROUPS
SSD_STATE = 128
SSD_CONV_K = 3
SSD_CHUNK = 128
SSD_BC = SSD_GROUPS * SSD_STATE
SSD_XBC = SSD_D + 2 * SSD_BC
SSD_PROJ = SSD_D + SSD_XBC + 2 * SSD_HEADS
HY_D = 512
HY_ORDER = 2
HY_SHORT_K = 3
HY_EMB = 33
HY_BANDS = (HY_EMB - 1) // 2
HY_HID = 64
HY_N_FILT = HY_ORDER * 2 * HY_D
HY_MIN_DECAY = math.log(1e-2) / 1.5
HY_MAX_DECAY = math.log(1e-2) / 0.3
SC_D = 512
SC_K = 3
OFF_A = 0
OFF_B = OFF_A + 2 * CONF_D
OFF_C = OFF_B + SSD_PROJ
OFF_D = OFF_C + 3 * HY_D
OFF_G = OFF_D + 3 * SC_D
PROJ_TOTAL = OFF_G + N_BRANCH * D_MODEL
MOE_GROUPS = 4
MOE_EPG = 8
MOE_EXPERTS = MOE_GROUPS * MOE_EPG
MOE_TOP_K = 2
MOE_FF = 512
MOE_BLOCK = 128
DN_ALPHA = (2 * DEPTH) ** 0.25
DN_BETA = (8 * DEPTH) ** -0.25
LN_EPS = 1e-5
F32 = jnp.float32

kernel_name = "hybrid_conv_ssd_hyena_moe_diffusion_block"


def layer_norm(x, g, b):
    xf = x.astype(F32)
    mu = jnp.mean(xf, -1, keepdims=True)
    var = jnp.mean(jnp.square(xf - mu), -1, keepdims=True)
    return ((xf - mu) * lax.rsqrt(var + LN_EPS) * g.astype(F32) + b.astype(F32)).astype(x.dtype)


def rms_norm(x, g):
    xf = x.astype(F32)
    return (xf * lax.rsqrt(jnp.mean(jnp.square(xf), -1, keepdims=True) + LN_EPS) * g.astype(F32)).astype(x.dtype)


def modulate(h, shift, scale):
    return h * (1 + scale) + shift


def dwconv_seq(x, w):
    k = w.shape[0]
    return lax.conv_general_dilated(x, w[:, None, :].astype(x.dtype), window_strides=(1,), padding=((k // 2, k // 2),), dimension_numbers=("NWC", "WIO", "NWC"), feature_group_count=x.shape[-1])


def dwconv_grid_cols(x, w, rows):
    bsz, n, ch = x.shape
    k = w.shape[0]
    y = lax.conv_general_dilated(x.reshape(bsz, rows, GRID_W, ch), w[:, None, None, :].astype(x.dtype), window_strides=(1, 1), padding=((k // 2, k // 2), (0, 0)), dimension_numbers=("NHWC", "HWIO", "NHWC"), feature_group_count=ch)
    return y.reshape(bsz, n, ch)


def conformer_branch(p_a, lp, rows):
    val, gate = jnp.split(p_a, 2, axis=-1)
    v = val * jax.nn.sigmoid(gate)
    v = dwconv_seq(v, lp["conf_dw_w"]) if rows is None else dwconv_grid_cols(v, lp["conf_dw_w"], rows)
    v = layer_norm(v + lp["conf_dw_b"], lp["conf_ln_g"], lp["conf_ln_b"])
    return jax.nn.silu(v)


def ssd_prepare(p_b, lp):
    bsz, n = p_b.shape[:2]
    z = p_b[..., :SSD_D]
    xbc = jax.nn.silu(dwconv_seq(p_b[..., SSD_D:SSD_D + SSD_XBC], lp["ssd_conv_w"]) + lp["ssd_conv_b"])
    xs = xbc[..., :SSD_D].reshape(bsz, n, SSD_GROUPS, SSD_HPG, SSD_HEAD_DIM)
    b_in = xbc[..., SSD_D:SSD_D + SSD_BC].reshape(bsz, n, SSD_GROUPS, SSD_STATE)
    c_out = xbc[..., SSD_D + SSD_BC:].reshape(bsz, n, SSD_GROUPS, SSD_STATE)
    dt_raw = p_b[..., SSD_D + SSD_XBC:].astype(F32).reshape(bsz, n, 2, SSD_HEADS)
    dt = jax.nn.softplus(dt_raw + lp["ssd_dt_bias"].astype(F32))
    a_rate = -jnp.exp(lp["ssd_a_log"].astype(F32))
    return z, xs, b_in, c_out, dt, a_rate


def _ssd_direction(xs, b_in, c_out, dt, a_rate, d):
    bsz, n = xs.shape[:2]
    dt_d = dt[:, :, d].reshape(bsz, n, SSD_GROUPS, SSD_HPG)
    log_decay = dt_d * a_rate[d].reshape(SSD_GROUPS, SSD_HPG)
    xdt = xs * dt_d[..., None].astype(xs.dtype)
    if d == 1:
        xdt, log_decay, b_in, c_out = (jnp.flip(t, axis=1) for t in (xdt, log_decay, b_in, c_out))
    return xdt, log_decay, b_in, c_out


def _ssd_blocks(xdt, log_decay, b_in):
    bsz, n = xdt.shape[:2]
    nc = n // SSD_CHUNK
    xc = xdt.reshape(bsz, nc, SSD_CHUNK, SSD_GROUPS, SSD_HPG, SSD_HEAD_DIM)
    a_cum = jnp.cumsum(log_decay.reshape(bsz, nc, SSD_CHUNK, SSD_GROUPS, SSD_HPG), axis=2).transpose(0, 3, 4, 1, 2)
    bc = b_in.reshape(bsz, nc, SSD_CHUNK, SSD_GROUPS, SSD_STATE)
    return xc, a_cum, bc


def _ssd_chunk_states(xc, a_cum, bc, init):
    decay_to_end = jnp.exp(a_cum[..., -1:] - a_cum)
    chunk_states = jnp.einsum("bclgn,bgecl,bclgep->cbgepn", bc.astype(F32), decay_to_end, xc.astype(F32))
    chunk_decay = jnp.moveaxis(jnp.exp(a_cum[..., -1]), -1, 0)[..., None, None]

    def step(h, inp):
        dec, s = inp
        return dec * h + s, h

    final, h_in = lax.scan(step, init, (chunk_decay, chunk_states))
    return h_in, final


def ssd_scan(xdt, log_decay, b_in, c_out, init):
    xc, a_cum, bc = _ssd_blocks(xdt, log_decay, b_in)
    cc = c_out.reshape(bc.shape).astype(F32)
    h_in, final = _ssd_chunk_states(xc, a_cum, bc, init)
    lower = jnp.tril(jnp.ones((SSD_CHUNK, SSD_CHUNK), dtype=bool))
    decay_in = jnp.exp(jnp.where(lower, a_cum[..., :, None] - a_cum[..., None, :], -jnp.inf))
    cb = jnp.einsum("bclgn,bcsgn->bgcls", cc, bc.astype(F32))
    y_diag = jnp.einsum("bgcls,bgecls,bcsgep->bclgep", cb, decay_in, xc.astype(F32))
    y_off = jnp.einsum("bclgn,cbgepn,bgecl->bclgep", cc, h_in, jnp.exp(a_cum))
    return (y_diag + y_off).reshape(xdt.shape).astype(xdt.dtype), final


def ssd_branch(p_b, init, lp):
    z, xs, b_in, c_out, dt, a_rate = ssd_prepare(p_b, lp)
    y = xs * lp["ssd_d"].reshape(SSD_GROUPS, SSD_HPG, 1).astype(xs.dtype)
    finals = []
    for d in range(2):
        xdt, log_decay, b_d, c_d = _ssd_direction(xs, b_in, c_out, dt, a_rate, d)
        y_d, f_d = ssd_scan(xdt, log_decay, b_d, c_d, init[d])
        y = y + (jnp.flip(y_d, axis=1) if d == 1 else y_d)
        finals.append(f_d)
    y = y.reshape(z.shape)
    return rms_norm(y * jax.nn.silu(z), lp["ssd_norm_g"]), jnp.stack(finals)


def ssd_context_states(p_b, init, lp):
    _, xs, b_in, c_out, dt, a_rate = ssd_prepare(p_b, lp)
    finals = []
    for d in range(2):
        xdt, log_decay, b_d, _ = _ssd_direction(xs, b_in, c_out, dt, a_rate, d)
        xc, a_cum, bc = _ssd_blocks(xdt, log_decay, b_d)
        finals.append(_ssd_chunk_states(xc, a_cum, bc, init[d])[1])
    return jnp.stack(finals)


def hyena_filters(n, lp):
    t01 = jnp.linspace(0.0, 1.0, n, dtype=F32)[:, None]
    omega = (2.0 * math.pi / n) * jnp.arange(n, dtype=F32)[:, None]
    bands = jnp.linspace(1e-4, HY_BANDS - 1, HY_BANDS, dtype=F32)
    feat = jnp.concatenate([t01, jnp.cos(bands * omega), -jnp.sin(bands * omega)], axis=-1)
    hid = jnp.sin(lp["hy_freq"][0].astype(F32) * (feat @ lp["hy_w1"].astype(F32) + lp["hy_b1"].astype(F32)))
    hid = jnp.sin(lp["hy_freq"][1].astype(F32) * (hid @ lp["hy_w2"].astype(F32) + lp["hy_b2"].astype(F32)))
    deltas = jnp.abs(jnp.linspace(HY_MIN_DECAY, HY_MAX_DECAY, HY_N_FILT, dtype=F32))
    filt = (hid @ lp["hy_w3"].astype(F32)) * jnp.exp(-t01 * deltas)
    filt = filt.reshape(n, HY_ORDER, 2, HY_D)
    fwd, bwd = filt[:, :, 0], filt[:, :, 1]
    k2 = jnp.concatenate([fwd, jnp.zeros((1, HY_ORDER, HY_D), F32), jnp.flip(bwd[1:], axis=0)], axis=0)
    return k2 / jnp.sum(jnp.abs(k2), axis=0, keepdims=True)


def fft_conv(z, k2):
    n = z.shape[1]
    zf = jnp.fft.rfft(z, n=2 * n, axis=1)
    kf = jnp.fft.rfft(k2, axis=0)
    return jnp.fft.irfft(zf * kf[None], n=2 * n, axis=1)[:, :n]


def hyena_branch(p_c, lp):
    q = dwconv_seq(p_c, lp["hy_short_w"]) + lp["hy_short_b"]
    v, x1, x2 = jnp.split(q, 3, axis=-1)
    k2 = hyena_filters(p_c.shape[1], lp)
    z = v.astype(F32)
    for o, gate in enumerate((x1, x2)):
        z = gate.astype(F32) * (fft_conv(z, k2[:, o]) + lp["hy_bias"][o].astype(F32) * z)
    return z.astype(p_c.dtype)


def short_conv_branch(p_d, lp):
    bg, cg, xv = jnp.split(p_d, 3, axis=-1)
    return bg * dwconv_seq(cg * xv, lp["sc_conv_w"])


def token_mixer(u, ssd_init, lp, rows):
    p = u @ lp["w_in"]
    y_a = conformer_branch(p[..., OFF_A:OFF_B], lp, rows)
    y_b, ssd_final = ssd_branch(p[..., OFF_B:OFF_C], ssd_init, lp)
    y_c = hyena_branch(p[..., OFF_C:OFF_D], lp)
    y_d = short_conv_branch(p[..., OFF_D:OFF_G], lp)
    gates = jax.nn.sigmoid(p[..., OFF_G:]).reshape(*u.shape[:-1], N_BRANCH, D_MODEL)
    merged = (gates[..., 0, :] * (y_a @ lp["w_branch_a"]) + gates[..., 1, :] * (y_b @ lp["w_branch_b"])
              + gates[..., 2, :] * (y_c @ lp["w_branch_c"]) + gates[..., 3, :] * (y_d @ lp["w_branch_d"]))
    return merged @ lp["w_out"], ssd_final


def moe_ffn(u, lp):
    shp = u.shape
    tok = u.reshape(-1, D_MODEL)
    n_tok = tok.shape[0]
    g_logits = (tok @ lp["rt_group_w"] + lp["rt_group_b"]).astype(F32)
    g_sel = jnp.argmax(g_logits, axis=-1).astype(jnp.int32)
    g_prob = jnp.take_along_axis(jax.nn.softmax(g_logits, axis=-1), g_sel[:, None], axis=-1)
    e_logits = (tok @ lp["rt_expert_w"] + lp["rt_expert_b"]).astype(F32).reshape(n_tok, MOE_GROUPS, MOE_EPG)
    e_logits = jnp.take_along_axis(e_logits, g_sel[:, None, None], axis=1)[:, 0]
    top_v, top_i = lax.top_k(e_logits, MOE_TOP_K)
    weights = jax.nn.softmax(top_v, axis=-1) * g_prob
    expert = g_sel[:, None] * MOE_EPG + top_i.astype(jnp.int32)
    n_assign = n_tok * MOE_TOP_K
    flat_e = expert.reshape(n_assign)
    order = jnp.argsort(flat_e)
    sorted_e = flat_e[order]
    counts = jnp.bincount(flat_e, length=MOE_EXPERTS)
    padded = (counts + MOE_BLOCK - 1) // MOE_BLOCK * MOE_BLOCK
    pad_end = jnp.cumsum(padded)
    pad_start = pad_end - padded
    start = jnp.cumsum(counts) - counts
    dest = pad_start[sorted_e] + jnp.arange(n_assign, dtype=jnp.int32) - start[sorted_e]
    n_blocks = (n_assign + MOE_EXPERTS * (MOE_BLOCK - 1) + MOE_BLOCK - 1) // MOE_BLOCK
    n_rows = n_blocks * MOE_BLOCK
    row_tok = jnp.full((n_rows,), n_tok, jnp.int32).at[dest].set((order // MOE_TOP_K).astype(jnp.int32))
    row_w = jnp.zeros((n_rows,), F32).at[dest].set(weights.reshape(n_assign)[order])
    block_e = jnp.minimum(jnp.searchsorted(pad_end, jnp.arange(n_blocks, dtype=jnp.int32) * MOE_BLOCK, side="right"), MOE_EXPERTS - 1)
    tok_pad = jnp.concatenate([tok, jnp.zeros((1, D_MODEL), tok.dtype)], axis=0)
    xin = tok_pad[row_tok].reshape(n_blocks, MOE_BLOCK, D_MODEL)

    def expert_block(args):
        xb, e = args
        hid = jax.nn.silu(xb @ lp["ex_w_gate"][e]) * (xb @ lp["ex_w_up"][e])
        return hid @ lp["ex_w_down"][e]

    y = lax.map(expert_block, (xin, block_e)).reshape(n_rows, D_MODEL)
    out = jnp.zeros((n_tok + 1, D_MODEL), F32).at[row_tok].add(y * row_w[:, None])
    return out[:n_tok].astype(u.dtype).reshape(shp)


def setup_inputs(seed: int = 0) -> dict:
    key = jax.random.key(seed)
    ks = iter(jax.random.split(key, 64))

    def nrm(shape, scale):
        return scale * jax.random.normal(next(ks), shape, F32)

    L = DEPTH
    dt0 = jnp.exp(jax.random.uniform(next(ks), (L, 2, SSD_HEADS), F32, math.log(1e-3), math.log(1e-1)))
    a0 = jax.random.uniform(next(ks), (L, 2, SSD_HEADS), F32, 1.0, 16.0)
    return {
        "x": nrm((BATCH, SEQ, D_MODEL), 1.0),
        "c": nrm((BATCH, D_MODEL), 1.0),
        "ctx": nrm((BATCH, CTX_LEN, D_MODEL), 1.0),
        "c_ctx": nrm((D_MODEL,), 1.0),
        "w_mod": nrm((L, D_MODEL, 6 * D_MODEL), D_MODEL ** -0.5),
        "b_mod": nrm((L, 6 * D_MODEL), 0.01),
        "ln_g": 1.0 + nrm((L, 2, D_MODEL), 0.01),
        "ln_b": nrm((L, 2, D_MODEL), 0.01),
        "w_in": nrm((L, D_MODEL, PROJ_TOTAL), D_MODEL ** -0.5),
        "conf_dw_w": nrm((L, CONF_K, CONF_D), CONF_K ** -0.5),
        "conf_dw_b": nrm((L, CONF_D), 0.01),
        "conf_ln_g": 1.0 + nrm((L, CONF_D), 0.01),
        "conf_ln_b": nrm((L, CONF_D), 0.01),
        "ssd_conv_w": nrm((L, SSD_CONV_K, SSD_XBC), SSD_CONV_K ** -0.5),
        "ssd_conv_b": nrm((L, SSD_XBC), 0.01),
        "ssd_a_log": jnp.log(a0),
        "ssd_dt_bias": dt0 + jnp.log(-jnp.expm1(-dt0)),
        "ssd_d": 1.0 + nrm((L, SSD_HEADS), 0.01),
        "ssd_norm_g": 1.0 + nrm((L, SSD_D), 0.01),
        "hy_short_w": nrm((L, HY_SHORT_K, 3 * HY_D), HY_SHORT_K ** -0.5),
        "hy_short_b": nrm((L, 3 * HY_D), 0.01),
        "hy_w1": nrm((L, HY_EMB, HY_HID), HY_EMB ** -0.5),
        "hy_b1": nrm((L, HY_HID), 0.1),
        "hy_w2": nrm((L, HY_HID, HY_HID), HY_HID ** -0.5),
        "hy_b2": nrm((L, HY_HID), 0.1),
        "hy_freq": 1.0 + nrm((L, 2, HY_HID), 0.01),
        "hy_w3": nrm((L, HY_HID, HY_N_FILT), HY_HID ** -0.5),
        "hy_bias": nrm((L, HY_ORDER, HY_D), 0.1),
        "sc_conv_w": nrm((L, SC_K, SC_D), SC_K ** -0.5),
        "w_branch_a": nrm((L, CONF_D, D_MODEL), CONF_D ** -0.5),
        "w_branch_b": nrm((L, SSD_D, D_MODEL), SSD_D ** -0.5),
        "w_branch_c": nrm((L, HY_D, D_MODEL), HY_D ** -0.5),
        "w_branch_d": nrm((L, SC_D, D_MODEL), SC_D ** -0.5),
        "w_out": nrm((L, D_MODEL, D_MODEL), DN_BETA * D_MODEL ** -0.5),
        "rt_group_w": nrm((L, D_MODEL, MOE_GROUPS), D_MODEL ** -0.5),
        "rt_group_b": nrm((L, MOE_GROUPS), 0.01),
        "rt_expert_w": nrm((L, D_MODEL, MOE_EXPERTS), D_MODEL ** -0.5),
        "rt_expert_b": nrm((L, MOE_EXPERTS), 0.01),
        "ex_w_gate": nrm((L, MOE_EXPERTS, D_MODEL, MOE_FF), D_MODEL ** -0.5),
        "ex_w_up": nrm((L, MOE_EXPERTS, D_MODEL, MOE_FF), D_MODEL ** -0.5),
        "ex_w_down": nrm((L, MOE_EXPERTS, MOE_FF, D_MODEL), DN_BETA * MOE_FF ** -0.5),
    }


def reference(x, c, ctx, c_ctx, w_mod, b_mod, ln_g, ln_b, w_in, conf_dw_w, conf_dw_b, conf_ln_g, conf_ln_b,
              ssd_conv_w, ssd_conv_b, ssd_a_log, ssd_dt_bias, ssd_d, ssd_norm_g, hy_short_w, hy_short_b,
              hy_w1, hy_b1, hy_w2, hy_b2, hy_freq, hy_w3, hy_bias, sc_conv_w, w_branch_a, w_branch_b,
              w_branch_c, w_branch_d, w_out, rt_group_w, rt_group_b, rt_expert_w, rt_expert_b,
              ex_w_gate, ex_w_up, ex_w_down):
    bsz, n_lat, _ = x.shape
    rows = n_lat // GRID_W
    ssd_zero = jnp.zeros((2, bsz, SSD_GROUPS, SSD_HPG, SSD_HEAD_DIM, SSD_STATE), F32)
    h_lat, h_ctx = x, ctx
    for l in range(DEPTH):
        lp = dict(w_in=w_in[l], conf_dw_w=conf_dw_w[l], conf_dw_b=conf_dw_b[l], conf_ln_g=conf_ln_g[l],
                  conf_ln_b=conf_ln_b[l], ssd_conv_w=ssd_conv_w[l], ssd_conv_b=ssd_conv_b[l],
                  ssd_a_log=ssd_a_log[l], ssd_dt_bias=ssd_dt_bias[l], ssd_d=ssd_d[l], ssd_norm_g=ssd_norm_g[l],
                  hy_short_w=hy_short_w[l], hy_short_b=hy_short_b[l], hy_w1=hy_w1[l], hy_b1=hy_b1[l],
                  hy_w2=hy_w2[l], hy_b2=hy_b2[l], hy_freq=hy_freq[l], hy_w3=hy_w3[l], hy_bias=hy_bias[l],
                  sc_conv_w=sc_conv_w[l], w_branch_a=w_branch_a[l], w_branch_b=w_branch_b[l],
                  w_branch_c=w_branch_c[l], w_branch_d=w_branch_d[l], w_out=w_out[l],
                  rt_group_w=rt_group_w[l], rt_group_b=rt_group_b[l], rt_expert_w=rt_expert_w[l],
                  rt_expert_b=rt_expert_b[l], ex_w_gate=ex_w_gate[l], ex_w_up=ex_w_up[l], ex_w_down=ex_w_down[l])
        last = l == DEPTH - 1
        mod_lat = jnp.split((jax.nn.silu(c) @ w_mod[l] + b_mod[l])[:, None, :], 6, axis=-1)
        mod_ctx = jnp.split(jax.nn.silu(c_ctx) @ w_mod[l] + b_mod[l], 6, axis=-1)
        u_ctx = modulate(h_ctx, mod_ctx[0], mod_ctx[1])
        if last:
            ctx_states = ssd_context_states(u_ctx @ w_in[l][:, OFF_B:OFF_C], ssd_zero, lp)
        else:
            mix_ctx, ctx_states = token_mixer(u_ctx, ssd_zero, lp, None)
        mix_lat, _ = token_mixer(modulate(h_lat, mod_lat[0], mod_lat[1]), ctx_states, lp, rows)
        h_lat = layer_norm(DN_ALPHA * h_lat + mod_lat[2] * mix_lat, ln_g[l, 0], ln_b[l, 0])
        ffn_lat = moe_ffn(modulate(h_lat, mod_lat[3], mod_lat[4]), lp)
        h_lat = layer_norm(DN_ALPHA * h_lat + mod_lat[5] * ffn_lat, ln_g[l, 1], ln_b[l, 1])
        if not last:
            h_ctx = layer_norm(DN_ALPHA * h_ctx + mod_ctx[2] * mix_ctx, ln_g[l, 0], ln_b[l, 0])
            ffn_ctx = moe_ffn(modulate(h_ctx, mod_ctx[3], mod_ctx[4]), lp)
            h_ctx = layer_norm(DN_ALPHA * h_ctx + mod_ctx[5] * ffn_ctx, ln_g[l, 1], ln_b[l, 1])
    return h_lat
```

```python
import functools
import math

import jax
import jax.numpy as jnp
from jax import lax
from jax.experimental import pallas as pl
from jax.experimental.pallas import tpu as pltpu

F32 = jnp.float32
BF16 = jnp.bfloat16
HIGHEST = lax.Precision.HIGHEST

D_MODEL = 1024
DEPTH = 4
GRID_W = 64
CONF_D = 512
CONF_K = 31
SSD_D = 768
SSD_HEADS = 12
SSD_HEAD_DIM = 64
SSD_GROUPS = 4
SSD_HPG = SSD_HEADS // SSD_GROUPS
SSD_STATE = 128
SSD_CHUNK = 128
SSD_BC = SSD_GROUPS * SSD_STATE
SSD_XBC = SSD_D + 2 * SSD_BC
SSD_PROJ = SSD_D + SSD_XBC + 2 * SSD_HEADS
HY_D = 512
HY_ORDER = 2
HY_EMB = 33
HY_BANDS = (HY_EMB - 1) // 2
HY_HID = 64
HY_N_FILT = HY_ORDER * 2 * HY_D
HY_MIN_DECAY = math.log(1e-2) / 1.5
HY_MAX_DECAY = math.log(1e-2) / 0.3
SC_D = 512
N_BRANCH = 4
OFF_A = 0
OFF_B = OFF_A + 2 * CONF_D
OFF_C = OFF_B + SSD_PROJ
OFF_D = OFF_C + 3 * HY_D
OFF_G = OFF_D + 3 * SC_D
MOE_GROUPS = 4
MOE_EPG = 8
MOE_EXPERTS = MOE_GROUPS * MOE_EPG
MOE_TOP_K = 2
MOE_FF = 512
DN_ALPHA = (2 * DEPTH) ** 0.25
LN_EPS = 1e-5

PG = 0
PA = PG + N_BRANCH * D_MODEL
PC = PA + 2 * CONF_D
PD = PC + 3 * HY_D
PX = PD + 3 * SC_D
PZ = PX + SSD_XBC
PDT = PZ + SSD_D
NP = PDT + 128

LANE = 128
SUBLANE = 8
FFT_N2 = 256
MOE_ROWS = 128
VMEM_LIMIT = 48 * 1024 * 1024


def _params(*sem):
    return pltpu.CompilerParams(dimension_semantics=sem, vmem_limit_bytes=VMEM_LIMIT)


def _silu(x):
    return x * jax.nn.sigmoid(x)


def _layer_norm(x, g, b):
    mu = jnp.mean(x, -1, keepdims=True)
    xc = x - mu
    var = jnp.mean(xc * xc, -1, keepdims=True)
    return xc * lax.rsqrt(var + LN_EPS) * g + b


def _dot(a, b):
    return jnp.dot(a, b, preferred_element_type=F32)


def _mod_kernel(cv_ref, w_ref, b_ref, o_ref):
    o_ref[...] = jnp.dot(_silu(cv_ref[...]), w_ref[...], precision=HIGHEST,
                         preferred_element_type=F32) + b_ref[...]


def _mod_vectors(cv, w_mod, b_mod):
    tn = 1536
    return pl.pallas_call(
        _mod_kernel,
        grid=(DEPTH, 6 * D_MODEL // tn),
        in_specs=[pl.BlockSpec((SUBLANE, D_MODEL), lambda l, j: (0, 0)),
                  pl.BlockSpec((None, D_MODEL, tn), lambda l, j: (l, 0, j)),
                  pl.BlockSpec((None, 1, tn), lambda l, j: (l, 0, j))],
        out_specs=pl.BlockSpec((None, SUBLANE, tn), lambda l, j: (l, 0, j)),
        out_shape=jax.ShapeDtypeStruct((DEPTH, SUBLANE, 6 * D_MODEL), F32),
        compiler_params=_params("parallel", "parallel"),
        name="mod_vectors",
    )(cv, w_mod, b_mod.reshape(DEPTH, 1, 6 * D_MODEL))


def _inproj_kernel(x_ref, sh_ref, sc_ref, w_ref, o_ref, xb_ref):
    @pl.when(pl.program_id(1) == 0)
    def _():
        xb_ref[...] = (x_ref[...] * (1.0 + sc_ref[...]) + sh_ref[...]).astype(BF16)

    o_ref[...] = _dot(xb_ref[...], w_ref[...])


def _inproj(h, shift, scale, w):
    n = h.shape[0]
    tm = min(n, 1024)
    tn = 640
    return pl.pallas_call(
        _inproj_kernel,
        grid=(n // tm, NP // tn),
        in_specs=[pl.BlockSpec((tm, D_MODEL), lambda i, j: (i, 0)),
                  pl.BlockSpec((1, D_MODEL), lambda i, j: (0, 0)),
                  pl.BlockSpec((1, D_MODEL), lambda i, j: (0, 0)),
                  pl.BlockSpec((D_MODEL, tn), lambda i, j: (0, j))],
        out_specs=pl.BlockSpec((tm, tn), lambda i, j: (i, j)),
        out_shape=jax.ShapeDtypeStruct((n, NP), F32),
        scratch_shapes=[pltpu.VMEM((tm, D_MODEL), BF16)],
        compiler_params=_params("parallel", "arbitrary"),
        name="inproj",
    )(h, shift, scale, w)


def _shifted(x, prev_row, next_row):
    t = x.shape[0]
    row = lax.broadcasted_iota(jnp.int32, x.shape, 0)
    xm = jnp.where(row == 0, prev_row, pltpu.roll(x, 1, 0))
    xp = jnp.where(row == t - 1, next_row, pltpu.roll(x, t - 1, 0))
    return xm, xp


def _conv3_kernel(cur_ref, prev_ref, next_ref, w_ref, b_ref, o_ref, *, silu, nt):
    i = pl.program_id(0)
    x = cur_ref[...]
    pv = jnp.where(i > 0, prev_ref[SUBLANE - 1:SUBLANE, :], 0.0)
    nx = jnp.where(i < nt - 1, next_ref[0:1, :], 0.0)
    xm, xp = _shifted(x, pv, nx)
    y = w_ref[0:1, :] * xm + w_ref[1:2, :] * x + w_ref[2:3, :] * xp + b_ref[...]
    o_ref[...] = _silu(y) if silu else y


def _halo_specs(t, ct, n, col0):
    r8 = t // SUBLANE
    last8 = n // SUBLANE - 1
    return [pl.BlockSpec((t, ct), lambda i, j: (i, col0 + j)),
            pl.BlockSpec((SUBLANE, ct), lambda i, j: (jnp.maximum(i * r8 - 1, 0), col0 + j)),
            pl.BlockSpec((SUBLANE, ct), lambda i, j: (jnp.minimum((i + 1) * r8, last8), col0 + j))]


def _conv3(p, col, width, w, b, *, silu):
    n = p.shape[0]
    t = min(n, 1024)
    ct = 256
    nt = n // t
    return pl.pallas_call(
        functools.partial(_conv3_kernel, silu=silu, nt=nt),
        grid=(nt, width // ct),
        in_specs=_halo_specs(t, ct, n, col // ct) + [
            pl.BlockSpec((3, ct), lambda i, j: (0, j)),
            pl.BlockSpec((1, ct), lambda i, j: (0, j))],
        out_specs=pl.BlockSpec((t, ct), lambda i, j: (i, j)),
        out_shape=jax.ShapeDtypeStruct((n, width), F32),
        compiler_params=_params("parallel", "parallel"),
        name="conv3",
    )(p, p, p, w, b)


def _gconv_kernel(bg_ref, cc_ref, cp_ref, cn_ref, xc_ref, xp_ref, xn_ref, w_ref, o_ref, *, nt):
    i = pl.program_id(0)
    x = cc_ref[...] * xc_ref[...]
    pv = jnp.where(i > 0, cp_ref[SUBLANE - 1:SUBLANE, :] * xp_ref[SUBLANE - 1:SUBLANE, :], 0.0)
    nx = jnp.where(i < nt - 1, cn_ref[0:1, :] * xn_ref[0:1, :], 0.0)
    xm, xp = _shifted(x, pv, nx)
    o_ref[...] = bg_ref[...] * (w_ref[0:1, :] * xm + w_ref[1:2, :] * x + w_ref[2:3, :] * xp)


def _gated_conv(p, w):
    n = p.shape[0]
    t = min(n, 1024)
    ct = 256
    nt = n // t
    nb = SC_D // ct
    return pl.pallas_call(
        functools.partial(_gconv_kernel, nt=nt),
        grid=(nt, nb),
        in_specs=([pl.BlockSpec((t, ct), lambda i, j: (i, PD // ct + j))]
                  + _halo_specs(t, ct, n, PD // ct + nb)
                  + _halo_specs(t, ct, n, PD // ct + 2 * nb)
                  + [pl.BlockSpec((3, ct), lambda i, j: (0, j))]),
        out_specs=pl.BlockSpec((t, ct), lambda i, j: (i, j)),
        out_shape=jax.ShapeDtypeStruct((n, SC_D), F32),
        compiler_params=_params("parallel", "parallel"),
        name="gated_conv",
    )(p, p, p, p, p, p, p, w)


CONF_RB = 64


def _conf_kernel(vc, gc, vp, gp, vn, gn, w_ref, b_ref, lg_ref, lb_ref, o_ref, buf, *, t, halo, dil, nt):
    i = pl.program_id(0)
    buf[halo:halo + t, :] = vc[...] * jax.nn.sigmoid(gc[...])
    buf[0:halo, :] = jnp.where(i > 0, vp[t - halo:t, :] * jax.nn.sigmoid(gp[t - halo:t, :]), 0.0)
    buf[halo + t:halo + t + halo, :] = jnp.where(
        i < nt - 1, vn[0:halo, :] * jax.nn.sigmoid(gn[0:halo, :]), 0.0)

    def block(r0):
        acc = jnp.zeros((CONF_RB, CONF_D), F32)
        for j in range(CONF_K):
            off = halo + (j - CONF_K // 2) * dil
            acc = acc + w_ref[j:j + 1, :] * buf[pl.ds(r0 + off, CONF_RB), :]
        v = _layer_norm(acc + b_ref[...], lg_ref[...], lb_ref[...])
        o_ref[pl.ds(r0, CONF_RB), :] = _silu(v)

    if dil % CONF_RB == 0:
        def body(rb, carry):
            block(pl.multiple_of(rb * CONF_RB, CONF_RB))
            return carry
        lax.fori_loop(0, t // CONF_RB, body, 0)
    else:
        for rb in range(t // CONF_RB):
            block(rb * CONF_RB)


def _conformer(p, w, b, lg, lb, *, dil):
    n = p.shape[0]
    t = min(n, 1024)
    nt = n // t
    halo = -(-(CONF_K // 2) * dil // SUBLANE) * SUBLANE
    assert halo <= t
    cb = PA // CONF_D

    def spec(col, shift):
        return pl.BlockSpec((t, CONF_D), lambda i: (jnp.clip(i + shift, 0, nt - 1), col))

    vec = pl.BlockSpec((1, CONF_D), lambda i: (0, 0))
    return pl.pallas_call(
        functools.partial(_conf_kernel, t=t, halo=halo, dil=dil, nt=nt),
        grid=(nt,),
        in_specs=[spec(cb, 0), spec(cb + 1, 0), spec(cb, -1), spec(cb + 1, -1), spec(cb, 1), spec(cb + 1, 1),
                  pl.BlockSpec((CONF_K, CONF_D), lambda i: (0, 0)), vec, vec, vec],
        out_specs=pl.BlockSpec((t, CONF_D), lambda i: (i, 0)),
        out_shape=jax.ShapeDtypeStruct((n, CONF_D), F32),
        scratch_shapes=[pltpu.VMEM((t + 2 * halo, CONF_D), F32)],
        compiler_params=_params("parallel"),
        name="conformer",
    )(p, p, p, p, p, p, w, b, lg, lb)


def _ssd_kernel(xbc_ref, dt_ref, dtb_ref, alog_ref, init_ref, y_ref, fin_ref, h_ref, *, nc):
    d = pl.program_id(0)
    c = pl.program_id(1)
    q = SSD_CHUNK
    hd = SSD_HEAD_DIM

    @pl.when(c == 0)
    def _():
        h_ref[...] = init_ref[...]

    raw = dt_ref[...] + dtb_ref[...]
    dt_all = jnp.maximum(raw, 0.0) + jnp.log(1.0 + jnp.exp(-jnp.abs(raw)))
    ld_all = dt_all * (-jnp.exp(alog_ref[...]))
    lane = lax.broadcasted_iota(jnp.int32, (q, LANE), 1)
    head = lane < SSD_HEADS
    dt_d = jnp.where(head, jnp.where(d == 0, dt_all, pltpu.roll(dt_all, LANE - SSD_HEADS, 1)), 0.0)
    ld_d = jnp.where(head, jnp.where(d == 0, ld_all, pltpu.roll(ld_all, LANE - SSD_HEADS, 1)), 0.0)

    li = lax.broadcasted_iota(jnp.int32, (q, q), 0)
    si = lax.broadcasted_iota(jnp.int32, (q, q), 1)
    mask = (li - si) * (1 - 2 * d) >= 0
    cum = jnp.dot(mask.astype(F32), ld_d, precision=HIGHEST, preferred_element_type=F32)
    tot = jnp.sum(ld_d, axis=0, keepdims=True)
    cum_t = cum.T
    dt_t = dt_d.T
    w_t = (jnp.exp(tot - cum) * dt_d).T
    a_out = jnp.exp(cum)
    e_tot = jnp.exp(tot)

    for g in range(SSD_GROUPS):
        bg = xbc_ref[:, SSD_D + g * SSD_STATE:SSD_D + (g + 1) * SSD_STATE]
        cg = xbc_ref[:, SSD_D + SSD_BC + g * SSD_STATE:SSD_D + SSD_BC + (g + 1) * SSD_STATE]
        bg_t = bg.T
        cb = _dot(cg.astype(BF16), bg_t.astype(BF16))
        for e in range(SSD_HPG):
            hh = g * SSD_HPG + e
            diff = cum[:, hh:hh + 1] - cum_t[hh:hh + 1, :]
            dec = jnp.exp(jnp.where(mask, diff, -1e30))
            m = (cb * dec * dt_t[hh:hh + 1, :]).astype(BF16)
            xe = xbc_ref[:, hh * hd:(hh + 1) * hd].astype(BF16)
            cs = (cg * a_out[:, hh:hh + 1]).astype(BF16)
            h_in = h_ref[hh]
            y_ref[:, hh * hd:(hh + 1) * hd] = _dot(m, xe) + _dot(cs, h_in.astype(BF16))
            s_new = _dot((bg_t * w_t[hh:hh + 1, :]).astype(BF16), xe)
            h_ref[hh] = e_tot[:, hh:hh + 1] * h_in + s_new

    @pl.when(c == nc - 1)
    def _():
        fin_ref[...] = h_ref[...]


def _ssd_scan(xbc, p, dt_bias, a_log, init):
    n = xbc.shape[0]
    q = SSD_CHUNK
    nc = n // q

    def chunk(d, c):
        return jnp.where(d == 0, c, nc - 1 - c)

    st = (SSD_HEADS, SSD_STATE, SSD_HEAD_DIM)
    vec = pl.BlockSpec((1, LANE), lambda d, c: (0, 0))
    return pl.pallas_call(
        functools.partial(_ssd_kernel, nc=nc),
        grid=(2, nc),
        in_specs=[pl.BlockSpec((q, SSD_XBC), lambda d, c: (chunk(d, c), 0)),
                  pl.BlockSpec((q, LANE), lambda d, c: (chunk(d, c), PDT // LANE)),
                  vec, vec,
                  pl.BlockSpec((None,) + st, lambda d, c: (d, 0, 0, 0))],
        out_specs=[pl.BlockSpec((None, q, SSD_D), lambda d, c: (d, chunk(d, c), 0)),
                   pl.BlockSpec((None,) + st, lambda d, c: (d, 0, 0, 0))],
        out_shape=[jax.ShapeDtypeStruct((2, n, SSD_D), F32),
                   jax.ShapeDtypeStruct((2,) + st, F32)],
        scratch_shapes=[pltpu.VMEM(st, F32)],
        compiler_params=_params("arbitrary", "arbitrary"),
        name="ssd_scan",
    )(xbc, p, dt_bias, a_log, init)


def _filt_kernel(feat_ref, w1_ref, b1_ref, w2_ref, b2_ref, fr_ref, w3_ref, dl_ref, k_ref, nrm_ref, *, n, t):
    i = pl.program_id(0)
    feat = feat_ref[...]
    hid = jnp.sin(fr_ref[0:1, :] * (jnp.dot(feat, w1_ref[...], precision=HIGHEST,
                                            preferred_element_type=F32) + b1_ref[...]))
    hid = jnp.sin(fr_ref[1:2, :] * (jnp.dot(hid, w2_ref[...], precision=HIGHEST,
                                            preferred_element_type=F32) + b2_ref[...]))
    filt = jnp.dot(hid, w3_ref[...], precision=HIGHEST, preferred_element_type=F32)
    filt = filt * jnp.exp(-feat[:, 0:1] * dl_ref[...])
    row = i * t + lax.broadcasted_iota(jnp.int32, filt.shape, 0)
    filt = jnp.where(row == n, 0.0, filt)
    k_ref[...] = filt.astype(k_ref.dtype)

    @pl.when(i == 0)
    def _():
        nrm_ref[...] = jnp.zeros_like(nrm_ref)

    nrm_ref[...] += jnp.sum(jnp.abs(filt), axis=0, keepdims=True)


def _hyena_filters(featx, w1, b1, w2, b2, freq, w3d, deltas, n, dtype):
    t = min(n, 512)
    half = n // t
    oc = HY_ORDER * HY_D
    full = lambda shape: pl.BlockSpec(shape, lambda i: tuple(0 for _ in shape))
    return pl.pallas_call(
        functools.partial(_filt_kernel, n=n, t=t),
        grid=(2 * n // t,),
        in_specs=[pl.BlockSpec((t, LANE), lambda i: (i, 0)),
                  full((LANE, LANE)), full((1, LANE)), full((LANE, LANE)), full((1, LANE)),
                  full((2, LANE)),
                  pl.BlockSpec((None, LANE, oc), lambda i: (i // half, 0, 0)),
                  pl.BlockSpec((None, 1, oc), lambda i: (i // half, 0, 0))],
        out_specs=[pl.BlockSpec((t, oc), lambda i: (i, 0)),
                   pl.BlockSpec((1, oc), lambda i: (0, 0))],
        out_shape=[jax.ShapeDtypeStruct((2 * n, oc), dtype),
                   jax.ShapeDtypeStruct((1, oc), F32)],
        compiler_params=_params("arbitrary"),
        name="hyena_filters",
    )(featx, w1, b1, w2, b2, freq, w3d, deltas)


def _dft_rows_kernel(f_ref, x_ref, o_ref):
    o_ref[...] = _dot(f_ref[...], x_ref[...].astype(BF16)).astype(o_ref.dtype)


def _dft_rows(fmat, x2d):
    m, k = fmat.shape
    ncols = x2d.shape[1]
    tn = min(ncols, 8192)
    return pl.pallas_call(
        _dft_rows_kernel,
        grid=(ncols // tn,),
        in_specs=[pl.BlockSpec((m, k), lambda j: (0, 0)),
                  pl.BlockSpec((k, tn), lambda j: (0, j))],
        out_specs=pl.BlockSpec((m, tn), lambda j: (0, j)),
        out_shape=jax.ShapeDtypeStruct((m, ncols), BF16),
        compiler_params=_params("parallel"),
        name="dft_rows",
    )(fmat, x2d)


def _spec_kernel(ar_ref, ai_ref, gr_ref, gi_ref, kr_ref, ki_ref):
    ar, ai, gr, gi = ar_ref[...], ai_ref[...], gr_ref[...], gi_ref[...]
    kr_ref[...] = _dot(gr, ar) - _dot(gi, ai)
    ki_ref[...] = _dot(gr, ai) + _dot(gi, ar)


def _filter_spectrum(a4, gr, gi):
    _, n1, n2, ch = a4.shape
    ct = 512
    blk = lambda ri: pl.BlockSpec((None, None, n2, ct), lambda f, j: (ri, f, 0, j))
    gspec = pl.BlockSpec((None, n2, n2), lambda f, j: (f, 0, 0))
    ospec = pl.BlockSpec((None, n2, ct), lambda f, j: (f, 0, j))
    return pl.pallas_call(
        _spec_kernel,
        grid=(n1, ch // ct),
        in_specs=[blk(0), blk(1), gspec, gspec],
        out_specs=[ospec, ospec],
        out_shape=[jax.ShapeDtypeStruct((n1, n2, ch), F32)] * 2,
        compiler_params=_params("parallel", "parallel"),
        name="filter_spectrum",
    )(a4, a4, gr, gi)


def _mid_kernel(ar_ref, ai_ref, gr_ref, gi_ref, grt_ref, git_ref, kr_ref, ki_ref, br_ref, bi_ref):
    ar, ai, gr, gi = ar_ref[...], ai_ref[...], gr_ref[...], gi_ref[...]
    xr = _dot(gr, ar) - _dot(gi, ai)
    xi = _dot(gr, ai) + _dot(gi, ar)
    kr, ki = kr_ref[...], ki_ref[...]
    yr = (xr * kr - xi * ki).astype(BF16)
    yi = (xr * ki + xi * kr).astype(BF16)
    grt, git = grt_ref[...], git_ref[...]
    br_ref[...] = (_dot(grt, yr) + _dot(git, yi)).astype(BF16)
    bi_ref[...] = (_dot(grt, yi) - _dot(git, yr)).astype(BF16)


def _hyena_mid(a4, tabs, kf_r, kf_i, order):
    _, n1, n2, ch = a4.shape
    gr, gi, grt, git = tabs
    blk = lambda ri: pl.BlockSpec((None, None, n2, ch), lambda f: (ri, f, 0, 0))
    kspec = pl.BlockSpec((None, n2, ch), lambda f: (f, 0, order))
    gspec = pl.BlockSpec((None, n2, n2), lambda f: (f, 0, 0))
    ospec = pl.BlockSpec((None, n2, ch), lambda f: (f, 0, 0))
    return pl.pallas_call(
        _mid_kernel,
        grid=(n1,),
        in_specs=[blk(0), blk(1), gspec, gspec, gspec, gspec, kspec, kspec],
        out_specs=[ospec, ospec],
        out_shape=[jax.ShapeDtypeStruct((n1, n2, ch), BF16)] * 2,
        compiler_params=_params("parallel"),
        name="hyena_mid",
    )(a4, a4, gr, gi, grt, git, kf_r, kf_i)


def _inv_kernel(f_ref, br_ref, bi_ref, s_ref, bias_ref, z_ref, g_ref, o_ref, *, n1):
    acc = _dot(f_ref[:, 0:n1], br_ref[...]) + _dot(f_ref[:, n1:2 * n1], bi_ref[...])
    o_ref[...] = g_ref[...] * (acc * s_ref[...] + bias_ref[...] * z_ref[...])


def _hyena_inverse(finv, b_r, b_i, s_row, bias_row, z2d, g2d):
    t1, k2 = finv.shape
    n1 = k2 // 2
    ncols = z2d.shape[1]
    tn = s_row.shape[1]
    col = pl.BlockSpec((n1, tn), lambda j: (0, j))
    row = pl.BlockSpec((1, tn), lambda j: (0, 0))
    io = pl.BlockSpec((t1, tn), lambda j: (0, j))
    return pl.pallas_call(
        functools.partial(_inv_kernel, n1=n1),
        grid=(ncols // tn,),
        in_specs=[pl.BlockSpec((t1, k2), lambda j: (0, 0)), col, col, row, row, io, io],
        out_specs=io,
        out_shape=jax.ShapeDtypeStruct((t1, ncols), F32),
        compiler_params=_params("parallel"),
        name="hyena_inverse",
    )(finv, b_r, b_i, s_row, bias_row, z2d, g2d)


def _dft_tables(n):
    n2 = FFT_N2
    n1 = 2 * n // n2
    tot = 2 * n
    two_pi = 2.0 * math.pi

    def cs(num, den):
        ang = (two_pi / den) * (num % den).astype(F32)
        return jnp.cos(ang), jnp.sin(ang)

    f1 = jnp.arange(n1, dtype=jnp.int32)
    t1 = jnp.arange(n1, dtype=jnp.int32)
    c1, s1 = cs(f1[:, None] * t1[None, :], n1)
    fwd_full = jnp.concatenate([c1, -s1], axis=0).astype(BF16)
    fwd_half = fwd_full[:, :n1 // 2]
    inv = jnp.concatenate([c1[:n1 // 2], -s1[:n1 // 2]], axis=1).astype(BF16)
    t2 = jnp.arange(n2, dtype=jnp.int32)
    twr, twi = cs(f1[:, None] * t2[None, :], tot)
    fr, fi = cs(t2[:, None] * t2[None, :], n2)
    twi, fi = -twi, -fi
    gr = twr[:, None, :] * fr[None] - twi[:, None, :] * fi[None]
    gi = twr[:, None, :] * fi[None] + twi[:, None, :] * fr[None]
    tabs = (gr.astype(BF16), gi.astype(BF16),
            gr.transpose(0, 2, 1).astype(BF16), gi.transpose(0, 2, 1).astype(BF16))
    return fwd_full, fwd_half, inv, tabs


def _hyena_long(q, k2, nrm, hy_bias, tables):
    n = q.shape[0]
    n2 = FFT_N2
    n1 = 2 * n // n2
    fwd_full, fwd_half, inv, tabs = tables
    oc = HY_ORDER * HY_D
    ak = _dft_rows(fwd_full, k2.reshape(n1, n2 * oc)).reshape(2, n1, n2, oc)
    kf_r, kf_i = _filter_spectrum(ak, tabs[0], tabs[1])
    tn = min(n2 * HY_D, 8192)
    rep = tn // HY_D
    z = q[:, :HY_D]
    for o in range(HY_ORDER):
        gate = q[:, (o + 1) * HY_D:(o + 2) * HY_D]
        z2d = z.reshape(n1 // 2, n2 * HY_D)
        a4 = _dft_rows(fwd_half, z2d).reshape(2, n1, n2, HY_D)
        b_r, b_i = _hyena_mid(a4, tabs, kf_r, kf_i, o)
        s_row = jnp.tile(1.0 / (2.0 * n * nrm[:, o * HY_D:(o + 1) * HY_D]), (1, rep))
        bias_row = jnp.tile(hy_bias[o][None, :], (1, rep))
        z = _hyena_inverse(inv, b_r.reshape(n1, n2 * HY_D), b_i.reshape(n1, n2 * HY_D), s_row, bias_row,
                           z2d, gate.reshape(n1 // 2, n2 * HY_D)).reshape(n, HY_D)
    return z


def _hy_ctx_kernel(v_ref, x1_ref, x2_ref, k0_ref, k1_ref, n0_ref, n1_ref, bias_ref, o_ref, kf, zs, *, n):
    zs[...] = v_ref[...]
    for o, (k_ref, nr_ref, x_ref) in enumerate(((k0_ref, n0_ref, x1_ref), (k1_ref, n1_ref, x2_ref))):
        kf[0:n, :] = k_ref[n:2 * n, :]
        kf[n:2 * n, :] = k_ref[0:n, :]

        def body(s, acc):
            return acc + kf[pl.ds(n - s, n), :] * zs[pl.ds(s, 1), :]

        acc = lax.fori_loop(0, n, body, jnp.zeros((n, LANE), F32))
        z = zs[...]
        zs[...] = x_ref[...] * (acc / nr_ref[...] + bias_ref[o:o + 1, :] * z)
    o_ref[...] = zs[...]


def _hyena_ctx(q, k2, nrm, hy_bias):
    n = q.shape[0]
    nb = HY_D // LANE
    col = lambda c0: pl.BlockSpec((n, LANE), lambda j: (0, c0 + j))
    kcol = lambda c0: pl.BlockSpec((2 * n, LANE), lambda j: (0, c0 + j))
    ncol = lambda c0: pl.BlockSpec((1, LANE), lambda j: (0, c0 + j))
    return pl.pallas_call(
        functools.partial(_hy_ctx_kernel, n=n),
        grid=(nb,),
        in_specs=[col(0), col(nb), col(2 * nb), kcol(0), kcol(nb), ncol(0), ncol(nb),
                  pl.BlockSpec((HY_ORDER, LANE), lambda j: (0, j))],
        out_specs=pl.BlockSpec((n, LANE), lambda j: (0, j)),
        out_shape=jax.ShapeDtypeStruct((n, HY_D), F32),
        scratch_shapes=[pltpu.VMEM((2 * n, LANE), F32), pltpu.VMEM((n, LANE), F32)],
        compiler_params=_params("parallel"),
        name="hyena_ctx",
    )(q, q, q, k2, k2, nrm, nrm, hy_bias)


def _merge_kernel(ya_ref, xs_ref, yf_ref, yb_ref, z_ref, yc_ref, yd_ref, g_ref, h_ref,
                  dv_ref, ng_ref, g1_ref, lg_ref, lb_ref,
                  wa_ref, wb_ref, wc_ref, wd_ref, wo_ref, o_ref):
    y = xs_ref[...] * dv_ref[...] + yf_ref[...] + yb_ref[...]
    gz = y * _silu(z_ref[...])
    ssd = gz * lax.rsqrt(jnp.mean(gz * gz, -1, keepdims=True) + LN_EPS) * ng_ref[...]
    d = D_MODEL
    m = jax.nn.sigmoid(g_ref[:, 0:d]) * _dot(ya_ref[...].astype(BF16), wa_ref[...])
    m = m + jax.nn.sigmoid(g_ref[:, d:2 * d]) * _dot(ssd.astype(BF16), wb_ref[...])
    m = m + jax.nn.sigmoid(g_ref[:, 2 * d:3 * d]) * _dot(yc_ref[...].astype(BF16), wc_ref[...])
    m = m + jax.nn.sigmoid(g_ref[:, 3 * d:4 * d]) * _dot(yd_ref[...].astype(BF16), wd_ref[...])
    mix = _dot(m.astype(BF16), wo_ref[...])
    o_ref[...] = _layer_norm(DN_ALPHA * h_ref[...] + g1_ref[...] * mix, lg_ref[...], lb_ref[...])


def _merge(ya, xbc, ydir, p, yc, yd, h, dvec, ng, gate1, lg, lb, wa, wb, wc, wd, wo):
    n = h.shape[0]
    t = 256
    tok = lambda w, col=0: pl.BlockSpec((t, w), lambda i: (i, col))
    vec = lambda w: pl.BlockSpec((1, w), lambda i: (0, 0))
    mat = lambda r: pl.BlockSpec((r, D_MODEL), lambda i: (0, 0))
    return pl.pallas_call(
        _merge_kernel,
        grid=(n // t,),
        in_specs=[tok(CONF_D), tok(SSD_D),
                  pl.BlockSpec((None, t, SSD_D), lambda i: (0, i, 0)),
                  pl.BlockSpec((None, t, SSD_D), lambda i: (1, i, 0)),
                  tok(SSD_D, PZ // SSD_D), tok(HY_D), tok(SC_D), tok(N_BRANCH * D_MODEL, 0), tok(D_MODEL),
                  vec(SSD_D), vec(SSD_D), vec(D_MODEL), vec(D_MODEL), vec(D_MODEL),
                  mat(CONF_D), mat(SSD_D), mat(HY_D), mat(SC_D), mat(D_MODEL)],
        out_specs=tok(D_MODEL),
        out_shape=jax.ShapeDtypeStruct((n, D_MODEL), F32),
        compiler_params=_params("parallel"),
        name="merge",
    )(ya, xbc, ydir, ydir, p, yc, yd, p, h, dvec, ng, gate1, lg, lb, wa, wb, wc, wd, wo)


def _router_kernel(h_ref, sh_ref, sc_ref, w_ref, b_ref, u_ref, sel_ref):
    u = h_ref[...] * (1.0 + sc_ref[...]) + sh_ref[...]
    u_ref[...] = u
    lg = jnp.dot(u, w_ref[...], precision=HIGHEST, preferred_element_type=F32) + b_ref[...]
    lane = lax.broadcasted_iota(jnp.int32, lg.shape, 1).astype(F32)
    neg = -1e30
    big = 1e9
    gl = jnp.where(lane < MOE_GROUPS, lg, neg)
    gmax = jnp.max(gl, -1, keepdims=True)
    gsel = jnp.min(jnp.where(gl == gmax, lane, big), -1, keepdims=True)
    gprob = 1.0 / jnp.sum(jnp.where(lane < MOE_GROUPS, jnp.exp(lg - gmax), 0.0), -1, keepdims=True)
    lo = MOE_GROUPS + gsel * MOE_EPG
    el = jnp.where(jnp.abs(lane - lo - (MOE_EPG - 1) / 2.0) < MOE_EPG / 2.0, lg, neg)
    m1 = jnp.max(el, -1, keepdims=True)
    i1 = jnp.min(jnp.where(el == m1, lane, big), -1, keepdims=True)
    el2 = jnp.where(lane == i1, neg, el)
    m2 = jnp.max(el2, -1, keepdims=True)
    i2 = jnp.min(jnp.where(el2 == m2, lane, big), -1, keepdims=True)
    t = jnp.exp(m2 - m1)
    w1 = gprob / (1.0 + t)
    w2 = gprob * t / (1.0 + t)
    sel_ref[...] = jnp.where(lane == 0, i1 - MOE_GROUPS,
                             jnp.where(lane == 1, i2 - MOE_GROUPS,
                                       jnp.where(lane == 2, w1, jnp.where(lane == 3, w2, 0.0))))


def _router(h, shift, scale, wr, br):
    n = h.shape[0]
    t = 256
    tok = lambda w: pl.BlockSpec((t, w), lambda i: (i, 0))
    vec = lambda w: pl.BlockSpec((1, w), lambda i: (0, 0))
    return pl.pallas_call(
        _router_kernel,
        grid=(n // t,),
        in_specs=[tok(D_MODEL), vec(D_MODEL), vec(D_MODEL),
                  pl.BlockSpec((D_MODEL, LANE), lambda i: (0, 0)), vec(LANE)],
        out_specs=[tok(D_MODEL), tok(LANE)],
        out_shape=[jax.ShapeDtypeStruct((n, D_MODEL), F32), jax.ShapeDtypeStruct((n, LANE), F32)],
        compiler_params=_params("parallel"),
        name="router",
    )(h, shift, scale, wr, br)


def _row_copy(src_hbm, dst, sem, src_row, dst_row):
    return pltpu.make_async_copy(src_hbm.at[pl.ds(src_row, 1), :], dst.at[pl.ds(dst_row, 1), :], sem)


def _expert_kernel(be_ref, rt_ref, x_hbm, wg_ref, wu_ref, wd_ref, o_ref, xbuf, sem):
    def issue(r, carry):
        _row_copy(x_hbm, xbuf, sem, rt_ref[0, 0, r], r).start()
        return carry

    def drain(r, carry):
        _row_copy(x_hbm, xbuf, sem, 0, r).wait()
        return carry

    lax.fori_loop(0, MOE_ROWS, issue, 0)
    lax.fori_loop(0, MOE_ROWS, drain, 0)
    x = xbuf[...].astype(BF16)
    hid = _silu(_dot(x, wg_ref[...])) * _dot(x, wu_ref[...])
    o_ref[...] = _dot(hid.astype(BF16), wd_ref[...])


def _experts(u, block_e, row_tok, wg, wu, wd):
    n_blocks = block_e.shape[0]
    gs = pltpu.PrefetchScalarGridSpec(
        num_scalar_prefetch=1,
        grid=(n_blocks,),
        in_specs=[pl.BlockSpec((1, 1, MOE_ROWS), lambda b, be: (b, 0, 0), memory_space=pltpu.SMEM),
                  pl.BlockSpec(memory_space=pl.ANY),
                  pl.BlockSpec((None, D_MODEL, MOE_FF), lambda b, be: (be[b], 0, 0)),
                  pl.BlockSpec((None, D_MODEL, MOE_FF), lambda b, be: (be[b], 0, 0)),
                  pl.BlockSpec((None, MOE_FF, D_MODEL), lambda b, be: (be[b], 0, 0))],
        out_specs=pl.BlockSpec((MOE_ROWS, D_MODEL), lambda b, be: (b, 0)),
        scratch_shapes=[pltpu.VMEM((MOE_ROWS, D_MODEL), F32), pltpu.SemaphoreType.DMA(())],
    )
    return pl.pallas_call(
        _expert_kernel,
        grid_spec=gs,
        out_shape=jax.ShapeDtypeStruct((n_blocks * MOE_ROWS, D_MODEL), F32),
        compiler_params=_params("arbitrary"),
        name="experts",
    )(block_e, row_tok.reshape(n_blocks, 1, MOE_ROWS), u, wg, wu, wd)


def _combine_kernel(pos_ref, y_hbm, h_ref, sel_ref, g2_ref, lg_ref, lb_ref, o_ref, ybuf, sem, *, t):
    def issue(r, carry):
        _row_copy(y_hbm, ybuf, sem, pos_ref[0, 0, r], r).start()
        return carry

    def drain(r, carry):
        _row_copy(y_hbm, ybuf, sem, 0, r).wait()
        return carry

    lax.fori_loop(0, 2 * t, issue, 0)
    lax.fori_loop(0, 2 * t, drain, 0)
    ffn = sel_ref[:, 2:3] * ybuf[0:t, :] + sel_ref[:, 3:4] * ybuf[t:2 * t, :]
    o_ref[...] = _layer_norm(DN_ALPHA * h_ref[...] + g2_ref[...] * ffn, lg_ref[...], lb_ref[...])


def _combine(y, pos, h, sel, gate2, lg, lb):
    n = h.shape[0]
    t = 256
    nt = n // t
    pos_t = pos.reshape(nt, t, MOE_TOP_K).transpose(0, 2, 1).reshape(nt, 1, 2 * t)
    tok = lambda w: pl.BlockSpec((t, w), lambda i: (i, 0))
    vec = pl.BlockSpec((1, D_MODEL), lambda i: (0, 0))
    return pl.pallas_call(
        functools.partial(_combine_kernel, t=t),
        grid=(nt,),
        in_specs=[pl.BlockSpec((1, 1, 2 * t), lambda i: (i, 0, 0), memory_space=pltpu.SMEM),
                  pl.BlockSpec(memory_space=pl.ANY),
                  tok(D_MODEL), tok(LANE), vec, vec, vec],
        out_specs=tok(D_MODEL),
        out_shape=jax.ShapeDtypeStruct((n, D_MODEL), F32),
        scratch_shapes=[pltpu.VMEM((2 * t, D_MODEL), F32), pltpu.SemaphoreType.DMA(())],
        compiler_params=_params("arbitrary"),
        name="combine",
    )(pos_t, y, h, sel, gate2, lg, lb)


def _moe(h, shift, scale, gate2, lg, lb, wr, br, wg, wu, wd):
    n = h.shape[0]
    u, sel = _router(h, shift, scale, wr, br)
    flat_e = sel[:, 0:MOE_TOP_K].astype(jnp.int32).reshape(-1)
    n_assign = n * MOE_TOP_K
    onehot = (flat_e[:, None] == jnp.arange(MOE_EXPERTS, dtype=jnp.int32)[None, :]).astype(jnp.int32)
    csum = jnp.cumsum(onehot, axis=0)
    counts = csum[-1]
    rank = jnp.take_along_axis(csum, flat_e[:, None], axis=1)[:, 0] - 1
    padded = (counts + MOE_ROWS - 1) // MOE_ROWS * MOE_ROWS
    pad_end = jnp.cumsum(padded)
    pad_start = pad_end - padded
    pos = pad_start[flat_e] + rank
    n_blocks = (n_assign + MOE_EXPERTS * (MOE_ROWS - 1) + MOE_ROWS - 1) // MOE_ROWS
    block_e = jnp.minimum(
        jnp.searchsorted(pad_end, jnp.arange(n_blocks, dtype=jnp.int32) * MOE_ROWS, side="right"),
        MOE_EXPERTS - 1).astype(jnp.int32)
    row_tok = jnp.zeros((n_blocks * MOE_ROWS,), jnp.int32).at[pos].set(
        jnp.arange(n_assign, dtype=jnp.int32) // MOE_TOP_K)
    y = _experts(u, block_e, row_tok, wg, wu, wd)
    return _combine(y, pos, h, sel, gate2, lg, lb)


def _mixer(h, mod, lw, ssd_init, tables, *, latent, need_mix):
    n = h.shape[0]
    p = _inproj(h, mod[0], mod[1], lw["w_in"])
    xbc = _conv3(p, PX, SSD_XBC, lw["ssd_conv_w"], lw["ssd_conv_b"], silu=True)
    ydir, finals = _ssd_scan(xbc, p, lw["ssd_dt_bias"], lw["ssd_a_log"], ssd_init)
    if not need_mix:
        return None, finals
    ya = _conformer(p, lw["conf_dw_w"], lw["conf_dw_b"], lw["conf_ln_g"], lw["conf_ln_b"],
                    dil=GRID_W if latent else 1)
    q = _conv3(p, PC, 3 * HY_D, lw["hy_short_w"], lw["hy_short_b"], silu=False)
    k2, nrm = _hyena_filters(lw["featx_lat" if latent else "featx_ctx"], lw["hy_w1"], lw["hy_b1"], lw["hy_w2"],
                             lw["hy_b2"], lw["hy_freq"], lw["hy_w3d"], lw["hy_deltas"], n,
                             BF16 if latent else F32)
    if latent:
        yc = _hyena_long(q, k2, nrm, lw["hy_bias"], tables)
    else:
        yc = _hyena_ctx(q, k2, nrm, lw["hy_bias"])
    yd = _gated_conv(p, lw["sc_conv_w"])
    h = _merge(ya, xbc, ydir, p, yc, yd, h, lw["ssd_dvec"], lw["ssd_norm_g"], mod[2], lw["ln_g0"], lw["ln_b0"],
               lw["w_branch_a"], lw["w_branch_b"], lw["w_branch_c"], lw["w_branch_d"], lw["w_out"])
    return h, finals


def _positional_features(n):
    t01 = jnp.linspace(0.0, 1.0, n, dtype=F32)[:, None]
    omega = (2.0 * math.pi / n) * jnp.arange(n, dtype=F32)[:, None]
    bands = jnp.linspace(1e-4, HY_BANDS - 1, HY_BANDS, dtype=F32)
    feat = jnp.concatenate([t01, jnp.cos(bands * omega), -jnp.sin(bands * omega)], axis=-1)
    featx = jnp.concatenate([feat, jnp.zeros((1, HY_EMB), F32), jnp.flip(feat[1:], axis=0)], axis=0)
    return _pad_lanes(featx)


def _relayout_w_in(w_in):
    seg = lambda a, b: w_in[:, :, a:b]
    ob = OFF_B
    parts = [seg(OFF_G, OFF_G + N_BRANCH * D_MODEL),
             seg(OFF_A, OFF_B),
             seg(OFF_C, OFF_D),
             seg(OFF_D, OFF_G),
             seg(ob + SSD_D, ob + SSD_D + SSD_XBC),
             seg(ob, ob + SSD_D),
             seg(ob + SSD_D + SSD_XBC, OFF_C),
             jnp.zeros(w_in.shape[:2] + (LANE - 2 * SSD_HEADS,), w_in.dtype)]
    return jnp.concatenate(parts, axis=-1).astype(BF16)


def _pad_lanes(v):
    return jnp.pad(v, ((0, 0), (0, LANE - v.shape[-1])))


def kernel(x, c, ctx, c_ctx, w_mod, b_mod, ln_g, ln_b, w_in, conf_dw_w, conf_dw_b, conf_ln_g, conf_ln_b,
           ssd_conv_w, ssd_conv_b, ssd_a_log, ssd_dt_bias, ssd_d, ssd_norm_g, hy_short_w, hy_short_b,
           hy_w1, hy_b1, hy_w2, hy_b2, hy_freq, hy_w3, hy_bias, sc_conv_w, w_branch_a, w_branch_b,
           w_branch_c, w_branch_d, w_out, rt_group_w, rt_group_b, rt_expert_w, rt_expert_b,
           ex_w_gate, ex_w_up, ex_w_down):
    assert x.shape[0] == 1 and ctx.shape[0] == 1
    n_lat, n_ctx = x.shape[1], ctx.shape[1]
    depth = w_in.shape[0]

    cv = jnp.concatenate([c, c_ctx[None, :], jnp.zeros((SUBLANE - 2, D_MODEL), F32)], axis=0)
    mods = _mod_vectors(cv, w_mod, b_mod)
    w_in_p = _relayout_w_in(w_in)
    tables = _dft_tables(n_lat)
    featx_lat = _positional_features(n_lat)
    featx_ctx = _positional_features(n_ctx)
    deltas = jnp.abs(jnp.linspace(HY_MIN_DECAY, HY_MAX_DECAY, HY_N_FILT, dtype=F32))
    deltas_d = deltas.reshape(HY_ORDER, 2, HY_D).transpose(1, 0, 2).reshape(2, 1, HY_ORDER * HY_D)
    router_w = jnp.concatenate([rt_group_w, rt_expert_w,
                                jnp.zeros((depth, D_MODEL, LANE - MOE_GROUPS - MOE_EXPERTS), F32)], axis=-1)
    router_b = jnp.concatenate([rt_group_b, rt_expert_b,
                                jnp.zeros((depth, LANE - MOE_GROUPS - MOE_EXPERTS), F32)], axis=-1)
    ssd_zero = jnp.zeros((2, SSD_HEADS, SSD_STATE, SSD_HEAD_DIM), F32)

    h_lat, h_ctx = x[0], ctx[0]
    for l in range(depth):
        row = lambda v: v[None, :]
        lw = dict(
            w_in=w_in_p[l], conf_dw_w=conf_dw_w[l], conf_dw_b=row(conf_dw_b[l]), conf_ln_g=row(conf_ln_g[l]),
            conf_ln_b=row(conf_ln_b[l]), ssd_conv_w=ssd_conv_w[l], ssd_conv_b=row(ssd_conv_b[l]),
            ssd_a_log=_pad_lanes(ssd_a_log[l].reshape(1, -1)), ssd_dt_bias=_pad_lanes(ssd_dt_bias[l].reshape(1, -1)),
            ssd_dvec=row(jnp.repeat(ssd_d[l], SSD_HEAD_DIM)), ssd_norm_g=row(ssd_norm_g[l]),
            hy_short_w=hy_short_w[l], hy_short_b=row(hy_short_b[l]),
            hy_w1=jnp.pad(hy_w1[l], ((0, LANE - HY_EMB), (0, LANE - HY_HID))), hy_b1=_pad_lanes(row(hy_b1[l])),
            hy_w2=jnp.pad(hy_w2[l], ((0, LANE - HY_HID), (0, LANE - HY_HID))), hy_b2=_pad_lanes(row(hy_b2[l])),
            hy_freq=_pad_lanes(hy_freq[l]),
            hy_w3d=jnp.pad(hy_w3[l].reshape(HY_HID, HY_ORDER, 2, HY_D).transpose(2, 0, 1, 3).reshape(
                2, HY_HID, HY_ORDER * HY_D), ((0, 0), (0, LANE - HY_HID), (0, 0))),
            hy_deltas=deltas_d, hy_bias=hy_bias[l], featx_lat=featx_lat, featx_ctx=featx_ctx,
            sc_conv_w=sc_conv_w[l], ln_g0=row(ln_g[l, 0]), ln_b0=row(ln_b[l, 0]),
            w_branch_a=w_branch_a[l].astype(BF16), w_branch_b=w_branch_b[l].astype(BF16),
            w_branch_c=w_branch_c[l].astype(BF16), w_branch_d=w_branch_d[l].astype(BF16),
            w_out=w_out[l].astype(BF16))
        moe_w = (router_w[l], row(router_b[l]), ex_w_gate[l].astype(BF16), ex_w_up[l].astype(BF16),
                 ex_w_down[l].astype(BF16))
        last = l == depth - 1
        d = D_MODEL
        mod_lat = [mods[l, 0:1, k * d:(k + 1) * d] for k in range(6)]
        mod_ctx = [mods[l, 1:2, k * d:(k + 1) * d] for k in range(6)]

        mix_ctx, ctx_states = _mixer(h_ctx, mod_ctx, lw, ssd_zero, None, latent=False, need_mix=not last)
        h_lat, _ = _mixer(h_lat, mod_lat, lw, ctx_states, tables, latent=True, need_mix=True)
        h_lat = _moe(h_lat, mod_lat[3], mod_lat[4], mod_lat[5], row(ln_g[l, 1]), row(ln_b[l, 1]), *moe_w)
        if not last:
            h_ctx = _moe(mix_ctx, mod_ctx[3], mod_ctx[4], mod_ctx[5], row(ln_g[l, 1]), row(ln_b[l, 1]), *moe_w)
    return h_lat[None]
```

```python
import functools
import math

import jax
import jax.numpy as jnp
from jax import lax
from jax.experimental import pallas as pl
from jax.experimental.pallas import tpu as pltpu

F32 = jnp.float32
BF16 = jnp.bfloat16
HIGHEST = lax.Precision.HIGHEST

D_MODEL = 1024
DEPTH = 4
GRID_W = 64
CONF_D = 512
CONF_K = 31
SSD_D = 768
SSD_HEADS = 12
SSD_HEAD_DIM = 64
SSD_GROUPS = 4
SSD_HPG = SSD_HEADS // SSD_GROUPS
SSD_STATE = 128
SSD_CHUNK = 128
SSD_BC = SSD_GROUPS * SSD_STATE
SSD_XBC = SSD_D + 2 * SSD_BC
SSD_PROJ = SSD_D + SSD_XBC + 2 * SSD_HEADS
HY_D = 512
HY_ORDER = 2
HY_EMB = 33
HY_BANDS = (HY_EMB - 1) // 2
HY_HID = 64
HY_N_FILT = HY_ORDER * 2 * HY_D
HY_MIN_DECAY = math.log(1e-2) / 1.5
HY_MAX_DECAY = math.log(1e-2) / 0.3
SC_D = 512
N_BRANCH = 4
OFF_A = 0
OFF_B = OFF_A + 2 * CONF_D
OFF_C = OFF_B + SSD_PROJ
OFF_D = OFF_C + 3 * HY_D
OFF_G = OFF_D + 3 * SC_D
MOE_GROUPS = 4
MOE_EPG = 8
MOE_EXPERTS = MOE_GROUPS * MOE_EPG
MOE_TOP_K = 2
MOE_FF = 512
DN_ALPHA = (2 * DEPTH) ** 0.25
LN_EPS = 1e-5

PG = 0
PA = PG + N_BRANCH * D_MODEL
PC = PA + 2 * CONF_D
PD = PC + 3 * HY_D
PX = PD + 3 * SC_D
PZ = PX + SSD_XBC
PDT = PZ + SSD_D
INPROJ_TN = 1024
NP = -(-(PDT + 128) // INPROJ_TN) * INPROJ_TN

LANE = 128
SUBLANE = 8
FFT_N2 = 256
MOE_ROWS = 128
VMEM_LIMIT = 48 * 1024 * 1024


def _params(*sem):
    return pltpu.CompilerParams(dimension_semantics=sem, vmem_limit_bytes=VMEM_LIMIT)


def _silu(x):
    return x * jax.nn.sigmoid(x)


def _layer_norm(x, g, b):
    mu = jnp.mean(x, -1, keepdims=True)
    xc = x - mu
    var = jnp.mean(xc * xc, -1, keepdims=True)
    return xc * lax.rsqrt(var + LN_EPS) * g + b


def _dot(a, b):
    return jnp.dot(a, b, preferred_element_type=F32)


def _mod_kernel(cv_ref, w_ref, b_ref, o_ref):
    o_ref[...] = jnp.dot(_silu(cv_ref[...]), w_ref[...], precision=HIGHEST,
                         preferred_element_type=F32) + b_ref[...]


def _mod_vectors(cv, w_mod, b_mod):
    tn = 1536
    return pl.pallas_call(
        _mod_kernel,
        grid=(DEPTH, 6 * D_MODEL // tn),
        in_specs=[pl.BlockSpec((SUBLANE, D_MODEL), lambda l, j: (0, 0)),
                  pl.BlockSpec((None, D_MODEL, tn), lambda l, j: (l, 0, j)),
                  pl.BlockSpec((None, 1, tn), lambda l, j: (l, 0, j))],
        out_specs=pl.BlockSpec((None, SUBLANE, tn), lambda l, j: (l, 0, j)),
        out_shape=jax.ShapeDtypeStruct((DEPTH, SUBLANE, 6 * D_MODEL), F32),
        compiler_params=_params("parallel", "parallel"),
        name="mod_vectors",
    )(cv, w_mod, b_mod.reshape(DEPTH, 1, 6 * D_MODEL))


def _inproj_kernel(x_ref, sh_ref, sc_ref, w_ref, o_ref, xb_ref):
    @pl.when(pl.program_id(1) == 0)
    def _():
        xb_ref[...] = (x_ref[...] * (1.0 + sc_ref[...]) + sh_ref[...]).astype(BF16)

    o_ref[...] = _dot(xb_ref[...], w_ref[...])


def _inproj(h, shift, scale, w):
    n = h.shape[0]
    tm = min(n, 1024)
    tn = INPROJ_TN
    return pl.pallas_call(
        _inproj_kernel,
        grid=(n // tm, NP // tn),
        in_specs=[pl.BlockSpec((tm, D_MODEL), lambda i, j: (i, 0)),
                  pl.BlockSpec((1, D_MODEL), lambda i, j: (0, 0)),
                  pl.BlockSpec((1, D_MODEL), lambda i, j: (0, 0)),
                  pl.BlockSpec((D_MODEL, tn), lambda i, j: (0, j))],
        out_specs=pl.BlockSpec((tm, tn), lambda i, j: (i, j)),
        out_shape=jax.ShapeDtypeStruct((n, NP), F32),
        scratch_shapes=[pltpu.VMEM((tm, D_MODEL), BF16)],
        compiler_params=_params("parallel", "arbitrary"),
        name="inproj",
    )(h, shift, scale, w)


def _shifted(x, prev_row, next_row):
    t = x.shape[0]
    row = lax.broadcasted_iota(jnp.int32, x.shape, 0)
    xm = jnp.where(row == 0, prev_row, pltpu.roll(x, 1, 0))
    xp = jnp.where(row == t - 1, next_row, pltpu.roll(x, t - 1, 0))
    return xm, xp


def _conv3_kernel(cur_ref, prev_ref, next_ref, w_ref, b_ref, o_ref, *, silu, nt):
    i = pl.program_id(0)
    x = cur_ref[...]
    pv = jnp.where(i > 0, prev_ref[SUBLANE - 1:SUBLANE, :], 0.0)
    nx = jnp.where(i < nt - 1, next_ref[0:1, :], 0.0)
    xm, xp = _shifted(x, pv, nx)
    y = w_ref[0:1, :] * xm + w_ref[1:2, :] * x + w_ref[2:3, :] * xp + b_ref[...]
    o_ref[...] = _silu(y) if silu else y


def _halo_specs(t, ct, n, col0):
    r8 = t // SUBLANE
    last8 = n // SUBLANE - 1
    return [pl.BlockSpec((t, ct), lambda i, j: (i, col0 + j)),
            pl.BlockSpec((SUBLANE, ct), lambda i, j: (jnp.maximum(i * r8 - 1, 0), col0 + j)),
            pl.BlockSpec((SUBLANE, ct), lambda i, j: (jnp.minimum((i + 1) * r8, last8), col0 + j))]


def _conv3(p, col, width, w, b, *, silu):
    n = p.shape[0]
    t = min(n, 1024)
    ct = 256
    nt = n // t
    return pl.pallas_call(
        functools.partial(_conv3_kernel, silu=silu, nt=nt),
        grid=(nt, width // ct),
        in_specs=_halo_specs(t, ct, n, col // ct) + [
            pl.BlockSpec((3, ct), lambda i, j: (0, j)),
            pl.BlockSpec((1, ct), lambda i, j: (0, j))],
        out_specs=pl.BlockSpec((t, ct), lambda i, j: (i, j)),
        out_shape=jax.ShapeDtypeStruct((n, width), F32),
        compiler_params=_params("parallel", "parallel"),
        name="conv3",
    )(p, p, p, w, b)


def _gconv_kernel(bg_ref, cc_ref, cp_ref, cn_ref, xc_ref, xp_ref, xn_ref, w_ref, o_ref, *, nt):
    i = pl.program_id(0)
    x = cc_ref[...] * xc_ref[...]
    pv = jnp.where(i > 0, cp_ref[SUBLANE - 1:SUBLANE, :] * xp_ref[SUBLANE - 1:SUBLANE, :], 0.0)
    nx = jnp.where(i < nt - 1, cn_ref[0:1, :] * xn_ref[0:1, :], 0.0)
    xm, xp = _shifted(x, pv, nx)
    o_ref[...] = bg_ref[...] * (w_ref[0:1, :] * xm + w_ref[1:2, :] * x + w_ref[2:3, :] * xp)


def _gated_conv(p, w):
    n = p.shape[0]
    t = min(n, 1024)
    ct = 256
    nt = n // t
    nb = SC_D // ct
    return pl.pallas_call(
        functools.partial(_gconv_kernel, nt=nt),
        grid=(nt, nb),
        in_specs=([pl.BlockSpec((t, ct), lambda i, j: (i, PD // ct + j))]
                  + _halo_specs(t, ct, n, PD // ct + nb)
                  + _halo_specs(t, ct, n, PD // ct + 2 * nb)
                  + [pl.BlockSpec((3, ct), lambda i, j: (0, j))]),
        out_specs=pl.BlockSpec((t, ct), lambda i, j: (i, j)),
        out_shape=jax.ShapeDtypeStruct((n, SC_D), F32),
        compiler_params=_params("parallel", "parallel"),
        name="gated_conv",
    )(p, p, p, p, p, p, p, w)


CONF_RB = 64


def _conf_kernel(vc, gc, vp, gp, vn, gn, w_ref, b_ref, lg_ref, lb_ref, o_ref, buf, *, t, halo, dil, nt):
    i = pl.program_id(0)
    buf[halo:halo + t, :] = vc[...] * jax.nn.sigmoid(gc[...])
    buf[0:halo, :] = jnp.where(i > 0, vp[t - halo:t, :] * jax.nn.sigmoid(gp[t - halo:t, :]), 0.0)
    buf[halo + t:halo + t + halo, :] = jnp.where(
        i < nt - 1, vn[0:halo, :] * jax.nn.sigmoid(gn[0:halo, :]), 0.0)

    def block(r0):
        acc = jnp.zeros((CONF_RB, CONF_D), F32)
        for j in range(CONF_K):
            off = halo + (j - CONF_K // 2) * dil
            acc = acc + w_ref[j:j + 1, :] * buf[pl.ds(r0 + off, CONF_RB), :]
        v = _layer_norm(acc + b_ref[...], lg_ref[...], lb_ref[...])
        o_ref[pl.ds(r0, CONF_RB), :] = _silu(v)

    if dil % CONF_RB == 0:
        def body(rb, carry):
            block(pl.multiple_of(rb * CONF_RB, CONF_RB))
            return carry
        lax.fori_loop(0, t // CONF_RB, body, 0)
    else:
        for rb in range(t // CONF_RB):
            block(rb * CONF_RB)


def _conformer(p, w, b, lg, lb, *, dil):
    n = p.shape[0]
    t = min(n, 1024)
    nt = n // t
    halo = -(-(CONF_K // 2) * dil // SUBLANE) * SUBLANE
    assert halo <= t
    cb = PA // CONF_D

    def spec(col, shift):
        return pl.BlockSpec((t, CONF_D), lambda i: (jnp.clip(i + shift, 0, nt - 1), col))

    vec = pl.BlockSpec((1, CONF_D), lambda i: (0, 0))
    return pl.pallas_call(
        functools.partial(_conf_kernel, t=t, halo=halo, dil=dil, nt=nt),
        grid=(nt,),
        in_specs=[spec(cb, 0), spec(cb + 1, 0), spec(cb, -1), spec(cb + 1, -1), spec(cb, 1), spec(cb + 1, 1),
                  pl.BlockSpec((CONF_K, CONF_D), lambda i: (0, 0)), vec, vec, vec],
        out_specs=pl.BlockSpec((t, CONF_D), lambda i: (i, 0)),
        out_shape=jax.ShapeDtypeStruct((n, CONF_D), F32),
        scratch_shapes=[pltpu.VMEM((t + 2 * halo, CONF_D), F32)],
        compiler_params=_params("parallel"),
        name="conformer",
    )(p, p, p, p, p, p, w, b, lg, lb)


def _ssd_kernel(xbc_ref, dt_ref, dtb_ref, alog_ref, init_ref, y_ref, fin_ref, h_ref, *, nc):
    d = pl.program_id(0)
    c = pl.program_id(1)
    q = SSD_CHUNK
    hd = SSD_HEAD_DIM

    @pl.when(c == 0)
    def _():
        h_ref[...] = init_ref[...]

    raw = dt_ref[...] + dtb_ref[...]
    dt_all = jnp.maximum(raw, 0.0) + jnp.log(1.0 + jnp.exp(-jnp.abs(raw)))
    ld_all = dt_all * (-jnp.exp(alog_ref[...]))
    lane = lax.broadcasted_iota(jnp.int32, (q, LANE), 1)
    head = lane < SSD_HEADS
    dt_d = jnp.where(head, jnp.where(d == 0, dt_all, pltpu.roll(dt_all, LANE - SSD_HEADS, 1)), 0.0)
    ld_d = jnp.where(head, jnp.where(d == 0, ld_all, pltpu.roll(ld_all, LANE - SSD_HEADS, 1)), 0.0)

    li = lax.broadcasted_iota(jnp.int32, (q, q), 0)
    si = lax.broadcasted_iota(jnp.int32, (q, q), 1)
    mask = (li - si) * (1 - 2 * d) >= 0
    cum = jnp.dot(mask.astype(F32), ld_d, precision=HIGHEST, preferred_element_type=F32)
    tot = jnp.sum(ld_d, axis=0, keepdims=True)
    cum_t = cum.T
    dt_t = dt_d.T
    w_t = (jnp.exp(tot - cum) * dt_d).T
    a_out = jnp.exp(cum)
    e_tot = jnp.exp(tot)

    for g in range(SSD_GROUPS):
        bg = xbc_ref[:, SSD_D + g * SSD_STATE:SSD_D + (g + 1) * SSD_STATE]
        cg = xbc_ref[:, SSD_D + SSD_BC + g * SSD_STATE:SSD_D + SSD_BC + (g + 1) * SSD_STATE]
        bg_t = bg.T
        cb = _dot(cg.astype(BF16), bg_t.astype(BF16))
        for e in range(SSD_HPG):
            hh = g * SSD_HPG + e
            diff = cum[:, hh:hh + 1] - cum_t[hh:hh + 1, :]
            dec = jnp.exp(jnp.where(mask, diff, -1e30))
            m = (cb * dec * dt_t[hh:hh + 1, :]).astype(BF16)
            xe = xbc_ref[:, hh * hd:(hh + 1) * hd].astype(BF16)
            cs = (cg * a_out[:, hh:hh + 1]).astype(BF16)
            h_in = h_ref[hh]
            y_ref[:, hh * hd:(hh + 1) * hd] = _dot(m, xe) + _dot(cs, h_in.astype(BF16))
            s_new = _dot((bg_t * w_t[hh:hh + 1, :]).astype(BF16), xe)
            h_ref[hh] = e_tot[:, hh:hh + 1] * h_in + s_new

    @pl.when(c == nc - 1)
    def _():
        fin_ref[...] = h_ref[...]


def _ssd_scan(xbc, p, dt_bias, a_log, init):
    n = xbc.shape[0]
    q = SSD_CHUNK
    nc = n // q

    def chunk(d, c):
        return jnp.where(d == 0, c, nc - 1 - c)

    st = (SSD_HEADS, SSD_STATE, SSD_HEAD_DIM)
    vec = pl.BlockSpec((1, LANE), lambda d, c: (0, 0))
    return pl.pallas_call(
        functools.partial(_ssd_kernel, nc=nc),
        grid=(2, nc),
        in_specs=[pl.BlockSpec((q, SSD_XBC), lambda d, c: (chunk(d, c), 0)),
                  pl.BlockSpec((q, LANE), lambda d, c: (chunk(d, c), PDT // LANE)),
                  vec, vec,
                  pl.BlockSpec((None,) + st, lambda d, c: (d, 0, 0, 0))],
        out_specs=[pl.BlockSpec((None, q, SSD_D), lambda d, c: (d, chunk(d, c), 0)),
                   pl.BlockSpec((None,) + st, lambda d, c: (d, 0, 0, 0))],
        out_shape=[jax.ShapeDtypeStruct((2, n, SSD_D), F32),
                   jax.ShapeDtypeStruct((2,) + st, F32)],
        scratch_shapes=[pltpu.VMEM(st, F32)],
        compiler_params=_params("arbitrary", "arbitrary"),
        name="ssd_scan",
    )(xbc, p, dt_bias, a_log, init)


def _filt_kernel(feat_ref, w1_ref, b1_ref, w2_ref, b2_ref, fr_ref, w3h_ref, w3l_ref, dl_ref, k_ref, nrm_ref, *,
                 n, t):
    i = pl.program_id(0)
    hf = t // 2
    feat = feat_ref[...]
    x = jnp.concatenate([feat[0:hf], feat[hf:t]], axis=1)
    hid = jnp.sin(fr_ref[0:1, :] * (jnp.dot(x, w1_ref[...], precision=HIGHEST,
                                            preferred_element_type=F32) + b1_ref[...]))
    hid = jnp.sin(fr_ref[1:2, :] * (jnp.dot(hid, w2_ref[...], precision=HIGHEST,
                                            preferred_element_type=F32) + b2_ref[...]))
    hi = hid.astype(BF16)
    lo = (hid - hi.astype(F32)).astype(BF16)

    @pl.when(i == 0)
    def _():
        nrm_ref[...] = jnp.zeros_like(nrm_ref)

    for half in range(2):
        wh, wl = w3h_ref[half], w3l_ref[half]
        filt = _dot(hi, wh) + _dot(lo, wh) + _dot(hi, wl)
        filt = filt * jnp.exp(-feat[half * hf:(half + 1) * hf, 0:1] * dl_ref[...])
        row = i * t + half * hf + lax.broadcasted_iota(jnp.int32, filt.shape, 0)
        filt = jnp.where(row == n, 0.0, filt)
        k_ref[half * hf:(half + 1) * hf, :] = filt.astype(k_ref.dtype)
        nrm_ref[...] += jnp.sum(jnp.abs(filt), axis=0, keepdims=True)


def _hyena_filters(featx, fw, n, dtype):
    t = min(n, 512)
    half = n // t
    oc = HY_ORDER * HY_D
    full = lambda shape: pl.BlockSpec(shape, lambda i: tuple(0 for _ in shape))
    w3spec = pl.BlockSpec((None, 2, LANE, oc), lambda i: (i // half, 0, 0, 0))
    return pl.pallas_call(
        functools.partial(_filt_kernel, n=n, t=t),
        grid=(2 * n // t,),
        in_specs=[pl.BlockSpec((t, LANE), lambda i: (i, 0)),
                  full((2 * LANE, LANE)), full((1, LANE)), full((LANE, LANE)), full((1, LANE)),
                  full((2, LANE)), w3spec, w3spec,
                  pl.BlockSpec((None, 1, oc), lambda i: (i // half, 0, 0))],
        out_specs=[pl.BlockSpec((t, oc), lambda i: (i, 0)),
                   pl.BlockSpec((1, oc), lambda i: (0, 0))],
        out_shape=[jax.ShapeDtypeStruct((2 * n, oc), dtype),
                   jax.ShapeDtypeStruct((1, oc), F32)],
        compiler_params=_params("arbitrary"),
        name="hyena_filters",
    )(featx, fw["w1"], fw["b1"], fw["w2"], fw["b2"], fw["freq"], fw["w3h"], fw["w3l"], fw["deltas"])


def _filter_weights(w1, b1, w2, b2, freq, w3, deltas_d):
    hh = HY_HID
    z = lambda r, c: jnp.zeros((r, c), F32)
    w1p = jnp.pad(w1, ((0, LANE - HY_EMB), (0, 0)))
    w1b = jnp.concatenate([jnp.concatenate([w1p, z(LANE, hh)], 1),
                           jnp.concatenate([z(LANE, hh), w1p], 1)], 0)
    w2b = jnp.concatenate([jnp.concatenate([w2, z(hh, hh)], 1),
                           jnp.concatenate([z(hh, hh), w2], 1)], 0)
    two = lambda v: jnp.concatenate([v, v], axis=-1)
    w3d = w3.reshape(hh, HY_ORDER, 2, HY_D).transpose(2, 0, 1, 3).reshape(2, hh, HY_ORDER * HY_D)
    zz = jnp.zeros_like(w3d)
    w3x = jnp.stack([jnp.concatenate([w3d, zz], 1), jnp.concatenate([zz, w3d], 1)], axis=1)
    w3h = w3x.astype(BF16)
    w3l = (w3x - w3h.astype(F32)).astype(BF16)
    return dict(w1=w1b, b1=two(b1[None, :]), w2=w2b, b2=two(b2[None, :]), freq=two(freq), w3h=w3h, w3l=w3l,
                deltas=deltas_d)


def _dft_rows_kernel(f_ref, x_ref, o_ref):
    o_ref[...] = _dot(f_ref[...], x_ref[...].astype(BF16)).astype(o_ref.dtype)


def _dft_rows(fmat, x2d):
    m, k = fmat.shape
    ncols = x2d.shape[1]
    tn = min(ncols, 8192)
    return pl.pallas_call(
        _dft_rows_kernel,
        grid=(ncols // tn,),
        in_specs=[pl.BlockSpec((m, k), lambda j: (0, 0)),
                  pl.BlockSpec((k, tn), lambda j: (0, j))],
        out_specs=pl.BlockSpec((m, tn), lambda j: (0, j)),
        out_shape=jax.ShapeDtypeStruct((m, ncols), BF16),
        compiler_params=_params("parallel"),
        name="dft_rows",
    )(fmat, x2d)


def _spec_kernel(ar_ref, ai_ref, gr_ref, gi_ref, kr_ref, ki_ref):
    ar, ai, gr, gi = ar_ref[...], ai_ref[...], gr_ref[...], gi_ref[...]
    kr_ref[...] = _dot(gr, ar) - _dot(gi, ai)
    ki_ref[...] = _dot(gr, ai) + _dot(gi, ar)


def _filter_spectrum(a4, gr, gi):
    _, _, n2, ch = a4.shape
    nh = gr.shape[0]
    ct = 512
    blk = lambda ri: pl.BlockSpec((None, None, n2, ct), lambda f, j: (ri, f, 0, j))
    gspec = pl.BlockSpec((None, n2, n2), lambda f, j: (f, 0, 0))
    ospec = pl.BlockSpec((None, n2, ct), lambda f, j: (f, 0, j))
    return pl.pallas_call(
        _spec_kernel,
        grid=(nh, ch // ct),
        in_specs=[blk(0), blk(1), gspec, gspec],
        out_specs=[ospec, ospec],
        out_shape=[jax.ShapeDtypeStruct((nh, n2, ch), F32)] * 2,
        compiler_params=_params("parallel", "parallel"),
        name="filter_spectrum",
    )(a4, a4, gr, gi)


def _mid_kernel(ar_ref, ai_ref, gr_ref, gi_ref, grt_ref, git_ref, kr_ref, ki_ref, br_ref, bi_ref, *, nh):
    f = pl.program_id(0)

    @pl.when(f < nh)
    def _():
        ar, ai, gr, gi = ar_ref[...], ai_ref[...], gr_ref[...], gi_ref[...]
        xr = _dot(gr, ar) - _dot(gi, ai)
        xi = _dot(gr, ai) + _dot(gi, ar)
        kr, ki = kr_ref[...], ki_ref[...]
        yr = (xr * kr - xi * ki).astype(BF16)
        yi = (xr * ki + xi * kr).astype(BF16)
        grt, git = grt_ref[...], git_ref[...]
        br_ref[...] = (_dot(grt, yr) + _dot(git, yi)).astype(BF16)
        bi_ref[...] = (_dot(grt, yi) - _dot(git, yr)).astype(BF16)

    @pl.when(f >= nh)
    def _():
        br_ref[...] = jnp.zeros_like(br_ref)
        bi_ref[...] = jnp.zeros_like(bi_ref)


def _hyena_mid(a4, tabs, kf_r, kf_i, order):
    _, nf, n2, ch = a4.shape
    gr, gi, grt, git = tabs
    nh = gr.shape[0]
    fi = lambda f: jnp.minimum(f, nh - 1)
    blk = lambda ri: pl.BlockSpec((None, None, n2, ch), lambda f: (ri, fi(f), 0, 0))
    kspec = pl.BlockSpec((None, n2, ch), lambda f: (fi(f), 0, order))
    gspec = pl.BlockSpec((None, n2, n2), lambda f: (fi(f), 0, 0))
    ospec = pl.BlockSpec((None, n2, ch), lambda f: (f, 0, 0))
    return pl.pallas_call(
        functools.partial(_mid_kernel, nh=nh),
        grid=(nf,),
        in_specs=[blk(0), blk(1), gspec, gspec, gspec, gspec, kspec, kspec],
        out_specs=[ospec, ospec],
        out_shape=[jax.ShapeDtypeStruct((nf, n2, ch), BF16)] * 2,
        compiler_params=_params("parallel"),
        name="hyena_mid",
    )(a4, a4, gr, gi, grt, git, kf_r, kf_i)


def _inv_kernel(f_ref, br_ref, bi_ref, s_ref, bias_ref, z_ref, g_ref, o_ref, *, nf):
    acc = _dot(f_ref[:, 0:nf], br_ref[...]) + _dot(f_ref[:, nf:2 * nf], bi_ref[...])
    o_ref[...] = g_ref[...] * (acc * s_ref[...] + bias_ref[...] * z_ref[...])


def _hyena_inverse(finv, b_r, b_i, s_row, bias_row, z2d, g2d):
    t1, k2 = finv.shape
    nf = k2 // 2
    ncols = z2d.shape[1]
    tn = s_row.shape[1]
    col = pl.BlockSpec((nf, tn), lambda j: (0, j))
    row = pl.BlockSpec((1, tn), lambda j: (0, 0))
    io = pl.BlockSpec((t1, tn), lambda j: (0, j))
    return pl.pallas_call(
        functools.partial(_inv_kernel, nf=nf),
        grid=(ncols // tn,),
        in_specs=[pl.BlockSpec((t1, k2), lambda j: (0, 0)), col, col, row, row, io, io],
        out_specs=io,
        out_shape=jax.ShapeDtypeStruct((t1, ncols), F32),
        compiler_params=_params("parallel"),
        name="hyena_inverse",
    )(finv, b_r, b_i, s_row, bias_row, z2d, g2d)


def _hyena_nf(n):
    nh = (2 * n // FFT_N2) // 2 + 1
    return -(-nh // 16) * 16


def _dft_tables(n):
    n2 = FFT_N2
    n1 = 2 * n // n2
    tot = 2 * n
    two_pi = 2.0 * math.pi

    def cs(num, den):
        ang = (two_pi / den) * (num % den).astype(F32)
        return jnp.cos(ang), jnp.sin(ang)

    nh = n1 // 2 + 1
    nf = _hyena_nf(n)
    f1 = jnp.arange(nh, dtype=jnp.int32)
    t1 = jnp.arange(n1, dtype=jnp.int32)
    c1, s1 = cs(f1[:, None] * t1[None, :], n1)
    zrow = jnp.zeros((nf - nh, n1), F32)
    fwd_full = jnp.concatenate([c1, zrow, -s1, zrow], axis=0).astype(BF16)
    fwd_half = fwd_full[:, :n1 // 2]
    wgt = jnp.where((f1 == 0) | (f1 == n1 // 2), 1.0, 2.0)[:, None]
    zcol = jnp.zeros((n1 // 2, nf - nh), F32)
    inv = jnp.concatenate([(wgt * c1[:, :n1 // 2]).T, zcol, -(wgt * s1[:, :n1 // 2]).T, zcol],
                          axis=1).astype(BF16)
    t2 = jnp.arange(n2, dtype=jnp.int32)
    twr, twi = cs(f1[:, None] * t2[None, :], tot)
    fr, fi = cs(t2[:, None] * t2[None, :], n2)
    twi, fi = -twi, -fi
    gr = twr[:, None, :] * fr[None] - twi[:, None, :] * fi[None]
    gi = twr[:, None, :] * fi[None] + twi[:, None, :] * fr[None]
    tabs = (gr.astype(BF16), gi.astype(BF16),
            gr.transpose(0, 2, 1).astype(BF16), gi.transpose(0, 2, 1).astype(BF16))
    return fwd_full, fwd_half, inv, tabs


def _hyena_long(q, k2, nrm, hy_bias, tables):
    n = q.shape[0]
    n2 = FFT_N2
    n1 = 2 * n // n2
    fwd_full, fwd_half, inv, tabs = tables
    nf = _hyena_nf(n)
    oc = HY_ORDER * HY_D
    ak = _dft_rows(fwd_full, k2.reshape(n1, n2 * oc)).reshape(2, nf, n2, oc)
    kf_r, kf_i = _filter_spectrum(ak, tabs[0], tabs[1])
    tn = min(n2 * HY_D, 8192)
    rep = tn // HY_D
    z = q[:, :HY_D]
    for o in range(HY_ORDER):
        gate = q[:, (o + 1) * HY_D:(o + 2) * HY_D]
        z2d = z.reshape(n1 // 2, n2 * HY_D)
        a4 = _dft_rows(fwd_half, z2d).reshape(2, nf, n2, HY_D)
        b_r, b_i = _hyena_mid(a4, tabs, kf_r, kf_i, o)
        s_row = jnp.tile(1.0 / (2.0 * n * nrm[:, o * HY_D:(o + 1) * HY_D]), (1, rep))
        bias_row = jnp.tile(hy_bias[o][None, :], (1, rep))
        z = _hyena_inverse(inv, b_r.reshape(nf, n2 * HY_D), b_i.reshape(nf, n2 * HY_D), s_row, bias_row,
                           z2d, gate.reshape(n1 // 2, n2 * HY_D)).reshape(n, HY_D)
    return z


def _hy_ctx_kernel(v_ref, x1_ref, x2_ref, k0_ref, k1_ref, n0_ref, n1_ref, bias_ref, o_ref, kf, zs, *, n):
    zs[...] = v_ref[...]
    for o, (k_ref, nr_ref, x_ref) in enumerate(((k0_ref, n0_ref, x1_ref), (k1_ref, n1_ref, x2_ref))):
        kf[0:n, :] = k_ref[n:2 * n, :]
        kf[n:2 * n, :] = k_ref[0:n, :]

        def body(s, acc):
            return acc + kf[pl.ds(n - s, n), :] * zs[pl.ds(s, 1), :]

        acc = lax.fori_loop(0, n, body, jnp.zeros((n, LANE), F32))
        z = zs[...]
        zs[...] = x_ref[...] * (acc / nr_ref[...] + bias_ref[o:o + 1, :] * z)
    o_ref[...] = zs[...]


def _hyena_ctx(q, k2, nrm, hy_bias):
    n = q.shape[0]
    nb = HY_D // LANE
    col = lambda c0: pl.BlockSpec((n, LANE), lambda j: (0, c0 + j))
    kcol = lambda c0: pl.BlockSpec((2 * n, LANE), lambda j: (0, c0 + j))
    ncol = lambda c0: pl.BlockSpec((1, LANE), lambda j: (0, c0 + j))
    return pl.pallas_call(
        functools.partial(_hy_ctx_kernel, n=n),
        grid=(nb,),
        in_specs=[col(0), col(nb), col(2 * nb), kcol(0), kcol(nb), ncol(0), ncol(nb),
                  pl.BlockSpec((HY_ORDER, LANE), lambda j: (0, j))],
        out_specs=pl.BlockSpec((n, LANE), lambda j: (0, j)),
        out_shape=jax.ShapeDtypeStruct((n, HY_D), F32),
        scratch_shapes=[pltpu.VMEM((2 * n, LANE), F32), pltpu.VMEM((n, LANE), F32)],
        compiler_params=_params("parallel"),
        name="hyena_ctx",
    )(q, q, q, k2, k2, nrm, nrm, hy_bias)


def _merge_kernel(ya_ref, xs_ref, yf_ref, yb_ref, z_ref, yc_ref, yd_ref, g_ref, h_ref,
                  dv_ref, ng_ref, g1_ref, lg_ref, lb_ref,
                  wa_ref, wb_ref, wc_ref, wd_ref, wo_ref, o_ref):
    y = xs_ref[...] * dv_ref[...] + yf_ref[...] + yb_ref[...]
    gz = y * _silu(z_ref[...])
    ssd = gz * lax.rsqrt(jnp.mean(gz * gz, -1, keepdims=True) + LN_EPS) * ng_ref[...]
    d = D_MODEL
    m = jax.nn.sigmoid(g_ref[:, 0:d]) * _dot(ya_ref[...].astype(BF16), wa_ref[...])
    m = m + jax.nn.sigmoid(g_ref[:, d:2 * d]) * _dot(ssd.astype(BF16), wb_ref[...])
    m = m + jax.nn.sigmoid(g_ref[:, 2 * d:3 * d]) * _dot(yc_ref[...].astype(BF16), wc_ref[...])
    m = m + jax.nn.sigmoid(g_ref[:, 3 * d:4 * d]) * _dot(yd_ref[...].astype(BF16), wd_ref[...])
    mix = _dot(m.astype(BF16), wo_ref[...])
    o_ref[...] = _layer_norm(DN_ALPHA * h_ref[...] + g1_ref[...] * mix, lg_ref[...], lb_ref[...])


def _merge(ya, xbc, ydir, p, yc, yd, h, dvec, ng, gate1, lg, lb, wa, wb, wc, wd, wo):
    n = h.shape[0]
    t = 256
    tok = lambda w, col=0: pl.BlockSpec((t, w), lambda i: (i, col))
    vec = lambda w: pl.BlockSpec((1, w), lambda i: (0, 0))
    mat = lambda r: pl.BlockSpec((r, D_MODEL), lambda i: (0, 0))
    return pl.pallas_call(
        _merge_kernel,
        grid=(n // t,),
        in_specs=[tok(CONF_D), tok(SSD_D),
                  pl.BlockSpec((None, t, SSD_D), lambda i: (0, i, 0)),
                  pl.BlockSpec((None, t, SSD_D), lambda i: (1, i, 0)),
                  tok(SSD_D, PZ // SSD_D), tok(HY_D), tok(SC_D), tok(N_BRANCH * D_MODEL, 0), tok(D_MODEL),
                  vec(SSD_D), vec(SSD_D), vec(D_MODEL), vec(D_MODEL), vec(D_MODEL),
                  mat(CONF_D), mat(SSD_D), mat(HY_D), mat(SC_D), mat(D_MODEL)],
        out_specs=tok(D_MODEL),
        out_shape=jax.ShapeDtypeStruct((n, D_MODEL), F32),
        compiler_params=_params("parallel"),
        name="merge",
    )(ya, xbc, ydir, ydir, p, yc, yd, p, h, dvec, ng, gate1, lg, lb, wa, wb, wc, wd, wo)


def _router_kernel(h_ref, sh_ref, sc_ref, w_ref, b_ref, sel_ref, cnt_ref):
    @pl.when(pl.program_id(0) == 0)
    def _():
        cnt_ref[...] = jnp.zeros_like(cnt_ref)

    u = h_ref[...] * (1.0 + sc_ref[...]) + sh_ref[...]
    lg = jnp.dot(u, w_ref[...], precision=HIGHEST, preferred_element_type=F32) + b_ref[...]
    lane = lax.broadcasted_iota(jnp.int32, lg.shape, 1).astype(F32)
    neg = -1e30
    big = 1e9
    gl = jnp.where(lane < MOE_GROUPS, lg, neg)
    gmax = jnp.max(gl, -1, keepdims=True)
    gsel = jnp.min(jnp.where(gl == gmax, lane, big), -1, keepdims=True)
    gprob = 1.0 / jnp.sum(jnp.where(lane < MOE_GROUPS, jnp.exp(lg - gmax), 0.0), -1, keepdims=True)
    lo = MOE_GROUPS + gsel * MOE_EPG
    el = jnp.where(jnp.abs(lane - lo - (MOE_EPG - 1) / 2.0) < MOE_EPG / 2.0, lg, neg)
    m1 = jnp.max(el, -1, keepdims=True)
    i1 = jnp.min(jnp.where(el == m1, lane, big), -1, keepdims=True)
    el2 = jnp.where(lane == i1, neg, el)
    m2 = jnp.max(el2, -1, keepdims=True)
    i2 = jnp.min(jnp.where(el2 == m2, lane, big), -1, keepdims=True)
    t = jnp.exp(m2 - m1)
    w1 = gprob / (1.0 + t)
    w2 = gprob * t / (1.0 + t)
    oh1 = jnp.where(lane == i1, 1.0, 0.0)
    oh2 = jnp.where(lane == i2, 1.0, 0.0)
    oh = oh1 + oh2
    tt = lg.shape[0]
    li = lax.broadcasted_iota(jnp.int32, (tt, tt), 0)
    si = lax.broadcasted_iota(jnp.int32, (tt, tt), 1)
    before = _dot(jnp.where(li > si, 1.0, 0.0).astype(BF16), oh.astype(BF16)) + cnt_ref[...]
    r1 = jnp.sum(oh1 * before, -1, keepdims=True)
    r2 = jnp.sum(oh2 * before, -1, keepdims=True)
    cnt_ref[...] += jnp.sum(oh, axis=0, keepdims=True)
    cols = (i1 - MOE_GROUPS, i2 - MOE_GROUPS, w1, w2, r1, r2)
    sel = jnp.zeros_like(lg)
    for k, v in enumerate(cols):
        sel = jnp.where(lane == k, v, sel)
    sel_ref[...] = sel


def _router(h, shift, scale, wr, br):
    n = h.shape[0]
    t = 256
    tok = lambda w: pl.BlockSpec((t, w), lambda i: (i, 0))
    vec = lambda w: pl.BlockSpec((1, w), lambda i: (0, 0))
    return pl.pallas_call(
        _router_kernel,
        grid=(n // t,),
        in_specs=[tok(D_MODEL), vec(D_MODEL), vec(D_MODEL),
                  pl.BlockSpec((D_MODEL, LANE), lambda i: (0, 0)), vec(LANE)],
        out_specs=[tok(LANE), vec(LANE)],
        out_shape=[jax.ShapeDtypeStruct((n, LANE), F32), jax.ShapeDtypeStruct((1, LANE), F32)],
        compiler_params=_params("arbitrary"),
        name="router",
    )(h, shift, scale, wr, br)


MOE_T = 256


def _dispatch_kernel(dst_ref, h_ref, sh_ref, sc_ref, zero_hbm, xin_hbm, ubuf, sem, *, nt):
    del zero_hbm
    t = MOE_T
    i = pl.program_id(0)
    slot = i % 2

    def wait_slot(s):
        for _ in range(MOE_TOP_K):
            pltpu.make_async_copy(ubuf.at[s], xin_hbm.at[pl.ds(0, t), :], sem.at[s]).wait()

    @pl.when(i >= 2)
    def _():
        wait_slot(slot)

    ubuf[slot] = h_ref[...] * (1.0 + sc_ref[...]) + sh_ref[...]

    def issue(r, carry):
        for k in range(MOE_TOP_K):
            pltpu.make_async_copy(ubuf.at[slot, pl.ds(r, 1), :],
                                  xin_hbm.at[pl.ds(dst_ref[0, 0, k * t + r], 1), :], sem.at[slot]).start()
        return carry

    lax.fori_loop(0, t, issue, 0, unroll=8)

    @pl.when(i == nt - 1)
    def _():
        wait_slot(slot)
        if nt > 1:
            wait_slot(1 - slot)


def _dispatch(pos_t, h, shift, scale, n_rows):
    n = h.shape[0]
    t = MOE_T
    nt = n // t
    vec = pl.BlockSpec((1, D_MODEL), lambda i: (0, 0))
    return pl.pallas_call(
        functools.partial(_dispatch_kernel, nt=nt),
        grid=(nt,),
        in_specs=[pl.BlockSpec((1, 1, MOE_TOP_K * t), lambda i: (i, 0, 0), memory_space=pltpu.SMEM),
                  pl.BlockSpec((t, D_MODEL), lambda i: (i, 0)), vec, vec,
                  pl.BlockSpec(memory_space=pl.ANY)],
        out_specs=pl.BlockSpec(memory_space=pl.ANY),
        out_shape=jax.ShapeDtypeStruct((n_rows, D_MODEL), F32),
        scratch_shapes=[pltpu.VMEM((2, t, D_MODEL), F32), pltpu.SemaphoreType.DMA((2,))],
        input_output_aliases={4: 0},
        compiler_params=_params("arbitrary"),
        name="dispatch",
    )(pos_t, h, shift, scale, jnp.zeros((n_rows, D_MODEL), F32))


def _expert_kernel(be_ref, nu_ref, x_ref, wg_ref, wu_ref, wd_ref, o_ref, wgb, wub, wdb):
    b = pl.program_id(0)

    @pl.when((b == 0) | (be_ref[b] != be_ref[jnp.maximum(b - 1, 0)]))
    def _():
        wgb[...] = wg_ref[...].astype(BF16)
        wub[...] = wu_ref[...].astype(BF16)
        wdb[...] = wd_ref[...].astype(BF16)

    @pl.when(b < nu_ref[0])
    def _():
        x = x_ref[...].astype(BF16)
        hid = _silu(_dot(x, wgb[...])) * _dot(x, wub[...])
        o_ref[...] = _dot(hid.astype(BF16), wdb[...])

    @pl.when(b >= nu_ref[0])
    def _():
        o_ref[...] = jnp.zeros_like(o_ref)


def _experts(xin, block_e, n_used, wg, wu, wd):
    n_blocks = block_e.shape[0]
    gs = pltpu.PrefetchScalarGridSpec(
        num_scalar_prefetch=2,
        grid=(n_blocks,),
        in_specs=[pl.BlockSpec((MOE_ROWS, D_MODEL), lambda b, be, nu: (b, 0)),
                  pl.BlockSpec((None, D_MODEL, MOE_FF), lambda b, be, nu: (be[b], 0, 0)),
                  pl.BlockSpec((None, D_MODEL, MOE_FF), lambda b, be, nu: (be[b], 0, 0)),
                  pl.BlockSpec((None, MOE_FF, D_MODEL), lambda b, be, nu: (be[b], 0, 0))],
        out_specs=pl.BlockSpec((MOE_ROWS, D_MODEL), lambda b, be, nu: (b, 0)),
        scratch_shapes=[pltpu.VMEM((D_MODEL, MOE_FF), BF16), pltpu.VMEM((D_MODEL, MOE_FF), BF16),
                        pltpu.VMEM((MOE_FF, D_MODEL), BF16)],
    )
    return pl.pallas_call(
        _expert_kernel,
        grid_spec=gs,
        out_shape=jax.ShapeDtypeStruct((n_blocks * MOE_ROWS, D_MODEL), F32),
        compiler_params=_params("arbitrary"),
        name="experts",
    )(block_e, n_used, xin, wg, wu, wd)


def _combine_kernel(pos_ref, posn_ref, y_hbm, h_ref, sel_ref, g2_ref, lg_ref, lb_ref, o_ref, ybuf, sem, *, nt):
    t = MOE_T
    i = pl.program_id(0)
    slot = i % 2

    def gather(p_ref, s):
        def issue(r, carry):
            pltpu.make_async_copy(y_hbm.at[pl.ds(p_ref[0, 0, r], 1), :], ybuf.at[s, pl.ds(r, 1), :],
                                  sem.at[s]).start()
            return carry
        lax.fori_loop(0, MOE_TOP_K * t, issue, 0, unroll=8)

    @pl.when(i == 0)
    def _():
        gather(pos_ref, 0)

    @pl.when(i + 1 < nt)
    def _():
        gather(posn_ref, 1 - slot)

    pltpu.make_async_copy(y_hbm.at[pl.ds(0, MOE_TOP_K * t), :], ybuf.at[slot], sem.at[slot]).wait()
    ffn = sel_ref[:, 2:3] * ybuf[slot, 0:t, :] + sel_ref[:, 3:4] * ybuf[slot, t:2 * t, :]
    o_ref[...] = _layer_norm(DN_ALPHA * h_ref[...] + g2_ref[...] * ffn, lg_ref[...], lb_ref[...])


def _combine(y, pos_t, h, sel, gate2, lg, lb):
    n = h.shape[0]
    t = MOE_T
    nt = n // t
    tok = lambda w: pl.BlockSpec((t, w), lambda i: (i, 0))
    vec = pl.BlockSpec((1, D_MODEL), lambda i: (0, 0))
    return pl.pallas_call(
        functools.partial(_combine_kernel, nt=nt),
        grid=(nt,),
        in_specs=[pl.BlockSpec((1, 1, MOE_TOP_K * t), lambda i: (i, 0, 0), memory_space=pltpu.SMEM),
                  pl.BlockSpec((1, 1, MOE_TOP_K * t), lambda i: (jnp.minimum(i + 1, nt - 1), 0, 0),
                               memory_space=pltpu.SMEM),
                  pl.BlockSpec(memory_space=pl.ANY),
                  tok(D_MODEL), tok(LANE), vec, vec, vec],
        out_specs=tok(D_MODEL),
        out_shape=jax.ShapeDtypeStruct((n, D_MODEL), F32),
        scratch_shapes=[pltpu.VMEM((2, MOE_TOP_K * t, D_MODEL), F32), pltpu.SemaphoreType.DMA((2,))],
        compiler_params=_params("arbitrary"),
        name="combine",
    )(pos_t, pos_t, y, h, sel, gate2, lg, lb)


def _moe(h, shift, scale, gate2, lg, lb, wr, br, wg, wu, wd):
    n = h.shape[0]
    t = MOE_T
    nt = n // t
    sel, cnt = _router(h, shift, scale, wr, br)
    counts = cnt[0, MOE_GROUPS:MOE_GROUPS + MOE_EXPERTS].astype(jnp.int32)
    padded = (counts + MOE_ROWS - 1) // MOE_ROWS * MOE_ROWS
    pad_end = jnp.cumsum(padded)
    pad_start = pad_end - padded
    n_blocks = (n * MOE_TOP_K + MOE_EXPERTS * (MOE_ROWS - 1) + MOE_ROWS - 1) // MOE_ROWS
    block_e = jnp.minimum(
        jnp.searchsorted(pad_end, jnp.arange(n_blocks, dtype=jnp.int32) * MOE_ROWS, side="right"),
        MOE_EXPERTS - 1).astype(jnp.int32)
    n_used = (pad_end[-1:] // MOE_ROWS).astype(jnp.int32)
    pos = pad_start[sel[:, 0:MOE_TOP_K].astype(jnp.int32)] + sel[:, 4:4 + MOE_TOP_K].astype(jnp.int32)
    pos_t = pos.reshape(nt, t, MOE_TOP_K).transpose(0, 2, 1).reshape(nt, 1, MOE_TOP_K * t)
    xin = _dispatch(pos_t, h, shift, scale, n_blocks * MOE_ROWS)
    y = _experts(xin, block_e, n_used, wg, wu, wd)
    return _combine(y, pos_t, h, sel, gate2, lg, lb)


def _mixer(h, mod, lw, ssd_init, tables, *, latent, need_mix):
    n = h.shape[0]
    p = _inproj(h, mod[0], mod[1], lw["w_in"])
    xbc = _conv3(p, PX, SSD_XBC, lw["ssd_conv_w"], lw["ssd_conv_b"], silu=True)
    ydir, finals = _ssd_scan(xbc, p, lw["ssd_dt_bias"], lw["ssd_a_log"], ssd_init)
    if not need_mix:
        return None, finals
    ya = _conformer(p, lw["conf_dw_w"], lw["conf_dw_b"], lw["conf_ln_g"], lw["conf_ln_b"],
                    dil=GRID_W if latent else 1)
    q = _conv3(p, PC, 3 * HY_D, lw["hy_short_w"], lw["hy_short_b"], silu=False)
    k2, nrm = _hyena_filters(lw["featx_lat" if latent else "featx_ctx"], lw["hy_filter"], n,
                             BF16 if latent else F32)
    if latent:
        yc = _hyena_long(q, k2, nrm, lw["hy_bias"], tables)
    else:
        yc = _hyena_ctx(q, k2, nrm, lw["hy_bias"])
    yd = _gated_conv(p, lw["sc_conv_w"])
    h = _merge(ya, xbc, ydir, p, yc, yd, h, lw["ssd_dvec"], lw["ssd_norm_g"], mod[2], lw["ln_g0"], lw["ln_b0"],
               lw["w_branch_a"], lw["w_branch_b"], lw["w_branch_c"], lw["w_branch_d"], lw["w_out"])
    return h, finals


def _positional_features(n):
    t01 = jnp.linspace(0.0, 1.0, n, dtype=F32)[:, None]
    omega = (2.0 * math.pi / n) * jnp.arange(n, dtype=F32)[:, None]
    bands = jnp.linspace(1e-4, HY_BANDS - 1, HY_BANDS, dtype=F32)
    feat = jnp.concatenate([t01, jnp.cos(bands * omega), -jnp.sin(bands * omega)], axis=-1)
    featx = jnp.concatenate([feat, jnp.zeros((1, HY_EMB), F32), jnp.flip(feat[1:], axis=0)], axis=0)
    return _pad_lanes(featx)


def _relayout_w_in(w_in):
    seg = lambda a, b: w_in[:, :, a:b]
    ob = OFF_B
    parts = [seg(OFF_G, OFF_G + N_BRANCH * D_MODEL),
             seg(OFF_A, OFF_B),
             seg(OFF_C, OFF_D),
             seg(OFF_D, OFF_G),
             seg(ob + SSD_D, ob + SSD_D + SSD_XBC),
             seg(ob, ob + SSD_D),
             seg(ob + SSD_D + SSD_XBC, OFF_C),
             jnp.zeros(w_in.shape[:2] + (NP - PDT - 2 * SSD_HEADS,), w_in.dtype)]
    return jnp.concatenate(parts, axis=-1).astype(BF16)


def _pad_lanes(v):
    return jnp.pad(v, ((0, 0), (0, LANE - v.shape[-1])))


def kernel(x, c, ctx, c_ctx, w_mod, b_mod, ln_g, ln_b, w_in, conf_dw_w, conf_dw_b, conf_ln_g, conf_ln_b,
           ssd_conv_w, ssd_conv_b, ssd_a_log, ssd_dt_bias, ssd_d, ssd_norm_g, hy_short_w, hy_short_b,
           hy_w1, hy_b1, hy_w2, hy_b2, hy_freq, hy_w3, hy_bias, sc_conv_w, w_branch_a, w_branch_b,
           w_branch_c, w_branch_d, w_out, rt_group_w, rt_group_b, rt_expert_w, rt_expert_b,
           ex_w_gate, ex_w_up, ex_w_down):
    assert x.shape[0] == 1 and ctx.shape[0] == 1
    n_lat, n_ctx = x.shape[1], ctx.shape[1]
    depth = w_in.shape[0]

    cv = jnp.concatenate([c, c_ctx[None, :], jnp.zeros((SUBLANE - 2, D_MODEL), F32)], axis=0)
    mods = _mod_vectors(cv, w_mod, b_mod)
    w_in_p = _relayout_w_in(w_in)
    tables = _dft_tables(n_lat)
    featx_lat = _positional_features(n_lat)
    featx_ctx = _positional_features(n_ctx)
    deltas = jnp.abs(jnp.linspace(HY_MIN_DECAY, HY_MAX_DECAY, HY_N_FILT, dtype=F32))
    deltas_d = deltas.reshape(HY_ORDER, 2, HY_D).transpose(1, 0, 2).reshape(2, 1, HY_ORDER * HY_D)
    router_w = jnp.concatenate([rt_group_w, rt_expert_w,
                                jnp.zeros((depth, D_MODEL, LANE - MOE_GROUPS - MOE_EXPERTS), F32)], axis=-1)
    router_b = jnp.concatenate([rt_group_b, rt_expert_b,
                                jnp.zeros((depth, LANE - MOE_GROUPS - MOE_EXPERTS), F32)], axis=-1)
    ssd_zero = jnp.zeros((2, SSD_HEADS, SSD_STATE, SSD_HEAD_DIM), F32)

    h_lat, h_ctx = x[0], ctx[0]
    for l in range(depth):
        row = lambda v: v[None, :]
        lw = dict(
            w_in=w_in_p[l], conf_dw_w=conf_dw_w[l], conf_dw_b=row(conf_dw_b[l]), conf_ln_g=row(conf_ln_g[l]),
            conf_ln_b=row(conf_ln_b[l]), ssd_conv_w=ssd_conv_w[l], ssd_conv_b=row(ssd_conv_b[l]),
            ssd_a_log=_pad_lanes(ssd_a_log[l].reshape(1, -1)), ssd_dt_bias=_pad_lanes(ssd_dt_bias[l].reshape(1, -1)),
            ssd_dvec=row(jnp.repeat(ssd_d[l], SSD_HEAD_DIM)), ssd_norm_g=row(ssd_norm_g[l]),
            hy_short_w=hy_short_w[l], hy_short_b=row(hy_short_b[l]),
            hy_filter=_filter_weights(hy_w1[l], hy_b1[l], hy_w2[l], hy_b2[l], hy_freq[l], hy_w3[l], deltas_d),
            hy_bias=hy_bias[l], featx_lat=featx_lat, featx_ctx=featx_ctx,
            sc_conv_w=sc_conv_w[l], ln_g0=row(ln_g[l, 0]), ln_b0=row(ln_b[l, 0]),
            w_branch_a=w_branch_a[l].astype(BF16), w_branch_b=w_branch_b[l].astype(BF16),
            w_branch_c=w_branch_c[l].astype(BF16), w_branch_d=w_branch_d[l].astype(BF16),
            w_out=w_out[l].astype(BF16))
        moe_w = (router_w[l], row(router_b[l]), ex_w_gate[l], ex_w_up[l], ex_w_down[l])
        last = l == depth - 1
        d = D_MODEL
        mod_lat = [mods[l, 0:1, k * d:(k + 1) * d] for k in range(6)]
        mod_ctx = [mods[l, 1:2, k * d:(k + 1) * d] for k in range(6)]

        mix_ctx, ctx_states = _mixer(h_ctx, mod_ctx, lw, ssd_zero, None, latent=False, need_mix=not last)
        h_lat, _ = _mixer(h_lat, mod_lat, lw, ctx_states, tables, latent=True, need_mix=True)
        h_lat = _moe(h_lat, mod_lat[3], mod_lat[4], mod_lat[5], row(ln_g[l, 1]), row(ln_b[l, 1]), *moe_w)
        if not last:
            h_ctx = _moe(mix_ctx, mod_ctx[3], mod_ctx[4], mod_ctx[5], row(ln_g[l, 1]), row(ln_b[l, 1]), *moe_w)
    return h_lat[None]
```

```python
import functools
import math

import jax
import jax.numpy as jnp
from jax import lax
from jax.experimental import pallas as pl
from jax.experimental.pallas import tpu as pltpu

F32 = jnp.float32
BF16 = jnp.bfloat16
HIGHEST = lax.Precision.HIGHEST

D_MODEL = 1024
DEPTH = 4
GRID_W = 64
CONF_D = 512
CONF_K = 31
SSD_D = 768
SSD_HEADS = 12
SSD_HEAD_DIM = 64
SSD_GROUPS = 4
SSD_HPG = SSD_HEADS // SSD_GROUPS
SSD_STATE = 128
SSD_CHUNK = 128
SSD_BC = SSD_GROUPS * SSD_STATE
SSD_XBC = SSD_D + 2 * SSD_BC
SSD_PROJ = SSD_D + SSD_XBC + 2 * SSD_HEADS
HY_D = 512
HY_ORDER = 2
HY_EMB = 33
HY_BANDS = (HY_EMB - 1) // 2
HY_HID = 64
HY_N_FILT = HY_ORDER * 2 * HY_D
HY_MIN_DECAY = math.log(1e-2) / 1.5
HY_MAX_DECAY = math.log(1e-2) / 0.3
SC_D = 512
N_BRANCH = 4
OFF_A = 0
OFF_B = OFF_A + 2 * CONF_D
OFF_C = OFF_B + SSD_PROJ
OFF_D = OFF_C + 3 * HY_D
OFF_G = OFF_D + 3 * SC_D
MOE_GROUPS = 4
MOE_EPG = 8
MOE_EXPERTS = MOE_GROUPS * MOE_EPG
MOE_TOP_K = 2
MOE_FF = 512
DN_ALPHA = (2 * DEPTH) ** 0.25
LN_EPS = 1e-5

PG = 0
PA = PG + N_BRANCH * D_MODEL
PC = PA + 2 * CONF_D
PD = PC + 3 * HY_D
PX = PD + 3 * SC_D
PZ = PX + SSD_XBC
PDT = PZ + SSD_D
INPROJ_TN = 1024
NP = -(-(PDT + 128) // INPROJ_TN) * INPROJ_TN

LANE = 128
SUBLANE = 8
FFT_N2 = 256
MOE_ROWS = 256
VMEM_LIMIT = 48 * 1024 * 1024


def _params(*sem):
    return pltpu.CompilerParams(dimension_semantics=sem, vmem_limit_bytes=VMEM_LIMIT)


def _silu(x):
    return x * jax.nn.sigmoid(x)


def _layer_norm(x, g, b):
    mu = jnp.mean(x, -1, keepdims=True)
    xc = x - mu
    var = jnp.mean(xc * xc, -1, keepdims=True)
    return xc * lax.rsqrt(var + LN_EPS) * g + b


def _dot(a, b):
    return jnp.dot(a, b, preferred_element_type=F32)


def _mod_kernel(cv_ref, w_ref, b_ref, o_ref):
    o_ref[...] = jnp.dot(_silu(cv_ref[...]), w_ref[...], precision=HIGHEST,
                         preferred_element_type=F32) + b_ref[...]


def _mod_vectors(cv, w_mod, b_mod):
    tn = 1536
    return pl.pallas_call(
        _mod_kernel,
        grid=(DEPTH, 6 * D_MODEL // tn),
        in_specs=[pl.BlockSpec((SUBLANE, D_MODEL), lambda l, j: (0, 0)),
                  pl.BlockSpec((None, D_MODEL, tn), lambda l, j: (l, 0, j)),
                  pl.BlockSpec((None, 1, tn), lambda l, j: (l, 0, j))],
        out_specs=pl.BlockSpec((None, SUBLANE, tn), lambda l, j: (l, 0, j)),
        out_shape=jax.ShapeDtypeStruct((DEPTH, SUBLANE, 6 * D_MODEL), F32),
        compiler_params=_params("parallel", "parallel"),
        name="mod_vectors",
    )(cv, w_mod, b_mod.reshape(DEPTH, 1, 6 * D_MODEL))


def _inproj_kernel(x_ref, sh_ref, sc_ref, w_ref, o_ref, xb_ref):
    @pl.when(pl.program_id(1) == 0)
    def _():
        xb_ref[...] = (x_ref[...] * (1.0 + sc_ref[...]) + sh_ref[...]).astype(BF16)

    o_ref[...] = _dot(xb_ref[...], w_ref[...])


def _inproj(h, shift, scale, w):
    n = h.shape[0]
    tm = min(n, 1024)
    tn = INPROJ_TN
    return pl.pallas_call(
        _inproj_kernel,
        grid=(n // tm, NP // tn),
        in_specs=[pl.BlockSpec((tm, D_MODEL), lambda i, j: (i, 0)),
                  pl.BlockSpec((1, D_MODEL), lambda i, j: (0, 0)),
                  pl.BlockSpec((1, D_MODEL), lambda i, j: (0, 0)),
                  pl.BlockSpec((D_MODEL, tn), lambda i, j: (0, j))],
        out_specs=pl.BlockSpec((tm, tn), lambda i, j: (i, j)),
        out_shape=jax.ShapeDtypeStruct((n, NP), F32),
        scratch_shapes=[pltpu.VMEM((tm, D_MODEL), BF16)],
        compiler_params=_params("parallel", "arbitrary"),
        name="inproj",
    )(h, shift, scale, w)


def _shifted(x, prev_row, next_row):
    t = x.shape[0]
    row = lax.broadcasted_iota(jnp.int32, x.shape, 0)
    xm = jnp.where(row == 0, prev_row, pltpu.roll(x, 1, 0))
    xp = jnp.where(row == t - 1, next_row, pltpu.roll(x, t - 1, 0))
    return xm, xp


def _conv3_kernel(cur_ref, prev_ref, next_ref, w_ref, b_ref, o_ref, *, silu, nt):
    i = pl.program_id(0)
    x = cur_ref[...]
    pv = jnp.where(i > 0, prev_ref[SUBLANE - 1:SUBLANE, :], 0.0)
    nx = jnp.where(i < nt - 1, next_ref[0:1, :], 0.0)
    xm, xp = _shifted(x, pv, nx)
    y = w_ref[0:1, :] * xm + w_ref[1:2, :] * x + w_ref[2:3, :] * xp + b_ref[...]
    o_ref[...] = _silu(y) if silu else y


def _halo_specs(t, ct, n, col0):
    r8 = t // SUBLANE
    last8 = n // SUBLANE - 1
    return [pl.BlockSpec((t, ct), lambda i, j: (i, col0 + j)),
            pl.BlockSpec((SUBLANE, ct), lambda i, j: (jnp.maximum(i * r8 - 1, 0), col0 + j)),
            pl.BlockSpec((SUBLANE, ct), lambda i, j: (jnp.minimum((i + 1) * r8, last8), col0 + j))]


def _conv3(p, col, width, w, b, *, silu):
    n = p.shape[0]
    t = min(n, 1024)
    ct = 256
    nt = n // t
    return pl.pallas_call(
        functools.partial(_conv3_kernel, silu=silu, nt=nt),
        grid=(nt, width // ct),
        in_specs=_halo_specs(t, ct, n, col // ct) + [
            pl.BlockSpec((3, ct), lambda i, j: (0, j)),
            pl.BlockSpec((1, ct), lambda i, j: (0, j))],
        out_specs=pl.BlockSpec((t, ct), lambda i, j: (i, j)),
        out_shape=jax.ShapeDtypeStruct((n, width), F32),
        compiler_params=_params("parallel", "parallel"),
        name="conv3",
    )(p, p, p, w, b)


def _gconv_kernel(bg_ref, cc_ref, cp_ref, cn_ref, xc_ref, xp_ref, xn_ref, w_ref, o_ref, *, nt):
    i = pl.program_id(0)
    x = cc_ref[...] * xc_ref[...]
    pv = jnp.where(i > 0, cp_ref[SUBLANE - 1:SUBLANE, :] * xp_ref[SUBLANE - 1:SUBLANE, :], 0.0)
    nx = jnp.where(i < nt - 1, cn_ref[0:1, :] * xn_ref[0:1, :], 0.0)
    xm, xp = _shifted(x, pv, nx)
    o_ref[...] = bg_ref[...] * (w_ref[0:1, :] * xm + w_ref[1:2, :] * x + w_ref[2:3, :] * xp)


def _gated_conv(p, w):
    n = p.shape[0]
    t = min(n, 1024)
    ct = 256
    nt = n // t
    nb = SC_D // ct
    return pl.pallas_call(
        functools.partial(_gconv_kernel, nt=nt),
        grid=(nt, nb),
        in_specs=([pl.BlockSpec((t, ct), lambda i, j: (i, PD // ct + j))]
                  + _halo_specs(t, ct, n, PD // ct + nb)
                  + _halo_specs(t, ct, n, PD // ct + 2 * nb)
                  + [pl.BlockSpec((3, ct), lambda i, j: (0, j))]),
        out_specs=pl.BlockSpec((t, ct), lambda i, j: (i, j)),
        out_shape=jax.ShapeDtypeStruct((n, SC_D), F32),
        compiler_params=_params("parallel", "parallel"),
        name="gated_conv",
    )(p, p, p, p, p, p, p, w)


CONF_RB = 64


def _conf_kernel(vc, gc, vp, gp, vn, gn, w_ref, b_ref, lg_ref, lb_ref, o_ref, buf, *, t, halo, dil, nt):
    i = pl.program_id(0)
    buf[halo:halo + t, :] = vc[...] * jax.nn.sigmoid(gc[...])
    buf[0:halo, :] = jnp.where(i > 0, vp[t - halo:t, :] * jax.nn.sigmoid(gp[t - halo:t, :]), 0.0)
    buf[halo + t:halo + t + halo, :] = jnp.where(
        i < nt - 1, vn[0:halo, :] * jax.nn.sigmoid(gn[0:halo, :]), 0.0)

    def block(r0):
        acc = jnp.zeros((CONF_RB, CONF_D), F32)
        for j in range(CONF_K):
            off = halo + (j - CONF_K // 2) * dil
            acc = acc + w_ref[j:j + 1, :] * buf[pl.ds(r0 + off, CONF_RB), :]
        v = _layer_norm(acc + b_ref[...], lg_ref[...], lb_ref[...])
        o_ref[pl.ds(r0, CONF_RB), :] = _silu(v)

    if dil % CONF_RB == 0:
        def body(rb, carry):
            block(pl.multiple_of(rb * CONF_RB, CONF_RB))
            return carry
        lax.fori_loop(0, t // CONF_RB, body, 0)
    else:
        for rb in range(t // CONF_RB):
            block(rb * CONF_RB)


def _conformer(p, w, b, lg, lb, *, dil):
    n = p.shape[0]
    t = min(n, 1024)
    nt = n // t
    halo = -(-(CONF_K // 2) * dil // SUBLANE) * SUBLANE
    assert halo <= t
    cb = PA // CONF_D

    def spec(col, shift):
        return pl.BlockSpec((t, CONF_D), lambda i: (jnp.clip(i + shift, 0, nt - 1), col))

    vec = pl.BlockSpec((1, CONF_D), lambda i: (0, 0))
    return pl.pallas_call(
        functools.partial(_conf_kernel, t=t, halo=halo, dil=dil, nt=nt),
        grid=(nt,),
        in_specs=[spec(cb, 0), spec(cb + 1, 0), spec(cb, -1), spec(cb + 1, -1), spec(cb, 1), spec(cb + 1, 1),
                  pl.BlockSpec((CONF_K, CONF_D), lambda i: (0, 0)), vec, vec, vec],
        out_specs=pl.BlockSpec((t, CONF_D), lambda i: (i, 0)),
        out_shape=jax.ShapeDtypeStruct((n, CONF_D), F32),
        scratch_shapes=[pltpu.VMEM((t + 2 * halo, CONF_D), F32)],
        compiler_params=_params("parallel"),
        name="conformer",
    )(p, p, p, p, p, p, w, b, lg, lb)


def _ssd_kernel(xbc_ref, dt_ref, dtb_ref, alog_ref, init_ref, y_ref, fin_ref, h_ref, *, nc):
    d = pl.program_id(0)
    c = pl.program_id(1)
    q = SSD_CHUNK
    hd = SSD_HEAD_DIM

    @pl.when(c == 0)
    def _():
        h_ref[...] = init_ref[...]

    raw = dt_ref[...] + dtb_ref[...]
    dt_all = jnp.maximum(raw, 0.0) + jnp.log(1.0 + jnp.exp(-jnp.abs(raw)))
    ld_all = dt_all * (-jnp.exp(alog_ref[...]))
    lane = lax.broadcasted_iota(jnp.int32, (q, LANE), 1)
    head = lane < SSD_HEADS
    dt_d = jnp.where(head, jnp.where(d == 0, dt_all, pltpu.roll(dt_all, LANE - SSD_HEADS, 1)), 0.0)
    ld_d = jnp.where(head, jnp.where(d == 0, ld_all, pltpu.roll(ld_all, LANE - SSD_HEADS, 1)), 0.0)

    li = lax.broadcasted_iota(jnp.int32, (q, q), 0)
    si = lax.broadcasted_iota(jnp.int32, (q, q), 1)
    mask = (li - si) * (1 - 2 * d) >= 0
    cum = jnp.dot(mask.astype(F32), ld_d, precision=HIGHEST, preferred_element_type=F32)
    tot = jnp.sum(ld_d, axis=0, keepdims=True)
    cum_t = cum.T
    dt_t = dt_d.T
    w_t = (jnp.exp(tot - cum) * dt_d).T
    a_out = jnp.exp(cum)
    e_tot = jnp.exp(tot)

    for g in range(SSD_GROUPS):
        bg = xbc_ref[:, SSD_D + g * SSD_STATE:SSD_D + (g + 1) * SSD_STATE]
        cg = xbc_ref[:, SSD_D + SSD_BC + g * SSD_STATE:SSD_D + SSD_BC + (g + 1) * SSD_STATE]
        bg_t = bg.T
        cb = _dot(cg.astype(BF16), bg_t.astype(BF16))
        for e in range(SSD_HPG):
            hh = g * SSD_HPG + e
            diff = cum[:, hh:hh + 1] - cum_t[hh:hh + 1, :]
            dec = jnp.exp(jnp.where(mask, diff, -1e30))
            m = (cb * dec * dt_t[hh:hh + 1, :]).astype(BF16)
            xe = xbc_ref[:, hh * hd:(hh + 1) * hd].astype(BF16)
            cs = (cg * a_out[:, hh:hh + 1]).astype(BF16)
            h_in = h_ref[hh]
            y_ref[:, hh * hd:(hh + 1) * hd] = _dot(m, xe) + _dot(cs, h_in.astype(BF16))
            s_new = _dot((bg_t * w_t[hh:hh + 1, :]).astype(BF16), xe)
            h_ref[hh] = e_tot[:, hh:hh + 1] * h_in + s_new

    @pl.when(c == nc - 1)
    def _():
        fin_ref[...] = h_ref[...]


def _ssd_scan(xbc, p, dt_bias, a_log, init):
    n = xbc.shape[0]
    q = SSD_CHUNK
    nc = n // q

    def chunk(d, c):
        return jnp.where(d == 0, c, nc - 1 - c)

    st = (SSD_HEADS, SSD_STATE, SSD_HEAD_DIM)
    vec = pl.BlockSpec((1, LANE), lambda d, c: (0, 0))
    return pl.pallas_call(
        functools.partial(_ssd_kernel, nc=nc),
        grid=(2, nc),
        in_specs=[pl.BlockSpec((q, SSD_XBC), lambda d, c: (chunk(d, c), 0)),
                  pl.BlockSpec((q, LANE), lambda d, c: (chunk(d, c), PDT // LANE)),
                  vec, vec,
                  pl.BlockSpec((None,) + st, lambda d, c: (d, 0, 0, 0))],
        out_specs=[pl.BlockSpec((None, q, SSD_D), lambda d, c: (d, chunk(d, c), 0)),
                   pl.BlockSpec((None,) + st, lambda d, c: (d, 0, 0, 0))],
        out_shape=[jax.ShapeDtypeStruct((2, n, SSD_D), F32),
                   jax.ShapeDtypeStruct((2,) + st, F32)],
        scratch_shapes=[pltpu.VMEM(st, F32)],
        compiler_params=_params("arbitrary", "arbitrary"),
        name="ssd_scan",
    )(xbc, p, dt_bias, a_log, init)


def _filt_kernel(feat_ref, w1_ref, b1_ref, w2_ref, b2_ref, fr_ref, w3h_ref, w3l_ref, dl_ref, k_ref, nrm_ref, *,
                 n, t):
    i = pl.program_id(0)
    hf = t // 2
    feat = feat_ref[...]
    x = jnp.concatenate([feat[0:hf], feat[hf:t]], axis=1)
    hid = jnp.sin(fr_ref[0:1, :] * (jnp.dot(x, w1_ref[...], precision=HIGHEST,
                                            preferred_element_type=F32) + b1_ref[...]))
    hid = jnp.sin(fr_ref[1:2, :] * (jnp.dot(hid, w2_ref[...], precision=HIGHEST,
                                            preferred_element_type=F32) + b2_ref[...]))
    hi = hid.astype(BF16)
    lo = (hid - hi.astype(F32)).astype(BF16)

    @pl.when(i == 0)
    def _():
        nrm_ref[...] = jnp.zeros_like(nrm_ref)

    for half in range(2):
        wh, wl = w3h_ref[half], w3l_ref[half]
        filt = _dot(hi, wh) + _dot(lo, wh) + _dot(hi, wl)
        filt = filt * jnp.exp(-feat[half * hf:(half + 1) * hf, 0:1] * dl_ref[...])
        row = i * t + half * hf + lax.broadcasted_iota(jnp.int32, filt.shape, 0)
        filt = jnp.where(row == n, 0.0, filt)
        k_ref[half * hf:(half + 1) * hf, :] = filt
        nrm_ref[...] += jnp.sum(jnp.abs(filt), axis=0, keepdims=True)


def _hyena_filters(featx, fw, n):
    t = min(n, 512)
    half = n // t
    oc = HY_ORDER * HY_D
    full = lambda shape: pl.BlockSpec(shape, lambda i: tuple(0 for _ in shape))
    w3spec = pl.BlockSpec((None, 2, LANE, oc), lambda i: (i // half, 0, 0, 0))
    return pl.pallas_call(
        functools.partial(_filt_kernel, n=n, t=t),
        grid=(2 * n // t,),
        in_specs=[pl.BlockSpec((t, LANE), lambda i: (i, 0)),
                  full((2 * LANE, LANE)), full((1, LANE)), full((LANE, LANE)), full((1, LANE)),
                  full((2, LANE)), w3spec, w3spec,
                  pl.BlockSpec((None, 1, oc), lambda i: (i // half, 0, 0))],
        out_specs=[pl.BlockSpec((t, oc), lambda i: (i, 0)),
                   pl.BlockSpec((1, oc), lambda i: (0, 0))],
        out_shape=[jax.ShapeDtypeStruct((2 * n, oc), F32),
                   jax.ShapeDtypeStruct((1, oc), F32)],
        compiler_params=_params("arbitrary"),
        name="hyena_filters",
    )(featx, fw["w1"], fw["b1"], fw["w2"], fw["b2"], fw["freq"], fw["w3h"], fw["w3l"], fw["deltas"])


def _filter_weights(w1, b1, w2, b2, freq, w3, deltas_d):
    hh = HY_HID
    z = lambda r, c: jnp.zeros((r, c), F32)
    w1p = jnp.pad(w1, ((0, LANE - HY_EMB), (0, 0)))
    w1b = jnp.concatenate([jnp.concatenate([w1p, z(LANE, hh)], 1),
                           jnp.concatenate([z(LANE, hh), w1p], 1)], 0)
    w2b = jnp.concatenate([jnp.concatenate([w2, z(hh, hh)], 1),
                           jnp.concatenate([z(hh, hh), w2], 1)], 0)
    two = lambda v: jnp.concatenate([v, v], axis=-1)
    w3d = w3.reshape(hh, HY_ORDER, 2, HY_D).transpose(2, 0, 1, 3).reshape(2, hh, HY_ORDER * HY_D)
    zz = jnp.zeros_like(w3d)
    w3x = jnp.stack([jnp.concatenate([w3d, zz], 1), jnp.concatenate([zz, w3d], 1)], axis=1)
    w3h = w3x.astype(BF16)
    w3l = (w3x - w3h.astype(F32)).astype(BF16)
    return dict(w1=w1b, b1=two(b1[None, :]), w2=w2b, b2=two(b2[None, :]), freq=two(freq), w3h=w3h, w3l=w3l,
                deltas=deltas_d)


DFT_LANES = 8192


def _dft_rows_kernel(f_ref, x_ref, o_ref, *, nj):
    x = jnp.concatenate([x_ref[:, jj, :] for jj in range(nj)], axis=1)
    o_ref[...] = _dot(f_ref[...], x.astype(BF16)).astype(o_ref.dtype)


def _dft_rows(fmat, x3, col, width):
    m, k = fmat.shape
    n2 = x3.shape[1]
    nj = min(DFT_LANES // width, n2)
    return pl.pallas_call(
        functools.partial(_dft_rows_kernel, nj=nj),
        grid=(n2 // nj,),
        in_specs=[pl.BlockSpec((m, k), lambda j: (0, 0)),
                  pl.BlockSpec((k, nj, width), lambda j: (0, j, col))],
        out_specs=pl.BlockSpec((m, nj * width), lambda j: (0, j)),
        out_shape=jax.ShapeDtypeStruct((m, n2 * width), BF16),
        compiler_params=_params("parallel"),
        name="dft_rows",
    )(fmat, x3)


def _spec_kernel(ar_ref, ai_ref, gr_ref, gi_ref, kr_ref, ki_ref):
    ar, ai, gr, gi = ar_ref[...], ai_ref[...], gr_ref[...], gi_ref[...]
    kr_ref[...] = _dot(gr, ar) - _dot(gi, ai)
    ki_ref[...] = _dot(gr, ai) + _dot(gi, ar)


def _filter_spectrum(a4, gr, gi):
    _, _, n2, ch = a4.shape
    nh = gr.shape[0]
    ct = 512
    blk = lambda ri: pl.BlockSpec((None, None, n2, ct), lambda f, j: (ri, f, 0, j))
    gspec = pl.BlockSpec((None, n2, n2), lambda f, j: (f, 0, 0))
    ospec = pl.BlockSpec((None, n2, ct), lambda f, j: (f, 0, j))
    return pl.pallas_call(
        _spec_kernel,
        grid=(nh, ch // ct),
        in_specs=[blk(0), blk(1), gspec, gspec],
        out_specs=[ospec, ospec],
        out_shape=[jax.ShapeDtypeStruct((nh, n2, ch), F32)] * 2,
        compiler_params=_params("parallel", "parallel"),
        name="filter_spectrum",
    )(a4, a4, gr, gi)


def _mid_kernel(ar_ref, ai_ref, gr_ref, gi_ref, grt_ref, git_ref, kr_ref, ki_ref, br_ref, bi_ref, *, nh):
    f = pl.program_id(0)

    @pl.when(f < nh)
    def _():
        ar, ai, gr, gi = ar_ref[...], ai_ref[...], gr_ref[...], gi_ref[...]
        xr = _dot(gr, ar) - _dot(gi, ai)
        xi = _dot(gr, ai) + _dot(gi, ar)
        kr, ki = kr_ref[...], ki_ref[...]
        yr = (xr * kr - xi * ki).astype(BF16)
        yi = (xr * ki + xi * kr).astype(BF16)
        grt, git = grt_ref[...], git_ref[...]
        br_ref[...] = (_dot(grt, yr) + _dot(git, yi)).astype(BF16)
        bi_ref[...] = (_dot(grt, yi) - _dot(git, yr)).astype(BF16)

    @pl.when(f >= nh)
    def _():
        br_ref[...] = jnp.zeros_like(br_ref)
        bi_ref[...] = jnp.zeros_like(bi_ref)


def _hyena_mid(a4, tabs, kf_r, kf_i, order):
    _, nf, n2, ch = a4.shape
    gr, gi, grt, git = tabs
    nh = gr.shape[0]
    fi = lambda f: jnp.minimum(f, nh - 1)
    blk = lambda ri: pl.BlockSpec((None, None, n2, ch), lambda f: (ri, fi(f), 0, 0))
    kspec = pl.BlockSpec((None, n2, ch), lambda f: (fi(f), 0, order))
    gspec = pl.BlockSpec((None, n2, n2), lambda f: (fi(f), 0, 0))
    ospec = pl.BlockSpec((None, n2, ch), lambda f: (f, 0, 0))
    return pl.pallas_call(
        functools.partial(_mid_kernel, nh=nh),
        grid=(nf,),
        in_specs=[blk(0), blk(1), gspec, gspec, gspec, gspec, kspec, kspec],
        out_specs=[ospec, ospec],
        out_shape=[jax.ShapeDtypeStruct((nf, n2, ch), BF16)] * 2,
        compiler_params=_params("parallel"),
        name="hyena_mid",
    )(a4, a4, gr, gi, grt, git, kf_r, kf_i)


def _inv_kernel(f_ref, br_ref, bi_ref, s_ref, bias_ref, z_ref, g_ref, o_ref, *, nf, nj, ch):
    acc = _dot(f_ref[:, 0:nf], br_ref[...]) + _dot(f_ref[:, nf:2 * nf], bi_ref[...])
    for jj in range(nj):
        y = acc[:, jj * ch:(jj + 1) * ch] * s_ref[...]
        o_ref[:, jj, :] = g_ref[:, jj, :] * (y + bias_ref[...] * z_ref[:, jj, :])


def _hyena_inverse(finv, b_r, b_i, scale, bias, z3, zcol, g3, gcol):
    t1, k2 = finv.shape
    nf = k2 // 2
    n2 = z3.shape[1]
    ch = HY_D
    nj = min(DFT_LANES // ch, n2)
    col = pl.BlockSpec((nf, nj * ch), lambda j: (0, j))
    row = pl.BlockSpec((1, ch), lambda j: (0, 0))
    return pl.pallas_call(
        functools.partial(_inv_kernel, nf=nf, nj=nj, ch=ch),
        grid=(n2 // nj,),
        in_specs=[pl.BlockSpec((t1, k2), lambda j: (0, 0)), col, col, row, row,
                  pl.BlockSpec((t1, nj, ch), lambda j: (0, j, zcol)),
                  pl.BlockSpec((t1, nj, ch), lambda j: (0, j, gcol))],
        out_specs=pl.BlockSpec((t1, nj, ch), lambda j: (0, j, 0)),
        out_shape=jax.ShapeDtypeStruct((t1, n2, ch), F32),
        compiler_params=_params("parallel"),
        name="hyena_inverse",
    )(finv, b_r, b_i, scale, bias, z3, g3)


def _hyena_nf(n):
    nh = (2 * n // FFT_N2) // 2 + 1
    return -(-nh // 16) * 16


def _dft_tables(n):
    n2 = FFT_N2
    n1 = 2 * n // n2
    tot = 2 * n
    two_pi = 2.0 * math.pi

    def cs(num, den):
        ang = (two_pi / den) * (num % den).astype(F32)
        return jnp.cos(ang), jnp.sin(ang)

    nh = n1 // 2 + 1
    nf = _hyena_nf(n)
    f1 = jnp.arange(nh, dtype=jnp.int32)
    t1 = jnp.arange(n1, dtype=jnp.int32)
    c1, s1 = cs(f1[:, None] * t1[None, :], n1)
    zrow = jnp.zeros((nf - nh, n1), F32)
    fwd_full = jnp.concatenate([c1, zrow, -s1, zrow], axis=0).astype(BF16)
    fwd_half = fwd_full[:, :n1 // 2]
    wgt = jnp.where((f1 == 0) | (f1 == n1 // 2), 1.0, 2.0)[:, None]
    zcol = jnp.zeros((n1 // 2, nf - nh), F32)
    inv = jnp.concatenate([(wgt * c1[:, :n1 // 2]).T, zcol, -(wgt * s1[:, :n1 // 2]).T, zcol],
                          axis=1).astype(BF16)
    t2 = jnp.arange(n2, dtype=jnp.int32)
    twr, twi = cs(f1[:, None] * t2[None, :], tot)
    fr, fi = cs(t2[:, None] * t2[None, :], n2)
    twi, fi = -twi, -fi
    gr = twr[:, None, :] * fr[None] - twi[:, None, :] * fi[None]
    gi = twr[:, None, :] * fi[None] + twi[:, None, :] * fr[None]
    tabs = (gr.astype(BF16), gi.astype(BF16),
            gr.transpose(0, 2, 1).astype(BF16), gi.transpose(0, 2, 1).astype(BF16))
    return fwd_full, fwd_half, inv, tabs


def _hyena_long(q, k2, nrm, hy_bias, tables):
    n = q.shape[0]
    n2 = FFT_N2
    n1 = 2 * n // n2
    fwd_full, fwd_half, inv, tabs = tables
    nf = _hyena_nf(n)
    oc = HY_ORDER * HY_D
    ak = _dft_rows(fwd_full, k2.reshape(n1, n2, oc), 0, oc).reshape(2, nf, n2, oc)
    kf_r, kf_i = _filter_spectrum(ak, tabs[0], tabs[1])
    q3 = q.reshape(n1 // 2, n2, 3 * HY_D)
    z3, zcol = q3, 0
    for o in range(HY_ORDER):
        a4 = _dft_rows(fwd_half, z3, zcol, HY_D).reshape(2, nf, n2, HY_D)
        b_r, b_i = _hyena_mid(a4, tabs, kf_r, kf_i, o)
        scale = 1.0 / (2.0 * n * nrm[:, o * HY_D:(o + 1) * HY_D])
        z3 = _hyena_inverse(inv, b_r.reshape(nf, n2 * HY_D), b_i.reshape(nf, n2 * HY_D), scale,
                            hy_bias[o][None, :], z3, zcol, q3, o + 1)
        zcol = 0
    return z3.reshape(n, HY_D)


def _hy_ctx_kernel(v_ref, x1_ref, x2_ref, k0_ref, k1_ref, n0_ref, n1_ref, bias_ref, o_ref, kf, zs, *, n):
    zs[...] = v_ref[...]
    for o, (k_ref, nr_ref, x_ref) in enumerate(((k0_ref, n0_ref, x1_ref), (k1_ref, n1_ref, x2_ref))):
        kf[0:n, :] = k_ref[n:2 * n, :]
        kf[n:2 * n, :] = k_ref[0:n, :]

        def body(s, acc):
            return acc + kf[pl.ds(n - s, n), :] * zs[pl.ds(s, 1), :]

        acc = lax.fori_loop(0, n, body, jnp.zeros((n, LANE), F32))
        z = zs[...]
        zs[...] = x_ref[...] * (acc / nr_ref[...] + bias_ref[o:o + 1, :] * z)
    o_ref[...] = zs[...]


def _hyena_ctx(q, k2, nrm, hy_bias):
    n = q.shape[0]
    nb = HY_D // LANE
    col = lambda c0: pl.BlockSpec((n, LANE), lambda j: (0, c0 + j))
    kcol = lambda c0: pl.BlockSpec((2 * n, LANE), lambda j: (0, c0 + j))
    ncol = lambda c0: pl.BlockSpec((1, LANE), lambda j: (0, c0 + j))
    return pl.pallas_call(
        functools.partial(_hy_ctx_kernel, n=n),
        grid=(nb,),
        in_specs=[col(0), col(nb), col(2 * nb), kcol(0), kcol(nb), ncol(0), ncol(nb),
                  pl.BlockSpec((HY_ORDER, LANE), lambda j: (0, j))],
        out_specs=pl.BlockSpec((n, LANE), lambda j: (0, j)),
        out_shape=jax.ShapeDtypeStruct((n, HY_D), F32),
        scratch_shapes=[pltpu.VMEM((2 * n, LANE), F32), pltpu.VMEM((n, LANE), F32)],
        compiler_params=_params("parallel"),
        name="hyena_ctx",
    )(q, q, q, k2, k2, nrm, nrm, hy_bias)


def _merge_kernel(ya_ref, xs_ref, yf_ref, yb_ref, z_ref, yc_ref, yd_ref, g_ref, h_ref,
                  dv_ref, ng_ref, g1_ref, lg_ref, lb_ref,
                  wa_ref, wb_ref, wc_ref, wd_ref, wo_ref, o_ref):
    y = xs_ref[...] * dv_ref[...] + yf_ref[...] + yb_ref[...]
    gz = y * _silu(z_ref[...])
    ssd = gz * lax.rsqrt(jnp.mean(gz * gz, -1, keepdims=True) + LN_EPS) * ng_ref[...]
    d = D_MODEL
    m = jax.nn.sigmoid(g_ref[:, 0:d]) * _dot(ya_ref[...].astype(BF16), wa_ref[...])
    m = m + jax.nn.sigmoid(g_ref[:, d:2 * d]) * _dot(ssd.astype(BF16), wb_ref[...])
    m = m + jax.nn.sigmoid(g_ref[:, 2 * d:3 * d]) * _dot(yc_ref[...].astype(BF16), wc_ref[...])
    m = m + jax.nn.sigmoid(g_ref[:, 3 * d:4 * d]) * _dot(yd_ref[...].astype(BF16), wd_ref[...])
    mix = _dot(m.astype(BF16), wo_ref[...])
    o_ref[...] = _layer_norm(DN_ALPHA * h_ref[...] + g1_ref[...] * mix, lg_ref[...], lb_ref[...])


def _merge(ya, xbc, ydir, p, yc, yd, h, dvec, ng, gate1, lg, lb, wa, wb, wc, wd, wo):
    n = h.shape[0]
    t = 256
    tok = lambda w, col=0: pl.BlockSpec((t, w), lambda i: (i, col))
    vec = lambda w: pl.BlockSpec((1, w), lambda i: (0, 0))
    mat = lambda r: pl.BlockSpec((r, D_MODEL), lambda i: (0, 0))
    return pl.pallas_call(
        _merge_kernel,
        grid=(n // t,),
        in_specs=[tok(CONF_D), tok(SSD_D),
                  pl.BlockSpec((None, t, SSD_D), lambda i: (0, i, 0)),
                  pl.BlockSpec((None, t, SSD_D), lambda i: (1, i, 0)),
                  tok(SSD_D, PZ // SSD_D), tok(HY_D), tok(SC_D), tok(N_BRANCH * D_MODEL, 0), tok(D_MODEL),
                  vec(SSD_D), vec(SSD_D), vec(D_MODEL), vec(D_MODEL), vec(D_MODEL),
                  mat(CONF_D), mat(SSD_D), mat(HY_D), mat(SC_D), mat(D_MODEL)],
        out_specs=tok(D_MODEL),
        out_shape=jax.ShapeDtypeStruct((n, D_MODEL), F32),
        compiler_params=_params("parallel"),
        name="merge",
    )(ya, xbc, ydir, ydir, p, yc, yd, p, h, dvec, ng, gate1, lg, lb, wa, wb, wc, wd, wo)


def _router_kernel(h_ref, sh_ref, sc_ref, w_ref, b_ref, sel_ref, cnt_ref, selt_ref):
    @pl.when(pl.program_id(0) == 0)
    def _():
        cnt_ref[...] = jnp.zeros_like(cnt_ref)

    u = h_ref[...] * (1.0 + sc_ref[...]) + sh_ref[...]
    lg = jnp.dot(u, w_ref[...], precision=HIGHEST, preferred_element_type=F32) + b_ref[...]
    lane = lax.broadcasted_iota(jnp.int32, lg.shape, 1).astype(F32)
    neg = -1e30
    big = 1e9
    gl = jnp.where(lane < MOE_GROUPS, lg, neg)
    gmax = jnp.max(gl, -1, keepdims=True)
    gsel = jnp.min(jnp.where(gl == gmax, lane, big), -1, keepdims=True)
    gprob = 1.0 / jnp.sum(jnp.where(lane < MOE_GROUPS, jnp.exp(lg - gmax), 0.0), -1, keepdims=True)
    lo = MOE_GROUPS + gsel * MOE_EPG
    el = jnp.where(jnp.abs(lane - lo - (MOE_EPG - 1) / 2.0) < MOE_EPG / 2.0, lg, neg)
    m1 = jnp.max(el, -1, keepdims=True)
    i1 = jnp.min(jnp.where(el == m1, lane, big), -1, keepdims=True)
    el2 = jnp.where(lane == i1, neg, el)
    m2 = jnp.max(el2, -1, keepdims=True)
    i2 = jnp.min(jnp.where(el2 == m2, lane, big), -1, keepdims=True)
    t = jnp.exp(m2 - m1)
    w1 = gprob / (1.0 + t)
    w2 = gprob * t / (1.0 + t)
    oh1 = jnp.where(lane == i1, 1.0, 0.0)
    oh2 = jnp.where(lane == i2, 1.0, 0.0)
    oh = oh1 + oh2
    tt = lg.shape[0]
    li = lax.broadcasted_iota(jnp.int32, (tt, tt), 0)
    si = lax.broadcasted_iota(jnp.int32, (tt, tt), 1)
    before = _dot(jnp.where(li > si, 1.0, 0.0).astype(BF16), oh.astype(BF16)) + cnt_ref[...]
    r1 = jnp.sum(oh1 * before, -1, keepdims=True)
    r2 = jnp.sum(oh2 * before, -1, keepdims=True)
    cnt_ref[...] += jnp.sum(oh, axis=0, keepdims=True)
    cols = (i1 - MOE_GROUPS, i2 - MOE_GROUPS, w1, w2, r1, r2)
    sel = jnp.zeros_like(lg)
    for k, v in enumerate(cols):
        sel = jnp.where(lane == k, v, sel)
    sel_ref[...] = sel
    selt_ref[...] = sel.T[0:SUBLANE, :]


def _router(h, shift, scale, wr, br):
    n = h.shape[0]
    t = 256
    tok = lambda w: pl.BlockSpec((t, w), lambda i: (i, 0))
    vec = lambda w: pl.BlockSpec((1, w), lambda i: (0, 0))
    return pl.pallas_call(
        _router_kernel,
        grid=(n // t,),
        in_specs=[tok(D_MODEL), vec(D_MODEL), vec(D_MODEL),
                  pl.BlockSpec((D_MODEL, LANE), lambda i: (0, 0)), vec(LANE)],
        out_specs=[tok(LANE), vec(LANE), pl.BlockSpec((SUBLANE, t), lambda i: (0, i))],
        out_shape=[jax.ShapeDtypeStruct((n, LANE), F32), jax.ShapeDtypeStruct((1, LANE), F32),
                   jax.ShapeDtypeStruct((SUBLANE, n), F32)],
        compiler_params=_params("arbitrary"),
        name="router",
    )(h, shift, scale, wr, br)


MOE_T = 256


def _dispatch_kernel(dst_ref, h_ref, sh_ref, sc_ref, zero_hbm, xin_hbm, ubuf, sem, *, nt):
    del zero_hbm
    t = MOE_T
    i = pl.program_id(0)
    slot = i % 2

    def wait_slot(s):
        for _ in range(MOE_TOP_K):
            pltpu.make_async_copy(ubuf.at[s], xin_hbm.at[pl.ds(0, t), :], sem.at[s]).wait()

    @pl.when(i >= 2)
    def _():
        wait_slot(slot)

    ubuf[slot] = h_ref[...] * (1.0 + sc_ref[...]) + sh_ref[...]

    def issue(r, carry):
        for k in range(MOE_TOP_K):
            pltpu.make_async_copy(ubuf.at[slot, pl.ds(r, 1), :],
                                  xin_hbm.at[pl.ds(dst_ref[0, 0, k * t + r], 1), :], sem.at[slot]).start()
        return carry

    lax.fori_loop(0, t, issue, 0, unroll=8)

    @pl.when(i == nt - 1)
    def _():
        wait_slot(slot)
        if nt > 1:
            wait_slot(1 - slot)


def _dispatch(pos_t, h, shift, scale, n_rows):
    n = h.shape[0]
    t = MOE_T
    nt = n // t
    vec = pl.BlockSpec((1, D_MODEL), lambda i: (0, 0))
    return pl.pallas_call(
        functools.partial(_dispatch_kernel, nt=nt),
        grid=(nt,),
        in_specs=[pl.BlockSpec((1, 1, MOE_TOP_K * t), lambda i: (i, 0, 0), memory_space=pltpu.SMEM),
                  pl.BlockSpec((t, D_MODEL), lambda i: (i, 0)), vec, vec,
                  pl.BlockSpec(memory_space=pl.ANY)],
        out_specs=pl.BlockSpec(memory_space=pl.ANY),
        out_shape=jax.ShapeDtypeStruct((n_rows, D_MODEL), F32),
        scratch_shapes=[pltpu.VMEM((2, t, D_MODEL), F32), pltpu.SemaphoreType.DMA((2,))],
        input_output_aliases={4: 0},
        compiler_params=_params("arbitrary"),
        name="dispatch",
    )(pos_t, h, shift, scale, jnp.zeros((n_rows, D_MODEL), F32))


def _expert_kernel(be_ref, nu_ref, x_ref, wg_ref, wu_ref, wd_ref, o_ref, wgb, wub, wdb):
    b = pl.program_id(0)

    @pl.when((b == 0) | (be_ref[b] != be_ref[jnp.maximum(b - 1, 0)]))
    def _():
        wgb[...] = wg_ref[...].astype(BF16)
        wub[...] = wu_ref[...].astype(BF16)
        wdb[...] = wd_ref[...].astype(BF16)

    @pl.when(b < nu_ref[0])
    def _():
        x = x_ref[...].astype(BF16)
        hid = _silu(_dot(x, wgb[...])) * _dot(x, wub[...])
        o_ref[...] = _dot(hid.astype(BF16), wdb[...])

    @pl.when(b >= nu_ref[0])
    def _():
        o_ref[...] = jnp.zeros_like(o_ref)


def _experts(xin, block_e, n_used, wg, wu, wd):
    n_blocks = block_e.shape[0]
    gs = pltpu.PrefetchScalarGridSpec(
        num_scalar_prefetch=2,
        grid=(n_blocks,),
        in_specs=[pl.BlockSpec((MOE_ROWS, D_MODEL), lambda b, be, nu: (b, 0)),
                  pl.BlockSpec((None, D_MODEL, MOE_FF), lambda b, be, nu: (be[b], 0, 0)),
                  pl.BlockSpec((None, D_MODEL, MOE_FF), lambda b, be, nu: (be[b], 0, 0)),
                  pl.BlockSpec((None, MOE_FF, D_MODEL), lambda b, be, nu: (be[b], 0, 0))],
        out_specs=pl.BlockSpec((MOE_ROWS, D_MODEL), lambda b, be, nu: (b, 0)),
        scratch_shapes=[pltpu.VMEM((D_MODEL, MOE_FF), BF16), pltpu.VMEM((D_MODEL, MOE_FF), BF16),
                        pltpu.VMEM((MOE_FF, D_MODEL), BF16)],
    )
    return pl.pallas_call(
        _expert_kernel,
        grid_spec=gs,
        out_shape=jax.ShapeDtypeStruct((n_blocks * MOE_ROWS, D_MODEL), F32),
        compiler_params=_params("arbitrary"),
        name="experts",
    )(block_e, n_used, xin, wg, wu, wd)


def _combine_kernel(pos_ref, posn_ref, y_hbm, h_ref, sel_ref, g2_ref, lg_ref, lb_ref, o_ref, ybuf, sem, *, nt):
    t = MOE_T
    i = pl.program_id(0)
    slot = i % 2

    def gather(p_ref, s):
        def issue(r, carry):
            pltpu.make_async_copy(y_hbm.at[pl.ds(p_ref[0, 0, r], 1), :], ybuf.at[s, pl.ds(r, 1), :],
                                  sem.at[s]).start()
            return carry
        lax.fori_loop(0, MOE_TOP_K * t, issue, 0, unroll=8)

    @pl.when(i == 0)
    def _():
        gather(pos_ref, 0)

    @pl.when(i + 1 < nt)
    def _():
        gather(posn_ref, 1 - slot)

    pltpu.make_async_copy(y_hbm.at[pl.ds(0, MOE_TOP_K * t), :], ybuf.at[slot], sem.at[slot]).wait()
    ffn = sel_ref[:, 2:3] * ybuf[slot, 0:t, :] + sel_ref[:, 3:4] * ybuf[slot, t:2 * t, :]
    o_ref[...] = _layer_norm(DN_ALPHA * h_ref[...] + g2_ref[...] * ffn, lg_ref[...], lb_ref[...])


def _combine(y, pos_t, h, sel, gate2, lg, lb):
    n = h.shape[0]
    t = MOE_T
    nt = n // t
    tok = lambda w: pl.BlockSpec((t, w), lambda i: (i, 0))
    vec = pl.BlockSpec((1, D_MODEL), lambda i: (0, 0))
    return pl.pallas_call(
        functools.partial(_combine_kernel, nt=nt),
        grid=(nt,),
        in_specs=[pl.BlockSpec((1, 1, MOE_TOP_K * t), lambda i: (i, 0, 0), memory_space=pltpu.SMEM),
                  pl.BlockSpec((1, 1, MOE_TOP_K * t), lambda i: (jnp.minimum(i + 1, nt - 1), 0, 0),
                               memory_space=pltpu.SMEM),
                  pl.BlockSpec(memory_space=pl.ANY),
                  tok(D_MODEL), tok(LANE), vec, vec, vec],
        out_specs=tok(D_MODEL),
        out_shape=jax.ShapeDtypeStruct((n, D_MODEL), F32),
        scratch_shapes=[pltpu.VMEM((2, MOE_TOP_K * t, D_MODEL), F32), pltpu.SemaphoreType.DMA((2,))],
        compiler_params=_params("arbitrary"),
        name="combine",
    )(pos_t, pos_t, y, h, sel, gate2, lg, lb)


def _moe(h, shift, scale, gate2, lg, lb, wr, br, wg, wu, wd):
    n = h.shape[0]
    t = MOE_T
    nt = n // t
    sel, cnt, selt = _router(h, shift, scale, wr, br)
    counts = cnt[0, MOE_GROUPS:MOE_GROUPS + MOE_EXPERTS].astype(jnp.int32)
    padded = (counts + MOE_ROWS - 1) // MOE_ROWS * MOE_ROWS
    pad_end = jnp.cumsum(padded)
    pad_start = pad_end - padded
    n_blocks = (n * MOE_TOP_K + MOE_EXPERTS * (MOE_ROWS - 1) + MOE_ROWS - 1) // MOE_ROWS
    blk_row = jnp.arange(n_blocks, dtype=jnp.int32) * MOE_ROWS
    block_e = jnp.minimum(jnp.sum((blk_row[:, None] >= pad_end[None, :]).astype(jnp.int32), axis=1),
                          MOE_EXPERTS - 1)
    n_used = (pad_end[-1:] // MOE_ROWS).astype(jnp.int32)
    e_kt = selt[0:MOE_TOP_K].astype(jnp.int32)
    ids = jnp.arange(MOE_EXPERTS, dtype=jnp.int32)[None, :, None]
    start_kt = jnp.sum(jnp.where(e_kt[:, None, :] == ids, pad_start[None, :, None], 0), axis=1)
    pos_kt = start_kt + selt[4:4 + MOE_TOP_K].astype(jnp.int32)
    pos_t = pos_kt.reshape(MOE_TOP_K, nt, t).transpose(1, 0, 2).reshape(nt, 1, MOE_TOP_K * t)
    xin = _dispatch(pos_t, h, shift, scale, n_blocks * MOE_ROWS)
    y = _experts(xin, block_e, n_used, wg, wu, wd)
    return _combine(y, pos_t, h, sel, gate2, lg, lb)


def _mixer(h, mod, lw, ssd_init, tables, *, latent, need_mix):
    n = h.shape[0]
    p = _inproj(h, mod[0], mod[1], lw["w_in"])
    xbc = _conv3(p, PX, SSD_XBC, lw["ssd_conv_w"], lw["ssd_conv_b"], silu=True)
    ydir, finals = _ssd_scan(xbc, p, lw["ssd_dt_bias"], lw["ssd_a_log"], ssd_init)
    if not need_mix:
        return None, finals
    ya = _conformer(p, lw["conf_dw_w"], lw["conf_dw_b"], lw["conf_ln_g"], lw["conf_ln_b"],
                    dil=GRID_W if latent else 1)
    q = _conv3(p, PC, 3 * HY_D, lw["hy_short_w"], lw["hy_short_b"], silu=False)
    k2, nrm = _hyena_filters(lw["featx_lat" if latent else "featx_ctx"], lw["hy_filter"], n)
    if latent:
        yc = _hyena_long(q, k2, nrm, lw["hy_bias"], tables)
    else:
        yc = _hyena_ctx(q, k2, nrm, lw["hy_bias"])
    yd = _gated_conv(p, lw["sc_conv_w"])
    h = _merge(ya, xbc, ydir, p, yc, yd, h, lw["ssd_dvec"], lw["ssd_norm_g"], mod[2], lw["ln_g0"], lw["ln_b0"],
               lw["w_branch_a"], lw["w_branch_b"], lw["w_branch_c"], lw["w_branch_d"], lw["w_out"])
    return h, finals


def _positional_features(n):
    t01 = jnp.linspace(0.0, 1.0, n, dtype=F32)[:, None]
    omega = (2.0 * math.pi / n) * jnp.arange(n, dtype=F32)[:, None]
    bands = jnp.linspace(1e-4, HY_BANDS - 1, HY_BANDS, dtype=F32)
    feat = jnp.concatenate([t01, jnp.cos(bands * omega), -jnp.sin(bands * omega)], axis=-1)
    featx = jnp.concatenate([feat, jnp.zeros((1, HY_EMB), F32), jnp.flip(feat[1:], axis=0)], axis=0)
    return _pad_lanes(featx)


def _relayout_w_in(w_in):
    seg = lambda a, b: w_in[:, :, a:b]
    ob = OFF_B
    parts = [seg(OFF_G, OFF_G + N_BRANCH * D_MODEL),
             seg(OFF_A, OFF_B),
             seg(OFF_C, OFF_D),
             seg(OFF_D, OFF_G),
             seg(ob + SSD_D, ob + SSD_D + SSD_XBC),
             seg(ob, ob + SSD_D),
             seg(ob + SSD_D + SSD_XBC, OFF_C),
             jnp.zeros(w_in.shape[:2] + (NP - PDT - 2 * SSD_HEADS,), w_in.dtype)]
    return jnp.concatenate(parts, axis=-1).astype(BF16)


def _pad_lanes(v):
    return jnp.pad(v, ((0, 0), (0, LANE - v.shape[-1])))


def kernel(x, c, ctx, c_ctx, w_mod, b_mod, ln_g, ln_b, w_in, conf_dw_w, conf_dw_b, conf_ln_g, conf_ln_b,
           ssd_conv_w, ssd_conv_b, ssd_a_log, ssd_dt_bias, ssd_d, ssd_norm_g, hy_short_w, hy_short_b,
           hy_w1, hy_b1, hy_w2, hy_b2, hy_freq, hy_w3, hy_bias, sc_conv_w, w_branch_a, w_branch_b,
           w_branch_c, w_branch_d, w_out, rt_group_w, rt_group_b, rt_expert_w, rt_expert_b,
           ex_w_gate, ex_w_up, ex_w_down):
    assert x.shape[0] == 1 and ctx.shape[0] == 1
    n_lat, n_ctx = x.shape[1], ctx.shape[1]
    depth = w_in.shape[0]

    cv = jnp.concatenate([c, c_ctx[None, :], jnp.zeros((SUBLANE - 2, D_MODEL), F32)], axis=0)
    mods = _mod_vectors(cv, w_mod, b_mod)
    w_in_p = _relayout_w_in(w_in)
    tables = _dft_tables(n_lat)
    featx_lat = _positional_features(n_lat)
    featx_ctx = _positional_features(n_ctx)
    deltas = jnp.abs(jnp.linspace(HY_MIN_DECAY, HY_MAX_DECAY, HY_N_FILT, dtype=F32))
    deltas_d = deltas.reshape(HY_ORDER, 2, HY_D).transpose(1, 0, 2).reshape(2, 1, HY_ORDER * HY_D)
    router_w = jnp.concatenate([rt_group_w, rt_expert_w,
                                jnp.zeros((depth, D_MODEL, LANE - MOE_GROUPS - MOE_EXPERTS), F32)], axis=-1)
    router_b = jnp.concatenate([rt_group_b, rt_expert_b,
                                jnp.zeros((depth, LANE - MOE_GROUPS - MOE_EXPERTS), F32)], axis=-1)
    ssd_zero = jnp.zeros((2, SSD_HEADS, SSD_STATE, SSD_HEAD_DIM), F32)

    h_lat, h_ctx = x[0], ctx[0]
    for l in range(depth):
        row = lambda v: v[None, :]
        lw = dict(
            w_in=w_in_p[l], conf_dw_w=conf_dw_w[l], conf_dw_b=row(conf_dw_b[l]), conf_ln_g=row(conf_ln_g[l]),
            conf_ln_b=row(conf_ln_b[l]), ssd_conv_w=ssd_conv_w[l], ssd_conv_b=row(ssd_conv_b[l]),
            ssd_a_log=_pad_lanes(ssd_a_log[l].reshape(1, -1)), ssd_dt_bias=_pad_lanes(ssd_dt_bias[l].reshape(1, -1)),
            ssd_dvec=row(jnp.repeat(ssd_d[l], SSD_HEAD_DIM)), ssd_norm_g=row(ssd_norm_g[l]),
            hy_short_w=hy_short_w[l], hy_short_b=row(hy_short_b[l]),
            hy_filter=_filter_weights(hy_w1[l], hy_b1[l], hy_w2[l], hy_b2[l], hy_freq[l], hy_w3[l], deltas_d),
            hy_bias=hy_bias[l], featx_lat=featx_lat, featx_ctx=featx_ctx,
            sc_conv_w=sc_conv_w[l], ln_g0=row(ln_g[l, 0]), ln_b0=row(ln_b[l, 0]),
            w_branch_a=w_branch_a[l].astype(BF16), w_branch_b=w_branch_b[l].astype(BF16),
            w_branch_c=w_branch_c[l].astype(BF16), w_branch_d=w_branch_d[l].astype(BF16),
            w_out=w_out[l].astype(BF16))
        moe_w = (router_w[l], row(router_b[l]), ex_w_gate[l], ex_w_up[l], ex_w_down[l])
        last = l == depth - 1
        d = D_MODEL
        mod_lat = [mods[l, 0:1, k * d:(k + 1) * d] for k in range(6)]
        mod_ctx = [mods[l, 1:2, k * d:(k + 1) * d] for k in range(6)]

        mix_ctx, ctx_states = _mixer(h_ctx, mod_ctx, lw, ssd_zero, None, latent=False, need_mix=not last)
        h_lat, _ = _mixer(h_lat, mod_lat, lw, ctx_states, tables, latent=True, need_mix=True)
        h_lat = _moe(h_lat, mod_lat[3], mod_lat[4], mod_lat[5], row(ln_g[l, 1]), row(ln_b[l, 1]), *moe_w)
        if not last:
            h_ctx = _moe(mix_ctx, mod_ctx[3], mod_ctx[4], mod_ctx[5], row(ln_g[l, 1]), row(ln_b[l, 1]), *moe_w)
    return h_lat[None]
```

```python
import functools
import math

import jax
import jax.numpy as jnp
from jax import lax
from jax.experimental import pallas as pl
from jax.experimental.pallas import tpu as pltpu

F32 = jnp.float32
BF16 = jnp.bfloat16
HIGHEST = lax.Precision.HIGHEST

D_MODEL = 1024
DEPTH = 4
GRID_W = 64
CONF_D = 512
CONF_K = 31
SSD_D = 768
SSD_HEADS = 12
SSD_HEAD_DIM = 64
SSD_GROUPS = 4
SSD_HPG = SSD_HEADS // SSD_GROUPS
SSD_STATE = 128
SSD_CHUNK = 128
SSD_BC = SSD_GROUPS * SSD_STATE
SSD_XBC = SSD_D + 2 * SSD_BC
SSD_PROJ = SSD_D + SSD_XBC + 2 * SSD_HEADS
HY_D = 512
HY_ORDER = 2
HY_EMB = 33
HY_BANDS = (HY_EMB - 1) // 2
HY_HID = 64
HY_N_FILT = HY_ORDER * 2 * HY_D
HY_MIN_DECAY = math.log(1e-2) / 1.5
HY_MAX_DECAY = math.log(1e-2) / 0.3
SC_D = 512
N_BRANCH = 4
OFF_A = 0
OFF_B = OFF_A + 2 * CONF_D
OFF_C = OFF_B + SSD_PROJ
OFF_D = OFF_C + 3 * HY_D
OFF_G = OFF_D + 3 * SC_D
MOE_GROUPS = 4
MOE_EPG = 8
MOE_EXPERTS = MOE_GROUPS * MOE_EPG
MOE_TOP_K = 2
MOE_FF = 512
DN_ALPHA = (2 * DEPTH) ** 0.25
LN_EPS = 1e-5

PG = 0
PA = PG + N_BRANCH * D_MODEL
PC = PA + 2 * CONF_D
PD = PC + 3 * HY_D
PX = PD + 3 * SC_D
PZ = PX + SSD_XBC
PDT = PZ + SSD_D
INPROJ_TN = 1024
NP = -(-(PDT + 128) // INPROJ_TN) * INPROJ_TN

LANE = 128
SUBLANE = 8
FFT_N2 = 256
MOE_ROWS = 256
VMEM_LIMIT = 48 * 1024 * 1024


def _params(*sem):
    return pltpu.CompilerParams(dimension_semantics=sem, vmem_limit_bytes=VMEM_LIMIT)


def _silu(x):
    return x * jax.nn.sigmoid(x)


def _layer_norm(x, g, b):
    mu = jnp.mean(x, -1, keepdims=True)
    xc = x - mu
    var = jnp.mean(xc * xc, -1, keepdims=True)
    return xc * lax.rsqrt(var + LN_EPS) * g + b


def _dot(a, b):
    return jnp.dot(a, b, preferred_element_type=F32)


def _mod_kernel(cv_ref, w_ref, b_ref, o_ref):
    o_ref[...] = jnp.dot(_silu(cv_ref[...]), w_ref[...], precision=HIGHEST,
                         preferred_element_type=F32) + b_ref[...]


def _mod_vectors(cv, w_mod, b_mod):
    tn = 1536
    return pl.pallas_call(
        _mod_kernel,
        grid=(DEPTH, 6 * D_MODEL // tn),
        in_specs=[pl.BlockSpec((SUBLANE, D_MODEL), lambda l, j: (0, 0)),
                  pl.BlockSpec((None, D_MODEL, tn), lambda l, j: (l, 0, j)),
                  pl.BlockSpec((None, 1, tn), lambda l, j: (l, 0, j))],
        out_specs=pl.BlockSpec((None, SUBLANE, tn), lambda l, j: (l, 0, j)),
        out_shape=jax.ShapeDtypeStruct((DEPTH, SUBLANE, 6 * D_MODEL), F32),
        compiler_params=_params("parallel", "parallel"),
        name="mod_vectors",
    )(cv, w_mod, b_mod.reshape(DEPTH, 1, 6 * D_MODEL))


def _inproj_kernel(x_ref, sh_ref, sc_ref, w_ref, o_ref, dt_ref, xb_ref, *, nj):
    j = pl.program_id(1)

    @pl.when(j == 0)
    def _():
        xb_ref[...] = (x_ref[...] * (1.0 + sc_ref[...]) + sh_ref[...]).astype(BF16)

    res = _dot(xb_ref[...], w_ref[...])
    o_ref[...] = res.astype(BF16)

    @pl.when(j == nj - 1)
    def _():
        off = PDT - (nj - 1) * INPROJ_TN
        dt_ref[...] = res[:, off:off + LANE]


def _inproj(h, shift, scale, w, layer):
    n = h.shape[0]
    tm = min(n, 1024)
    tn = INPROJ_TN
    nj = NP // tn
    assert PDT >= (nj - 1) * tn
    return pl.pallas_call(
        functools.partial(_inproj_kernel, nj=nj),
        grid=(n // tm, nj),
        in_specs=[pl.BlockSpec((tm, D_MODEL), lambda i, j: (i, 0)),
                  pl.BlockSpec((1, D_MODEL), lambda i, j: (0, 0)),
                  pl.BlockSpec((1, D_MODEL), lambda i, j: (0, 0)),
                  pl.BlockSpec((None, D_MODEL, tn), lambda i, j: (layer, 0, j))],
        out_specs=[pl.BlockSpec((tm, tn), lambda i, j: (i, j)),
                   pl.BlockSpec((tm, LANE), lambda i, j: (i, 0))],
        out_shape=[jax.ShapeDtypeStruct((n, NP), BF16), jax.ShapeDtypeStruct((n, LANE), F32)],
        scratch_shapes=[pltpu.VMEM((tm, D_MODEL), BF16)],
        compiler_params=_params("parallel", "arbitrary"),
        name="inproj",
    )(h, shift, scale, w)


def _shifted(x, prev_row, next_row):
    t = x.shape[0]
    row = lax.broadcasted_iota(jnp.int32, x.shape, 0)
    xm = jnp.where(row == 0, prev_row, pltpu.roll(x, 1, 0))
    xp = jnp.where(row == t - 1, next_row, pltpu.roll(x, t - 1, 0))
    return xm, xp


HALO_ROWS = 16


def _conv3_kernel(cur_ref, prev_ref, next_ref, w_ref, b_ref, o_ref, *, silu, nt):
    i = pl.program_id(0)
    x = cur_ref[...].astype(F32)
    pv = jnp.where(i > 0, prev_ref[HALO_ROWS - 1:HALO_ROWS, :].astype(F32), 0.0)
    nx = jnp.where(i < nt - 1, next_ref[0:1, :].astype(F32), 0.0)
    xm, xp = _shifted(x, pv, nx)
    y = w_ref[0:1, :] * xm + w_ref[1:2, :] * x + w_ref[2:3, :] * xp + b_ref[...]
    o_ref[...] = _silu(y) if silu else y


def _halo_specs(t, ct, n, col0):
    rb = t // HALO_ROWS
    last = n // HALO_ROWS - 1
    return [pl.BlockSpec((t, ct), lambda i, j: (i, col0 + j)),
            pl.BlockSpec((HALO_ROWS, ct), lambda i, j: (jnp.maximum(i * rb - 1, 0), col0 + j)),
            pl.BlockSpec((HALO_ROWS, ct), lambda i, j: (jnp.minimum((i + 1) * rb, last), col0 + j))]


def _conv3(p, col, width, w, b, *, silu):
    n = p.shape[0]
    t = min(n, 1024)
    ct = 256
    nt = n // t
    return pl.pallas_call(
        functools.partial(_conv3_kernel, silu=silu, nt=nt),
        grid=(nt, width // ct),
        in_specs=_halo_specs(t, ct, n, col // ct) + [
            pl.BlockSpec((3, ct), lambda i, j: (0, j)),
            pl.BlockSpec((1, ct), lambda i, j: (0, j))],
        out_specs=pl.BlockSpec((t, ct), lambda i, j: (i, j)),
        out_shape=jax.ShapeDtypeStruct((n, width), F32),
        compiler_params=_params("parallel", "parallel"),
        name="conv3",
    )(p, p, p, w, b)


def _gconv_kernel(bg_ref, cc_ref, cp_ref, cn_ref, xc_ref, xp_ref, xn_ref, w_ref, o_ref, *, nt):
    i = pl.program_id(0)
    f = lambda v: v.astype(F32)
    last = slice(HALO_ROWS - 1, HALO_ROWS)
    x = f(cc_ref[...]) * f(xc_ref[...])
    pv = jnp.where(i > 0, f(cp_ref[last, :]) * f(xp_ref[last, :]), 0.0)
    nx = jnp.where(i < nt - 1, f(cn_ref[0:1, :]) * f(xn_ref[0:1, :]), 0.0)
    xm, xp = _shifted(x, pv, nx)
    o_ref[...] = f(bg_ref[...]) * (w_ref[0:1, :] * xm + w_ref[1:2, :] * x + w_ref[2:3, :] * xp)


def _gated_conv(p, w):
    n = p.shape[0]
    t = min(n, 1024)
    ct = 256
    nt = n // t
    nb = SC_D // ct
    return pl.pallas_call(
        functools.partial(_gconv_kernel, nt=nt),
        grid=(nt, nb),
        in_specs=([pl.BlockSpec((t, ct), lambda i, j: (i, PD // ct + j))]
                  + _halo_specs(t, ct, n, PD // ct + nb)
                  + _halo_specs(t, ct, n, PD // ct + 2 * nb)
                  + [pl.BlockSpec((3, ct), lambda i, j: (0, j))]),
        out_specs=pl.BlockSpec((t, ct), lambda i, j: (i, j)),
        out_shape=jax.ShapeDtypeStruct((n, SC_D), F32),
        compiler_params=_params("parallel", "parallel"),
        name="gated_conv",
    )(p, p, p, p, p, p, p, w)


CONF_RB = 64


def _conf_kernel(vc, gc, vp, gp, vn, gn, w_ref, b_ref, lg_ref, lb_ref, o_ref, buf, *, t, halo, dil, nt):
    i = pl.program_id(0)
    glu = lambda v, g: v.astype(F32) * jax.nn.sigmoid(g.astype(F32))
    buf[halo:halo + t, :] = glu(vc[...], gc[...])
    buf[0:halo, :] = jnp.where(i > 0, glu(vp[t - halo:t, :], gp[t - halo:t, :]), 0.0)
    buf[halo + t:halo + t + halo, :] = jnp.where(i < nt - 1, glu(vn[0:halo, :], gn[0:halo, :]), 0.0)

    def block(r0):
        acc = jnp.zeros((CONF_RB, CONF_D), F32)
        for j in range(CONF_K):
            off = halo + (j - CONF_K // 2) * dil
            acc = acc + w_ref[j:j + 1, :] * buf[pl.ds(r0 + off, CONF_RB), :]
        v = _layer_norm(acc + b_ref[...], lg_ref[...], lb_ref[...])
        o_ref[pl.ds(r0, CONF_RB), :] = _silu(v)

    if dil % CONF_RB == 0:
        def body(rb, carry):
            block(pl.multiple_of(rb * CONF_RB, CONF_RB))
            return carry
        lax.fori_loop(0, t // CONF_RB, body, 0)
    else:
        for rb in range(t // CONF_RB):
            block(rb * CONF_RB)


def _conformer(p, w, b, lg, lb, *, dil):
    n = p.shape[0]
    t = min(n, 1024)
    nt = n // t
    halo = -(-(CONF_K // 2) * dil // SUBLANE) * SUBLANE
    assert halo <= t
    cb = PA // CONF_D

    def spec(col, shift):
        return pl.BlockSpec((t, CONF_D), lambda i: (jnp.clip(i + shift, 0, nt - 1), col))

    vec = pl.BlockSpec((1, CONF_D), lambda i: (0, 0))
    return pl.pallas_call(
        functools.partial(_conf_kernel, t=t, halo=halo, dil=dil, nt=nt),
        grid=(nt,),
        in_specs=[spec(cb, 0), spec(cb + 1, 0), spec(cb, -1), spec(cb + 1, -1), spec(cb, 1), spec(cb + 1, 1),
                  pl.BlockSpec((CONF_K, CONF_D), lambda i: (0, 0)), vec, vec, vec],
        out_specs=pl.BlockSpec((t, CONF_D), lambda i: (i, 0)),
        out_shape=jax.ShapeDtypeStruct((n, CONF_D), F32),
        scratch_shapes=[pltpu.VMEM((t + 2 * halo, CONF_D), F32)],
        compiler_params=_params("parallel"),
        name="conformer",
    )(p, p, p, p, p, p, w, b, lg, lb)


SSD_STATE_SHAPE = (SSD_HEADS, SSD_STATE, SSD_HEAD_DIM)
SSD_CHUNKS_PER_STEP = 2


def _ssd_kernel(xbc_ref, dt_ref, dtb_ref, alog_ref, init_ref, y_ref, fin_ref, h_ref, *, ns, cps):
    d = pl.program_id(0)
    c = pl.program_id(1)
    q = SSD_CHUNK
    hd = SSD_HEAD_DIM

    @pl.when(c == 0)
    def _():
        h_ref[...] = init_ref[...]

    lane = lax.broadcasted_iota(jnp.int32, (q, LANE), 1)
    head = lane < SSD_HEADS
    li = lax.broadcasted_iota(jnp.int32, (q, q), 0)
    si = lax.broadcasted_iota(jnp.int32, (q, q), 1)
    mask = (li - si) * (1 - 2 * d) >= 0
    tri = mask.astype(F32)
    a_rate = -jnp.exp(alog_ref[...])

    def one_chunk(r0):
        rows = pl.ds(r0, q)
        raw = dt_ref[rows, :] + dtb_ref[...]
        dt_all = jnp.maximum(raw, 0.0) + jnp.log(1.0 + jnp.exp(-jnp.abs(raw)))
        ld_all = dt_all * a_rate
        dt_d = jnp.where(head, jnp.where(d == 0, dt_all, pltpu.roll(dt_all, LANE - SSD_HEADS, 1)), 0.0)
        ld_d = jnp.where(head, jnp.where(d == 0, ld_all, pltpu.roll(ld_all, LANE - SSD_HEADS, 1)), 0.0)
        cum = jnp.dot(tri, ld_d, precision=HIGHEST, preferred_element_type=F32)
        tot = jnp.sum(ld_d, axis=0, keepdims=True)
        cum_t = cum.T
        dt_t = dt_d.T
        w_t = (jnp.exp(tot - cum) * dt_d).T
        a_out = jnp.exp(cum)
        e_tot = jnp.exp(tot)

        for g in range(SSD_GROUPS):
            bg = xbc_ref[rows, SSD_D + g * SSD_STATE:SSD_D + (g + 1) * SSD_STATE]
            cg = xbc_ref[rows, SSD_D + SSD_BC + g * SSD_STATE:SSD_D + SSD_BC + (g + 1) * SSD_STATE]
            bg_t = bg.T
            cb = _dot(cg.astype(BF16), bg_t.astype(BF16))
            for e in range(SSD_HPG):
                hh = g * SSD_HPG + e
                diff = cum[:, hh:hh + 1] - cum_t[hh:hh + 1, :]
                dec = jnp.exp(jnp.where(mask, diff, -1e30))
                m = (cb * dec * dt_t[hh:hh + 1, :]).astype(BF16)
                xe = xbc_ref[rows, hh * hd:(hh + 1) * hd].astype(BF16)
                cs = (cg * a_out[:, hh:hh + 1]).astype(BF16)
                h_in = h_ref[hh]
                y_ref[rows, hh * hd:(hh + 1) * hd] = _dot(m, xe) + _dot(cs, h_in.astype(BF16))
                s_new = _dot((bg_t * w_t[hh:hh + 1, :]).astype(BF16), xe)
                h_ref[hh] = e_tot[:, hh:hh + 1] * h_in + s_new

    for k in range(cps):
        one_chunk(pl.multiple_of(jnp.where(d == 0, k, cps - 1 - k) * q, q))

    @pl.when(c == ns - 1)
    def _():
        fin_ref[...] = h_ref[...]


def _ssd_scan(xbc, p, dt_bias, a_log, init):
    n = xbc.shape[0]
    cps = SSD_CHUNKS_PER_STEP
    q = SSD_CHUNK * cps
    ns = n // q

    def chunk(d, c):
        return jnp.where(d == 0, c, ns - 1 - c)

    st = SSD_STATE_SHAPE
    vec = pl.BlockSpec((1, LANE), lambda d, c: (0, 0))
    return pl.pallas_call(
        functools.partial(_ssd_kernel, ns=ns, cps=cps),
        grid=(2, ns),
        in_specs=[pl.BlockSpec((q, SSD_XBC), lambda d, c: (chunk(d, c), 0)),
                  pl.BlockSpec((q, LANE), lambda d, c: (chunk(d, c), 0)),
                  vec, vec,
                  pl.BlockSpec((None,) + st, lambda d, c: (d, 0, 0, 0))],
        out_specs=[pl.BlockSpec((None, q, SSD_D), lambda d, c: (d, chunk(d, c), 0)),
                   pl.BlockSpec((None,) + st, lambda d, c: (d, 0, 0, 0))],
        out_shape=[jax.ShapeDtypeStruct((2, n, SSD_D), F32),
                   jax.ShapeDtypeStruct((2,) + st, F32)],
        scratch_shapes=[pltpu.VMEM(st, F32)],
        compiler_params=_params("arbitrary", "arbitrary"),
        name="ssd_scan",
    )(xbc, p, dt_bias, a_log, init)


def _filt_kernel(feat_ref, w1_ref, b1_ref, w2_ref, b2_ref, fr_ref, w3h_ref, w3l_ref, dl_ref, k_ref, nrm_ref, *,
                 n, t):
    i = pl.program_id(0)
    hf = t // 2
    feat = feat_ref[...]
    x = jnp.concatenate([feat[0:hf], feat[hf:t]], axis=1)
    hid = jnp.sin(fr_ref[0:1, :] * (jnp.dot(x, w1_ref[...], precision=HIGHEST,
                                            preferred_element_type=F32) + b1_ref[...]))
    hid = jnp.sin(fr_ref[1:2, :] * (jnp.dot(hid, w2_ref[...], precision=HIGHEST,
                                            preferred_element_type=F32) + b2_ref[...]))
    hi = hid.astype(BF16)
    lo = (hid - hi.astype(F32)).astype(BF16)

    @pl.when(i == 0)
    def _():
        nrm_ref[...] = jnp.zeros_like(nrm_ref)

    for half in range(2):
        wh, wl = w3h_ref[half], w3l_ref[half]
        filt = _dot(hi, wh) + _dot(lo, wh) + _dot(hi, wl)
        filt = filt * jnp.exp(-feat[half * hf:(half + 1) * hf, 0:1] * dl_ref[...])
        row = i * t + half * hf + lax.broadcasted_iota(jnp.int32, filt.shape, 0)
        filt = jnp.where(row == n, 0.0, filt)
        k_ref[half * hf:(half + 1) * hf, :] = filt
        nrm_ref[...] += jnp.sum(jnp.abs(filt), axis=0, keepdims=True)


def _hyena_filters(featx, fw, n):
    t = min(n, 512)
    half = n // t
    oc = HY_ORDER * HY_D
    full = lambda shape: pl.BlockSpec(shape, lambda i: tuple(0 for _ in shape))
    w3spec = pl.BlockSpec((None, 2, LANE, oc), lambda i: (i // half, 0, 0, 0))
    return pl.pallas_call(
        functools.partial(_filt_kernel, n=n, t=t),
        grid=(2 * n // t,),
        in_specs=[pl.BlockSpec((t, LANE), lambda i: (i, 0)),
                  full((2 * LANE, LANE)), full((1, LANE)), full((LANE, LANE)), full((1, LANE)),
                  full((2, LANE)), w3spec, w3spec,
                  pl.BlockSpec((None, 1, oc), lambda i: (i // half, 0, 0))],
        out_specs=[pl.BlockSpec((t, oc), lambda i: (i, 0)),
                   pl.BlockSpec((1, oc), lambda i: (0, 0))],
        out_shape=[jax.ShapeDtypeStruct((2 * n, oc), F32),
                   jax.ShapeDtypeStruct((1, oc), F32)],
        compiler_params=_params("arbitrary"),
        name="hyena_filters",
    )(featx, fw["w1"], fw["b1"], fw["w2"], fw["b2"], fw["freq"], fw["w3h"], fw["w3l"], fw["deltas"])


def _filter_weights(w1, b1, w2, b2, freq, w3, deltas_d):
    hh = HY_HID
    z = lambda r, c: jnp.zeros((r, c), F32)
    w1p = jnp.pad(w1, ((0, LANE - HY_EMB), (0, 0)))
    w1b = jnp.concatenate([jnp.concatenate([w1p, z(LANE, hh)], 1),
                           jnp.concatenate([z(LANE, hh), w1p], 1)], 0)
    w2b = jnp.concatenate([jnp.concatenate([w2, z(hh, hh)], 1),
                           jnp.concatenate([z(hh, hh), w2], 1)], 0)
    two = lambda v: jnp.concatenate([v, v], axis=-1)
    w3d = w3.reshape(hh, HY_ORDER, 2, HY_D).transpose(2, 0, 1, 3).reshape(2, hh, HY_ORDER * HY_D)
    zz = jnp.zeros_like(w3d)
    w3x = jnp.stack([jnp.concatenate([w3d, zz], 1), jnp.concatenate([zz, w3d], 1)], axis=1)
    w3h = w3x.astype(BF16)
    w3l = (w3x - w3h.astype(F32)).astype(BF16)
    return dict(w1=w1b, b1=two(b1[None, :]), w2=w2b, b2=two(b2[None, :]), freq=two(freq), w3h=w3h, w3l=w3l,
                deltas=deltas_d)


DFT_LANES = 8192


def _dft_rows_kernel(f_ref, x_ref, o_ref, *, nj):
    x = jnp.concatenate([x_ref[:, jj, :] for jj in range(nj)], axis=1)
    o_ref[...] = _dot(f_ref[...], x.astype(BF16)).astype(o_ref.dtype)


def _dft_rows(fmat, x3, col, width):
    m, k = fmat.shape
    n2 = x3.shape[1]
    nj = min(DFT_LANES // width, n2)
    return pl.pallas_call(
        functools.partial(_dft_rows_kernel, nj=nj),
        grid=(n2 // nj,),
        in_specs=[pl.BlockSpec((m, k), lambda j: (0, 0)),
                  pl.BlockSpec((k, nj, width), lambda j: (0, j, col))],
        out_specs=pl.BlockSpec((m, nj * width), lambda j: (0, j)),
        out_shape=jax.ShapeDtypeStruct((m, n2 * width), BF16),
        compiler_params=_params("parallel"),
        name="dft_rows",
    )(fmat, x3)


def _spec_kernel(ar_ref, ai_ref, gr_ref, gi_ref, kr_ref, ki_ref):
    ar, ai, gr, gi = ar_ref[...], ai_ref[...], gr_ref[...], gi_ref[...]
    kr_ref[...] = _dot(gr, ar) - _dot(gi, ai)
    ki_ref[...] = _dot(gr, ai) + _dot(gi, ar)


def _filter_spectrum(a4, gr, gi):
    _, _, n2, ch = a4.shape
    nh = gr.shape[0]
    ct = 512
    blk = lambda ri: pl.BlockSpec((None, None, n2, ct), lambda f, j: (ri, f, 0, j))
    gspec = pl.BlockSpec((None, n2, n2), lambda f, j: (f, 0, 0))
    ospec = pl.BlockSpec((None, n2, ct), lambda f, j: (f, 0, j))
    return pl.pallas_call(
        _spec_kernel,
        grid=(nh, ch // ct),
        in_specs=[blk(0), blk(1), gspec, gspec],
        out_specs=[ospec, ospec],
        out_shape=[jax.ShapeDtypeStruct((nh, n2, ch), F32)] * 2,
        compiler_params=_params("parallel", "parallel"),
        name="filter_spectrum",
    )(a4, a4, gr, gi)


def _mid_kernel(ar_ref, ai_ref, gr_ref, gi_ref, grt_ref, git_ref, kr_ref, ki_ref, br_ref, bi_ref, *, nh):
    f = pl.program_id(0)

    @pl.when(f < nh)
    def _():
        ar, ai, gr, gi = ar_ref[...], ai_ref[...], gr_ref[...], gi_ref[...]
        xr = _dot(gr, ar) - _dot(gi, ai)
        xi = _dot(gr, ai) + _dot(gi, ar)
        kr, ki = kr_ref[...], ki_ref[...]
        yr = (xr * kr - xi * ki).astype(BF16)
        yi = (xr * ki + xi * kr).astype(BF16)
        grt, git = grt_ref[...], git_ref[...]
        br_ref[...] = (_dot(grt, yr) + _dot(git, yi)).astype(BF16)
        bi_ref[...] = (_dot(grt, yi) - _dot(git, yr)).astype(BF16)

    @pl.when(f >= nh)
    def _():
        br_ref[...] = jnp.zeros_like(br_ref)
        bi_ref[...] = jnp.zeros_like(bi_ref)


def _hyena_mid(a4, tabs, kf_r, kf_i, order):
    _, nf, n2, ch = a4.shape
    gr, gi, grt, git = tabs
    nh = gr.shape[0]
    fi = lambda f: jnp.minimum(f, nh - 1)
    blk = lambda ri: pl.BlockSpec((None, None, n2, ch), lambda f: (ri, fi(f), 0, 0))
    kspec = pl.BlockSpec((None, n2, ch), lambda f: (fi(f), 0, order))
    gspec = pl.BlockSpec((None, n2, n2), lambda f: (fi(f), 0, 0))
    ospec = pl.BlockSpec((None, n2, ch), lambda f: (f, 0, 0))
    return pl.pallas_call(
        functools.partial(_mid_kernel, nh=nh),
        grid=(nf,),
        in_specs=[blk(0), blk(1), gspec, gspec, gspec, gspec, kspec, kspec],
        out_specs=[ospec, ospec],
        out_shape=[jax.ShapeDtypeStruct((nf, n2, ch), BF16)] * 2,
        compiler_params=_params("parallel"),
        name="hyena_mid",
    )(a4, a4, gr, gi, grt, git, kf_r, kf_i)


def _inv_kernel(f_ref, br_ref, bi_ref, s_ref, bias_ref, z_ref, g_ref, o_ref, *, nf, nj, ch):
    acc = _dot(f_ref[:, 0:nf], br_ref[...]) + _dot(f_ref[:, nf:2 * nf], bi_ref[...])
    for jj in range(nj):
        y = acc[:, jj * ch:(jj + 1) * ch] * s_ref[...]
        o_ref[:, jj, :] = g_ref[:, jj, :] * (y + bias_ref[...] * z_ref[:, jj, :])


def _hyena_inverse(finv, b_r, b_i, scale, bias, z3, zcol, g3, gcol):
    t1, k2 = finv.shape
    nf = k2 // 2
    n2 = z3.shape[1]
    ch = HY_D
    nj = min(DFT_LANES // ch, n2)
    col = pl.BlockSpec((nf, nj * ch), lambda j: (0, j))
    row = pl.BlockSpec((1, ch), lambda j: (0, 0))
    return pl.pallas_call(
        functools.partial(_inv_kernel, nf=nf, nj=nj, ch=ch),
        grid=(n2 // nj,),
        in_specs=[pl.BlockSpec((t1, k2), lambda j: (0, 0)), col, col, row, row,
                  pl.BlockSpec((t1, nj, ch), lambda j: (0, j, zcol)),
                  pl.BlockSpec((t1, nj, ch), lambda j: (0, j, gcol))],
        out_specs=pl.BlockSpec((t1, nj, ch), lambda j: (0, j, 0)),
        out_shape=jax.ShapeDtypeStruct((t1, n2, ch), F32),
        compiler_params=_params("parallel"),
        name="hyena_inverse",
    )(finv, b_r, b_i, scale, bias, z3, g3)


def _hyena_nf(n):
    nh = (2 * n // FFT_N2) // 2 + 1
    return -(-nh // 16) * 16


def _dft_tables(n):
    n2 = FFT_N2
    n1 = 2 * n // n2
    tot = 2 * n
    two_pi = 2.0 * math.pi

    def cs(num, den):
        ang = (two_pi / den) * (num % den).astype(F32)
        return jnp.cos(ang), jnp.sin(ang)

    nh = n1 // 2 + 1
    nf = _hyena_nf(n)
    f1 = jnp.arange(nh, dtype=jnp.int32)
    t1 = jnp.arange(n1, dtype=jnp.int32)
    c1, s1 = cs(f1[:, None] * t1[None, :], n1)
    zrow = jnp.zeros((nf - nh, n1), F32)
    fwd_full = jnp.concatenate([c1, zrow, -s1, zrow], axis=0).astype(BF16)
    fwd_half = fwd_full[:, :n1 // 2]
    wgt = jnp.where((f1 == 0) | (f1 == n1 // 2), 1.0, 2.0)[:, None]
    zcol = jnp.zeros((n1 // 2, nf - nh), F32)
    inv = jnp.concatenate([(wgt * c1[:, :n1 // 2]).T, zcol, -(wgt * s1[:, :n1 // 2]).T, zcol],
                          axis=1).astype(BF16)
    t2 = jnp.arange(n2, dtype=jnp.int32)
    twr, twi = cs(f1[:, None] * t2[None, :], tot)
    fr, fi = cs(t2[:, None] * t2[None, :], n2)
    twi, fi = -twi, -fi
    gr = twr[:, None, :] * fr[None] - twi[:, None, :] * fi[None]
    gi = twr[:, None, :] * fi[None] + twi[:, None, :] * fr[None]
    tabs = (gr.astype(BF16), gi.astype(BF16),
            gr.transpose(0, 2, 1).astype(BF16), gi.transpose(0, 2, 1).astype(BF16))
    return fwd_full, fwd_half, inv, tabs


def _hyena_long(q, k2, nrm, hy_bias, tables):
    n = q.shape[0]
    n2 = FFT_N2
    n1 = 2 * n // n2
    fwd_full, fwd_half, inv, tabs = tables
    nf = _hyena_nf(n)
    oc = HY_ORDER * HY_D
    ak = _dft_rows(fwd_full, k2.reshape(n1, n2, oc), 0, oc).reshape(2, nf, n2, oc)
    kf_r, kf_i = _filter_spectrum(ak, tabs[0], tabs[1])
    q3 = q.reshape(n1 // 2, n2, 3 * HY_D)
    z3, zcol = q3, 0
    for o in range(HY_ORDER):
        a4 = _dft_rows(fwd_half, z3, zcol, HY_D).reshape(2, nf, n2, HY_D)
        b_r, b_i = _hyena_mid(a4, tabs, kf_r, kf_i, o)
        scale = 1.0 / (2.0 * n * nrm[:, o * HY_D:(o + 1) * HY_D])
        z3 = _hyena_inverse(inv, b_r.reshape(nf, n2 * HY_D), b_i.reshape(nf, n2 * HY_D), scale,
                            hy_bias[o][None, :], z3, zcol, q3, o + 1)
        zcol = 0
    return z3.reshape(n, HY_D)


def _hy_ctx_kernel(v_ref, x1_ref, x2_ref, k0_ref, k1_ref, n0_ref, n1_ref, bias_ref, o_ref, kf, zs, *, n):
    zs[...] = v_ref[...]
    for o, (k_ref, nr_ref, x_ref) in enumerate(((k0_ref, n0_ref, x1_ref), (k1_ref, n1_ref, x2_ref))):
        kf[0:n, :] = k_ref[n:2 * n, :]
        kf[n:2 * n, :] = k_ref[0:n, :]

        def body(s, acc):
            return acc + kf[pl.ds(n - s, n), :] * zs[pl.ds(s, 1), :]

        acc = lax.fori_loop(0, n, body, jnp.zeros((n, LANE), F32))
        z = zs[...]
        zs[...] = x_ref[...] * (acc / nr_ref[...] + bias_ref[o:o + 1, :] * z)
    o_ref[...] = zs[...]


def _hyena_ctx(q, k2, nrm, hy_bias):
    n = q.shape[0]
    nb = HY_D // LANE
    col = lambda c0: pl.BlockSpec((n, LANE), lambda j: (0, c0 + j))
    kcol = lambda c0: pl.BlockSpec((2 * n, LANE), lambda j: (0, c0 + j))
    ncol = lambda c0: pl.BlockSpec((1, LANE), lambda j: (0, c0 + j))
    return pl.pallas_call(
        functools.partial(_hy_ctx_kernel, n=n),
        grid=(nb,),
        in_specs=[col(0), col(nb), col(2 * nb), kcol(0), kcol(nb), ncol(0), ncol(nb),
                  pl.BlockSpec((HY_ORDER, LANE), lambda j: (0, j))],
        out_specs=pl.BlockSpec((n, LANE), lambda j: (0, j)),
        out_shape=jax.ShapeDtypeStruct((n, HY_D), F32),
        scratch_shapes=[pltpu.VMEM((2 * n, LANE), F32), pltpu.VMEM((n, LANE), F32)],
        compiler_params=_params("parallel"),
        name="hyena_ctx",
    )(q, q, q, k2, k2, nrm, nrm, hy_bias)


def _merge_kernel(ya_ref, xs_ref, yf_ref, yb_ref, z_ref, yc_ref, yd_ref, g_ref, h_ref,
                  dv_ref, ng_ref, g1_ref, lg_ref, lb_ref,
                  wa_ref, wb_ref, wc_ref, wd_ref, wo_ref, o_ref):
    y = xs_ref[...] * dv_ref[...] + yf_ref[...] + yb_ref[...]
    gz = y * _silu(z_ref[...].astype(F32))
    ssd = gz * lax.rsqrt(jnp.mean(gz * gz, -1, keepdims=True) + LN_EPS) * ng_ref[...]
    d = D_MODEL
    gate = lambda k: jax.nn.sigmoid(g_ref[:, k * d:(k + 1) * d].astype(F32))
    m = gate(0) * _dot(ya_ref[...].astype(BF16), wa_ref[...])
    m = m + gate(1) * _dot(ssd.astype(BF16), wb_ref[...])
    m = m + gate(2) * _dot(yc_ref[...].astype(BF16), wc_ref[...])
    m = m + gate(3) * _dot(yd_ref[...].astype(BF16), wd_ref[...])
    mix = _dot(m.astype(BF16), wo_ref[...])
    o_ref[...] = _layer_norm(DN_ALPHA * h_ref[...] + g1_ref[...] * mix, lg_ref[...], lb_ref[...])


def _merge(ya, xbc, ydir, p, yc, yd, h, dvec, ng, gate1, lg, lb, wa, wb, wc, wd, wo):
    n = h.shape[0]
    t = 256
    tok = lambda w, col=0: pl.BlockSpec((t, w), lambda i: (i, col))
    vec = lambda w: pl.BlockSpec((1, w), lambda i: (0, 0))
    mat = lambda r: pl.BlockSpec((r, D_MODEL), lambda i: (0, 0))
    return pl.pallas_call(
        _merge_kernel,
        grid=(n // t,),
        in_specs=[tok(CONF_D), tok(SSD_D),
                  pl.BlockSpec((None, t, SSD_D), lambda i: (0, i, 0)),
                  pl.BlockSpec((None, t, SSD_D), lambda i: (1, i, 0)),
                  tok(SSD_D, PZ // SSD_D), tok(HY_D), tok(SC_D), tok(N_BRANCH * D_MODEL, 0), tok(D_MODEL),
                  vec(SSD_D), vec(SSD_D), vec(D_MODEL), vec(D_MODEL), vec(D_MODEL),
                  mat(CONF_D), mat(SSD_D), mat(HY_D), mat(SC_D), mat(D_MODEL)],
        out_specs=tok(D_MODEL),
        out_shape=jax.ShapeDtypeStruct((n, D_MODEL), F32),
        compiler_params=_params("parallel"),
        name="merge",
    )(ya, xbc, ydir, ydir, p, yc, yd, p, h, dvec, ng, gate1, lg, lb, wa, wb, wc, wd, wo)


def _router_kernel(h_ref, sh_ref, sc_ref, w_ref, b_ref, sel_ref, cnt_ref, selt_ref):
    @pl.when(pl.program_id(0) == 0)
    def _():
        cnt_ref[...] = jnp.zeros_like(cnt_ref)

    u = h_ref[...] * (1.0 + sc_ref[...]) + sh_ref[...]
    lg = jnp.dot(u, w_ref[...], precision=HIGHEST, preferred_element_type=F32) + b_ref[...]
    lane = lax.broadcasted_iota(jnp.int32, lg.shape, 1).astype(F32)
    neg = -1e30
    big = 1e9
    gl = jnp.where(lane < MOE_GROUPS, lg, neg)
    gmax = jnp.max(gl, -1, keepdims=True)
    gsel = jnp.min(jnp.where(gl == gmax, lane, big), -1, keepdims=True)
    gprob = 1.0 / jnp.sum(jnp.where(lane < MOE_GROUPS, jnp.exp(lg - gmax), 0.0), -1, keepdims=True)
    lo = MOE_GROUPS + gsel * MOE_EPG
    el = jnp.where(jnp.abs(lane - lo - (MOE_EPG - 1) / 2.0) < MOE_EPG / 2.0, lg, neg)
    m1 = jnp.max(el, -1, keepdims=True)
    i1 = jnp.min(jnp.where(el == m1, lane, big), -1, keepdims=True)
    el2 = jnp.where(lane == i1, neg, el)
    m2 = jnp.max(el2, -1, keepdims=True)
    i2 = jnp.min(jnp.where(el2 == m2, lane, big), -1, keepdims=True)
    t = jnp.exp(m2 - m1)
    w1 = gprob / (1.0 + t)
    w2 = gprob * t / (1.0 + t)
    oh1 = jnp.where(lane == i1, 1.0, 0.0)
    oh2 = jnp.where(lane == i2, 1.0, 0.0)
    oh = oh1 + oh2
    tt = lg.shape[0]
    li = lax.broadcasted_iota(jnp.int32, (tt, tt), 0)
    si = lax.broadcasted_iota(jnp.int32, (tt, tt), 1)
    before = _dot(jnp.where(li > si, 1.0, 0.0).astype(BF16), oh.astype(BF16)) + cnt_ref[...]
    r1 = jnp.sum(oh1 * before, -1, keepdims=True)
    r2 = jnp.sum(oh2 * before, -1, keepdims=True)
    cnt_ref[...] += jnp.sum(oh, axis=0, keepdims=True)
    cols = (i1 - MOE_GROUPS, i2 - MOE_GROUPS, w1, w2, r1, r2)
    sel = jnp.zeros_like(lg)
    for k, v in enumerate(cols):
        sel = jnp.where(lane == k, v, sel)
    sel_ref[...] = sel
    selt_ref[...] = sel.T[0:SUBLANE, :]


def _router(h, shift, scale, wr, br):
    n = h.shape[0]
    t = 256
    tok = lambda w: pl.BlockSpec((t, w), lambda i: (i, 0))
    vec = lambda w: pl.BlockSpec((1, w), lambda i: (0, 0))
    return pl.pallas_call(
        _router_kernel,
        grid=(n // t,),
        in_specs=[tok(D_MODEL), vec(D_MODEL), vec(D_MODEL),
                  pl.BlockSpec((D_MODEL, LANE), lambda i: (0, 0)), vec(LANE)],
        out_specs=[tok(LANE), vec(LANE), pl.BlockSpec((SUBLANE, t), lambda i: (0, i))],
        out_shape=[jax.ShapeDtypeStruct((n, LANE), F32), jax.ShapeDtypeStruct((1, LANE), F32),
                   jax.ShapeDtypeStruct((SUBLANE, n), F32)],
        compiler_params=_params("arbitrary"),
        name="router",
    )(h, shift, scale, wr, br)


MOE_T = 256


def _dispatch_kernel(dst_ref, h_ref, sh_ref, sc_ref, zero_hbm, xin_hbm, ubuf, sem, *, nt):
    del zero_hbm
    t = MOE_T
    i = pl.program_id(0)
    slot = i % 2

    def wait_slot(s):
        for _ in range(MOE_TOP_K):
            pltpu.make_async_copy(ubuf.at[s], xin_hbm.at[pl.ds(0, t), :], sem.at[s]).wait()

    @pl.when(i >= 2)
    def _():
        wait_slot(slot)

    ubuf[slot] = h_ref[...] * (1.0 + sc_ref[...]) + sh_ref[...]

    def issue(r, carry):
        for k in range(MOE_TOP_K):
            pltpu.make_async_copy(ubuf.at[slot, pl.ds(r, 1), :],
                                  xin_hbm.at[pl.ds(dst_ref[0, 0, k * t + r], 1), :], sem.at[slot]).start()
        return carry

    lax.fori_loop(0, t, issue, 0, unroll=8)

    @pl.when(i == nt - 1)
    def _():
        wait_slot(slot)
        if nt > 1:
            wait_slot(1 - slot)


def _dispatch(pos_t, h, shift, scale, n_rows):
    n = h.shape[0]
    t = MOE_T
    nt = n // t
    vec = pl.BlockSpec((1, D_MODEL), lambda i: (0, 0))
    return pl.pallas_call(
        functools.partial(_dispatch_kernel, nt=nt),
        grid=(nt,),
        in_specs=[pl.BlockSpec((1, 1, MOE_TOP_K * t), lambda i: (i, 0, 0), memory_space=pltpu.SMEM),
                  pl.BlockSpec((t, D_MODEL), lambda i: (i, 0)), vec, vec,
                  pl.BlockSpec(memory_space=pl.ANY)],
        out_specs=pl.BlockSpec(memory_space=pl.ANY),
        out_shape=jax.ShapeDtypeStruct((n_rows, D_MODEL), F32),
        scratch_shapes=[pltpu.VMEM((2, t, D_MODEL), F32), pltpu.SemaphoreType.DMA((2,))],
        input_output_aliases={4: 0},
        compiler_params=_params("arbitrary"),
        name="dispatch",
    )(pos_t, h, shift, scale, jnp.zeros((n_rows, D_MODEL), F32))


def _expert_kernel(be_ref, nu_ref, x_ref, wg_ref, wu_ref, wd_ref, o_ref, wgb, wub, wdb):
    b = pl.program_id(0)

    @pl.when((b == 0) | (be_ref[b] != be_ref[jnp.maximum(b - 1, 0)]))
    def _():
        wgb[...] = wg_ref[...].astype(BF16)
        wub[...] = wu_ref[...].astype(BF16)
        wdb[...] = wd_ref[...].astype(BF16)

    @pl.when(b < nu_ref[0])
    def _():
        x = x_ref[...].astype(BF16)
        hid = _silu(_dot(x, wgb[...])) * _dot(x, wub[...])
        o_ref[...] = _dot(hid.astype(BF16), wdb[...])

    @pl.when(b >= nu_ref[0])
    def _():
        o_ref[...] = jnp.zeros_like(o_ref)


def _experts(xin, block_e, n_used, wg, wu, wd, layer):
    n_blocks = block_e.shape[0]
    gs = pltpu.PrefetchScalarGridSpec(
        num_scalar_prefetch=2,
        grid=(n_blocks,),
        in_specs=[pl.BlockSpec((MOE_ROWS, D_MODEL), lambda b, be, nu: (b, 0)),
                  pl.BlockSpec((None, None, D_MODEL, MOE_FF), lambda b, be, nu: (layer, be[b], 0, 0)),
                  pl.BlockSpec((None, None, D_MODEL, MOE_FF), lambda b, be, nu: (layer, be[b], 0, 0)),
                  pl.BlockSpec((None, None, MOE_FF, D_MODEL), lambda b, be, nu: (layer, be[b], 0, 0))],
        out_specs=pl.BlockSpec((MOE_ROWS, D_MODEL), lambda b, be, nu: (b, 0)),
        scratch_shapes=[pltpu.VMEM((D_MODEL, MOE_FF), BF16), pltpu.VMEM((D_MODEL, MOE_FF), BF16),
                        pltpu.VMEM((MOE_FF, D_MODEL), BF16)],
    )
    return pl.pallas_call(
        _expert_kernel,
        grid_spec=gs,
        out_shape=jax.ShapeDtypeStruct((n_blocks * MOE_ROWS, D_MODEL), F32),
        compiler_params=_params("arbitrary"),
        name="experts",
    )(block_e, n_used, xin, wg, wu, wd)


def _combine_kernel(pos_ref, posn_ref, y_hbm, h_ref, sel_ref, g2_ref, lg_ref, lb_ref, o_ref, ybuf, sem, *, nt):
    t = MOE_T
    i = pl.program_id(0)
    slot = i % 2

    def gather(p_ref, s):
        def issue(r, carry):
            pltpu.make_async_copy(y_hbm.at[pl.ds(p_ref[0, 0, r], 1), :], ybuf.at[s, pl.ds(r, 1), :],
                                  sem.at[s]).start()
            return carry
        lax.fori_loop(0, MOE_TOP_K * t, issue, 0, unroll=8)

    @pl.when(i == 0)
    def _():
        gather(pos_ref, 0)

    @pl.when(i + 1 < nt)
    def _():
        gather(posn_ref, 1 - slot)

    pltpu.make_async_copy(y_hbm.at[pl.ds(0, MOE_TOP_K * t), :], ybuf.at[slot], sem.at[slot]).wait()
    ffn = sel_ref[:, 2:3] * ybuf[slot, 0:t, :] + sel_ref[:, 3:4] * ybuf[slot, t:2 * t, :]
    o_ref[...] = _layer_norm(DN_ALPHA * h_ref[...] + g2_ref[...] * ffn, lg_ref[...], lb_ref[...])


def _combine(y, pos_t, h, sel, gate2, lg, lb):
    n = h.shape[0]
    t = MOE_T
    nt = n // t
    tok = lambda w: pl.BlockSpec((t, w), lambda i: (i, 0))
    vec = pl.BlockSpec((1, D_MODEL), lambda i: (0, 0))
    return pl.pallas_call(
        functools.partial(_combine_kernel, nt=nt),
        grid=(nt,),
        in_specs=[pl.BlockSpec((1, 1, MOE_TOP_K * t), lambda i: (i, 0, 0), memory_space=pltpu.SMEM),
                  pl.BlockSpec((1, 1, MOE_TOP_K * t), lambda i: (jnp.minimum(i + 1, nt - 1), 0, 0),
                               memory_space=pltpu.SMEM),
                  pl.BlockSpec(memory_space=pl.ANY),
                  tok(D_MODEL), tok(LANE), vec, vec, vec],
        out_specs=tok(D_MODEL),
        out_shape=jax.ShapeDtypeStruct((n, D_MODEL), F32),
        scratch_shapes=[pltpu.VMEM((2, MOE_TOP_K * t, D_MODEL), F32), pltpu.SemaphoreType.DMA((2,))],
        compiler_params=_params("arbitrary"),
        name="combine",
    )(pos_t, pos_t, y, h, sel, gate2, lg, lb)


def _moe(h, shift, scale, gate2, lg, lb, wr, br, wg, wu, wd, layer):
    n = h.shape[0]
    t = MOE_T
    nt = n // t
    sel, cnt, selt = _router(h, shift, scale, wr, br)
    counts = cnt[0, MOE_GROUPS:MOE_GROUPS + MOE_EXPERTS].astype(jnp.int32)
    padded = (counts + MOE_ROWS - 1) // MOE_ROWS * MOE_ROWS
    pad_end = jnp.cumsum(padded)
    pad_start = pad_end - padded
    n_blocks = (n * MOE_TOP_K + MOE_EXPERTS * (MOE_ROWS - 1) + MOE_ROWS - 1) // MOE_ROWS
    blk_row = jnp.arange(n_blocks, dtype=jnp.int32) * MOE_ROWS
    block_e = jnp.minimum(jnp.sum((blk_row[:, None] >= pad_end[None, :]).astype(jnp.int32), axis=1),
                          MOE_EXPERTS - 1)
    n_used = (pad_end[-1:] // MOE_ROWS).astype(jnp.int32)
    e_kt = selt[0:MOE_TOP_K].astype(jnp.int32)
    ids = jnp.arange(MOE_EXPERTS, dtype=jnp.int32)[None, :, None]
    start_kt = jnp.sum(jnp.where(e_kt[:, None, :] == ids, pad_start[None, :, None], 0), axis=1)
    pos_kt = start_kt + selt[4:4 + MOE_TOP_K].astype(jnp.int32)
    pos_t = pos_kt.reshape(MOE_TOP_K, nt, t).transpose(1, 0, 2).reshape(nt, 1, MOE_TOP_K * t)
    xin = _dispatch(pos_t, h, shift, scale, n_blocks * MOE_ROWS)
    y = _experts(xin, block_e, n_used, wg, wu, wd, layer)
    return _combine(y, pos_t, h, sel, gate2, lg, lb)


def _mixer(h, mod, lw, ssd_init, tables, *, latent, need_mix):
    n = h.shape[0]
    p, dt_raw = _inproj(h, mod[0], mod[1], lw["w_in"], lw["layer"])
    xbc = _conv3(p, PX, SSD_XBC, lw["ssd_conv_w"], lw["ssd_conv_b"], silu=True)
    ydir, finals = _ssd_scan(xbc, dt_raw, lw["ssd_dt_bias"], lw["ssd_a_log"], ssd_init)
    if not need_mix:
        return None, finals
    ya = _conformer(p, lw["conf_dw_w"], lw["conf_dw_b"], lw["conf_ln_g"], lw["conf_ln_b"],
                    dil=GRID_W if latent else 1)
    q = _conv3(p, PC, 3 * HY_D, lw["hy_short_w"], lw["hy_short_b"], silu=False)
    k2, nrm = _hyena_filters(lw["featx_lat" if latent else "featx_ctx"], lw["hy_filter"], n)
    if latent:
        yc = _hyena_long(q, k2, nrm, lw["hy_bias"], tables)
    else:
        yc = _hyena_ctx(q, k2, nrm, lw["hy_bias"])
    yd = _gated_conv(p, lw["sc_conv_w"])
    h = _merge(ya, xbc, ydir, p, yc, yd, h, lw["ssd_dvec"], lw["ssd_norm_g"], mod[2], lw["ln_g0"], lw["ln_b0"],
               lw["w_branch_a"], lw["w_branch_b"], lw["w_branch_c"], lw["w_branch_d"], lw["w_out"])
    return h, finals


def _positional_features(n):
    t01 = jnp.linspace(0.0, 1.0, n, dtype=F32)[:, None]
    omega = (2.0 * math.pi / n) * jnp.arange(n, dtype=F32)[:, None]
    bands = jnp.linspace(1e-4, HY_BANDS - 1, HY_BANDS, dtype=F32)
    feat = jnp.concatenate([t01, jnp.cos(bands * omega), -jnp.sin(bands * omega)], axis=-1)
    featx = jnp.concatenate([feat, jnp.zeros((1, HY_EMB), F32), jnp.flip(feat[1:], axis=0)], axis=0)
    return _pad_lanes(featx)


def _relayout_w_in(w_in):
    seg = lambda a, b: w_in[:, :, a:b]
    ob = OFF_B
    parts = [seg(OFF_G, OFF_G + N_BRANCH * D_MODEL),
             seg(OFF_A, OFF_B),
             seg(OFF_C, OFF_D),
             seg(OFF_D, OFF_G),
             seg(ob + SSD_D, ob + SSD_D + SSD_XBC),
             seg(ob, ob + SSD_D),
             seg(ob + SSD_D + SSD_XBC, OFF_C),
             jnp.zeros(w_in.shape[:2] + (NP - PDT - 2 * SSD_HEADS,), w_in.dtype)]
    return jnp.concatenate(parts, axis=-1).astype(BF16)


def _pad_lanes(v):
    return jnp.pad(v, ((0, 0), (0, LANE - v.shape[-1])))


def kernel(x, c, ctx, c_ctx, w_mod, b_mod, ln_g, ln_b, w_in, conf_dw_w, conf_dw_b, conf_ln_g, conf_ln_b,
           ssd_conv_w, ssd_conv_b, ssd_a_log, ssd_dt_bias, ssd_d, ssd_norm_g, hy_short_w, hy_short_b,
           hy_w1, hy_b1, hy_w2, hy_b2, hy_freq, hy_w3, hy_bias, sc_conv_w, w_branch_a, w_branch_b,
           w_branch_c, w_branch_d, w_out, rt_group_w, rt_group_b, rt_expert_w, rt_expert_b,
           ex_w_gate, ex_w_up, ex_w_down):
    assert x.shape[0] == 1 and ctx.shape[0] == 1
    n_lat, n_ctx = x.shape[1], ctx.shape[1]
    depth = w_in.shape[0]

    cv = jnp.concatenate([c, c_ctx[None, :], jnp.zeros((SUBLANE - 2, D_MODEL), F32)], axis=0)
    mods = _mod_vectors(cv, w_mod, b_mod)
    w_in_p = _relayout_w_in(w_in)
    tables = _dft_tables(n_lat)
    featx_lat = _positional_features(n_lat)
    featx_ctx = _positional_features(n_ctx)
    deltas = jnp.abs(jnp.linspace(HY_MIN_DECAY, HY_MAX_DECAY, HY_N_FILT, dtype=F32))
    deltas_d = deltas.reshape(HY_ORDER, 2, HY_D).transpose(1, 0, 2).reshape(2, 1, HY_ORDER * HY_D)
    router_w = jnp.concatenate([rt_group_w, rt_expert_w,
                                jnp.zeros((depth, D_MODEL, LANE - MOE_GROUPS - MOE_EXPERTS), F32)], axis=-1)
    router_b = jnp.concatenate([rt_group_b, rt_expert_b,
                                jnp.zeros((depth, LANE - MOE_GROUPS - MOE_EXPERTS), F32)], axis=-1)
    ssd_zero = jnp.zeros((2,) + SSD_STATE_SHAPE, F32)

    h_lat, h_ctx = x[0], ctx[0]
    for l in range(depth):
        row = lambda v: v[None, :]
        lw = dict(
            w_in=w_in_p, layer=l, conf_dw_w=conf_dw_w[l], conf_dw_b=row(conf_dw_b[l]), conf_ln_g=row(conf_ln_g[l]),
            conf_ln_b=row(conf_ln_b[l]), ssd_conv_w=ssd_conv_w[l], ssd_conv_b=row(ssd_conv_b[l]),
            ssd_a_log=_pad_lanes(ssd_a_log[l].reshape(1, -1)), ssd_dt_bias=_pad_lanes(ssd_dt_bias[l].reshape(1, -1)),
            ssd_dvec=row(jnp.repeat(ssd_d[l], SSD_HEAD_DIM)), ssd_norm_g=row(ssd_norm_g[l]),
            hy_short_w=hy_short_w[l], hy_short_b=row(hy_short_b[l]),
            hy_filter=_filter_weights(hy_w1[l], hy_b1[l], hy_w2[l], hy_b2[l], hy_freq[l], hy_w3[l], deltas_d),
            hy_bias=hy_bias[l], featx_lat=featx_lat, featx_ctx=featx_ctx,
            sc_conv_w=sc_conv_w[l], ln_g0=row(ln_g[l, 0]), ln_b0=row(ln_b[l, 0]),
            w_branch_a=w_branch_a[l].astype(BF16), w_branch_b=w_branch_b[l].astype(BF16),
            w_branch_c=w_branch_c[l].astype(BF16), w_branch_d=w_branch_d[l].astype(BF16),
            w_out=w_out[l].astype(BF16))
        moe_w = (router_w[l], row(router_b[l]), ex_w_gate, ex_w_up, ex_w_down, l)
        last = l == depth - 1
        d = D_MODEL
        mod_lat = [mods[l, 0:1, k * d:(k + 1) * d] for k in range(6)]
        mod_ctx = [mods[l, 1:2, k * d:(k + 1) * d] for k in range(6)]

        mix_ctx, ctx_states = _mixer(h_ctx, mod_ctx, lw, ssd_zero, None, latent=False, need_mix=not last)
        h_lat, _ = _mixer(h_lat, mod_lat, lw, ctx_states, tables, latent=True, need_mix=True)
        h_lat = _moe(h_lat, mod_lat[3], mod_lat[4], mod_lat[5], row(ln_g[l, 1]), row(ln_b[l, 1]), *moe_w)
        if not last:
            h_ctx = _moe(mix_ctx, mod_ctx[3], mod_ctx[4], mod_ctx[5], row(ln_g[l, 1]), row(ln_b[l, 1]), *moe_w)
    return h_lat[None]
```

```python
import functools
import math

import jax
import jax.numpy as jnp
from jax import lax
from jax.experimental import pallas as pl
from jax.experimental.pallas import tpu as pltpu

F32 = jnp.float32
BF16 = jnp.bfloat16
HIGHEST = lax.Precision.HIGHEST

D_MODEL = 1024
DEPTH = 4
GRID_W = 64
CONF_D = 512
CONF_K = 31
SSD_D = 768
SSD_HEADS = 12
SSD_HEAD_DIM = 64
SSD_GROUPS = 4
SSD_HPG = SSD_HEADS // SSD_GROUPS
SSD_STATE = 128
SSD_CHUNK = 128
SSD_BC = SSD_GROUPS * SSD_STATE
SSD_XBC = SSD_D + 2 * SSD_BC
SSD_PROJ = SSD_D + SSD_XBC + 2 * SSD_HEADS
HY_D = 512
HY_ORDER = 2
HY_EMB = 33
HY_BANDS = (HY_EMB - 1) // 2
HY_HID = 64
HY_N_FILT = HY_ORDER * 2 * HY_D
HY_MIN_DECAY = math.log(1e-2) / 1.5
HY_MAX_DECAY = math.log(1e-2) / 0.3
SC_D = 512
N_BRANCH = 4
OFF_A = 0
OFF_B = OFF_A + 2 * CONF_D
OFF_C = OFF_B + SSD_PROJ
OFF_D = OFF_C + 3 * HY_D
OFF_G = OFF_D + 3 * SC_D
MOE_GROUPS = 4
MOE_EPG = 8
MOE_EXPERTS = MOE_GROUPS * MOE_EPG
MOE_TOP_K = 2
MOE_FF = 512
DN_ALPHA = (2 * DEPTH) ** 0.25
LN_EPS = 1e-5

PG = 0
PA = PG + N_BRANCH * D_MODEL
PC = PA + 2 * CONF_D
PD = PC + 3 * HY_D
PX = PD + 3 * SC_D
PZ = PX + SSD_XBC
PDT = PZ + SSD_D
INPROJ_TN = 1024
NP = -(-(PDT + 128) // INPROJ_TN) * INPROJ_TN

LANE = 128
SUBLANE = 8
FFT_N2 = 256
MOE_ROWS = 256
VMEM_LIMIT = 48 * 1024 * 1024


def _params(*sem):
    return pltpu.CompilerParams(dimension_semantics=sem, vmem_limit_bytes=VMEM_LIMIT)


def _silu(x):
    return x * jax.nn.sigmoid(x)


def _layer_norm(x, g, b):
    mu = jnp.mean(x, -1, keepdims=True)
    xc = x - mu
    var = jnp.mean(xc * xc, -1, keepdims=True)
    return xc * lax.rsqrt(var + LN_EPS) * g + b


def _dot(a, b):
    return jnp.dot(a, b, preferred_element_type=F32)


def _mod_kernel(cv_ref, w_ref, b_ref, o_ref):
    o_ref[...] = jnp.dot(_silu(cv_ref[...]), w_ref[...], precision=HIGHEST,
                         preferred_element_type=F32) + b_ref[...]


def _mod_vectors(cv, w_mod, b_mod):
    tn = 1536
    return pl.pallas_call(
        _mod_kernel,
        grid=(DEPTH, 6 * D_MODEL // tn),
        in_specs=[pl.BlockSpec((SUBLANE, D_MODEL), lambda l, j: (0, 0)),
                  pl.BlockSpec((None, D_MODEL, tn), lambda l, j: (l, 0, j)),
                  pl.BlockSpec((None, 1, tn), lambda l, j: (l, 0, j))],
        out_specs=pl.BlockSpec((None, SUBLANE, tn), lambda l, j: (l, 0, j)),
        out_shape=jax.ShapeDtypeStruct((DEPTH, SUBLANE, 6 * D_MODEL), F32),
        compiler_params=_params("parallel", "parallel"),
        name="mod_vectors",
    )(cv, w_mod, b_mod.reshape(DEPTH, 1, 6 * D_MODEL))


def _inproj_kernel(x_ref, sh_ref, sc_ref, w_ref, o_ref, dt_ref, xb_ref, *, nj):
    j = pl.program_id(1)

    @pl.when(j == 0)
    def _():
        xb_ref[...] = (x_ref[...] * (1.0 + sc_ref[...]) + sh_ref[...]).astype(BF16)

    res = _dot(xb_ref[...], w_ref[...])
    o_ref[...] = res.astype(BF16)

    @pl.when(j == nj - 1)
    def _():
        off = PDT - (nj - 1) * INPROJ_TN
        dt_ref[...] = res[:, off:off + LANE]


def _inproj(h, shift, scale, w, layer):
    n = h.shape[0]
    tm = min(n, 2048)
    tn = INPROJ_TN
    nj = NP // tn
    assert PDT >= (nj - 1) * tn
    return pl.pallas_call(
        functools.partial(_inproj_kernel, nj=nj),
        grid=(n // tm, nj),
        in_specs=[pl.BlockSpec((tm, D_MODEL), lambda i, j: (i, 0)),
                  pl.BlockSpec((1, D_MODEL), lambda i, j: (0, 0)),
                  pl.BlockSpec((1, D_MODEL), lambda i, j: (0, 0)),
                  pl.BlockSpec((None, D_MODEL, tn), lambda i, j: (layer, 0, j))],
        out_specs=[pl.BlockSpec((tm, tn), lambda i, j: (i, j)),
                   pl.BlockSpec((tm, LANE), lambda i, j: (i, 0))],
        out_shape=[jax.ShapeDtypeStruct((n, NP), BF16), jax.ShapeDtypeStruct((n, LANE), F32)],
        scratch_shapes=[pltpu.VMEM((tm, D_MODEL), BF16)],
        compiler_params=_params("parallel", "arbitrary"),
        name="inproj",
    )(h, shift, scale, w)


def _shifted(x, prev_row, next_row):
    t = x.shape[0]
    row = lax.broadcasted_iota(jnp.int32, x.shape, 0)
    xm = jnp.where(row == 0, prev_row, pltpu.roll(x, 1, 0))
    xp = jnp.where(row == t - 1, next_row, pltpu.roll(x, t - 1, 0))
    return xm, xp


HALO_ROWS = 16


def _conv3_kernel(cur_ref, prev_ref, next_ref, w_ref, b_ref, o_ref, *, silu, nt):
    i = pl.program_id(0)
    x = cur_ref[...].astype(F32)
    pv = jnp.where(i > 0, prev_ref[HALO_ROWS - 1:HALO_ROWS, :].astype(F32), 0.0)
    nx = jnp.where(i < nt - 1, next_ref[0:1, :].astype(F32), 0.0)
    xm, xp = _shifted(x, pv, nx)
    y = w_ref[0:1, :] * xm + w_ref[1:2, :] * x + w_ref[2:3, :] * xp + b_ref[...]
    o_ref[...] = (_silu(y) if silu else y).astype(o_ref.dtype)


def _halo_specs(t, ct, n, col0):
    rb = t // HALO_ROWS
    last = n // HALO_ROWS - 1
    return [pl.BlockSpec((t, ct), lambda i, j: (i, col0 + j)),
            pl.BlockSpec((HALO_ROWS, ct), lambda i, j: (jnp.maximum(i * rb - 1, 0), col0 + j)),
            pl.BlockSpec((HALO_ROWS, ct), lambda i, j: (jnp.minimum((i + 1) * rb, last), col0 + j))]


def _conv3(p, col, width, w, b, *, silu, out_dtype):
    n = p.shape[0]
    t = min(n, 1024)
    ct = 256
    nt = n // t
    return pl.pallas_call(
        functools.partial(_conv3_kernel, silu=silu, nt=nt),
        grid=(nt, width // ct),
        in_specs=_halo_specs(t, ct, n, col // ct) + [
            pl.BlockSpec((3, ct), lambda i, j: (0, j)),
            pl.BlockSpec((1, ct), lambda i, j: (0, j))],
        out_specs=pl.BlockSpec((t, ct), lambda i, j: (i, j)),
        out_shape=jax.ShapeDtypeStruct((n, width), out_dtype),
        compiler_params=_params("parallel", "parallel"),
        name="conv3",
    )(p, p, p, w, b)


def _gconv_kernel(bg_ref, cc_ref, cp_ref, cn_ref, xc_ref, xp_ref, xn_ref, w_ref, o_ref, *, nt):
    i = pl.program_id(0)
    f = lambda v: v.astype(F32)
    last = slice(HALO_ROWS - 1, HALO_ROWS)
    x = f(cc_ref[...]) * f(xc_ref[...])
    pv = jnp.where(i > 0, f(cp_ref[last, :]) * f(xp_ref[last, :]), 0.0)
    nx = jnp.where(i < nt - 1, f(cn_ref[0:1, :]) * f(xn_ref[0:1, :]), 0.0)
    xm, xp = _shifted(x, pv, nx)
    o_ref[...] = f(bg_ref[...]) * (w_ref[0:1, :] * xm + w_ref[1:2, :] * x + w_ref[2:3, :] * xp)


def _gated_conv(p, w):
    n = p.shape[0]
    t = min(n, 1024)
    ct = 256
    nt = n // t
    nb = SC_D // ct
    return pl.pallas_call(
        functools.partial(_gconv_kernel, nt=nt),
        grid=(nt, nb),
        in_specs=([pl.BlockSpec((t, ct), lambda i, j: (i, PD // ct + j))]
                  + _halo_specs(t, ct, n, PD // ct + nb)
                  + _halo_specs(t, ct, n, PD // ct + 2 * nb)
                  + [pl.BlockSpec((3, ct), lambda i, j: (0, j))]),
        out_specs=pl.BlockSpec((t, ct), lambda i, j: (i, j)),
        out_shape=jax.ShapeDtypeStruct((n, SC_D), F32),
        compiler_params=_params("parallel", "parallel"),
        name="gated_conv",
    )(p, p, p, p, p, p, p, w)


CONF_RB = 64


def _conf_kernel(vc, gc, vp, gp, vn, gn, w_ref, b_ref, lg_ref, lb_ref, o_ref, buf, *, t, halo, dil, nt):
    i = pl.program_id(0)
    glu = lambda v, g: v.astype(F32) * jax.nn.sigmoid(g.astype(F32))
    buf[halo:halo + t, :] = glu(vc[...], gc[...])
    buf[0:halo, :] = jnp.where(i > 0, glu(vp[t - halo:t, :], gp[t - halo:t, :]), 0.0)
    buf[halo + t:halo + t + halo, :] = jnp.where(i < nt - 1, glu(vn[0:halo, :], gn[0:halo, :]), 0.0)

    def block(r0):
        acc = jnp.zeros((CONF_RB, CONF_D), F32)
        for j in range(CONF_K):
            off = halo + (j - CONF_K // 2) * dil
            acc = acc + w_ref[j:j + 1, :] * buf[pl.ds(r0 + off, CONF_RB), :]
        v = _layer_norm(acc + b_ref[...], lg_ref[...], lb_ref[...])
        o_ref[pl.ds(r0, CONF_RB), :] = _silu(v)

    if dil % CONF_RB == 0:
        def body(rb, carry):
            block(pl.multiple_of(rb * CONF_RB, CONF_RB))
            return carry
        lax.fori_loop(0, t // CONF_RB, body, 0)
    else:
        for rb in range(t // CONF_RB):
            block(rb * CONF_RB)


def _conformer(p, w, b, lg, lb, *, dil):
    n = p.shape[0]
    t = min(n, 1024)
    nt = n // t
    halo = -(-(CONF_K // 2) * dil // SUBLANE) * SUBLANE
    assert halo <= t
    cb = PA // CONF_D

    def spec(col, shift):
        return pl.BlockSpec((t, CONF_D), lambda i: (jnp.clip(i + shift, 0, nt - 1), col))

    vec = pl.BlockSpec((1, CONF_D), lambda i: (0, 0))
    return pl.pallas_call(
        functools.partial(_conf_kernel, t=t, halo=halo, dil=dil, nt=nt),
        grid=(nt,),
        in_specs=[spec(cb, 0), spec(cb + 1, 0), spec(cb, -1), spec(cb + 1, -1), spec(cb, 1), spec(cb + 1, 1),
                  pl.BlockSpec((CONF_K, CONF_D), lambda i: (0, 0)), vec, vec, vec],
        out_specs=pl.BlockSpec((t, CONF_D), lambda i: (i, 0)),
        out_shape=jax.ShapeDtypeStruct((n, CONF_D), F32),
        scratch_shapes=[pltpu.VMEM((t + 2 * halo, CONF_D), F32)],
        compiler_params=_params("parallel"),
        name="conformer",
    )(p, p, p, p, p, p, w, b, lg, lb)


CONF_COLS = 16


def _conf_grid_kernel(v_ref, g_ref, w_ref, b_ref, lg_ref, lb_ref, o_ref, buf, *, rows):
    half = CONF_K // 2
    zeros = jnp.zeros((half,) + buf.shape[1:], F32)
    buf[0:half] = zeros
    buf[half + rows:half + rows + half] = zeros
    buf[half:half + rows] = v_ref[...].astype(F32) * jax.nn.sigmoid(g_ref[...].astype(F32))

    def body(r, carry):
        acc = w_ref[0:1, :] * buf[r]
        for j in range(1, CONF_K):
            acc = acc + w_ref[j:j + 1, :] * buf[r + j]
        v = _layer_norm(acc + b_ref[...], lg_ref[...], lb_ref[...])
        o_ref[r] = _silu(v).astype(o_ref.dtype)
        return carry

    lax.fori_loop(0, rows, body, 0, unroll=2)


def _conformer_grid(p, w, b, lg, lb):
    n = p.shape[0]
    rows = n // GRID_W
    p3 = p.reshape(rows, GRID_W, p.shape[1])
    cb = PA // CONF_D
    vec = pl.BlockSpec((1, CONF_D), lambda j: (0, 0))
    blk = lambda col: pl.BlockSpec((rows, CONF_COLS, CONF_D), lambda j: (0, j, col))
    out = pl.pallas_call(
        functools.partial(_conf_grid_kernel, rows=rows),
        grid=(GRID_W // CONF_COLS,),
        in_specs=[blk(cb), blk(cb + 1), pl.BlockSpec((CONF_K, CONF_D), lambda j: (0, 0)), vec, vec, vec],
        out_specs=blk(0),
        out_shape=jax.ShapeDtypeStruct((rows, GRID_W, CONF_D), BF16),
        scratch_shapes=[pltpu.VMEM((rows + 2 * (CONF_K // 2), CONF_COLS, CONF_D), F32)],
        compiler_params=_params("parallel"),
        name="conformer_grid",
    )(p3, p3, w, b, lg, lb)
    return out.reshape(n, CONF_D)


SSD_STATE_SHAPE = (SSD_HEADS, SSD_STATE, SSD_HEAD_DIM)
SSD_CHUNKS_PER_STEP = 2


def _ssd_kernel(xbc_ref, dt_ref, dtb_ref, alog_ref, init_ref, y_ref, fin_ref, h_ref, *, ns, cps):
    d = pl.program_id(0)
    c = pl.program_id(1)
    q = SSD_CHUNK
    hd = SSD_HEAD_DIM

    @pl.when(c == 0)
    def _():
        h_ref[...] = init_ref[...]

    lane = lax.broadcasted_iota(jnp.int32, (q, LANE), 1)
    head = lane < SSD_HEADS
    li = lax.broadcasted_iota(jnp.int32, (q, q), 0)
    si = lax.broadcasted_iota(jnp.int32, (q, q), 1)
    mask = (li - si) * (1 - 2 * d) >= 0
    tri = mask.astype(F32)
    a_rate = -jnp.exp(alog_ref[...])

    def one_chunk(r0):
        rows = pl.ds(r0, q)
        raw = dt_ref[rows, :] + dtb_ref[...]
        dt_all = jnp.maximum(raw, 0.0) + jnp.log(1.0 + jnp.exp(-jnp.abs(raw)))
        ld_all = dt_all * a_rate
        dt_d = jnp.where(head, jnp.where(d == 0, dt_all, pltpu.roll(dt_all, LANE - SSD_HEADS, 1)), 0.0)
        ld_d = jnp.where(head, jnp.where(d == 0, ld_all, pltpu.roll(ld_all, LANE - SSD_HEADS, 1)), 0.0)
        cum = jnp.dot(tri, ld_d, precision=HIGHEST, preferred_element_type=F32)
        tot = jnp.sum(ld_d, axis=0, keepdims=True)
        cum_t = cum.T
        dt_t = dt_d.T
        w_t = (jnp.exp(tot - cum) * dt_d).T
        a_out = jnp.exp(cum)
        e_tot = jnp.exp(tot)

        for g in range(SSD_GROUPS):
            bg = xbc_ref[rows, SSD_D + g * SSD_STATE:SSD_D + (g + 1) * SSD_STATE].astype(F32)
            cg = xbc_ref[rows, SSD_D + SSD_BC + g * SSD_STATE:SSD_D + SSD_BC + (g + 1) * SSD_STATE].astype(F32)
            bg_t = bg.T
            cb = _dot(cg.astype(BF16), bg_t.astype(BF16))
            for e in range(SSD_HPG):
                hh = g * SSD_HPG + e
                diff = cum[:, hh:hh + 1] - cum_t[hh:hh + 1, :]
                dec = jnp.exp(jnp.where(mask, diff, -1e30))
                m = (cb * dec * dt_t[hh:hh + 1, :]).astype(BF16)
                xe = xbc_ref[rows, hh * hd:(hh + 1) * hd].astype(BF16)
                cs = (cg * a_out[:, hh:hh + 1]).astype(BF16)
                h_in = h_ref[hh]
                y_ref[rows, hh * hd:(hh + 1) * hd] = _dot(m, xe) + _dot(cs, h_in.astype(BF16))
                s_new = _dot((bg_t * w_t[hh:hh + 1, :]).astype(BF16), xe)
                h_ref[hh] = e_tot[:, hh:hh + 1] * h_in + s_new

    for k in range(cps):
        one_chunk(pl.multiple_of(jnp.where(d == 0, k, cps - 1 - k) * q, q))

    @pl.when(c == ns - 1)
    def _():
        fin_ref[...] = h_ref[...]


def _ssd_scan(xbc, p, dt_bias, a_log, init):
    n = xbc.shape[0]
    cps = SSD_CHUNKS_PER_STEP
    q = SSD_CHUNK * cps
    ns = n // q

    def chunk(d, c):
        return jnp.where(d == 0, c, ns - 1 - c)

    st = SSD_STATE_SHAPE
    vec = pl.BlockSpec((1, LANE), lambda d, c: (0, 0))
    return pl.pallas_call(
        functools.partial(_ssd_kernel, ns=ns, cps=cps),
        grid=(2, ns),
        in_specs=[pl.BlockSpec((q, SSD_XBC), lambda d, c: (chunk(d, c), 0)),
                  pl.BlockSpec((q, LANE), lambda d, c: (chunk(d, c), 0)),
                  vec, vec,
                  pl.BlockSpec((None,) + st, lambda d, c: (d, 0, 0, 0))],
        out_specs=[pl.BlockSpec((None, q, SSD_D), lambda d, c: (d, chunk(d, c), 0)),
                   pl.BlockSpec((None,) + st, lambda d, c: (d, 0, 0, 0))],
        out_shape=[jax.ShapeDtypeStruct((2, n, SSD_D), F32),
                   jax.ShapeDtypeStruct((2,) + st, F32)],
        scratch_shapes=[pltpu.VMEM(st, F32)],
        compiler_params=_params("arbitrary", "arbitrary"),
        name="ssd_scan",
    )(xbc, p, dt_bias, a_log, init)


def _filt_kernel(feat_ref, w1_ref, b1_ref, w2_ref, b2_ref, fr_ref, w3h_ref, w3l_ref, dl_ref, k_ref, nrm_ref, *,
                 n, t):
    i = pl.program_id(0)
    hf = t // 2
    feat = feat_ref[...]
    x = jnp.concatenate([feat[0:hf], feat[hf:t]], axis=1)
    hid = jnp.sin(fr_ref[0:1, :] * (jnp.dot(x, w1_ref[...], precision=HIGHEST,
                                            preferred_element_type=F32) + b1_ref[...]))
    hid = jnp.sin(fr_ref[1:2, :] * (jnp.dot(hid, w2_ref[...], precision=HIGHEST,
                                            preferred_element_type=F32) + b2_ref[...]))
    hi = hid.astype(BF16)
    lo = (hid - hi.astype(F32)).astype(BF16)

    @pl.when(i == 0)
    def _():
        nrm_ref[...] = jnp.zeros_like(nrm_ref)

    for half in range(2):
        wh, wl = w3h_ref[half], w3l_ref[half]
        filt = _dot(hi, wh) + _dot(lo, wh) + _dot(hi, wl)
        filt = filt * jnp.exp(-feat[half * hf:(half + 1) * hf, 0:1] * dl_ref[...])
        row = i * t + half * hf + lax.broadcasted_iota(jnp.int32, filt.shape, 0)
        filt = jnp.where(row == n, 0.0, filt)
        k_ref[half * hf:(half + 1) * hf, :] = filt
        nrm_ref[...] += jnp.sum(jnp.abs(filt), axis=0, keepdims=True)


def _hyena_filters(featx, fw, n):
    t = min(n, 512)
    half = n // t
    oc = HY_ORDER * HY_D
    full = lambda shape: pl.BlockSpec(shape, lambda i: tuple(0 for _ in shape))
    w3spec = pl.BlockSpec((None, 2, LANE, oc), lambda i: (i // half, 0, 0, 0))
    return pl.pallas_call(
        functools.partial(_filt_kernel, n=n, t=t),
        grid=(2 * n // t,),
        in_specs=[pl.BlockSpec((t, LANE), lambda i: (i, 0)),
                  full((2 * LANE, LANE)), full((1, LANE)), full((LANE, LANE)), full((1, LANE)),
                  full((2, LANE)), w3spec, w3spec,
                  pl.BlockSpec((None, 1, oc), lambda i: (i // half, 0, 0))],
        out_specs=[pl.BlockSpec((t, oc), lambda i: (i, 0)),
                   pl.BlockSpec((1, oc), lambda i: (0, 0))],
        out_shape=[jax.ShapeDtypeStruct((2 * n, oc), F32),
                   jax.ShapeDtypeStruct((1, oc), F32)],
        compiler_params=_params("arbitrary"),
        name="hyena_filters",
    )(featx, fw["w1"], fw["b1"], fw["w2"], fw["b2"], fw["freq"], fw["w3h"], fw["w3l"], fw["deltas"])


def _filter_weights(w1, b1, w2, b2, freq, w3, deltas_d):
    hh = HY_HID
    z = lambda r, c: jnp.zeros((r, c), F32)
    w1p = jnp.pad(w1, ((0, LANE - HY_EMB), (0, 0)))
    w1b = jnp.concatenate([jnp.concatenate([w1p, z(LANE, hh)], 1),
                           jnp.concatenate([z(LANE, hh), w1p], 1)], 0)
    w2b = jnp.concatenate([jnp.concatenate([w2, z(hh, hh)], 1),
                           jnp.concatenate([z(hh, hh), w2], 1)], 0)
    two = lambda v: jnp.concatenate([v, v], axis=-1)
    w3d = w3.reshape(hh, HY_ORDER, 2, HY_D).transpose(2, 0, 1, 3).reshape(2, hh, HY_ORDER * HY_D)
    zz = jnp.zeros_like(w3d)
    w3x = jnp.stack([jnp.concatenate([w3d, zz], 1), jnp.concatenate([zz, w3d], 1)], axis=1)
    w3h = w3x.astype(BF16)
    w3l = (w3x - w3h.astype(F32)).astype(BF16)
    return dict(w1=w1b, b1=two(b1[None, :]), w2=w2b, b2=two(b2[None, :]), freq=two(freq), w3h=w3h, w3l=w3l,
                deltas=deltas_d)


DFT_LANES = 8192


def _dft_rows_kernel(f_ref, x_ref, o_ref, *, nj):
    x = jnp.concatenate([x_ref[:, jj, :] for jj in range(nj)], axis=1)
    o_ref[...] = _dot(f_ref[...], x.astype(BF16)).astype(o_ref.dtype)


def _dft_rows(fmat, x3, col, width):
    m, k = fmat.shape
    n2 = x3.shape[1]
    nj = min(DFT_LANES // width, n2)
    return pl.pallas_call(
        functools.partial(_dft_rows_kernel, nj=nj),
        grid=(n2 // nj,),
        in_specs=[pl.BlockSpec((m, k), lambda j: (0, 0)),
                  pl.BlockSpec((k, nj, width), lambda j: (0, j, col))],
        out_specs=pl.BlockSpec((m, nj * width), lambda j: (0, j)),
        out_shape=jax.ShapeDtypeStruct((m, n2 * width), BF16),
        compiler_params=_params("parallel"),
        name="dft_rows",
    )(fmat, x3)


def _spec_kernel(ar_ref, ai_ref, gr_ref, gi_ref, kr_ref, ki_ref):
    ar, ai, gr, gi = ar_ref[...], ai_ref[...], gr_ref[...], gi_ref[...]
    kr_ref[...] = (_dot(gr, ar) - _dot(gi, ai)).astype(BF16)
    ki_ref[...] = (_dot(gr, ai) + _dot(gi, ar)).astype(BF16)


def _filter_spectrum(a4, gr, gi):
    _, _, n2, ch = a4.shape
    nh = gr.shape[0]
    ct = 512
    blk = lambda ri: pl.BlockSpec((None, None, n2, ct), lambda f, j: (ri, f, 0, j))
    gspec = pl.BlockSpec((None, n2, n2), lambda f, j: (f, 0, 0))
    ospec = pl.BlockSpec((None, n2, ct), lambda f, j: (f, 0, j))
    return pl.pallas_call(
        _spec_kernel,
        grid=(nh, ch // ct),
        in_specs=[blk(0), blk(1), gspec, gspec],
        out_specs=[ospec, ospec],
        out_shape=[jax.ShapeDtypeStruct((nh, n2, ch), BF16)] * 2,
        compiler_params=_params("parallel", "parallel"),
        name="filter_spectrum",
    )(a4, a4, gr, gi)


def _mid_kernel(ar_ref, ai_ref, gr_ref, gi_ref, grt_ref, git_ref, kr_ref, ki_ref, br_ref, bi_ref, *, nh):
    f = pl.program_id(0)

    @pl.when(f < nh)
    def _():
        ar, ai, gr, gi = ar_ref[...], ai_ref[...], gr_ref[...], gi_ref[...]
        xr = _dot(gr, ar) - _dot(gi, ai)
        xi = _dot(gr, ai) + _dot(gi, ar)
        kr, ki = kr_ref[...].astype(F32), ki_ref[...].astype(F32)
        yr = (xr * kr - xi * ki).astype(BF16)
        yi = (xr * ki + xi * kr).astype(BF16)
        grt, git = grt_ref[...], git_ref[...]
        br_ref[...] = (_dot(grt, yr) + _dot(git, yi)).astype(BF16)
        bi_ref[...] = (_dot(grt, yi) - _dot(git, yr)).astype(BF16)

    @pl.when(f >= nh)
    def _():
        br_ref[...] = jnp.zeros_like(br_ref)
        bi_ref[...] = jnp.zeros_like(bi_ref)


def _hyena_mid(a4, tabs, kf_r, kf_i, order):
    _, nf, n2, ch = a4.shape
    gr, gi, grt, git = tabs
    nh = gr.shape[0]
    fi = lambda f: jnp.minimum(f, nh - 1)
    blk = lambda ri: pl.BlockSpec((None, None, n2, ch), lambda f: (ri, fi(f), 0, 0))
    kspec = pl.BlockSpec((None, n2, ch), lambda f: (fi(f), 0, order))
    gspec = pl.BlockSpec((None, n2, n2), lambda f: (fi(f), 0, 0))
    ospec = pl.BlockSpec((None, n2, ch), lambda f: (f, 0, 0))
    return pl.pallas_call(
        functools.partial(_mid_kernel, nh=nh),
        grid=(nf,),
        in_specs=[blk(0), blk(1), gspec, gspec, gspec, gspec, kspec, kspec],
        out_specs=[ospec, ospec],
        out_shape=[jax.ShapeDtypeStruct((nf, n2, ch), BF16)] * 2,
        compiler_params=_params("parallel"),
        name="hyena_mid",
    )(a4, a4, gr, gi, grt, git, kf_r, kf_i)


def _inv_kernel(f_ref, br_ref, bi_ref, s_ref, bias_ref, z_ref, g_ref, o_ref, *, nf, nj, ch):
    acc = _dot(f_ref[:, 0:nf], br_ref[...]) + _dot(f_ref[:, nf:2 * nf], bi_ref[...])
    for jj in range(nj):
        y = acc[:, jj * ch:(jj + 1) * ch] * s_ref[...]
        o_ref[:, jj, :] = g_ref[:, jj, :] * (y + bias_ref[...] * z_ref[:, jj, :])


def _hyena_inverse(finv, b_r, b_i, scale, bias, z3, zcol, g3, gcol):
    t1, k2 = finv.shape
    nf = k2 // 2
    n2 = z3.shape[1]
    ch = HY_D
    nj = min(DFT_LANES // ch, n2)
    col = pl.BlockSpec((nf, nj * ch), lambda j: (0, j))
    row = pl.BlockSpec((1, ch), lambda j: (0, 0))
    return pl.pallas_call(
        functools.partial(_inv_kernel, nf=nf, nj=nj, ch=ch),
        grid=(n2 // nj,),
        in_specs=[pl.BlockSpec((t1, k2), lambda j: (0, 0)), col, col, row, row,
                  pl.BlockSpec((t1, nj, ch), lambda j: (0, j, zcol)),
                  pl.BlockSpec((t1, nj, ch), lambda j: (0, j, gcol))],
        out_specs=pl.BlockSpec((t1, nj, ch), lambda j: (0, j, 0)),
        out_shape=jax.ShapeDtypeStruct((t1, n2, ch), F32),
        compiler_params=_params("parallel"),
        name="hyena_inverse",
    )(finv, b_r, b_i, scale, bias, z3, g3)


def _hyena_nf(n):
    nh = (2 * n // FFT_N2) // 2 + 1
    return -(-nh // 16) * 16


def _dft_tables(n):
    n2 = FFT_N2
    n1 = 2 * n // n2
    tot = 2 * n
    two_pi = 2.0 * math.pi

    def cs(num, den):
        ang = (two_pi / den) * (num % den).astype(F32)
        return jnp.cos(ang), jnp.sin(ang)

    nh = n1 // 2 + 1
    nf = _hyena_nf(n)
    f1 = jnp.arange(nh, dtype=jnp.int32)
    t1 = jnp.arange(n1, dtype=jnp.int32)
    c1, s1 = cs(f1[:, None] * t1[None, :], n1)
    zrow = jnp.zeros((nf - nh, n1), F32)
    fwd_full = jnp.concatenate([c1, zrow, -s1, zrow], axis=0).astype(BF16)
    fwd_half = fwd_full[:, :n1 // 2]
    wgt = jnp.where((f1 == 0) | (f1 == n1 // 2), 1.0, 2.0)[:, None]
    zcol = jnp.zeros((n1 // 2, nf - nh), F32)
    inv = jnp.concatenate([(wgt * c1[:, :n1 // 2]).T, zcol, -(wgt * s1[:, :n1 // 2]).T, zcol],
                          axis=1).astype(BF16)
    t2 = jnp.arange(n2, dtype=jnp.int32)
    twr, twi = cs(f1[:, None] * t2[None, :], tot)
    fr, fi = cs(t2[:, None] * t2[None, :], n2)
    twi, fi = -twi, -fi
    gr = twr[:, None, :] * fr[None] - twi[:, None, :] * fi[None]
    gi = twr[:, None, :] * fi[None] + twi[:, None, :] * fr[None]
    tabs = (gr.astype(BF16), gi.astype(BF16),
            gr.transpose(0, 2, 1).astype(BF16), gi.transpose(0, 2, 1).astype(BF16))
    return fwd_full, fwd_half, inv, tabs


def _hyena_long(q, k2, nrm, hy_bias, tables):
    n = q.shape[0]
    n2 = FFT_N2
    n1 = 2 * n // n2
    fwd_full, fwd_half, inv, tabs = tables
    nf = _hyena_nf(n)
    oc = HY_ORDER * HY_D
    ak = _dft_rows(fwd_full, k2.reshape(n1, n2, oc), 0, oc).reshape(2, nf, n2, oc)
    kf_r, kf_i = _filter_spectrum(ak, tabs[0], tabs[1])
    q3 = q.reshape(n1 // 2, n2, 3 * HY_D)
    z3, zcol = q3, 0
    for o in range(HY_ORDER):
        a4 = _dft_rows(fwd_half, z3, zcol, HY_D).reshape(2, nf, n2, HY_D)
        b_r, b_i = _hyena_mid(a4, tabs, kf_r, kf_i, o)
        scale = 1.0 / (2.0 * n * nrm[:, o * HY_D:(o + 1) * HY_D])
        z3 = _hyena_inverse(inv, b_r.reshape(nf, n2 * HY_D), b_i.reshape(nf, n2 * HY_D), scale,
                            hy_bias[o][None, :], z3, zcol, q3, o + 1)
        zcol = 0
    return z3.reshape(n, HY_D)


def _hy_ctx_kernel(v_ref, x1_ref, x2_ref, k0_ref, k1_ref, n0_ref, n1_ref, bias_ref, o_ref, kf, zs, *, n):
    zs[...] = v_ref[...]
    for o, (k_ref, nr_ref, x_ref) in enumerate(((k0_ref, n0_ref, x1_ref), (k1_ref, n1_ref, x2_ref))):
        kf[0:n, :] = k_ref[n:2 * n, :]
        kf[n:2 * n, :] = k_ref[0:n, :]

        def body(s, acc):
            return acc + kf[pl.ds(n - s, n), :] * zs[pl.ds(s, 1), :]

        acc = lax.fori_loop(0, n, body, jnp.zeros((n, LANE), F32))
        z = zs[...]
        zs[...] = x_ref[...] * (acc / nr_ref[...] + bias_ref[o:o + 1, :] * z)
    o_ref[...] = zs[...]


def _hyena_ctx(q, k2, nrm, hy_bias):
    n = q.shape[0]
    nb = HY_D // LANE
    col = lambda c0: pl.BlockSpec((n, LANE), lambda j: (0, c0 + j))
    kcol = lambda c0: pl.BlockSpec((2 * n, LANE), lambda j: (0, c0 + j))
    ncol = lambda c0: pl.BlockSpec((1, LANE), lambda j: (0, c0 + j))
    return pl.pallas_call(
        functools.partial(_hy_ctx_kernel, n=n),
        grid=(nb,),
        in_specs=[col(0), col(nb), col(2 * nb), kcol(0), kcol(nb), ncol(0), ncol(nb),
                  pl.BlockSpec((HY_ORDER, LANE), lambda j: (0, j))],
        out_specs=pl.BlockSpec((n, LANE), lambda j: (0, j)),
        out_shape=jax.ShapeDtypeStruct((n, HY_D), F32),
        scratch_shapes=[pltpu.VMEM((2 * n, LANE), F32), pltpu.VMEM((n, LANE), F32)],
        compiler_params=_params("parallel"),
        name="hyena_ctx",
    )(q, q, q, k2, k2, nrm, nrm, hy_bias)


def _merge_kernel(ya_ref, xs_ref, yf_ref, yb_ref, z_ref, yc_ref, yd_ref, g_ref, h_ref,
                  dv_ref, ng_ref, g1_ref, lg_ref, lb_ref,
                  wa_ref, wb_ref, wc_ref, wd_ref, wo_ref, o_ref):
    y = xs_ref[...].astype(F32) * dv_ref[...] + yf_ref[...] + yb_ref[...]
    gz = y * _silu(z_ref[...].astype(F32))
    ssd = gz * lax.rsqrt(jnp.mean(gz * gz, -1, keepdims=True) + LN_EPS) * ng_ref[...]
    d = D_MODEL
    gate = lambda k: jax.nn.sigmoid(g_ref[:, k * d:(k + 1) * d].astype(F32))
    m = gate(0) * _dot(ya_ref[...].astype(BF16), wa_ref[...])
    m = m + gate(1) * _dot(ssd.astype(BF16), wb_ref[...])
    m = m + gate(2) * _dot(yc_ref[...].astype(BF16), wc_ref[...])
    m = m + gate(3) * _dot(yd_ref[...].astype(BF16), wd_ref[...])
    mix = _dot(m.astype(BF16), wo_ref[...])
    o_ref[...] = _layer_norm(DN_ALPHA * h_ref[...] + g1_ref[...] * mix, lg_ref[...], lb_ref[...])


def _merge(ya, xbc, ydir, p, yc, yd, h, dvec, ng, gate1, lg, lb, wa, wb, wc, wd, wo):
    n = h.shape[0]
    t = 256
    tok = lambda w, col=0: pl.BlockSpec((t, w), lambda i: (i, col))
    vec = lambda w: pl.BlockSpec((1, w), lambda i: (0, 0))
    mat = lambda r: pl.BlockSpec((r, D_MODEL), lambda i: (0, 0))
    return pl.pallas_call(
        _merge_kernel,
        grid=(n // t,),
        in_specs=[tok(CONF_D), tok(SSD_D),
                  pl.BlockSpec((None, t, SSD_D), lambda i: (0, i, 0)),
                  pl.BlockSpec((None, t, SSD_D), lambda i: (1, i, 0)),
                  tok(SSD_D, PZ // SSD_D), tok(HY_D), tok(SC_D), tok(N_BRANCH * D_MODEL, 0), tok(D_MODEL),
                  vec(SSD_D), vec(SSD_D), vec(D_MODEL), vec(D_MODEL), vec(D_MODEL),
                  mat(CONF_D), mat(SSD_D), mat(HY_D), mat(SC_D), mat(D_MODEL)],
        out_specs=tok(D_MODEL),
        out_shape=jax.ShapeDtypeStruct((n, D_MODEL), F32),
        compiler_params=_params("parallel"),
        name="merge",
    )(ya, xbc, ydir, ydir, p, yc, yd, p, h, dvec, ng, gate1, lg, lb, wa, wb, wc, wd, wo)


def _router_kernel(h_ref, sh_ref, sc_ref, w_ref, b_ref, sel_ref, cnt_ref, selt_ref):
    @pl.when(pl.program_id(0) == 0)
    def _():
        cnt_ref[...] = jnp.zeros_like(cnt_ref)

    u = h_ref[...] * (1.0 + sc_ref[...]) + sh_ref[...]
    lg = jnp.dot(u, w_ref[...], precision=HIGHEST, preferred_element_type=F32) + b_ref[...]
    lane = lax.broadcasted_iota(jnp.int32, lg.shape, 1).astype(F32)
    neg = -1e30
    big = 1e9
    gl = jnp.where(lane < MOE_GROUPS, lg, neg)
    gmax = jnp.max(gl, -1, keepdims=True)
    gsel = jnp.min(jnp.where(gl == gmax, lane, big), -1, keepdims=True)
    gprob = 1.0 / jnp.sum(jnp.where(lane < MOE_GROUPS, jnp.exp(lg - gmax), 0.0), -1, keepdims=True)
    lo = MOE_GROUPS + gsel * MOE_EPG
    el = jnp.where(jnp.abs(lane - lo - (MOE_EPG - 1) / 2.0) < MOE_EPG / 2.0, lg, neg)
    m1 = jnp.max(el, -1, keepdims=True)
    i1 = jnp.min(jnp.where(el == m1, lane, big), -1, keepdims=True)
    el2 = jnp.where(lane == i1, neg, el)
    m2 = jnp.max(el2, -1, keepdims=True)
    i2 = jnp.min(jnp.where(el2 == m2, lane, big), -1, keepdims=True)
    t = jnp.exp(m2 - m1)
    w1 = gprob / (1.0 + t)
    w2 = gprob * t / (1.0 + t)
    oh1 = jnp.where(lane == i1, 1.0, 0.0)
    oh2 = jnp.where(lane == i2, 1.0, 0.0)
    oh = oh1 + oh2
    tt = lg.shape[0]
    li = lax.broadcasted_iota(jnp.int32, (tt, tt), 0)
    si = lax.broadcasted_iota(jnp.int32, (tt, tt), 1)
    before = _dot(jnp.where(li > si, 1.0, 0.0).astype(BF16), oh.astype(BF16)) + cnt_ref[...]
    r1 = jnp.sum(oh1 * before, -1, keepdims=True)
    r2 = jnp.sum(oh2 * before, -1, keepdims=True)
    cnt_ref[...] += jnp.sum(oh, axis=0, keepdims=True)
    cols = (i1 - MOE_GROUPS, i2 - MOE_GROUPS, w1, w2, r1, r2)
    sel = jnp.zeros_like(lg)
    for k, v in enumerate(cols):
        sel = jnp.where(lane == k, v, sel)
    sel_ref[...] = sel
    selt_ref[...] = sel.T[0:SUBLANE, :]


def _router(h, shift, scale, wr, br):
    n = h.shape[0]
    t = 256
    tok = lambda w: pl.BlockSpec((t, w), lambda i: (i, 0))
    vec = lambda w: pl.BlockSpec((1, w), lambda i: (0, 0))
    return pl.pallas_call(
        _router_kernel,
        grid=(n // t,),
        in_specs=[tok(D_MODEL), vec(D_MODEL), vec(D_MODEL),
                  pl.BlockSpec((D_MODEL, LANE), lambda i: (0, 0)), vec(LANE)],
        out_specs=[tok(LANE), vec(LANE), pl.BlockSpec((SUBLANE, t), lambda i: (0, i))],
        out_shape=[jax.ShapeDtypeStruct((n, LANE), F32), jax.ShapeDtypeStruct((1, LANE), F32),
                   jax.ShapeDtypeStruct((SUBLANE, n), F32)],
        compiler_params=_params("arbitrary"),
        name="router",
    )(h, shift, scale, wr, br)


MOE_T = 256
ROW_WORDS = D_MODEL // 2


def _pack_rows(x):
    c = x.shape[1] // 2
    bits = lambda v: lax.bitcast_convert_type(v.astype(BF16).astype(F32), jnp.uint32)
    return bits(x[:, :c]) | (bits(x[:, c:]) >> 16)


def _unpack_rows(w):
    hi = lax.bitcast_convert_type(w & jnp.uint32(0xFFFF0000), F32)
    lo = lax.bitcast_convert_type(w << 16, F32)
    return jnp.concatenate([hi, lo], axis=1)


def _dispatch_kernel(dst_ref, h_ref, sh_ref, sc_ref, zero_hbm, xin_hbm, ubuf, sem, *, nt):
    del zero_hbm
    t = MOE_T
    i = pl.program_id(0)
    slot = i % 2

    def wait_slot(s):
        for _ in range(MOE_TOP_K):
            pltpu.make_async_copy(ubuf.at[s], xin_hbm.at[pl.ds(0, t), :], sem.at[s]).wait()

    @pl.when(i >= 2)
    def _():
        wait_slot(slot)

    ubuf[slot] = _pack_rows(h_ref[...] * (1.0 + sc_ref[...]) + sh_ref[...])

    def issue(r, carry):
        for k in range(MOE_TOP_K):
            pltpu.make_async_copy(ubuf.at[slot, pl.ds(r, 1), :],
                                  xin_hbm.at[pl.ds(dst_ref[0, 0, k * t + r], 1), :], sem.at[slot]).start()
        return carry

    lax.fori_loop(0, t, issue, 0, unroll=8)

    @pl.when(i == nt - 1)
    def _():
        wait_slot(slot)
        if nt > 1:
            wait_slot(1 - slot)


def _dispatch(pos_t, h, shift, scale, n_rows):
    n = h.shape[0]
    t = MOE_T
    nt = n // t
    vec = pl.BlockSpec((1, D_MODEL), lambda i: (0, 0))
    return pl.pallas_call(
        functools.partial(_dispatch_kernel, nt=nt),
        grid=(nt,),
        in_specs=[pl.BlockSpec((1, 1, MOE_TOP_K * t), lambda i: (i, 0, 0), memory_space=pltpu.SMEM),
                  pl.BlockSpec((t, D_MODEL), lambda i: (i, 0)), vec, vec,
                  pl.BlockSpec(memory_space=pl.ANY)],
        out_specs=pl.BlockSpec(memory_space=pl.ANY),
        out_shape=jax.ShapeDtypeStruct((n_rows, ROW_WORDS), jnp.uint32),
        scratch_shapes=[pltpu.VMEM((2, t, ROW_WORDS), jnp.uint32), pltpu.SemaphoreType.DMA((2,))],
        input_output_aliases={4: 0},
        compiler_params=_params("arbitrary"),
        name="dispatch",
    )(pos_t, h, shift, scale, jnp.zeros((n_rows, ROW_WORDS), jnp.uint32))


def _expert_kernel(be_ref, nu_ref, x_ref, wg_ref, wu_ref, wd_ref, o_ref, wgb, wub, wdb):
    b = pl.program_id(0)

    @pl.when((b == 0) | (be_ref[b] != be_ref[jnp.maximum(b - 1, 0)]))
    def _():
        wgb[...] = wg_ref[...].astype(BF16)
        wub[...] = wu_ref[...].astype(BF16)
        wdb[...] = wd_ref[...].astype(BF16)

    @pl.when(b < nu_ref[0])
    def _():
        x = _unpack_rows(x_ref[...]).astype(BF16)
        hid = _silu(_dot(x, wgb[...])) * _dot(x, wub[...])
        o_ref[...] = _pack_rows(_dot(hid.astype(BF16), wdb[...]))

    @pl.when(b >= nu_ref[0])
    def _():
        o_ref[...] = jnp.zeros_like(o_ref)


def _experts(xin, block_e, n_used, wg, wu, wd, layer):
    n_blocks = block_e.shape[0]
    gs = pltpu.PrefetchScalarGridSpec(
        num_scalar_prefetch=2,
        grid=(n_blocks,),
        in_specs=[pl.BlockSpec((MOE_ROWS, ROW_WORDS), lambda b, be, nu: (b, 0)),
                  pl.BlockSpec((None, None, D_MODEL, MOE_FF), lambda b, be, nu: (layer, be[b], 0, 0)),
                  pl.BlockSpec((None, None, D_MODEL, MOE_FF), lambda b, be, nu: (layer, be[b], 0, 0)),
                  pl.BlockSpec((None, None, MOE_FF, D_MODEL), lambda b, be, nu: (layer, be[b], 0, 0))],
        out_specs=pl.BlockSpec((MOE_ROWS, ROW_WORDS), lambda b, be, nu: (b, 0)),
        scratch_shapes=[pltpu.VMEM((D_MODEL, MOE_FF), BF16), pltpu.VMEM((D_MODEL, MOE_FF), BF16),
                        pltpu.VMEM((MOE_FF, D_MODEL), BF16)],
    )
    return pl.pallas_call(
        _expert_kernel,
        grid_spec=gs,
        out_shape=jax.ShapeDtypeStruct((n_blocks * MOE_ROWS, ROW_WORDS), jnp.uint32),
        compiler_params=_params("arbitrary"),
        name="experts",
    )(block_e, n_used, xin, wg, wu, wd)


def _combine_kernel(pos_ref, posn_ref, y_hbm, h_ref, sel_ref, g2_ref, lg_ref, lb_ref, o_ref, ybuf, sem, *, nt):
    t = MOE_T
    i = pl.program_id(0)
    slot = i % 2

    def gather(p_ref, s):
        def issue(r, carry):
            pltpu.make_async_copy(y_hbm.at[pl.ds(p_ref[0, 0, r], 1), :], ybuf.at[s, pl.ds(r, 1), :],
                                  sem.at[s]).start()
            return carry
        lax.fori_loop(0, MOE_TOP_K * t, issue, 0, unroll=8)

    @pl.when(i == 0)
    def _():
        gather(pos_ref, 0)

    @pl.when(i + 1 < nt)
    def _():
        gather(posn_ref, 1 - slot)

    pltpu.make_async_copy(y_hbm.at[pl.ds(0, MOE_TOP_K * t), :], ybuf.at[slot], sem.at[slot]).wait()
    ffn = (sel_ref[:, 2:3] * _unpack_rows(ybuf[slot, 0:t, :])
           + sel_ref[:, 3:4] * _unpack_rows(ybuf[slot, t:2 * t, :]))
    o_ref[...] = _layer_norm(DN_ALPHA * h_ref[...] + g2_ref[...] * ffn, lg_ref[...], lb_ref[...])


def _combine(y, pos_t, h, sel, gate2, lg, lb):
    n = h.shape[0]
    t = MOE_T
    nt = n // t
    tok = lambda w: pl.BlockSpec((t, w), lambda i: (i, 0))
    vec = pl.BlockSpec((1, D_MODEL), lambda i: (0, 0))
    return pl.pallas_call(
        functools.partial(_combine_kernel, nt=nt),
        grid=(nt,),
        in_specs=[pl.BlockSpec((1, 1, MOE_TOP_K * t), lambda i: (i, 0, 0), memory_space=pltpu.SMEM),
                  pl.BlockSpec((1, 1, MOE_TOP_K * t), lambda i: (jnp.minimum(i + 1, nt - 1), 0, 0),
                               memory_space=pltpu.SMEM),
                  pl.BlockSpec(memory_space=pl.ANY),
                  tok(D_MODEL), tok(LANE), vec, vec, vec],
        out_specs=tok(D_MODEL),
        out_shape=jax.ShapeDtypeStruct((n, D_MODEL), F32),
        scratch_shapes=[pltpu.VMEM((2, MOE_TOP_K * t, ROW_WORDS), jnp.uint32), pltpu.SemaphoreType.DMA((2,))],
        compiler_params=_params("arbitrary"),
        name="combine",
    )(pos_t, pos_t, y, h, sel, gate2, lg, lb)


def _moe(h, shift, scale, gate2, lg, lb, wr, br, wg, wu, wd, layer):
    n = h.shape[0]
    t = MOE_T
    nt = n // t
    sel, cnt, selt = _router(h, shift, scale, wr, br)
    counts = cnt[0, MOE_GROUPS:MOE_GROUPS + MOE_EXPERTS].astype(jnp.int32)
    padded = (counts + MOE_ROWS - 1) // MOE_ROWS * MOE_ROWS
    pad_end = jnp.cumsum(padded)
    pad_start = pad_end - padded
    n_blocks = (n * MOE_TOP_K + MOE_EXPERTS * (MOE_ROWS - 1) + MOE_ROWS - 1) // MOE_ROWS
    blk_row = jnp.arange(n_blocks, dtype=jnp.int32) * MOE_ROWS
    block_e = jnp.minimum(jnp.sum((blk_row[:, None] >= pad_end[None, :]).astype(jnp.int32), axis=1),
                          MOE_EXPERTS - 1)
    n_used = (pad_end[-1:] // MOE_ROWS).astype(jnp.int32)
    e_kt = selt[0:MOE_TOP_K].astype(jnp.int32)
    ids = jnp.arange(MOE_EXPERTS, dtype=jnp.int32)[None, :, None]
    start_kt = jnp.sum(jnp.where(e_kt[:, None, :] == ids, pad_start[None, :, None], 0), axis=1)
    pos_kt = start_kt + selt[4:4 + MOE_TOP_K].astype(jnp.int32)
    pos_t = pos_kt.reshape(MOE_TOP_K, nt, t).transpose(1, 0, 2).reshape(nt, 1, MOE_TOP_K * t)
    xin = _dispatch(pos_t, h, shift, scale, n_blocks * MOE_ROWS)
    y = _experts(xin, block_e, n_used, wg, wu, wd, layer)
    return _combine(y, pos_t, h, sel, gate2, lg, lb)


def _mixer(h, mod, lw, ssd_init, tables, *, latent, need_mix):
    n = h.shape[0]
    p, dt_raw = _inproj(h, mod[0], mod[1], lw["w_in"], lw["layer"])
    xbc = _conv3(p, PX, SSD_XBC, lw["ssd_conv_w"], lw["ssd_conv_b"], silu=True, out_dtype=BF16)
    ydir, finals = _ssd_scan(xbc, dt_raw, lw["ssd_dt_bias"], lw["ssd_a_log"], ssd_init)
    if not need_mix:
        return None, finals
    conf_w = (lw["conf_dw_w"], lw["conf_dw_b"], lw["conf_ln_g"], lw["conf_ln_b"])
    ya = _conformer_grid(p, *conf_w) if latent else _conformer(p, *conf_w, dil=1)
    q = _conv3(p, PC, 3 * HY_D, lw["hy_short_w"], lw["hy_short_b"], silu=False, out_dtype=F32)
    k2, nrm = _hyena_filters(lw["featx_lat" if latent else "featx_ctx"], lw["hy_filter"], n)
    if latent:
        yc = _hyena_long(q, k2, nrm, lw["hy_bias"], tables)
    else:
        yc = _hyena_ctx(q, k2, nrm, lw["hy_bias"])
    yd = _gated_conv(p, lw["sc_conv_w"])
    h = _merge(ya, xbc, ydir, p, yc, yd, h, lw["ssd_dvec"], lw["ssd_norm_g"], mod[2], lw["ln_g0"], lw["ln_b0"],
               lw["w_branch_a"], lw["w_branch_b"], lw["w_branch_c"], lw["w_branch_d"], lw["w_out"])
    return h, finals


def _positional_features(n):
    t01 = jnp.linspace(0.0, 1.0, n, dtype=F32)[:, None]
    omega = (2.0 * math.pi / n) * jnp.arange(n, dtype=F32)[:, None]
    bands = jnp.linspace(1e-4, HY_BANDS - 1, HY_BANDS, dtype=F32)
    feat = jnp.concatenate([t01, jnp.cos(bands * omega), -jnp.sin(bands * omega)], axis=-1)
    featx = jnp.concatenate([feat, jnp.zeros((1, HY_EMB), F32), jnp.flip(feat[1:], axis=0)], axis=0)
    return _pad_lanes(featx)


def _relayout_w_in(w_in):
    seg = lambda a, b: w_in[:, :, a:b]
    ob = OFF_B
    parts = [seg(OFF_G, OFF_G + N_BRANCH * D_MODEL),
             seg(OFF_A, OFF_B),
             seg(OFF_C, OFF_D),
             seg(OFF_D, OFF_G),
             seg(ob + SSD_D, ob + SSD_D + SSD_XBC),
             seg(ob, ob + SSD_D),
             seg(ob + SSD_D + SSD_XBC, OFF_C),
             jnp.zeros(w_in.shape[:2] + (NP - PDT - 2 * SSD_HEADS,), w_in.dtype)]
    return jnp.concatenate(parts, axis=-1).astype(BF16)


def _pad_lanes(v):
    return jnp.pad(v, ((0, 0), (0, LANE - v.shape[-1])))


def kernel(x, c, ctx, c_ctx, w_mod, b_mod, ln_g, ln_b, w_in, conf_dw_w, conf_dw_b, conf_ln_g, conf_ln_b,
           ssd_conv_w, ssd_conv_b, ssd_a_log, ssd_dt_bias, ssd_d, ssd_norm_g, hy_short_w, hy_short_b,
           hy_w1, hy_b1, hy_w2, hy_b2, hy_freq, hy_w3, hy_bias, sc_conv_w, w_branch_a, w_branch_b,
           w_branch_c, w_branch_d, w_out, rt_group_w, rt_group_b, rt_expert_w, rt_expert_b,
           ex_w_gate, ex_w_up, ex_w_down):
    assert x.shape[0] == 1 and ctx.shape[0] == 1
    n_lat, n_ctx = x.shape[1], ctx.shape[1]
    depth = w_in.shape[0]

    cv = jnp.concatenate([c, c_ctx[None, :], jnp.zeros((SUBLANE - 2, D_MODEL), F32)], axis=0)
    mods = _mod_vectors(cv, w_mod, b_mod)
    w_in_p = _relayout_w_in(w_in)
    tables = _dft_tables(n_lat)
    featx_lat = _positional_features(n_lat)
    featx_ctx = _positional_features(n_ctx)
    deltas = jnp.abs(jnp.linspace(HY_MIN_DECAY, HY_MAX_DECAY, HY_N_FILT, dtype=F32))
    deltas_d = deltas.reshape(HY_ORDER, 2, HY_D).transpose(1, 0, 2).reshape(2, 1, HY_ORDER * HY_D)
    router_w = jnp.concatenate([rt_group_w, rt_expert_w,
                                jnp.zeros((depth, D_MODEL, LANE - MOE_GROUPS - MOE_EXPERTS), F32)], axis=-1)
    router_b = jnp.concatenate([rt_group_b, rt_expert_b,
                                jnp.zeros((depth, LANE - MOE_GROUPS - MOE_EXPERTS), F32)], axis=-1)
    ssd_zero = jnp.zeros((2,) + SSD_STATE_SHAPE, F32)

    h_lat, h_ctx = x[0], ctx[0]
    for l in range(depth):
        row = lambda v: v[None, :]
        lw = dict(
            w_in=w_in_p, layer=l, conf_dw_w=conf_dw_w[l], conf_dw_b=row(conf_dw_b[l]), conf_ln_g=row(conf_ln_g[l]),
            conf_ln_b=row(conf_ln_b[l]), ssd_conv_w=ssd_conv_w[l], ssd_conv_b=row(ssd_conv_b[l]),
            ssd_a_log=_pad_lanes(ssd_a_log[l].reshape(1, -1)), ssd_dt_bias=_pad_lanes(ssd_dt_bias[l].reshape(1, -1)),
            ssd_dvec=row(jnp.repeat(ssd_d[l], SSD_HEAD_DIM)), ssd_norm_g=row(ssd_norm_g[l]),
            hy_short_w=hy_short_w[l], hy_short_b=row(hy_short_b[l]),
            hy_filter=_filter_weights(hy_w1[l], hy_b1[l], hy_w2[l], hy_b2[l], hy_freq[l], hy_w3[l], deltas_d),
            hy_bias=hy_bias[l], featx_lat=featx_lat, featx_ctx=featx_ctx,
            sc_conv_w=sc_conv_w[l], ln_g0=row(ln_g[l, 0]), ln_b0=row(ln_b[l, 0]),
            w_branch_a=w_branch_a[l].astype(BF16), w_branch_b=w_branch_b[l].astype(BF16),
            w_branch_c=w_branch_c[l].astype(BF16), w_branch_d=w_branch_d[l].astype(BF16),
            w_out=w_out[l].astype(BF16))
        moe_w = (router_w[l], row(router_b[l]), ex_w_gate, ex_w_up, ex_w_down, l)
        last = l == depth - 1
        d = D_MODEL
        mod_lat = [mods[l, 0:1, k * d:(k + 1) * d] for k in range(6)]
        mod_ctx = [mods[l, 1:2, k * d:(k + 1) * d] for k in range(6)]

        mix_ctx, ctx_states = _mixer(h_ctx, mod_ctx, lw, ssd_zero, None, latent=False, need_mix=not last)
        h_lat, _ = _mixer(h_lat, mod_lat, lw, ctx_states, tables, latent=True, need_mix=True)
        h_lat = _moe(h_lat, mod_lat[3], mod_lat[4], mod_lat[5], row(ln_g[l, 1]), row(ln_b[l, 1]), *moe_w)
        if not last:
            h_ctx = _moe(mix_ctx, mod_ctx[3], mod_ctx[4], mod_ctx[5], row(ln_g[l, 1]), row(ln_b[l, 1]), *moe_w)
    return h_lat[None]
```

```python
import functools
import math

import jax
import jax.numpy as jnp
from jax import lax
from jax.experimental import pallas as pl
from jax.experimental.pallas import tpu as pltpu

F32 = jnp.float32
BF16 = jnp.bfloat16
HIGHEST = lax.Precision.HIGHEST

D_MODEL = 1024
DEPTH = 4
GRID_W = 64
CONF_D = 512
CONF_K = 31
SSD_D = 768
SSD_HEADS = 12
SSD_HEAD_DIM = 64
SSD_GROUPS = 4
SSD_HPG = SSD_HEADS // SSD_GROUPS
SSD_STATE = 128
SSD_CHUNK = 128
SSD_BC = SSD_GROUPS * SSD_STATE
SSD_XBC = SSD_D + 2 * SSD_BC
SSD_PROJ = SSD_D + SSD_XBC + 2 * SSD_HEADS
HY_D = 512
HY_ORDER = 2
HY_EMB = 33
HY_BANDS = (HY_EMB - 1) // 2
HY_HID = 64
HY_N_FILT = HY_ORDER * 2 * HY_D
HY_MIN_DECAY = math.log(1e-2) / 1.5
HY_MAX_DECAY = math.log(1e-2) / 0.3
SC_D = 512
N_BRANCH = 4
OFF_A = 0
OFF_B = OFF_A + 2 * CONF_D
OFF_C = OFF_B + SSD_PROJ
OFF_D = OFF_C + 3 * HY_D
OFF_G = OFF_D + 3 * SC_D
MOE_GROUPS = 4
MOE_EPG = 8
MOE_EXPERTS = MOE_GROUPS * MOE_EPG
MOE_TOP_K = 2
MOE_FF = 512
DN_ALPHA = (2 * DEPTH) ** 0.25
LN_EPS = 1e-5

PG = 0
PA = PG + N_BRANCH * D_MODEL
PC = PA + 2 * CONF_D
PD = PC + 3 * HY_D
PX = PD + 3 * SC_D
PZ = PX + SSD_XBC
PDT = PZ + SSD_D
INPROJ_TN = 1024
NP = -(-(PDT + 128) // INPROJ_TN) * INPROJ_TN

LANE = 128
SUBLANE = 8
FFT_N2 = 256
MOE_ROWS = 256
VMEM_LIMIT = 48 * 1024 * 1024


def _params(*sem):
    return pltpu.CompilerParams(dimension_semantics=sem, vmem_limit_bytes=VMEM_LIMIT)


def _sigmoid(x):
    return 0.5 * jnp.tanh(0.5 * x) + 0.5


def _silu(x):
    return x * _sigmoid(x)


def _layer_norm(x, g, b):
    mu = jnp.mean(x, -1, keepdims=True)
    xc = x - mu
    var = jnp.mean(xc * xc, -1, keepdims=True)
    return xc * lax.rsqrt(var + LN_EPS) * g + b


def _dot(a, b):
    return jnp.dot(a, b, preferred_element_type=F32)


def _mod_kernel(cv_ref, w_ref, b_ref, o_ref):
    o_ref[...] = jnp.dot(_silu(cv_ref[...]), w_ref[...], precision=HIGHEST,
                         preferred_element_type=F32) + b_ref[...]


def _mod_vectors(cv, w_mod, b_mod):
    tn = 1536
    return pl.pallas_call(
        _mod_kernel,
        grid=(DEPTH, 6 * D_MODEL // tn),
        in_specs=[pl.BlockSpec((SUBLANE, D_MODEL), lambda l, j: (0, 0)),
                  pl.BlockSpec((None, D_MODEL, tn), lambda l, j: (l, 0, j)),
                  pl.BlockSpec((None, 1, tn), lambda l, j: (l, 0, j))],
        out_specs=pl.BlockSpec((None, SUBLANE, tn), lambda l, j: (l, 0, j)),
        out_shape=jax.ShapeDtypeStruct((DEPTH, SUBLANE, 6 * D_MODEL), F32),
        compiler_params=_params("parallel", "parallel"),
        name="mod_vectors",
    )(cv, w_mod, b_mod.reshape(DEPTH, 1, 6 * D_MODEL))


def _inproj_kernel(x_ref, sh_ref, sc_ref, w_ref, o_ref, dt_ref, xb_ref, *, nj):
    j = pl.program_id(1)

    @pl.when(j == 0)
    def _():
        xb_ref[...] = (x_ref[...] * (1.0 + sc_ref[...]) + sh_ref[...]).astype(BF16)

    res = _dot(xb_ref[...], w_ref[...])
    o_ref[...] = res.astype(BF16)

    @pl.when(j == nj - 1)
    def _():
        off = PDT - (nj - 1) * INPROJ_TN
        dt_ref[...] = res[:, off:off + LANE]


def _inproj(h, shift, scale, w, layer):
    n = h.shape[0]
    tm = min(n, 2048)
    tn = INPROJ_TN
    nj = NP // tn
    assert PDT >= (nj - 1) * tn
    return pl.pallas_call(
        functools.partial(_inproj_kernel, nj=nj),
        grid=(n // tm, nj),
        in_specs=[pl.BlockSpec((tm, D_MODEL), lambda i, j: (i, 0)),
                  pl.BlockSpec((1, D_MODEL), lambda i, j: (0, 0)),
                  pl.BlockSpec((1, D_MODEL), lambda i, j: (0, 0)),
                  pl.BlockSpec((None, D_MODEL, tn), lambda i, j: (layer, 0, j))],
        out_specs=[pl.BlockSpec((tm, tn), lambda i, j: (i, j)),
                   pl.BlockSpec((tm, LANE), lambda i, j: (i, 0))],
        out_shape=[jax.ShapeDtypeStruct((n, NP), BF16), jax.ShapeDtypeStruct((n, LANE), F32)],
        scratch_shapes=[pltpu.VMEM((tm, D_MODEL), BF16)],
        compiler_params=_params("parallel", "arbitrary"),
        name="inproj",
    )(h, shift, scale, w)


def _shifted(x, prev_row, next_row):
    t = x.shape[0]
    row = lax.broadcasted_iota(jnp.int32, x.shape, 0)
    xm = jnp.where(row == 0, prev_row, pltpu.roll(x, 1, 0))
    xp = jnp.where(row == t - 1, next_row, pltpu.roll(x, t - 1, 0))
    return xm, xp


HALO_ROWS = 16


def _conv3_kernel(cur_ref, prev_ref, next_ref, w_ref, b_ref, o_ref, *, silu, nt):
    i = pl.program_id(0)
    x = cur_ref[...].astype(F32)
    pv = jnp.where(i > 0, prev_ref[HALO_ROWS - 1:HALO_ROWS, :].astype(F32), 0.0)
    nx = jnp.where(i < nt - 1, next_ref[0:1, :].astype(F32), 0.0)
    xm, xp = _shifted(x, pv, nx)
    y = w_ref[0:1, :] * xm + w_ref[1:2, :] * x + w_ref[2:3, :] * xp + b_ref[...]
    o_ref[...] = (_silu(y) if silu else y).astype(o_ref.dtype)


def _halo_specs(t, ct, n, col0):
    rb = t // HALO_ROWS
    last = n // HALO_ROWS - 1
    return [pl.BlockSpec((t, ct), lambda i, j: (i, col0 + j)),
            pl.BlockSpec((HALO_ROWS, ct), lambda i, j: (jnp.maximum(i * rb - 1, 0), col0 + j)),
            pl.BlockSpec((HALO_ROWS, ct), lambda i, j: (jnp.minimum((i + 1) * rb, last), col0 + j))]


def _conv3(p, col, width, w, b, *, silu, out_dtype):
    n = p.shape[0]
    t = min(n, 1024)
    ct = 256
    nt = n // t
    return pl.pallas_call(
        functools.partial(_conv3_kernel, silu=silu, nt=nt),
        grid=(nt, width // ct),
        in_specs=_halo_specs(t, ct, n, col // ct) + [
            pl.BlockSpec((3, ct), lambda i, j: (0, j)),
            pl.BlockSpec((1, ct), lambda i, j: (0, j))],
        out_specs=pl.BlockSpec((t, ct), lambda i, j: (i, j)),
        out_shape=jax.ShapeDtypeStruct((n, width), out_dtype),
        compiler_params=_params("parallel", "parallel"),
        name="conv3",
    )(p, p, p, w, b)


def _gconv_kernel(bg_ref, cc_ref, cp_ref, cn_ref, xc_ref, xp_ref, xn_ref, w_ref, o_ref, *, nt):
    i = pl.program_id(0)
    f = lambda v: v.astype(F32)
    last = slice(HALO_ROWS - 1, HALO_ROWS)
    x = f(cc_ref[...]) * f(xc_ref[...])
    pv = jnp.where(i > 0, f(cp_ref[last, :]) * f(xp_ref[last, :]), 0.0)
    nx = jnp.where(i < nt - 1, f(cn_ref[0:1, :]) * f(xn_ref[0:1, :]), 0.0)
    xm, xp = _shifted(x, pv, nx)
    o_ref[...] = f(bg_ref[...]) * (w_ref[0:1, :] * xm + w_ref[1:2, :] * x + w_ref[2:3, :] * xp)


def _gated_conv(p, w):
    n = p.shape[0]
    t = min(n, 1024)
    ct = 256
    nt = n // t
    nb = SC_D // ct
    return pl.pallas_call(
        functools.partial(_gconv_kernel, nt=nt),
        grid=(nt, nb),
        in_specs=([pl.BlockSpec((t, ct), lambda i, j: (i, PD // ct + j))]
                  + _halo_specs(t, ct, n, PD // ct + nb)
                  + _halo_specs(t, ct, n, PD // ct + 2 * nb)
                  + [pl.BlockSpec((3, ct), lambda i, j: (0, j))]),
        out_specs=pl.BlockSpec((t, ct), lambda i, j: (i, j)),
        out_shape=jax.ShapeDtypeStruct((n, SC_D), F32),
        compiler_params=_params("parallel", "parallel"),
        name="gated_conv",
    )(p, p, p, p, p, p, p, w)


CONF_RB = 64


def _conf_kernel(vc, gc, vp, gp, vn, gn, w_ref, b_ref, lg_ref, lb_ref, o_ref, buf, *, t, halo, dil, nt):
    i = pl.program_id(0)
    glu = lambda v, g: v.astype(F32) * _sigmoid(g.astype(F32))
    buf[halo:halo + t, :] = glu(vc[...], gc[...])
    buf[0:halo, :] = jnp.where(i > 0, glu(vp[t - halo:t, :], gp[t - halo:t, :]), 0.0)
    buf[halo + t:halo + t + halo, :] = jnp.where(i < nt - 1, glu(vn[0:halo, :], gn[0:halo, :]), 0.0)

    def block(r0):
        acc = jnp.zeros((CONF_RB, CONF_D), F32)
        for j in range(CONF_K):
            off = halo + (j - CONF_K // 2) * dil
            acc = acc + w_ref[j:j + 1, :] * buf[pl.ds(r0 + off, CONF_RB), :]
        v = _layer_norm(acc + b_ref[...], lg_ref[...], lb_ref[...])
        o_ref[pl.ds(r0, CONF_RB), :] = _silu(v)

    if dil % CONF_RB == 0:
        def body(rb, carry):
            block(pl.multiple_of(rb * CONF_RB, CONF_RB))
            return carry
        lax.fori_loop(0, t // CONF_RB, body, 0)
    else:
        for rb in range(t // CONF_RB):
            block(rb * CONF_RB)


def _conformer(p, w, b, lg, lb, *, dil):
    n = p.shape[0]
    t = min(n, 1024)
    nt = n // t
    halo = -(-(CONF_K // 2) * dil // SUBLANE) * SUBLANE
    assert halo <= t
    cb = PA // CONF_D

    def spec(col, shift):
        return pl.BlockSpec((t, CONF_D), lambda i: (jnp.clip(i + shift, 0, nt - 1), col))

    vec = pl.BlockSpec((1, CONF_D), lambda i: (0, 0))
    return pl.pallas_call(
        functools.partial(_conf_kernel, t=t, halo=halo, dil=dil, nt=nt),
        grid=(nt,),
        in_specs=[spec(cb, 0), spec(cb + 1, 0), spec(cb, -1), spec(cb + 1, -1), spec(cb, 1), spec(cb + 1, 1),
                  pl.BlockSpec((CONF_K, CONF_D), lambda i: (0, 0)), vec, vec, vec],
        out_specs=pl.BlockSpec((t, CONF_D), lambda i: (i, 0)),
        out_shape=jax.ShapeDtypeStruct((n, CONF_D), F32),
        scratch_shapes=[pltpu.VMEM((t + 2 * halo, CONF_D), F32)],
        compiler_params=_params("parallel"),
        name="conformer",
    )(p, p, p, p, p, p, w, b, lg, lb)


CONF_COLS = 16
CONF_ROWS_PER_ITER = 4


def _conf_grid_kernel(v_ref, g_ref, w_ref, b_ref, lg_ref, lb_ref, o_ref, buf, *, rows):
    half = CONF_K // 2
    zeros = jnp.zeros((half,) + buf.shape[1:], F32)
    buf[0:half] = zeros
    buf[half + rows:half + rows + half] = zeros
    buf[half:half + rows] = v_ref[...].astype(F32) * _sigmoid(g_ref[...].astype(F32))

    rb = CONF_ROWS_PER_ITER

    def body(it, carry):
        r0 = it * rb
        accs = [None] * rb
        for j in range(CONF_K):
            wj = w_ref[j:j + 1, :]
            for s in range(rb):
                term = wj * buf[r0 + s + j]
                accs[s] = term if j == 0 else accs[s] + term
        for s in range(rb):
            v = _layer_norm(accs[s] + b_ref[...], lg_ref[...], lb_ref[...])
            o_ref[r0 + s] = _silu(v).astype(o_ref.dtype)
        return carry

    lax.fori_loop(0, rows // rb, body, 0)


def _conformer_grid(p, w, b, lg, lb):
    n = p.shape[0]
    rows = n // GRID_W
    p3 = p.reshape(rows, GRID_W, p.shape[1])
    cb = PA // CONF_D
    vec = pl.BlockSpec((1, CONF_D), lambda j: (0, 0))
    blk = lambda col: pl.BlockSpec((rows, CONF_COLS, CONF_D), lambda j: (0, j, col))
    out = pl.pallas_call(
        functools.partial(_conf_grid_kernel, rows=rows),
        grid=(GRID_W // CONF_COLS,),
        in_specs=[blk(cb), blk(cb + 1), pl.BlockSpec((CONF_K, CONF_D), lambda j: (0, 0)), vec, vec, vec],
        out_specs=blk(0),
        out_shape=jax.ShapeDtypeStruct((rows, GRID_W, CONF_D), BF16),
        scratch_shapes=[pltpu.VMEM((rows + 2 * (CONF_K // 2), CONF_COLS, CONF_D), F32)],
        compiler_params=_params("parallel"),
        name="conformer_grid",
    )(p3, p3, w, b, lg, lb)
    return out.reshape(n, CONF_D)


SSD_STATE_SHAPE = (SSD_HEADS, SSD_STATE, SSD_HEAD_DIM)
SSD_CHUNKS_PER_STEP = 2


def _ssd_kernel(xbc_ref, dt_ref, dtb_ref, alog_ref, init_ref, y_ref, fin_ref, h_ref, *, ns, cps):
    d = pl.program_id(0)
    c = pl.program_id(1)
    q = SSD_CHUNK
    hd = SSD_HEAD_DIM

    @pl.when(c == 0)
    def _():
        h_ref[...] = init_ref[...]

    lane = lax.broadcasted_iota(jnp.int32, (q, LANE), 1)
    head = lane < SSD_HEADS
    li = lax.broadcasted_iota(jnp.int32, (q, q), 0)
    si = lax.broadcasted_iota(jnp.int32, (q, q), 1)
    mask = (li - si) * (1 - 2 * d) >= 0
    tri = mask.astype(F32)
    a_rate = -jnp.exp(alog_ref[...])

    def one_chunk(r0):
        rows = pl.ds(r0, q)
        raw = dt_ref[rows, :] + dtb_ref[...]
        dt_all = jnp.maximum(raw, 0.0) + jnp.log(1.0 + jnp.exp(-jnp.abs(raw)))
        ld_all = dt_all * a_rate
        dt_d = jnp.where(head, jnp.where(d == 0, dt_all, pltpu.roll(dt_all, LANE - SSD_HEADS, 1)), 0.0)
        ld_d = jnp.where(head, jnp.where(d == 0, ld_all, pltpu.roll(ld_all, LANE - SSD_HEADS, 1)), 0.0)
        cum = jnp.dot(tri, ld_d, precision=HIGHEST, preferred_element_type=F32)
        tot = jnp.sum(ld_d, axis=0, keepdims=True)
        cum_t = cum.T
        dt_t = dt_d.T
        w_t = (jnp.exp(tot - cum) * dt_d).T
        a_out = jnp.exp(cum)
        e_tot = jnp.exp(tot)

        for g in range(SSD_GROUPS):
            bg = xbc_ref[rows, SSD_D + g * SSD_STATE:SSD_D + (g + 1) * SSD_STATE].astype(F32)
            cg = xbc_ref[rows, SSD_D + SSD_BC + g * SSD_STATE:SSD_D + SSD_BC + (g + 1) * SSD_STATE].astype(F32)
            bg_t = bg.T
            cb = _dot(cg.astype(BF16), bg_t.astype(BF16))
            for e in range(SSD_HPG):
                hh = g * SSD_HPG + e
                diff = cum[:, hh:hh + 1] - cum_t[hh:hh + 1, :]
                dec = jnp.exp(jnp.where(mask, diff, -1e30))
                m = (cb * dec * dt_t[hh:hh + 1, :]).astype(BF16)
                xe = xbc_ref[rows, hh * hd:(hh + 1) * hd].astype(BF16)
                cs = (cg * a_out[:, hh:hh + 1]).astype(BF16)
                h_in = h_ref[hh]
                y_ref[rows, hh * hd:(hh + 1) * hd] = _dot(m, xe) + _dot(cs, h_in.astype(BF16))
                s_new = _dot((bg_t * w_t[hh:hh + 1, :]).astype(BF16), xe)
                h_ref[hh] = e_tot[:, hh:hh + 1] * h_in + s_new

    for k in range(cps):
        one_chunk(pl.multiple_of(jnp.where(d == 0, k, cps - 1 - k) * q, q))

    @pl.when(c == ns - 1)
    def _():
        fin_ref[...] = h_ref[...]


def _ssd_scan(xbc, p, dt_bias, a_log, init):
    n = xbc.shape[0]
    cps = SSD_CHUNKS_PER_STEP
    q = SSD_CHUNK * cps
    ns = n // q

    def chunk(d, c):
        return jnp.where(d == 0, c, ns - 1 - c)

    st = SSD_STATE_SHAPE
    vec = pl.BlockSpec((1, LANE), lambda d, c: (0, 0))
    return pl.pallas_call(
        functools.partial(_ssd_kernel, ns=ns, cps=cps),
        grid=(2, ns),
        in_specs=[pl.BlockSpec((q, SSD_XBC), lambda d, c: (chunk(d, c), 0)),
                  pl.BlockSpec((q, LANE), lambda d, c: (chunk(d, c), 0)),
                  vec, vec,
                  pl.BlockSpec((None,) + st, lambda d, c: (d, 0, 0, 0))],
        out_specs=[pl.BlockSpec((None, q, SSD_D), lambda d, c: (d, chunk(d, c), 0)),
                   pl.BlockSpec((None,) + st, lambda d, c: (d, 0, 0, 0))],
        out_shape=[jax.ShapeDtypeStruct((2, n, SSD_D), F32),
                   jax.ShapeDtypeStruct((2,) + st, F32)],
        scratch_shapes=[pltpu.VMEM(st, F32)],
        compiler_params=_params("arbitrary", "arbitrary"),
        name="ssd_scan",
    )(xbc, p, dt_bias, a_log, init)


def _filt_kernel(feat_ref, w1_ref, b1_ref, w2_ref, b2_ref, fr_ref, w3h_ref, w3l_ref, dl_ref, k_ref, nrm_ref, *,
                 t):
    i = pl.program_id(0)
    hf = t // 2
    feat = feat_ref[...]
    x = jnp.concatenate([feat[0:hf], feat[hf:t]], axis=1)
    hid = jnp.sin(fr_ref[0:1, :] * (jnp.dot(x, w1_ref[...], precision=HIGHEST,
                                            preferred_element_type=F32) + b1_ref[...]))
    hid = jnp.sin(fr_ref[1:2, :] * (jnp.dot(hid, w2_ref[...], precision=HIGHEST,
                                            preferred_element_type=F32) + b2_ref[...]))
    hi = hid.astype(BF16)
    lo = (hid - hi.astype(F32)).astype(BF16)

    @pl.when(i == 0)
    def _():
        nrm_ref[...] = jnp.zeros_like(nrm_ref)

    for dirn in range(2):
        for half in range(2):
            wh, wl = w3h_ref[dirn, half], w3l_ref[dirn, half]
            filt = _dot(hi, wh) + _dot(lo, wh) + _dot(hi, wl)
            filt = filt * jnp.exp(-feat[half * hf:(half + 1) * hf, 0:1] * dl_ref[dirn])
            if dirn == 1:
                row = i * t + half * hf + lax.broadcasted_iota(jnp.int32, filt.shape, 0)
                filt = jnp.where(row == 0, 0.0, filt)
            k_ref[dirn, half * hf:(half + 1) * hf, :] = filt
            nrm_ref[...] += jnp.sum(jnp.abs(filt), axis=0, keepdims=True)


def _hyena_filters(feat, fw, n):
    t = min(n, 512)
    oc = HY_ORDER * HY_D
    full = lambda shape: pl.BlockSpec(shape, lambda i: tuple(0 for _ in shape))
    return pl.pallas_call(
        functools.partial(_filt_kernel, t=t),
        grid=(n // t,),
        in_specs=[pl.BlockSpec((t, LANE), lambda i: (i, 0)),
                  full((2 * LANE, LANE)), full((1, LANE)), full((LANE, LANE)), full((1, LANE)),
                  full((2, LANE)), full((2, 2, LANE, oc)), full((2, 2, LANE, oc)), full((2, 1, oc))],
        out_specs=[pl.BlockSpec((2, t, oc), lambda i: (0, i, 0)),
                   pl.BlockSpec((1, oc), lambda i: (0, 0))],
        out_shape=[jax.ShapeDtypeStruct((2, n, oc), F32),
                   jax.ShapeDtypeStruct((1, oc), F32)],
        compiler_params=_params("arbitrary"),
        name="hyena_filters",
    )(feat, fw["w1"], fw["b1"], fw["w2"], fw["b2"], fw["freq"], fw["w3h"], fw["w3l"], fw["deltas"])


def _filter_weights(w1, b1, w2, b2, freq, w3, deltas_d):
    hh = HY_HID
    z = lambda r, c: jnp.zeros((r, c), F32)
    w1p = jnp.pad(w1, ((0, LANE - HY_EMB), (0, 0)))
    w1b = jnp.concatenate([jnp.concatenate([w1p, z(LANE, hh)], 1),
                           jnp.concatenate([z(LANE, hh), w1p], 1)], 0)
    w2b = jnp.concatenate([jnp.concatenate([w2, z(hh, hh)], 1),
                           jnp.concatenate([z(hh, hh), w2], 1)], 0)
    two = lambda v: jnp.concatenate([v, v], axis=-1)
    w3d = w3.reshape(hh, HY_ORDER, 2, HY_D).transpose(2, 0, 1, 3).reshape(2, hh, HY_ORDER * HY_D)
    zz = jnp.zeros_like(w3d)
    w3x = jnp.stack([jnp.concatenate([w3d, zz], 1), jnp.concatenate([zz, w3d], 1)], axis=1)
    w3h = w3x.astype(BF16)
    w3l = (w3x - w3h.astype(F32)).astype(BF16)
    return dict(w1=w1b, b1=two(b1[None, :]), w2=w2b, b2=two(b2[None, :]), freq=two(freq), w3h=w3h, w3l=w3l,
                deltas=deltas_d)


DFT_LANES = 8192


def _dft_rows_kernel(f_ref, x_ref, o_ref, *, nj):
    x = jnp.concatenate([x_ref[:, jj, :] for jj in range(nj)], axis=1)
    o_ref[...] = _dot(f_ref[...], x.astype(BF16)).astype(o_ref.dtype)


def _dft_rows(fmat, x3, col, width, lead=None):
    m, k = fmat.shape
    n2 = x3.shape[-2]
    nj = min(DFT_LANES // width, n2)
    if lead is None:
        xspec = pl.BlockSpec((k, nj, width), lambda j: (0, j, col))
    else:
        xspec = pl.BlockSpec((None, k, nj, width), lambda j: (lead, 0, j, col))
    return pl.pallas_call(
        functools.partial(_dft_rows_kernel, nj=nj),
        grid=(n2 // nj,),
        in_specs=[pl.BlockSpec((m, k), lambda j: (0, 0)), xspec],
        out_specs=pl.BlockSpec((m, nj * width), lambda j: (0, j)),
        out_shape=jax.ShapeDtypeStruct((m, n2 * width), BF16),
        compiler_params=_params("parallel"),
        name="dft_rows",
    )(fmat, x3)


def _spec_kernel(ar_ref, ai_ref, br_ref, bi_ref, gr_ref, gi_ref, kr_ref, ki_ref):
    f32 = lambda r: r[...].astype(F32)
    ar, ai, br, bi = f32(ar_ref), f32(ai_ref), f32(br_ref), f32(bi_ref)
    gr, gi = gr_ref[...], gi_ref[...]
    kr_ref[...] = (_dot(gr, (ar + br).astype(BF16)) - _dot(gi, (ai + bi).astype(BF16))).astype(BF16)
    ki_ref[...] = (_dot(gr, (ai - bi).astype(BF16)) + _dot(gi, (ar - br).astype(BF16))).astype(BF16)


def _filter_spectrum(a4, b4, gr, gi):
    _, _, n2, ch = a4.shape
    nh = gr.shape[0]
    ct = 512
    blk = lambda ri: pl.BlockSpec((None, None, n2, ct), lambda f, j: (ri, f, 0, j))
    gspec = pl.BlockSpec((None, n2, n2), lambda f, j: (f, 0, 0))
    ospec = pl.BlockSpec((None, n2, ct), lambda f, j: (f, 0, j))
    return pl.pallas_call(
        _spec_kernel,
        grid=(nh, ch // ct),
        in_specs=[blk(0), blk(1), blk(0), blk(1), gspec, gspec],
        out_specs=[ospec, ospec],
        out_shape=[jax.ShapeDtypeStruct((nh, n2, ch), BF16)] * 2,
        compiler_params=_params("parallel", "parallel"),
        name="filter_spectrum",
    )(a4, a4, b4, b4, gr, gi)


def _mid_kernel(ar_ref, ai_ref, gr_ref, gi_ref, grt_ref, git_ref, kr_ref, ki_ref, br_ref, bi_ref, *, nh):
    f = pl.program_id(0)

    @pl.when(f < nh)
    def _():
        ar, ai, gr, gi = ar_ref[...], ai_ref[...], gr_ref[...], gi_ref[...]
        xr = _dot(gr, ar) - _dot(gi, ai)
        xi = _dot(gr, ai) + _dot(gi, ar)
        kr, ki = kr_ref[...].astype(F32), ki_ref[...].astype(F32)
        yr = (xr * kr - xi * ki).astype(BF16)
        yi = (xr * ki + xi * kr).astype(BF16)
        grt, git = grt_ref[...], git_ref[...]
        br_ref[...] = (_dot(grt, yr) + _dot(git, yi)).astype(BF16)
        bi_ref[...] = (_dot(grt, yi) - _dot(git, yr)).astype(BF16)

    @pl.when(f >= nh)
    def _():
        br_ref[...] = jnp.zeros_like(br_ref)
        bi_ref[...] = jnp.zeros_like(bi_ref)


def _hyena_mid(a4, tabs, kf_r, kf_i, order):
    _, nf, n2, ch = a4.shape
    gr, gi, grt, git = tabs
    nh = gr.shape[0]
    fi = lambda f: jnp.minimum(f, nh - 1)
    blk = lambda ri: pl.BlockSpec((None, None, n2, ch), lambda f: (ri, fi(f), 0, 0))
    kspec = pl.BlockSpec((None, n2, ch), lambda f: (fi(f), 0, order))
    gspec = pl.BlockSpec((None, n2, n2), lambda f: (fi(f), 0, 0))
    ospec = pl.BlockSpec((None, n2, ch), lambda f: (f, 0, 0))
    return pl.pallas_call(
        functools.partial(_mid_kernel, nh=nh),
        grid=(nf,),
        in_specs=[blk(0), blk(1), gspec, gspec, gspec, gspec, kspec, kspec],
        out_specs=[ospec, ospec],
        out_shape=[jax.ShapeDtypeStruct((nf, n2, ch), BF16)] * 2,
        compiler_params=_params("parallel"),
        name="hyena_mid",
    )(a4, a4, gr, gi, grt, git, kf_r, kf_i)


def _inv_kernel(f_ref, br_ref, bi_ref, s_ref, bias_ref, z_ref, g_ref, o_ref, *, nf, nj, ch):
    acc = _dot(f_ref[:, 0:nf], br_ref[...]) + _dot(f_ref[:, nf:2 * nf], bi_ref[...])
    for jj in range(nj):
        y = acc[:, jj * ch:(jj + 1) * ch] * s_ref[...]
        o_ref[:, jj, :] = g_ref[:, jj, :] * (y + bias_ref[...] * z_ref[:, jj, :])


def _hyena_inverse(finv, b_r, b_i, scale, bias, z3, zcol, g3, gcol):
    t1, k2 = finv.shape
    nf = k2 // 2
    n2 = z3.shape[1]
    ch = HY_D
    nj = min(DFT_LANES // ch, n2)
    col = pl.BlockSpec((nf, nj * ch), lambda j: (0, j))
    row = pl.BlockSpec((1, ch), lambda j: (0, 0))
    return pl.pallas_call(
        functools.partial(_inv_kernel, nf=nf, nj=nj, ch=ch),
        grid=(n2 // nj,),
        in_specs=[pl.BlockSpec((t1, k2), lambda j: (0, 0)), col, col, row, row,
                  pl.BlockSpec((t1, nj, ch), lambda j: (0, j, zcol)),
                  pl.BlockSpec((t1, nj, ch), lambda j: (0, j, gcol))],
        out_specs=pl.BlockSpec((t1, nj, ch), lambda j: (0, j, 0)),
        out_shape=jax.ShapeDtypeStruct((t1, n2, ch), F32),
        compiler_params=_params("parallel"),
        name="hyena_inverse",
    )(finv, b_r, b_i, scale, bias, z3, g3)


def _hyena_nf(n):
    nh = (2 * n // FFT_N2) // 2 + 1
    return -(-nh // 16) * 16


def _dft_tables(n):
    n2 = FFT_N2
    n1 = 2 * n // n2
    tot = 2 * n
    two_pi = 2.0 * math.pi

    def cs(num, den):
        ang = (two_pi / den) * (num % den).astype(F32)
        return jnp.cos(ang), jnp.sin(ang)

    nh = n1 // 2 + 1
    nf = _hyena_nf(n)
    f1 = jnp.arange(nh, dtype=jnp.int32)
    t1 = jnp.arange(n1, dtype=jnp.int32)
    c1, s1 = cs(f1[:, None] * t1[None, :], n1)
    zrow = jnp.zeros((nf - nh, n1), F32)
    fwd_full = jnp.concatenate([c1, zrow, -s1, zrow], axis=0).astype(BF16)
    fwd_half = fwd_full[:, :n1 // 2]
    wgt = jnp.where((f1 == 0) | (f1 == n1 // 2), 1.0, 2.0)[:, None]
    zcol = jnp.zeros((n1 // 2, nf - nh), F32)
    inv = jnp.concatenate([(wgt * c1[:, :n1 // 2]).T, zcol, -(wgt * s1[:, :n1 // 2]).T, zcol],
                          axis=1).astype(BF16)
    t2 = jnp.arange(n2, dtype=jnp.int32)
    twr, twi = cs(f1[:, None] * t2[None, :], tot)
    fr, fi = cs(t2[:, None] * t2[None, :], n2)
    twi, fi = -twi, -fi
    gr = twr[:, None, :] * fr[None] - twi[:, None, :] * fi[None]
    gi = twr[:, None, :] * fi[None] + twi[:, None, :] * fr[None]
    tabs = (gr.astype(BF16), gi.astype(BF16),
            gr.transpose(0, 2, 1).astype(BF16), gi.transpose(0, 2, 1).astype(BF16))
    return fwd_full, fwd_half, inv, tabs


def _hyena_long(q, kab, nrm, hy_bias, tables):
    n = q.shape[0]
    n2 = FFT_N2
    n1 = 2 * n // n2
    _, fwd_half, inv, tabs = tables
    nf = _hyena_nf(n)
    oc = HY_ORDER * HY_D
    kab4 = kab.reshape(2, n1 // 2, n2, oc)
    ak, bk = (_dft_rows(fwd_half, kab4, 0, oc, lead=d).reshape(2, nf, n2, oc) for d in range(2))
    kf_r, kf_i = _filter_spectrum(ak, bk, tabs[0], tabs[1])
    q3 = q.reshape(n1 // 2, n2, 3 * HY_D)
    z3, zcol = q3, 0
    for o in range(HY_ORDER):
        a4 = _dft_rows(fwd_half, z3, zcol, HY_D).reshape(2, nf, n2, HY_D)
        b_r, b_i = _hyena_mid(a4, tabs, kf_r, kf_i, o)
        scale = 1.0 / (2.0 * n * nrm[:, o * HY_D:(o + 1) * HY_D])
        z3 = _hyena_inverse(inv, b_r.reshape(nf, n2 * HY_D), b_i.reshape(nf, n2 * HY_D), scale,
                            hy_bias[o][None, :], z3, zcol, q3, o + 1)
        zcol = 0
    return z3.reshape(n, HY_D)


def _hy_ctx_kernel(v_ref, x1_ref, x2_ref, a0_ref, b0_ref, a1_ref, b1_ref, n0_ref, n1_ref, bias_ref, o_ref, zp, *, n):
    zp[...] = jnp.zeros_like(zp)
    zp[n:2 * n, :] = v_ref[...]
    orders = ((a0_ref, b0_ref, n0_ref, x1_ref), (a1_ref, b1_ref, n1_ref, x2_ref))
    for o, (a_ref, b_ref, nr_ref, x_ref) in enumerate(orders):
        def body(lag, acc):
            return (acc + a_ref[pl.ds(lag, 1), :] * zp[pl.ds(n - lag, n), :]
                    + b_ref[pl.ds(lag, 1), :] * zp[pl.ds(n + lag, n), :])

        acc = lax.fori_loop(0, n, body, jnp.zeros((n, LANE), F32))
        z = zp[n:2 * n, :]
        zp[n:2 * n, :] = x_ref[...] * (acc / nr_ref[...] + bias_ref[o:o + 1, :] * z)
    o_ref[...] = zp[n:2 * n, :]


def _hyena_ctx(q, kab, nrm, hy_bias):
    n = q.shape[0]
    nb = HY_D // LANE
    col = lambda c0: pl.BlockSpec((n, LANE), lambda j: (0, c0 + j))
    kcol = lambda d, c0: pl.BlockSpec((None, n, LANE), lambda j: (d, 0, c0 + j))
    ncol = lambda c0: pl.BlockSpec((1, LANE), lambda j: (0, c0 + j))
    return pl.pallas_call(
        functools.partial(_hy_ctx_kernel, n=n),
        grid=(nb,),
        in_specs=[col(0), col(nb), col(2 * nb), kcol(0, 0), kcol(1, 0), kcol(0, nb), kcol(1, nb),
                  ncol(0), ncol(nb), pl.BlockSpec((HY_ORDER, LANE), lambda j: (0, j))],
        out_specs=pl.BlockSpec((n, LANE), lambda j: (0, j)),
        out_shape=jax.ShapeDtypeStruct((n, HY_D), F32),
        scratch_shapes=[pltpu.VMEM((3 * n, LANE), F32)],
        compiler_params=_params("parallel"),
        name="hyena_ctx",
    )(q, q, q, kab, kab, kab, kab, nrm, nrm, hy_bias)


def _merge_kernel(ya_ref, xs_ref, yf_ref, yb_ref, z_ref, yc_ref, yd_ref, g_ref, h_ref,
                  dv_ref, ng_ref, g1_ref, lg_ref, lb_ref,
                  wa_ref, wb_ref, wc_ref, wd_ref, wo_ref, o_ref):
    y = xs_ref[...].astype(F32) * dv_ref[...] + yf_ref[...] + yb_ref[...]
    gz = y * _silu(z_ref[...].astype(F32))
    ssd = gz * lax.rsqrt(jnp.mean(gz * gz, -1, keepdims=True) + LN_EPS) * ng_ref[...]
    d = D_MODEL
    gate = lambda k: _sigmoid(g_ref[:, k * d:(k + 1) * d].astype(F32))
    m = gate(0) * _dot(ya_ref[...].astype(BF16), wa_ref[...])
    m = m + gate(1) * _dot(ssd.astype(BF16), wb_ref[...])
    m = m + gate(2) * _dot(yc_ref[...].astype(BF16), wc_ref[...])
    m = m + gate(3) * _dot(yd_ref[...].astype(BF16), wd_ref[...])
    mix = _dot(m.astype(BF16), wo_ref[...])
    o_ref[...] = _layer_norm(DN_ALPHA * h_ref[...] + g1_ref[...] * mix, lg_ref[...], lb_ref[...])


def _merge(ya, xbc, ydir, p, yc, yd, h, dvec, ng, gate1, lg, lb, wa, wb, wc, wd, wo):
    n = h.shape[0]
    t = 256
    tok = lambda w, col=0: pl.BlockSpec((t, w), lambda i: (i, col))
    vec = lambda w: pl.BlockSpec((1, w), lambda i: (0, 0))
    mat = lambda r: pl.BlockSpec((r, D_MODEL), lambda i: (0, 0))
    return pl.pallas_call(
        _merge_kernel,
        grid=(n // t,),
        in_specs=[tok(CONF_D), tok(SSD_D),
                  pl.BlockSpec((None, t, SSD_D), lambda i: (0, i, 0)),
                  pl.BlockSpec((None, t, SSD_D), lambda i: (1, i, 0)),
                  tok(SSD_D, PZ // SSD_D), tok(HY_D), tok(SC_D), tok(N_BRANCH * D_MODEL, 0), tok(D_MODEL),
                  vec(SSD_D), vec(SSD_D), vec(D_MODEL), vec(D_MODEL), vec(D_MODEL),
                  mat(CONF_D), mat(SSD_D), mat(HY_D), mat(SC_D), mat(D_MODEL)],
        out_specs=tok(D_MODEL),
        out_shape=jax.ShapeDtypeStruct((n, D_MODEL), F32),
        compiler_params=_params("parallel"),
        name="merge",
    )(ya, xbc, ydir, ydir, p, yc, yd, p, h, dvec, ng, gate1, lg, lb, wa, wb, wc, wd, wo)


def _router_kernel(h_ref, sh_ref, sc_ref, w_ref, b_ref, sel_ref, cnt_ref, selt_ref):
    @pl.when(pl.program_id(0) == 0)
    def _():
        cnt_ref[...] = jnp.zeros_like(cnt_ref)

    u = h_ref[...] * (1.0 + sc_ref[...]) + sh_ref[...]
    lg = jnp.dot(u, w_ref[...], precision=HIGHEST, preferred_element_type=F32) + b_ref[...]
    lane = lax.broadcasted_iota(jnp.int32, lg.shape, 1).astype(F32)
    neg = -1e30
    big = 1e9
    gl = jnp.where(lane < MOE_GROUPS, lg, neg)
    gmax = jnp.max(gl, -1, keepdims=True)
    gsel = jnp.min(jnp.where(gl == gmax, lane, big), -1, keepdims=True)
    gprob = 1.0 / jnp.sum(jnp.where(lane < MOE_GROUPS, jnp.exp(lg - gmax), 0.0), -1, keepdims=True)
    lo = MOE_GROUPS + gsel * MOE_EPG
    el = jnp.where(jnp.abs(lane - lo - (MOE_EPG - 1) / 2.0) < MOE_EPG / 2.0, lg, neg)
    m1 = jnp.max(el, -1, keepdims=True)
    i1 = jnp.min(jnp.where(el == m1, lane, big), -1, keepdims=True)
    el2 = jnp.where(lane == i1, neg, el)
    m2 = jnp.max(el2, -1, keepdims=True)
    i2 = jnp.min(jnp.where(el2 == m2, lane, big), -1, keepdims=True)
    t = jnp.exp(m2 - m1)
    w1 = gprob / (1.0 + t)
    w2 = gprob * t / (1.0 + t)
    oh1 = jnp.where(lane == i1, 1.0, 0.0)
    oh2 = jnp.where(lane == i2, 1.0, 0.0)
    oh = oh1 + oh2
    tt = lg.shape[0]
    li = lax.broadcasted_iota(jnp.int32, (tt, tt), 0)
    si = lax.broadcasted_iota(jnp.int32, (tt, tt), 1)
    before = _dot(jnp.where(li > si, 1.0, 0.0).astype(BF16), oh.astype(BF16)) + cnt_ref[...]
    r1 = jnp.sum(oh1 * before, -1, keepdims=True)
    r2 = jnp.sum(oh2 * before, -1, keepdims=True)
    cnt_ref[...] += jnp.sum(oh, axis=0, keepdims=True)
    cols = (i1 - MOE_GROUPS, i2 - MOE_GROUPS, w1, w2, r1, r2)
    sel = jnp.zeros_like(lg)
    for k, v in enumerate(cols):
        sel = jnp.where(lane == k, v, sel)
    sel_ref[...] = sel
    selt_ref[...] = sel.T[0:SUBLANE, :]


def _router(h, shift, scale, wr, br):
    n = h.shape[0]
    t = 256
    tok = lambda w: pl.BlockSpec((t, w), lambda i: (i, 0))
    vec = lambda w: pl.BlockSpec((1, w), lambda i: (0, 0))
    return pl.pallas_call(
        _router_kernel,
        grid=(n // t,),
        in_specs=[tok(D_MODEL), vec(D_MODEL), vec(D_MODEL),
                  pl.BlockSpec((D_MODEL, LANE), lambda i: (0, 0)), vec(LANE)],
        out_specs=[tok(LANE), vec(LANE), pl.BlockSpec((SUBLANE, t), lambda i: (0, i))],
        out_shape=[jax.ShapeDtypeStruct((n, LANE), F32), jax.ShapeDtypeStruct((1, LANE), F32),
                   jax.ShapeDtypeStruct((SUBLANE, n), F32)],
        compiler_params=_params("arbitrary"),
        name="router",
    )(h, shift, scale, wr, br)


MOE_T = 256
ROW_WORDS = D_MODEL // 2


def _pack_rows(x):
    c = x.shape[1] // 2
    bits = lambda v: lax.bitcast_convert_type(v.astype(BF16).astype(F32), jnp.uint32)
    return bits(x[:, :c]) | (bits(x[:, c:]) >> 16)


def _unpack_rows(w):
    hi = lax.bitcast_convert_type(w & jnp.uint32(0xFFFF0000), F32)
    lo = lax.bitcast_convert_type(w << 16, F32)
    return jnp.concatenate([hi, lo], axis=1)


def _dispatch_kernel(dst_ref, h_ref, sh_ref, sc_ref, zero_hbm, xin_hbm, ubuf, sem, *, nt):
    del zero_hbm
    t = MOE_T
    i = pl.program_id(0)
    slot = i % 2

    def wait_slot(s):
        for _ in range(MOE_TOP_K):
            pltpu.make_async_copy(ubuf.at[s], xin_hbm.at[pl.ds(0, t), :], sem.at[s]).wait()

    @pl.when(i >= 2)
    def _():
        wait_slot(slot)

    ubuf[slot] = _pack_rows(h_ref[...] * (1.0 + sc_ref[...]) + sh_ref[...])

    def issue(r, carry):
        for k in range(MOE_TOP_K):
            pltpu.make_async_copy(ubuf.at[slot, pl.ds(r, 1), :],
                                  xin_hbm.at[pl.ds(dst_ref[0, 0, k * t + r], 1), :], sem.at[slot]).start()
        return carry

    lax.fori_loop(0, t, issue, 0, unroll=8)

    @pl.when(i == nt - 1)
    def _():
        wait_slot(slot)
        if nt > 1:
            wait_slot(1 - slot)


def _dispatch(pos_t, h, shift, scale, n_rows):
    n = h.shape[0]
    t = MOE_T
    nt = n // t
    vec = pl.BlockSpec((1, D_MODEL), lambda i: (0, 0))
    return pl.pallas_call(
        functools.partial(_dispatch_kernel, nt=nt),
        grid=(nt,),
        in_specs=[pl.BlockSpec((1, 1, MOE_TOP_K * t), lambda i: (i, 0, 0), memory_space=pltpu.SMEM),
                  pl.BlockSpec((t, D_MODEL), lambda i: (i, 0)), vec, vec,
                  pl.BlockSpec(memory_space=pl.ANY)],
        out_specs=pl.BlockSpec(memory_space=pl.ANY),
        out_shape=jax.ShapeDtypeStruct((n_rows, ROW_WORDS), jnp.uint32),
        scratch_shapes=[pltpu.VMEM((2, t, ROW_WORDS), jnp.uint32), pltpu.SemaphoreType.DMA((2,))],
        input_output_aliases={4: 0},
        compiler_params=_params("arbitrary"),
        name="dispatch",
    )(pos_t, h, shift, scale, jnp.zeros((n_rows, ROW_WORDS), jnp.uint32))


def _expert_kernel(be_ref, nu_ref, x_ref, wg_ref, wu_ref, wd_ref, o_ref, wgb, wub, wdb):
    b = pl.program_id(0)

    @pl.when((b == 0) | (be_ref[b] != be_ref[jnp.maximum(b - 1, 0)]))
    def _():
        wgb[...] = wg_ref[...].astype(BF16)
        wub[...] = wu_ref[...].astype(BF16)
        wdb[...] = wd_ref[...].astype(BF16)

    @pl.when(b < nu_ref[0])
    def _():
        x = _unpack_rows(x_ref[...]).astype(BF16)
        hid = _silu(_dot(x, wgb[...])) * _dot(x, wub[...])
        o_ref[...] = _pack_rows(_dot(hid.astype(BF16), wdb[...]))

    @pl.when(b >= nu_ref[0])
    def _():
        o_ref[...] = jnp.zeros_like(o_ref)


def _experts(xin, block_e, n_used, wg, wu, wd, layer):
    n_blocks = block_e.shape[0]
    gs = pltpu.PrefetchScalarGridSpec(
        num_scalar_prefetch=2,
        grid=(n_blocks,),
        in_specs=[pl.BlockSpec((MOE_ROWS, ROW_WORDS), lambda b, be, nu: (b, 0)),
                  pl.BlockSpec((None, None, D_MODEL, MOE_FF), lambda b, be, nu: (layer, be[b], 0, 0)),
                  pl.BlockSpec((None, None, D_MODEL, MOE_FF), lambda b, be, nu: (layer, be[b], 0, 0)),
                  pl.BlockSpec((None, None, MOE_FF, D_MODEL), lambda b, be, nu: (layer, be[b], 0, 0))],
        out_specs=pl.BlockSpec((MOE_ROWS, ROW_WORDS), lambda b, be, nu: (b, 0)),
        scratch_shapes=[pltpu.VMEM((D_MODEL, MOE_FF), BF16), pltpu.VMEM((D_MODEL, MOE_FF), BF16),
                        pltpu.VMEM((MOE_FF, D_MODEL), BF16)],
    )
    return pl.pallas_call(
        _expert_kernel,
        grid_spec=gs,
        out_shape=jax.ShapeDtypeStruct((n_blocks * MOE_ROWS, ROW_WORDS), jnp.uint32),
        compiler_params=_params("arbitrary"),
        name="experts",
    )(block_e, n_used, xin, wg, wu, wd)


def _combine_kernel(pos_ref, posn_ref, y_hbm, h_ref, sel_ref, g2_ref, lg_ref, lb_ref, o_ref, ybuf, sem, *, nt):
    t = MOE_T
    i = pl.program_id(0)
    slot = i % 2

    def gather(p_ref, s):
        def issue(r, carry):
            pltpu.make_async_copy(y_hbm.at[pl.ds(p_ref[0, 0, r], 1), :], ybuf.at[s, pl.ds(r, 1), :],
                                  sem.at[s]).start()
            return carry
        lax.fori_loop(0, MOE_TOP_K * t, issue, 0, unroll=8)

    @pl.when(i == 0)
    def _():
        gather(pos_ref, 0)

    @pl.when(i + 1 < nt)
    def _():
        gather(posn_ref, 1 - slot)

    pltpu.make_async_copy(y_hbm.at[pl.ds(0, MOE_TOP_K * t), :], ybuf.at[slot], sem.at[slot]).wait()
    ffn = (sel_ref[:, 2:3] * _unpack_rows(ybuf[slot, 0:t, :])
           + sel_ref[:, 3:4] * _unpack_rows(ybuf[slot, t:2 * t, :]))
    o_ref[...] = _layer_norm(DN_ALPHA * h_ref[...] + g2_ref[...] * ffn, lg_ref[...], lb_ref[...])


def _combine(y, pos_t, h, sel, gate2, lg, lb):
    n = h.shape[0]
    t = MOE_T
    nt = n // t
    tok = lambda w: pl.BlockSpec((t, w), lambda i: (i, 0))
    vec = pl.BlockSpec((1, D_MODEL), lambda i: (0, 0))
    return pl.pallas_call(
        functools.partial(_combine_kernel, nt=nt),
        grid=(nt,),
        in_specs=[pl.BlockSpec((1, 1, MOE_TOP_K * t), lambda i: (i, 0, 0), memory_space=pltpu.SMEM),
                  pl.BlockSpec((1, 1, MOE_TOP_K * t), lambda i: (jnp.minimum(i + 1, nt - 1), 0, 0),
                               memory_space=pltpu.SMEM),
                  pl.BlockSpec(memory_space=pl.ANY),
                  tok(D_MODEL), tok(LANE), vec, vec, vec],
        out_specs=tok(D_MODEL),
        out_shape=jax.ShapeDtypeStruct((n, D_MODEL), F32),
        scratch_shapes=[pltpu.VMEM((2, MOE_TOP_K * t, ROW_WORDS), jnp.uint32), pltpu.SemaphoreType.DMA((2,))],
        compiler_params=_params("arbitrary"),
        name="combine",
    )(pos_t, pos_t, y, h, sel, gate2, lg, lb)


def _moe(h, shift, scale, gate2, lg, lb, wr, br, wg, wu, wd, layer):
    n = h.shape[0]
    t = MOE_T
    nt = n // t
    sel, cnt, selt = _router(h, shift, scale, wr, br)
    counts = cnt[0, MOE_GROUPS:MOE_GROUPS + MOE_EXPERTS].astype(jnp.int32)
    padded = (counts + MOE_ROWS - 1) // MOE_ROWS * MOE_ROWS
    pad_end = jnp.cumsum(padded)
    pad_start = pad_end - padded
    n_blocks = (n * MOE_TOP_K + MOE_EXPERTS * (MOE_ROWS - 1) + MOE_ROWS - 1) // MOE_ROWS
    blk_row = jnp.arange(n_blocks, dtype=jnp.int32) * MOE_ROWS
    block_e = jnp.minimum(jnp.sum((blk_row[:, None] >= pad_end[None, :]).astype(jnp.int32), axis=1),
                          MOE_EXPERTS - 1)
    n_used = (pad_end[-1:] // MOE_ROWS).astype(jnp.int32)
    e_kt = selt[0:MOE_TOP_K].astype(jnp.int32)
    ids = jnp.arange(MOE_EXPERTS, dtype=jnp.int32)[None, :, None]
    start_kt = jnp.sum(jnp.where(e_kt[:, None, :] == ids, pad_start[None, :, None], 0), axis=1)
    pos_kt = start_kt + selt[4:4 + MOE_TOP_K].astype(jnp.int32)
    pos_t = pos_kt.reshape(MOE_TOP_K, nt, t).transpose(1, 0, 2).reshape(nt, 1, MOE_TOP_K * t)
    xin = _dispatch(pos_t, h, shift, scale, n_blocks * MOE_ROWS)
    y = _experts(xin, block_e, n_used, wg, wu, wd, layer)
    return _combine(y, pos_t, h, sel, gate2, lg, lb)


def _mixer(h, mod, lw, ssd_init, tables, *, latent, need_mix):
    n = h.shape[0]
    p, dt_raw = _inproj(h, mod[0], mod[1], lw["w_in"], lw["layer"])
    xbc = _conv3(p, PX, SSD_XBC, lw["ssd_conv_w"], lw["ssd_conv_b"], silu=True, out_dtype=BF16)
    ydir, finals = _ssd_scan(xbc, dt_raw, lw["ssd_dt_bias"], lw["ssd_a_log"], ssd_init)
    if not need_mix:
        return None, finals
    conf_w = (lw["conf_dw_w"], lw["conf_dw_b"], lw["conf_ln_g"], lw["conf_ln_b"])
    ya = _conformer_grid(p, *conf_w) if latent else _conformer(p, *conf_w, dil=1)
    q = _conv3(p, PC, 3 * HY_D, lw["hy_short_w"], lw["hy_short_b"], silu=False, out_dtype=F32)
    k2, nrm = _hyena_filters(lw["feat_lat" if latent else "feat_ctx"], lw["hy_filter"], n)
    if latent:
        yc = _hyena_long(q, k2, nrm, lw["hy_bias"], tables)
    else:
        yc = _hyena_ctx(q, k2, nrm, lw["hy_bias"])
    yd = _gated_conv(p, lw["sc_conv_w"])
    h = _merge(ya, xbc, ydir, p, yc, yd, h, lw["ssd_dvec"], lw["ssd_norm_g"], mod[2], lw["ln_g0"], lw["ln_b0"],
               lw["w_branch_a"], lw["w_branch_b"], lw["w_branch_c"], lw["w_branch_d"], lw["w_out"])
    return h, finals


def _positional_features(n):
    t01 = jnp.linspace(0.0, 1.0, n, dtype=F32)[:, None]
    omega = (2.0 * math.pi / n) * jnp.arange(n, dtype=F32)[:, None]
    bands = jnp.linspace(1e-4, HY_BANDS - 1, HY_BANDS, dtype=F32)
    feat = jnp.concatenate([t01, jnp.cos(bands * omega), -jnp.sin(bands * omega)], axis=-1)
    return _pad_lanes(feat)


def _relayout_w_in(w_in):
    seg = lambda a, b: w_in[:, :, a:b]
    ob = OFF_B
    parts = [seg(OFF_G, OFF_G + N_BRANCH * D_MODEL),
             seg(OFF_A, OFF_B),
             seg(OFF_C, OFF_D),
             seg(OFF_D, OFF_G),
             seg(ob + SSD_D, ob + SSD_D + SSD_XBC),
             seg(ob, ob + SSD_D),
             seg(ob + SSD_D + SSD_XBC, OFF_C),
             jnp.zeros(w_in.shape[:2] + (NP - PDT - 2 * SSD_HEADS,), w_in.dtype)]
    return jnp.concatenate(parts, axis=-1).astype(BF16)


def _pad_lanes(v):
    return jnp.pad(v, ((0, 0), (0, LANE - v.shape[-1])))


def kernel(x, c, ctx, c_ctx, w_mod, b_mod, ln_g, ln_b, w_in, conf_dw_w, conf_dw_b, conf_ln_g, conf_ln_b,
           ssd_conv_w, ssd_conv_b, ssd_a_log, ssd_dt_bias, ssd_d, ssd_norm_g, hy_short_w, hy_short_b,
           hy_w1, hy_b1, hy_w2, hy_b2, hy_freq, hy_w3, hy_bias, sc_conv_w, w_branch_a, w_branch_b,
           w_branch_c, w_branch_d, w_out, rt_group_w, rt_group_b, rt_expert_w, rt_expert_b,
           ex_w_gate, ex_w_up, ex_w_down):
    assert x.shape[0] == 1 and ctx.shape[0] == 1
    n_lat, n_ctx = x.shape[1], ctx.shape[1]
    depth = w_in.shape[0]

    cv = jnp.concatenate([c, c_ctx[None, :], jnp.zeros((SUBLANE - 2, D_MODEL), F32)], axis=0)
    mods = _mod_vectors(cv, w_mod, b_mod)
    w_in_p = _relayout_w_in(w_in)
    tables = _dft_tables(n_lat)
    feat_lat = _positional_features(n_lat)
    feat_ctx = _positional_features(n_ctx)
    deltas = jnp.abs(jnp.linspace(HY_MIN_DECAY, HY_MAX_DECAY, HY_N_FILT, dtype=F32))
    deltas_d = deltas.reshape(HY_ORDER, 2, HY_D).transpose(1, 0, 2).reshape(2, 1, HY_ORDER * HY_D)
    router_w = jnp.concatenate([rt_group_w, rt_expert_w,
                                jnp.zeros((depth, D_MODEL, LANE - MOE_GROUPS - MOE_EXPERTS), F32)], axis=-1)
    router_b = jnp.concatenate([rt_group_b, rt_expert_b,
                                jnp.zeros((depth, LANE - MOE_GROUPS - MOE_EXPERTS), F32)], axis=-1)
    ssd_zero = jnp.zeros((2,) + SSD_STATE_SHAPE, F32)

    h_lat, h_ctx = x[0], ctx[0]
    for l in range(depth):
        row = lambda v: v[None, :]
        lw = dict(
            w_in=w_in_p, layer=l, conf_dw_w=conf_dw_w[l], conf_dw_b=row(conf_dw_b[l]), conf_ln_g=row(conf_ln_g[l]),
            conf_ln_b=row(conf_ln_b[l]), ssd_conv_w=ssd_conv_w[l], ssd_conv_b=row(ssd_conv_b[l]),
            ssd_a_log=_pad_lanes(ssd_a_log[l].reshape(1, -1)), ssd_dt_bias=_pad_lanes(ssd_dt_bias[l].reshape(1, -1)),
            ssd_dvec=row(jnp.repeat(ssd_d[l], SSD_HEAD_DIM)), ssd_norm_g=row(ssd_norm_g[l]),
            hy_short_w=hy_short_w[l], hy_short_b=row(hy_short_b[l]),
            hy_filter=_filter_weights(hy_w1[l], hy_b1[l], hy_w2[l], hy_b2[l], hy_freq[l], hy_w3[l], deltas_d),
            hy_bias=hy_bias[l], feat_lat=feat_lat, feat_ctx=feat_ctx,
            sc_conv_w=sc_conv_w[l], ln_g0=row(ln_g[l, 0]), ln_b0=row(ln_b[l, 0]),
            w_branch_a=w_branch_a[l].astype(BF16), w_branch_b=w_branch_b[l].astype(BF16),
            w_branch_c=w_branch_c[l].astype(BF16), w_branch_d=w_branch_d[l].astype(BF16),
            w_out=w_out[l].astype(BF16))
        moe_w = (router_w[l], row(router_b[l]), ex_w_gate, ex_w_up, ex_w_down, l)
        last = l == depth - 1
        d = D_MODEL
        mod_lat = [mods[l, 0:1, k * d:(k + 1) * d] for k in range(6)]
        mod_ctx = [mods[l, 1:2, k * d:(k + 1) * d] for k in range(6)]

        mix_ctx, ctx_states = _mixer(h_ctx, mod_ctx, lw, ssd_zero, None, latent=False, need_mix=not last)
        h_lat, _ = _mixer(h_lat, mod_lat, lw, ctx_states, tables, latent=True, need_mix=True)
        h_lat = _moe(h_lat, mod_lat[3], mod_lat[4], mod_lat[5], row(ln_g[l, 1]), row(ln_b[l, 1]), *moe_w)
        if not last:
            h_ctx = _moe(mix_ctx, mod_ctx[3], mod_ctx[4], mod_ctx[5], row(ln_g[l, 1]), row(ln_b[l, 1]), *moe_w)
    return h_lat[None]
```

```python
import functools
import math

import jax
import jax.numpy as jnp
from jax import lax
from jax.experimental import pallas as pl
from jax.experimental.pallas import tpu as pltpu

F32 = jnp.float32
BF16 = jnp.bfloat16
HIGHEST = lax.Precision.HIGHEST

D_MODEL = 1024
DEPTH = 4
GRID_W = 64
CONF_D = 512
CONF_K = 31
SSD_D = 768
SSD_HEADS = 12
SSD_HEAD_DIM = 64
SSD_GROUPS = 4
SSD_HPG = SSD_HEADS // SSD_GROUPS
SSD_STATE = 128
SSD_CHUNK = 128
SSD_BC = SSD_GROUPS * SSD_STATE
SSD_XBC = SSD_D + 2 * SSD_BC
SSD_PROJ = SSD_D + SSD_XBC + 2 * SSD_HEADS
HY_D = 512
HY_ORDER = 2
HY_EMB = 33
HY_BANDS = (HY_EMB - 1) // 2
HY_HID = 64
HY_N_FILT = HY_ORDER * 2 * HY_D
HY_MIN_DECAY = math.log(1e-2) / 1.5
HY_MAX_DECAY = math.log(1e-2) / 0.3
SC_D = 512
N_BRANCH = 4
OFF_A = 0
OFF_B = OFF_A + 2 * CONF_D
OFF_C = OFF_B + SSD_PROJ
OFF_D = OFF_C + 3 * HY_D
OFF_G = OFF_D + 3 * SC_D
MOE_GROUPS = 4
MOE_EPG = 8
MOE_EXPERTS = MOE_GROUPS * MOE_EPG
MOE_TOP_K = 2
MOE_FF = 512
DN_ALPHA = (2 * DEPTH) ** 0.25
LN_EPS = 1e-5

PG = 0
PA = PG + N_BRANCH * D_MODEL
PC = PA + 2 * CONF_D
PD = PC + 3 * HY_D
PX = PD + 3 * SC_D
PZ = PX + SSD_XBC
PDT = PZ + SSD_D
INPROJ_TN = 1024
NP = -(-(PDT + 128) // INPROJ_TN) * INPROJ_TN

LANE = 128
SUBLANE = 8
FFT_N2 = 256
MOE_ROWS = 256
VMEM_LIMIT = 48 * 1024 * 1024


def _params(*sem):
    return pltpu.CompilerParams(dimension_semantics=sem, vmem_limit_bytes=VMEM_LIMIT)


def _sigmoid(x):
    return 0.5 * jnp.tanh(0.5 * x) + 0.5


def _silu(x):
    return x * _sigmoid(x)


def _layer_norm(x, g, b):
    mu = jnp.mean(x, -1, keepdims=True)
    xc = x - mu
    var = jnp.mean(xc * xc, -1, keepdims=True)
    return xc * lax.rsqrt(var + LN_EPS) * g + b


def _dot(a, b):
    return jnp.dot(a, b, preferred_element_type=F32)


def _mod_kernel(cv_ref, w_ref, b_ref, o_ref):
    o_ref[...] = jnp.dot(_silu(cv_ref[...]), w_ref[...], precision=HIGHEST,
                         preferred_element_type=F32) + b_ref[...]


def _mod_vectors(cv, w_mod, b_mod):
    tn = 1536
    return pl.pallas_call(
        _mod_kernel,
        grid=(DEPTH, 6 * D_MODEL // tn),
        in_specs=[pl.BlockSpec((SUBLANE, D_MODEL), lambda l, j: (0, 0)),
                  pl.BlockSpec((None, D_MODEL, tn), lambda l, j: (l, 0, j)),
                  pl.BlockSpec((None, 1, tn), lambda l, j: (l, 0, j))],
        out_specs=pl.BlockSpec((None, SUBLANE, tn), lambda l, j: (l, 0, j)),
        out_shape=jax.ShapeDtypeStruct((DEPTH, SUBLANE, 6 * D_MODEL), F32),
        compiler_params=_params("parallel", "parallel"),
        name="mod_vectors",
    )(cv, w_mod, b_mod.reshape(DEPTH, 1, 6 * D_MODEL))


def _inproj_kernel(x_ref, sh_ref, sc_ref, w_ref, o_ref, dt_ref, xb_ref, *, nj):
    j = pl.program_id(1)

    @pl.when(j == 0)
    def _():
        xb_ref[...] = (x_ref[...] * (1.0 + sc_ref[...]) + sh_ref[...]).astype(BF16)

    res = _dot(xb_ref[...], w_ref[...])
    o_ref[...] = res.astype(BF16)

    @pl.when(j == nj - 1)
    def _():
        off = PDT - (nj - 1) * INPROJ_TN
        dt_ref[...] = res[:, off:off + LANE]


def _inproj(h, shift, scale, w, layer):
    n = h.shape[0]
    tm = min(n, 2048)
    tn = INPROJ_TN
    nj = NP // tn
    assert PDT >= (nj - 1) * tn
    return pl.pallas_call(
        functools.partial(_inproj_kernel, nj=nj),
        grid=(n // tm, nj),
        in_specs=[pl.BlockSpec((tm, D_MODEL), lambda i, j: (i, 0)),
                  pl.BlockSpec((1, D_MODEL), lambda i, j: (0, 0)),
                  pl.BlockSpec((1, D_MODEL), lambda i, j: (0, 0)),
                  pl.BlockSpec((None, D_MODEL, tn), lambda i, j: (layer, 0, j))],
        out_specs=[pl.BlockSpec((tm, tn), lambda i, j: (i, j)),
                   pl.BlockSpec((tm, LANE), lambda i, j: (i, 0))],
        out_shape=[jax.ShapeDtypeStruct((n, NP), BF16), jax.ShapeDtypeStruct((n, LANE), F32)],
        scratch_shapes=[pltpu.VMEM((tm, D_MODEL), BF16)],
        compiler_params=_params("parallel", "arbitrary"),
        name="inproj",
    )(h, shift, scale, w)


def _shifted(x, prev_row, next_row):
    t = x.shape[0]
    row = lax.broadcasted_iota(jnp.int32, x.shape, 0)
    xm = jnp.where(row == 0, prev_row, pltpu.roll(x, 1, 0))
    xp = jnp.where(row == t - 1, next_row, pltpu.roll(x, t - 1, 0))
    return xm, xp


HALO_ROWS = 16


def _conv3_kernel(cur_ref, prev_ref, next_ref, w_ref, b_ref, o_ref, *, silu, nt):
    i = pl.program_id(0)
    x = cur_ref[...].astype(F32)
    pv = jnp.where(i > 0, prev_ref[HALO_ROWS - 1:HALO_ROWS, :].astype(F32), 0.0)
    nx = jnp.where(i < nt - 1, next_ref[0:1, :].astype(F32), 0.0)
    xm, xp = _shifted(x, pv, nx)
    y = w_ref[0:1, :] * xm + w_ref[1:2, :] * x + w_ref[2:3, :] * xp + b_ref[...]
    o_ref[...] = (_silu(y) if silu else y).astype(o_ref.dtype)


def _halo_specs(t, ct, n, col0):
    rb = t // HALO_ROWS
    last = n // HALO_ROWS - 1
    return [pl.BlockSpec((t, ct), lambda i, j: (i, col0 + j)),
            pl.BlockSpec((HALO_ROWS, ct), lambda i, j: (jnp.maximum(i * rb - 1, 0), col0 + j)),
            pl.BlockSpec((HALO_ROWS, ct), lambda i, j: (jnp.minimum((i + 1) * rb, last), col0 + j))]


def _conv3(p, col, width, w, b, *, silu, out_dtype):
    n = p.shape[0]
    t = min(n, 1024)
    ct = 256
    nt = n // t
    return pl.pallas_call(
        functools.partial(_conv3_kernel, silu=silu, nt=nt),
        grid=(nt, width // ct),
        in_specs=_halo_specs(t, ct, n, col // ct) + [
            pl.BlockSpec((3, ct), lambda i, j: (0, j)),
            pl.BlockSpec((1, ct), lambda i, j: (0, j))],
        out_specs=pl.BlockSpec((t, ct), lambda i, j: (i, j)),
        out_shape=jax.ShapeDtypeStruct((n, width), out_dtype),
        compiler_params=_params("parallel", "parallel"),
        name="conv3",
    )(p, p, p, w, b)


def _gconv_kernel(bg_ref, cc_ref, cp_ref, cn_ref, xc_ref, xp_ref, xn_ref, w_ref, o_ref, *, nt):
    i = pl.program_id(0)
    f = lambda v: v.astype(F32)
    last = slice(HALO_ROWS - 1, HALO_ROWS)
    x = f(cc_ref[...]) * f(xc_ref[...])
    pv = jnp.where(i > 0, f(cp_ref[last, :]) * f(xp_ref[last, :]), 0.0)
    nx = jnp.where(i < nt - 1, f(cn_ref[0:1, :]) * f(xn_ref[0:1, :]), 0.0)
    xm, xp = _shifted(x, pv, nx)
    o_ref[...] = f(bg_ref[...]) * (w_ref[0:1, :] * xm + w_ref[1:2, :] * x + w_ref[2:3, :] * xp)


def _gated_conv(p, w):
    n = p.shape[0]
    t = min(n, 1024)
    ct = 256
    nt = n // t
    nb = SC_D // ct
    return pl.pallas_call(
        functools.partial(_gconv_kernel, nt=nt),
        grid=(nt, nb),
        in_specs=([pl.BlockSpec((t, ct), lambda i, j: (i, PD // ct + j))]
                  + _halo_specs(t, ct, n, PD // ct + nb)
                  + _halo_specs(t, ct, n, PD // ct + 2 * nb)
                  + [pl.BlockSpec((3, ct), lambda i, j: (0, j))]),
        out_specs=pl.BlockSpec((t, ct), lambda i, j: (i, j)),
        out_shape=jax.ShapeDtypeStruct((n, SC_D), F32),
        compiler_params=_params("parallel", "parallel"),
        name="gated_conv",
    )(p, p, p, p, p, p, p, w)


CONF_RB = 64


def _conf_kernel(vc, gc, vp, gp, vn, gn, w_ref, b_ref, lg_ref, lb_ref, o_ref, buf, *, t, halo, dil, nt):
    i = pl.program_id(0)
    glu = lambda v, g: v.astype(F32) * _sigmoid(g.astype(F32))
    buf[halo:halo + t, :] = glu(vc[...], gc[...])
    buf[0:halo, :] = jnp.where(i > 0, glu(vp[t - halo:t, :], gp[t - halo:t, :]), 0.0)
    buf[halo + t:halo + t + halo, :] = jnp.where(i < nt - 1, glu(vn[0:halo, :], gn[0:halo, :]), 0.0)

    def block(r0):
        acc = jnp.zeros((CONF_RB, CONF_D), F32)
        for j in range(CONF_K):
            off = halo + (j - CONF_K // 2) * dil
            acc = acc + w_ref[j:j + 1, :] * buf[pl.ds(r0 + off, CONF_RB), :]
        v = _layer_norm(acc + b_ref[...], lg_ref[...], lb_ref[...])
        o_ref[pl.ds(r0, CONF_RB), :] = _silu(v)

    if dil % CONF_RB == 0:
        def body(rb, carry):
            block(pl.multiple_of(rb * CONF_RB, CONF_RB))
            return carry
        lax.fori_loop(0, t // CONF_RB, body, 0)
    else:
        for rb in range(t // CONF_RB):
            block(rb * CONF_RB)


def _conformer(p, w, b, lg, lb, *, dil):
    n = p.shape[0]
    t = min(n, 1024)
    nt = n // t
    halo = -(-(CONF_K // 2) * dil // SUBLANE) * SUBLANE
    assert halo <= t
    cb = PA // CONF_D

    def spec(col, shift):
        return pl.BlockSpec((t, CONF_D), lambda i: (jnp.clip(i + shift, 0, nt - 1), col))

    vec = pl.BlockSpec((1, CONF_D), lambda i: (0, 0))
    return pl.pallas_call(
        functools.partial(_conf_kernel, t=t, halo=halo, dil=dil, nt=nt),
        grid=(nt,),
        in_specs=[spec(cb, 0), spec(cb + 1, 0), spec(cb, -1), spec(cb + 1, -1), spec(cb, 1), spec(cb + 1, 1),
                  pl.BlockSpec((CONF_K, CONF_D), lambda i: (0, 0)), vec, vec, vec],
        out_specs=pl.BlockSpec((t, CONF_D), lambda i: (i, 0)),
        out_shape=jax.ShapeDtypeStruct((n, CONF_D), F32),
        scratch_shapes=[pltpu.VMEM((t + 2 * halo, CONF_D), F32)],
        compiler_params=_params("parallel"),
        name="conformer",
    )(p, p, p, p, p, p, w, b, lg, lb)


CONF_COLS = 16
CONF_ROWS_PER_ITER = 4


def _conf_grid_kernel(v_ref, g_ref, w_ref, b_ref, lg_ref, lb_ref, o_ref, buf, *, rows):
    half = CONF_K // 2
    zeros = jnp.zeros((half,) + buf.shape[1:], F32)
    buf[0:half] = zeros
    buf[half + rows:half + rows + half] = zeros
    buf[half:half + rows] = v_ref[...].astype(F32) * _sigmoid(g_ref[...].astype(F32))

    rb = CONF_ROWS_PER_ITER

    def body(it, carry):
        r0 = it * rb
        accs = [None] * rb
        for j in range(CONF_K):
            wj = w_ref[j:j + 1, :]
            for s in range(rb):
                term = wj * buf[r0 + s + j]
                accs[s] = term if j == 0 else accs[s] + term
        for s in range(rb):
            v = _layer_norm(accs[s] + b_ref[...], lg_ref[...], lb_ref[...])
            o_ref[r0 + s] = _silu(v).astype(o_ref.dtype)
        return carry

    lax.fori_loop(0, rows // rb, body, 0)


def _conformer_grid(p, w, b, lg, lb):
    n = p.shape[0]
    rows = n // GRID_W
    p3 = p.reshape(rows, GRID_W, p.shape[1])
    cb = PA // CONF_D
    vec = pl.BlockSpec((1, CONF_D), lambda j: (0, 0))
    blk = lambda col: pl.BlockSpec((rows, CONF_COLS, CONF_D), lambda j: (0, j, col))
    out = pl.pallas_call(
        functools.partial(_conf_grid_kernel, rows=rows),
        grid=(GRID_W // CONF_COLS,),
        in_specs=[blk(cb), blk(cb + 1), pl.BlockSpec((CONF_K, CONF_D), lambda j: (0, 0)), vec, vec, vec],
        out_specs=blk(0),
        out_shape=jax.ShapeDtypeStruct((rows, GRID_W, CONF_D), BF16),
        scratch_shapes=[pltpu.VMEM((rows + 2 * (CONF_K // 2), CONF_COLS, CONF_D), F32)],
        compiler_params=_params("parallel"),
        name="conformer_grid",
    )(p3, p3, w, b, lg, lb)
    return out.reshape(n, CONF_D)


SSD_STATE_SHAPE = (SSD_HEADS, SSD_STATE, SSD_HEAD_DIM)
SSD_CHUNKS_PER_STEP = 2


def _ssd_kernel(xbc_ref, dt_ref, dtb_ref, alog_ref, init_ref, y_ref, fin_ref, h_ref, *, ns, cps):
    d = pl.program_id(0)
    c = pl.program_id(1)
    q = SSD_CHUNK
    hd = SSD_HEAD_DIM

    @pl.when(c == 0)
    def _():
        h_ref[...] = init_ref[...]

    lane = lax.broadcasted_iota(jnp.int32, (q, LANE), 1)
    head = lane < SSD_HEADS
    li = lax.broadcasted_iota(jnp.int32, (q, q), 0)
    si = lax.broadcasted_iota(jnp.int32, (q, q), 1)
    mask = (li - si) * (1 - 2 * d) >= 0
    tri = mask.astype(F32)
    a_rate = -jnp.exp(alog_ref[...])

    def one_chunk(r0):
        rows = pl.ds(r0, q)
        raw = dt_ref[rows, :] + dtb_ref[...]
        dt_all = jnp.maximum(raw, 0.0) + jnp.log(1.0 + jnp.exp(-jnp.abs(raw)))
        ld_all = dt_all * a_rate
        dt_d = jnp.where(head, jnp.where(d == 0, dt_all, pltpu.roll(dt_all, LANE - SSD_HEADS, 1)), 0.0)
        ld_d = jnp.where(head, jnp.where(d == 0, ld_all, pltpu.roll(ld_all, LANE - SSD_HEADS, 1)), 0.0)
        cum = jnp.dot(tri, ld_d, precision=HIGHEST, preferred_element_type=F32)
        tot = jnp.sum(ld_d, axis=0, keepdims=True)
        cum_t = cum.T
        dt_t = dt_d.T
        w_t = (jnp.exp(tot - cum) * dt_d).T
        a_out = jnp.exp(cum)
        e_tot = jnp.exp(tot)

        for g in range(SSD_GROUPS):
            bg = xbc_ref[rows, SSD_D + g * SSD_STATE:SSD_D + (g + 1) * SSD_STATE].astype(F32)
            cg = xbc_ref[rows, SSD_D + SSD_BC + g * SSD_STATE:SSD_D + SSD_BC + (g + 1) * SSD_STATE].astype(F32)
            bg_t = bg.T
            cb = _dot(cg.astype(BF16), bg_t.astype(BF16))
            for e in range(SSD_HPG):
                hh = g * SSD_HPG + e
                diff = cum[:, hh:hh + 1] - cum_t[hh:hh + 1, :]
                dec = jnp.exp(jnp.where(mask, diff, -1e30))
                m = (cb * dec * dt_t[hh:hh + 1, :]).astype(BF16)
                xe = xbc_ref[rows, hh * hd:(hh + 1) * hd].astype(BF16)
                cs = (cg * a_out[:, hh:hh + 1]).astype(BF16)
                h_in = h_ref[hh]
                y_ref[rows, hh * hd:(hh + 1) * hd] = _dot(m, xe) + _dot(cs, h_in.astype(BF16))
                s_new = _dot((bg_t * w_t[hh:hh + 1, :]).astype(BF16), xe)
                h_ref[hh] = e_tot[:, hh:hh + 1] * h_in + s_new

    for k in range(cps):
        one_chunk(pl.multiple_of(jnp.where(d == 0, k, cps - 1 - k) * q, q))

    @pl.when(c == ns - 1)
    def _():
        fin_ref[...] = h_ref[...]


def _ssd_scan(xbc, p, dt_bias, a_log, init):
    n = xbc.shape[0]
    cps = SSD_CHUNKS_PER_STEP
    q = SSD_CHUNK * cps
    ns = n // q

    def chunk(d, c):
        return jnp.where(d == 0, c, ns - 1 - c)

    st = SSD_STATE_SHAPE
    vec = pl.BlockSpec((1, LANE), lambda d, c: (0, 0))
    return pl.pallas_call(
        functools.partial(_ssd_kernel, ns=ns, cps=cps),
        grid=(2, ns),
        in_specs=[pl.BlockSpec((q, SSD_XBC), lambda d, c: (chunk(d, c), 0)),
                  pl.BlockSpec((q, LANE), lambda d, c: (chunk(d, c), 0)),
                  vec, vec,
                  pl.BlockSpec((None,) + st, lambda d, c: (d, 0, 0, 0))],
        out_specs=[pl.BlockSpec((None, q, SSD_D), lambda d, c: (d, chunk(d, c), 0)),
                   pl.BlockSpec((None,) + st, lambda d, c: (d, 0, 0, 0))],
        out_shape=[jax.ShapeDtypeStruct((2, n, SSD_D), F32),
                   jax.ShapeDtypeStruct((2,) + st, F32)],
        scratch_shapes=[pltpu.VMEM(st, F32)],
        compiler_params=_params("arbitrary", "arbitrary"),
        name="ssd_scan",
    )(xbc, p, dt_bias, a_log, init)


def _filt_kernel(feat_ref, w1_ref, b1_ref, w2_ref, b2_ref, fr_ref, w3h_ref, w3l_ref, dl_ref, k_ref, nrm_ref, *,
                 n, t):
    i = pl.program_id(0)
    hf = t // 2
    feat = feat_ref[...]
    x = jnp.concatenate([feat[0:hf], feat[hf:t]], axis=1)
    hid = jnp.sin(fr_ref[0:1, :] * (jnp.dot(x, w1_ref[...], precision=HIGHEST,
                                            preferred_element_type=F32) + b1_ref[...]))
    hid = jnp.sin(fr_ref[1:2, :] * (jnp.dot(hid, w2_ref[...], precision=HIGHEST,
                                            preferred_element_type=F32) + b2_ref[...]))
    hi = hid.astype(BF16)
    lo = (hid - hi.astype(F32)).astype(BF16)

    @pl.when(i == 0)
    def _():
        nrm_ref[...] = jnp.zeros_like(nrm_ref)

    for half in range(2):
        wh, wl = w3h_ref[half], w3l_ref[half]
        filt = _dot(hi, wh) + _dot(lo, wh) + _dot(hi, wl)
        filt = filt * jnp.exp(-feat[half * hf:(half + 1) * hf, 0:1] * dl_ref[...])
        row = i * t + half * hf + lax.broadcasted_iota(jnp.int32, filt.shape, 0)
        filt = jnp.where(row == n, 0.0, filt)
        k_ref[half * hf:(half + 1) * hf, :] = filt
        nrm_ref[...] += jnp.sum(jnp.abs(filt), axis=0, keepdims=True)


def _hyena_filters(featx, fw, n):
    t = min(n, 512)
    half = n // t
    oc = HY_ORDER * HY_D
    full = lambda shape: pl.BlockSpec(shape, lambda i: tuple(0 for _ in shape))
    w3spec = pl.BlockSpec((None, 2, LANE, oc), lambda i: (i // half, 0, 0, 0))
    return pl.pallas_call(
        functools.partial(_filt_kernel, n=n, t=t),
        grid=(2 * n // t,),
        in_specs=[pl.BlockSpec((t, LANE), lambda i: (i, 0)),
                  full((2 * LANE, LANE)), full((1, LANE)), full((LANE, LANE)), full((1, LANE)),
                  full((2, LANE)), w3spec, w3spec,
                  pl.BlockSpec((None, 1, oc), lambda i: (i // half, 0, 0))],
        out_specs=[pl.BlockSpec((t, oc), lambda i: (i, 0)),
                   pl.BlockSpec((1, oc), lambda i: (0, 0))],
        out_shape=[jax.ShapeDtypeStruct((2 * n, oc), F32),
                   jax.ShapeDtypeStruct((1, oc), F32)],
        compiler_params=_params("arbitrary"),
        name="hyena_filters",
    )(featx, fw["w1"], fw["b1"], fw["w2"], fw["b2"], fw["freq"], fw["w3h"], fw["w3l"], fw["deltas"])


def _filter_weights(w1, b1, w2, b2, freq, w3, deltas_d):
    hh = HY_HID
    z = lambda r, c: jnp.zeros((r, c), F32)
    w1p = jnp.pad(w1, ((0, LANE - HY_EMB), (0, 0)))
    w1b = jnp.concatenate([jnp.concatenate([w1p, z(LANE, hh)], 1),
                           jnp.concatenate([z(LANE, hh), w1p], 1)], 0)
    w2b = jnp.concatenate([jnp.concatenate([w2, z(hh, hh)], 1),
                           jnp.concatenate([z(hh, hh), w2], 1)], 0)
    two = lambda v: jnp.concatenate([v, v], axis=-1)
    w3d = w3.reshape(hh, HY_ORDER, 2, HY_D).transpose(2, 0, 1, 3).reshape(2, hh, HY_ORDER * HY_D)
    zz = jnp.zeros_like(w3d)
    w3x = jnp.stack([jnp.concatenate([w3d, zz], 1), jnp.concatenate([zz, w3d], 1)], axis=1)
    w3h = w3x.astype(BF16)
    w3l = (w3x - w3h.astype(F32)).astype(BF16)
    return dict(w1=w1b, b1=two(b1[None, :]), w2=w2b, b2=two(b2[None, :]), freq=two(freq), w3h=w3h, w3l=w3l,
                deltas=deltas_d)


DFT_LANES = 8192


def _dft_rows_kernel(f_ref, x_ref, o_ref, *, nj):
    x = jnp.concatenate([x_ref[:, jj, :] for jj in range(nj)], axis=1)
    o_ref[...] = _dot(f_ref[...], x.astype(BF16)).astype(o_ref.dtype)


def _dft_rows(fmat, x3, col, width):
    m, k = fmat.shape
    n2 = x3.shape[1]
    nj = min(DFT_LANES // width, n2)
    return pl.pallas_call(
        functools.partial(_dft_rows_kernel, nj=nj),
        grid=(n2 // nj,),
        in_specs=[pl.BlockSpec((m, k), lambda j: (0, 0)),
                  pl.BlockSpec((k, nj, width), lambda j: (0, j, col))],
        out_specs=pl.BlockSpec((m, nj * width), lambda j: (0, j)),
        out_shape=jax.ShapeDtypeStruct((m, n2 * width), BF16),
        compiler_params=_params("parallel"),
        name="dft_rows",
    )(fmat, x3)


def _spec_kernel(ar_ref, ai_ref, gr_ref, gi_ref, kr_ref, ki_ref):
    ar, ai, gr, gi = ar_ref[...], ai_ref[...], gr_ref[...], gi_ref[...]
    kr_ref[...] = (_dot(gr, ar) - _dot(gi, ai)).astype(BF16)
    ki_ref[...] = (_dot(gr, ai) + _dot(gi, ar)).astype(BF16)


def _filter_spectrum(a4, gr, gi):
    _, _, n2, ch = a4.shape
    nh = gr.shape[0]
    ct = 512
    blk = lambda ri: pl.BlockSpec((None, None, n2, ct), lambda f, j: (ri, f, 0, j))
    gspec = pl.BlockSpec((None, n2, n2), lambda f, j: (f, 0, 0))
    ospec = pl.BlockSpec((None, n2, ct), lambda f, j: (f, 0, j))
    return pl.pallas_call(
        _spec_kernel,
        grid=(nh, ch // ct),
        in_specs=[blk(0), blk(1), gspec, gspec],
        out_specs=[ospec, ospec],
        out_shape=[jax.ShapeDtypeStruct((nh, n2, ch), BF16)] * 2,
        compiler_params=_params("parallel", "parallel"),
        name="filter_spectrum",
    )(a4, a4, gr, gi)


def _mid_kernel(ar_ref, ai_ref, gr_ref, gi_ref, grt_ref, git_ref, kr_ref, ki_ref, br_ref, bi_ref, *, nh):
    f = pl.program_id(0)

    @pl.when(f < nh)
    def _():
        ar, ai, gr, gi = ar_ref[...], ai_ref[...], gr_ref[...], gi_ref[...]
        xr = _dot(gr, ar) - _dot(gi, ai)
        xi = _dot(gr, ai) + _dot(gi, ar)
        kr, ki = kr_ref[...].astype(F32), ki_ref[...].astype(F32)
        yr = (xr * kr - xi * ki).astype(BF16)
        yi = (xr * ki + xi * kr).astype(BF16)
        grt, git = grt_ref[...], git_ref[...]
        br_ref[...] = (_dot(grt, yr) + _dot(git, yi)).astype(BF16)
        bi_ref[...] = (_dot(grt, yi) - _dot(git, yr)).astype(BF16)

    @pl.when(f >= nh)
    def _():
        br_ref[...] = jnp.zeros_like(br_ref)
        bi_ref[...] = jnp.zeros_like(bi_ref)


def _hyena_mid(a4, tabs, kf_r, kf_i, order):
    _, nf, n2, ch = a4.shape
    gr, gi, grt, git = tabs
    nh = gr.shape[0]
    fi = lambda f: jnp.minimum(f, nh - 1)
    blk = lambda ri: pl.BlockSpec((None, None, n2, ch), lambda f: (ri, fi(f), 0, 0))
    kspec = pl.BlockSpec((None, n2, ch), lambda f: (fi(f), 0, order))
    gspec = pl.BlockSpec((None, n2, n2), lambda f: (fi(f), 0, 0))
    ospec = pl.BlockSpec((None, n2, ch), lambda f: (f, 0, 0))
    return pl.pallas_call(
        functools.partial(_mid_kernel, nh=nh),
        grid=(nf,),
        in_specs=[blk(0), blk(1), gspec, gspec, gspec, gspec, kspec, kspec],
        out_specs=[ospec, ospec],
        out_shape=[jax.ShapeDtypeStruct((nf, n2, ch), BF16)] * 2,
        compiler_params=_params("parallel"),
        name="hyena_mid",
    )(a4, a4, gr, gi, grt, git, kf_r, kf_i)


def _inv_kernel(f_ref, br_ref, bi_ref, s_ref, bias_ref, z_ref, g_ref, o_ref, *, nf, nj, ch):
    acc = _dot(f_ref[:, 0:nf], br_ref[...]) + _dot(f_ref[:, nf:2 * nf], bi_ref[...])
    for jj in range(nj):
        y = acc[:, jj * ch:(jj + 1) * ch] * s_ref[...]
        o_ref[:, jj, :] = g_ref[:, jj, :] * (y + bias_ref[...] * z_ref[:, jj, :])


def _hyena_inverse(finv, b_r, b_i, scale, bias, z3, zcol, g3, gcol):
    t1, k2 = finv.shape
    nf = k2 // 2
    n2 = z3.shape[1]
    ch = HY_D
    nj = min(DFT_LANES // ch, n2)
    col = pl.BlockSpec((nf, nj * ch), lambda j: (0, j))
    row = pl.BlockSpec((1, ch), lambda j: (0, 0))
    return pl.pallas_call(
        functools.partial(_inv_kernel, nf=nf, nj=nj, ch=ch),
        grid=(n2 // nj,),
        in_specs=[pl.BlockSpec((t1, k2), lambda j: (0, 0)), col, col, row, row,
                  pl.BlockSpec((t1, nj, ch), lambda j: (0, j, zcol)),
                  pl.BlockSpec((t1, nj, ch), lambda j: (0, j, gcol))],
        out_specs=pl.BlockSpec((t1, nj, ch), lambda j: (0, j, 0)),
        out_shape=jax.ShapeDtypeStruct((t1, n2, ch), F32),
        compiler_params=_params("parallel"),
        name="hyena_inverse",
    )(finv, b_r, b_i, scale, bias, z3, g3)


def _hyena_nf(n):
    nh = (2 * n // FFT_N2) // 2 + 1
    return -(-nh // 16) * 16


def _dft_tables(n):
    n2 = FFT_N2
    n1 = 2 * n // n2
    tot = 2 * n
    two_pi = 2.0 * math.pi

    def cs(num, den):
        ang = (two_pi / den) * (num % den).astype(F32)
        return jnp.cos(ang), jnp.sin(ang)

    nh = n1 // 2 + 1
    nf = _hyena_nf(n)
    f1 = jnp.arange(nh, dtype=jnp.int32)
    t1 = jnp.arange(n1, dtype=jnp.int32)
    c1, s1 = cs(f1[:, None] * t1[None, :], n1)
    zrow = jnp.zeros((nf - nh, n1), F32)
    fwd_full = jnp.concatenate([c1, zrow, -s1, zrow], axis=0).astype(BF16)
    fwd_half = fwd_full[:, :n1 // 2]
    wgt = jnp.where((f1 == 0) | (f1 == n1 // 2), 1.0, 2.0)[:, None]
    zcol = jnp.zeros((n1 // 2, nf - nh), F32)
    inv = jnp.concatenate([(wgt * c1[:, :n1 // 2]).T, zcol, -(wgt * s1[:, :n1 // 2]).T, zcol],
                          axis=1).astype(BF16)
    t2 = jnp.arange(n2, dtype=jnp.int32)
    twr, twi = cs(f1[:, None] * t2[None, :], tot)
    fr, fi = cs(t2[:, None] * t2[None, :], n2)
    twi, fi = -twi, -fi
    gr = twr[:, None, :] * fr[None] - twi[:, None, :] * fi[None]
    gi = twr[:, None, :] * fi[None] + twi[:, None, :] * fr[None]
    tabs = (gr.astype(BF16), gi.astype(BF16),
            gr.transpose(0, 2, 1).astype(BF16), gi.transpose(0, 2, 1).astype(BF16))
    return fwd_full, fwd_half, inv, tabs


def _hyena_long(q, k2, nrm, hy_bias, tables):
    n = q.shape[0]
    n2 = FFT_N2
    n1 = 2 * n // n2
    fwd_full, fwd_half, inv, tabs = tables
    nf = _hyena_nf(n)
    oc = HY_ORDER * HY_D
    ak = _dft_rows(fwd_full, k2.reshape(n1, n2, oc), 0, oc).reshape(2, nf, n2, oc)
    kf_r, kf_i = _filter_spectrum(ak, tabs[0], tabs[1])
    q3 = q.reshape(n1 // 2, n2, 3 * HY_D)
    z3, zcol = q3, 0
    for o in range(HY_ORDER):
        a4 = _dft_rows(fwd_half, z3, zcol, HY_D).reshape(2, nf, n2, HY_D)
        b_r, b_i = _hyena_mid(a4, tabs, kf_r, kf_i, o)
        scale = 1.0 / (2.0 * n * nrm[:, o * HY_D:(o + 1) * HY_D])
        z3 = _hyena_inverse(inv, b_r.reshape(nf, n2 * HY_D), b_i.reshape(nf, n2 * HY_D), scale,
                            hy_bias[o][None, :], z3, zcol, q3, o + 1)
        zcol = 0
    return z3.reshape(n, HY_D)


def _hy_ctx_kernel(v_ref, x1_ref, x2_ref, k0_ref, k1_ref, n0_ref, n1_ref, bias_ref, o_ref, kf, zs, *, n):
    zs[...] = v_ref[...]
    for o, (k_ref, nr_ref, x_ref) in enumerate(((k0_ref, n0_ref, x1_ref), (k1_ref, n1_ref, x2_ref))):
        kf[0:n, :] = k_ref[n:2 * n, :]
        kf[n:2 * n, :] = k_ref[0:n, :]

        def body(s, acc):
            return acc + kf[pl.ds(n - s, n), :] * zs[pl.ds(s, 1), :]

        acc = lax.fori_loop(0, n, body, jnp.zeros((n, LANE), F32))
        z = zs[...]
        zs[...] = x_ref[...] * (acc / nr_ref[...] + bias_ref[o:o + 1, :] * z)
    o_ref[...] = zs[...]


def _hyena_ctx(q, k2, nrm, hy_bias):
    n = q.shape[0]
    nb = HY_D // LANE
    col = lambda c0: pl.BlockSpec((n, LANE), lambda j: (0, c0 + j))
    kcol = lambda c0: pl.BlockSpec((2 * n, LANE), lambda j: (0, c0 + j))
    ncol = lambda c0: pl.BlockSpec((1, LANE), lambda j: (0, c0 + j))
    return pl.pallas_call(
        functools.partial(_hy_ctx_kernel, n=n),
        grid=(nb,),
        in_specs=[col(0), col(nb), col(2 * nb), kcol(0), kcol(nb), ncol(0), ncol(nb),
                  pl.BlockSpec((HY_ORDER, LANE), lambda j: (0, j))],
        out_specs=pl.BlockSpec((n, LANE), lambda j: (0, j)),
        out_shape=jax.ShapeDtypeStruct((n, HY_D), F32),
        scratch_shapes=[pltpu.VMEM((2 * n, LANE), F32), pltpu.VMEM((n, LANE), F32)],
        compiler_params=_params("parallel"),
        name="hyena_ctx",
    )(q, q, q, k2, k2, nrm, nrm, hy_bias)


def _merge_kernel(ya_ref, xs_ref, yf_ref, yb_ref, z_ref, yc_ref, yd_ref, g_ref, h_ref,
                  dv_ref, ng_ref, g1_ref, lg_ref, lb_ref,
                  wa_ref, wb_ref, wc_ref, wd_ref, wo_ref, o_ref):
    y = xs_ref[...].astype(F32) * dv_ref[...] + yf_ref[...] + yb_ref[...]
    gz = y * _silu(z_ref[...].astype(F32))
    ssd = gz * lax.rsqrt(jnp.mean(gz * gz, -1, keepdims=True) + LN_EPS) * ng_ref[...]
    d = D_MODEL
    gate = lambda k: _sigmoid(g_ref[:, k * d:(k + 1) * d].astype(F32))
    m = gate(0) * _dot(ya_ref[...].astype(BF16), wa_ref[...])
    m = m + gate(1) * _dot(ssd.astype(BF16), wb_ref[...])
    m = m + gate(2) * _dot(yc_ref[...].astype(BF16), wc_ref[...])
    m = m + gate(3) * _dot(yd_ref[...].astype(BF16), wd_ref[...])
    mix = _dot(m.astype(BF16), wo_ref[...])
    o_ref[...] = _layer_norm(DN_ALPHA * h_ref[...] + g1_ref[...] * mix, lg_ref[...], lb_ref[...])


def _merge(ya, xbc, ydir, p, yc, yd, h, dvec, ng, gate1, lg, lb, wa, wb, wc, wd, wo):
    n = h.shape[0]
    t = 256
    tok = lambda w, col=0: pl.BlockSpec((t, w), lambda i: (i, col))
    vec = lambda w: pl.BlockSpec((1, w), lambda i: (0, 0))
    mat = lambda r: pl.BlockSpec((r, D_MODEL), lambda i: (0, 0))
    return pl.pallas_call(
        _merge_kernel,
        grid=(n // t,),
        in_specs=[tok(CONF_D), tok(SSD_D),
                  pl.BlockSpec((None, t, SSD_D), lambda i: (0, i, 0)),
                  pl.BlockSpec((None, t, SSD_D), lambda i: (1, i, 0)),
                  tok(SSD_D, PZ // SSD_D), tok(HY_D), tok(SC_D), tok(N_BRANCH * D_MODEL, 0), tok(D_MODEL),
                  vec(SSD_D), vec(SSD_D), vec(D_MODEL), vec(D_MODEL), vec(D_MODEL),
                  mat(CONF_D), mat(SSD_D), mat(HY_D), mat(SC_D), mat(D_MODEL)],
        out_specs=tok(D_MODEL),
        out_shape=jax.ShapeDtypeStruct((n, D_MODEL), F32),
        compiler_params=_params("parallel"),
        name="merge",
    )(ya, xbc, ydir, ydir, p, yc, yd, p, h, dvec, ng, gate1, lg, lb, wa, wb, wc, wd, wo)


MOE_T = 256


def _stream_tiles(hs):
    tiles = [h.shape[0] // MOE_T for h in hs]
    first = [sum(tiles[:s]) for s in range(len(hs))]
    return tiles, first


def _stream_specs(tiles, first, width):
    return [pl.BlockSpec((MOE_T, width), lambda i, nt=nt, f=f: (jnp.clip(i - f, 0, nt - 1), 0))
            for nt, f in zip(tiles, first)]


def _stream_vec_spec(first):
    def index(i):
        s = 0
        for f in first[1:]:
            s = s + (i >= f).astype(jnp.int32)
        return (s, 0, 0)
    return pl.BlockSpec((None, 1, D_MODEL), index)


def _stream_tile(i, refs, first):
    x = refs[0][...]
    for r, f in zip(refs[1:], first[1:]):
        x = jnp.where(i >= f, r[...], x)
    return x


def _router_kernel(*refs, first):
    ns = len(first)
    h_refs = refs[:ns]
    sh_ref, sc_ref, wh_ref, wl_ref, b_ref, sel_ref, cnt_ref, selt_ref = refs[ns:]
    i = pl.program_id(0)

    @pl.when(i == 0)
    def _():
        cnt_ref[...] = jnp.zeros_like(cnt_ref)

    u = _stream_tile(i, h_refs, first) * (1.0 + sc_ref[...]) + sh_ref[...]
    u_hi = u.astype(BF16)
    u_lo = (u - u_hi.astype(F32)).astype(BF16)
    lg = _dot(u_hi, wh_ref[...]) + _dot(u_lo, wh_ref[...]) + _dot(u_hi, wl_ref[...]) + b_ref[...]
    lane = lax.broadcasted_iota(jnp.int32, lg.shape, 1).astype(F32)
    neg = -1e30
    big = 1e9
    gl = jnp.where(lane < MOE_GROUPS, lg, neg)
    gmax = jnp.max(gl, -1, keepdims=True)
    gsel = jnp.min(jnp.where(gl == gmax, lane, big), -1, keepdims=True)
    gprob = 1.0 / jnp.sum(jnp.where(lane < MOE_GROUPS, jnp.exp(lg - gmax), 0.0), -1, keepdims=True)
    lo = MOE_GROUPS + gsel * MOE_EPG
    el = jnp.where(jnp.abs(lane - lo - (MOE_EPG - 1) / 2.0) < MOE_EPG / 2.0, lg, neg)
    m1 = jnp.max(el, -1, keepdims=True)
    i1 = jnp.min(jnp.where(el == m1, lane, big), -1, keepdims=True)
    el2 = jnp.where(lane == i1, neg, el)
    m2 = jnp.max(el2, -1, keepdims=True)
    i2 = jnp.min(jnp.where(el2 == m2, lane, big), -1, keepdims=True)
    t = jnp.exp(m2 - m1)
    w1 = gprob / (1.0 + t)
    w2 = gprob * t / (1.0 + t)
    oh1 = jnp.where(lane == i1, 1.0, 0.0)
    oh2 = jnp.where(lane == i2, 1.0, 0.0)
    oh = oh1 + oh2
    tt = lg.shape[0]
    li = lax.broadcasted_iota(jnp.int32, (tt, tt), 0)
    si = lax.broadcasted_iota(jnp.int32, (tt, tt), 1)
    before = _dot(jnp.where(li > si, 1.0, 0.0).astype(BF16), oh.astype(BF16)) + cnt_ref[...]
    r1 = jnp.sum(oh1 * before, -1, keepdims=True)
    r2 = jnp.sum(oh2 * before, -1, keepdims=True)
    cnt_ref[...] += jnp.sum(oh, axis=0, keepdims=True)
    cols = (i1 - MOE_GROUPS, i2 - MOE_GROUPS, w1, w2, r1, r2)
    sel = jnp.zeros_like(lg)
    for k, v in enumerate(cols):
        sel = jnp.where(lane == k, v, sel)
    sel_ref[...] = sel
    selt_ref[...] = sel.T[0:SUBLANE, :]


def _router(hs, shift, scale, wr, br):
    tiles, first = _stream_tiles(hs)
    t = MOE_T
    n = t * sum(tiles)
    vec = lambda w: pl.BlockSpec((1, w), lambda i: (0, 0))
    wr_hi = wr.astype(BF16)
    wr_lo = (wr - wr_hi.astype(F32)).astype(BF16)
    wspec = pl.BlockSpec((D_MODEL, LANE), lambda i: (0, 0))
    return pl.pallas_call(
        functools.partial(_router_kernel, first=first),
        grid=(sum(tiles),),
        in_specs=_stream_specs(tiles, first, D_MODEL) + [
            _stream_vec_spec(first), _stream_vec_spec(first), wspec, wspec, vec(LANE)],
        out_specs=[pl.BlockSpec((t, LANE), lambda i: (i, 0)), vec(LANE),
                   pl.BlockSpec((SUBLANE, t), lambda i: (0, i))],
        out_shape=[jax.ShapeDtypeStruct((n, LANE), F32), jax.ShapeDtypeStruct((1, LANE), F32),
                   jax.ShapeDtypeStruct((SUBLANE, n), F32)],
        compiler_params=_params("arbitrary"),
        name="router",
    )(*hs, shift, scale, wr_hi, wr_lo, br)


ROW_WORDS = D_MODEL // 2


def _pack_rows(x):
    c = x.shape[1] // 2
    bits = lambda v: lax.bitcast_convert_type(v.astype(BF16).astype(F32), jnp.uint32)
    return bits(x[:, :c]) | (bits(x[:, c:]) >> 16)


def _unpack_rows(w):
    hi = lax.bitcast_convert_type(w & jnp.uint32(0xFFFF0000), F32)
    lo = lax.bitcast_convert_type(w << 16, F32)
    return jnp.concatenate([hi, lo], axis=1)


def _dispatch_kernel(dst_ref, *refs, first, nt):
    ns = len(first)
    h_refs = refs[:ns]
    sh_ref, sc_ref, zero_hbm, xin_hbm, ubuf, sem = refs[ns:]
    del zero_hbm
    t = MOE_T
    i = pl.program_id(0)
    slot = i % 2

    def wait_slot(s):
        for _ in range(MOE_TOP_K):
            pltpu.make_async_copy(ubuf.at[s], xin_hbm.at[pl.ds(0, t), :], sem.at[s]).wait()

    @pl.when(i >= 2)
    def _():
        wait_slot(slot)

    ubuf[slot] = _pack_rows(_stream_tile(i, h_refs, first) * (1.0 + sc_ref[...]) + sh_ref[...])

    def issue(r, carry):
        for k in range(MOE_TOP_K):
            pltpu.make_async_copy(ubuf.at[slot, pl.ds(r, 1), :],
                                  xin_hbm.at[pl.ds(dst_ref[0, 0, k * t + r], 1), :], sem.at[slot]).start()
        return carry

    lax.fori_loop(0, t, issue, 0, unroll=8)

    @pl.when(i == nt - 1)
    def _():
        wait_slot(slot)
        if nt > 1:
            wait_slot(1 - slot)


def _dispatch(pos_t, hs, shift, scale, n_rows):
    tiles, first = _stream_tiles(hs)
    t = MOE_T
    nt = sum(tiles)
    return pl.pallas_call(
        functools.partial(_dispatch_kernel, first=first, nt=nt),
        grid=(nt,),
        in_specs=([pl.BlockSpec((1, 1, MOE_TOP_K * t), lambda i: (i, 0, 0), memory_space=pltpu.SMEM)]
                  + _stream_specs(tiles, first, D_MODEL)
                  + [_stream_vec_spec(first), _stream_vec_spec(first), pl.BlockSpec(memory_space=pl.ANY)]),
        out_specs=pl.BlockSpec(memory_space=pl.ANY),
        out_shape=jax.ShapeDtypeStruct((n_rows, ROW_WORDS), jnp.uint32),
        scratch_shapes=[pltpu.VMEM((2, t, ROW_WORDS), jnp.uint32), pltpu.SemaphoreType.DMA((2,))],
        input_output_aliases={3 + len(hs): 0},
        compiler_params=_params("arbitrary"),
        name="dispatch",
    )(pos_t, *hs, shift, scale, jnp.zeros((n_rows, ROW_WORDS), jnp.uint32))


def _expert_kernel(be_ref, nu_ref, x_ref, wg_ref, wu_ref, wd_ref, o_ref, wgb, wub, wdb):
    b = pl.program_id(0)

    @pl.when((b == 0) | (be_ref[b] != be_ref[jnp.maximum(b - 1, 0)]))
    def _():
        wgb[...] = wg_ref[...].astype(BF16)
        wub[...] = wu_ref[...].astype(BF16)
        wdb[...] = wd_ref[...].astype(BF16)

    @pl.when(b < nu_ref[0])
    def _():
        x = _unpack_rows(x_ref[...]).astype(BF16)
        hid = _silu(_dot(x, wgb[...])) * _dot(x, wub[...])
        o_ref[...] = _pack_rows(_dot(hid.astype(BF16), wdb[...]))

    @pl.when(b >= nu_ref[0])
    def _():
        o_ref[...] = jnp.zeros_like(o_ref)


def _experts(xin, block_e, n_used, wg, wu, wd, layer):
    n_blocks = block_e.shape[0]
    gs = pltpu.PrefetchScalarGridSpec(
        num_scalar_prefetch=2,
        grid=(n_blocks,),
        in_specs=[pl.BlockSpec((MOE_ROWS, ROW_WORDS), lambda b, be, nu: (b, 0)),
                  pl.BlockSpec((None, None, D_MODEL, MOE_FF), lambda b, be, nu: (layer, be[b], 0, 0)),
                  pl.BlockSpec((None, None, D_MODEL, MOE_FF), lambda b, be, nu: (layer, be[b], 0, 0)),
                  pl.BlockSpec((None, None, MOE_FF, D_MODEL), lambda b, be, nu: (layer, be[b], 0, 0))],
        out_specs=pl.BlockSpec((MOE_ROWS, ROW_WORDS), lambda b, be, nu: (b, 0)),
        scratch_shapes=[pltpu.VMEM((D_MODEL, MOE_FF), BF16), pltpu.VMEM((D_MODEL, MOE_FF), BF16),
                        pltpu.VMEM((MOE_FF, D_MODEL), BF16)],
    )
    return pl.pallas_call(
        _expert_kernel,
        grid_spec=gs,
        out_shape=jax.ShapeDtypeStruct((n_blocks * MOE_ROWS, ROW_WORDS), jnp.uint32),
        compiler_params=_params("arbitrary"),
        name="experts",
    )(block_e, n_used, xin, wg, wu, wd)


def _combine_kernel(pos_ref, posn_ref, y_hbm, *refs, first, tiles):
    ns = len(first)
    nt = sum(tiles)
    h_refs = refs[:ns]
    sel_ref, g2_ref, lg_ref, lb_ref = refs[ns:ns + 4]
    o_refs = refs[ns + 4:2 * ns + 4]
    ybuf, sem = refs[2 * ns + 4:]
    t = MOE_T
    i = pl.program_id(0)
    slot = i % 2

    def gather(p_ref, s):
        def issue(r, carry):
            pltpu.make_async_copy(y_hbm.at[pl.ds(p_ref[0, 0, r], 1), :], ybuf.at[s, pl.ds(r, 1), :],
                                  sem.at[s]).start()
            return carry
        lax.fori_loop(0, MOE_TOP_K * t, issue, 0, unroll=8)

    @pl.when(i == 0)
    def _():
        gather(pos_ref, 0)

    @pl.when(i + 1 < nt)
    def _():
        gather(posn_ref, 1 - slot)

    pltpu.make_async_copy(y_hbm.at[pl.ds(0, MOE_TOP_K * t), :], ybuf.at[slot], sem.at[slot]).wait()
    ffn = (sel_ref[:, 2:3] * _unpack_rows(ybuf[slot, 0:t, :])
           + sel_ref[:, 3:4] * _unpack_rows(ybuf[slot, t:2 * t, :]))
    out = _layer_norm(DN_ALPHA * _stream_tile(i, h_refs, first) + g2_ref[...] * ffn, lg_ref[...], lb_ref[...])
    for s in range(ns):
        @pl.when((i >= first[s]) & (i < first[s] + tiles[s]))
        def _(s=s):
            o_refs[s][...] = out


def _combine(y, pos_t, hs, sel, gate2, lg, lb):
    tiles, first = _stream_tiles(hs)
    t = MOE_T
    nt = sum(tiles)
    vec = pl.BlockSpec((1, D_MODEL), lambda i: (0, 0))
    return pl.pallas_call(
        functools.partial(_combine_kernel, first=first, tiles=tiles),
        grid=(nt,),
        in_specs=([pl.BlockSpec((1, 1, MOE_TOP_K * t), lambda i: (i, 0, 0), memory_space=pltpu.SMEM),
                   pl.BlockSpec((1, 1, MOE_TOP_K * t), lambda i: (jnp.minimum(i + 1, nt - 1), 0, 0),
                                memory_space=pltpu.SMEM),
                   pl.BlockSpec(memory_space=pl.ANY)]
                  + _stream_specs(tiles, first, D_MODEL)
                  + [pl.BlockSpec((t, LANE), lambda i: (i, 0)), _stream_vec_spec(first), vec, vec]),
        out_specs=_stream_specs(tiles, first, D_MODEL),
        out_shape=[jax.ShapeDtypeStruct(h.shape, F32) for h in hs],
        scratch_shapes=[pltpu.VMEM((2, MOE_TOP_K * t, ROW_WORDS), jnp.uint32), pltpu.SemaphoreType.DMA((2,))],
        compiler_params=_params("arbitrary"),
        name="combine",
    )(pos_t, pos_t, y, *hs, sel, gate2, lg, lb)


def _moe(hs, shift, scale, gate2, lg, lb, wr, br, wg, wu, wd, layer):
    t = MOE_T
    n = sum(h.shape[0] for h in hs)
    nt = n // t
    sel, cnt, selt = _router(hs, shift, scale, wr, br)
    counts = cnt[0, MOE_GROUPS:MOE_GROUPS + MOE_EXPERTS].astype(jnp.int32)
    padded = (counts + MOE_ROWS - 1) // MOE_ROWS * MOE_ROWS
    pad_end = jnp.cumsum(padded)
    pad_start = pad_end - padded
    n_blocks = (n * MOE_TOP_K + MOE_EXPERTS * (MOE_ROWS - 1) + MOE_ROWS - 1) // MOE_ROWS
    blk_row = jnp.arange(n_blocks, dtype=jnp.int32) * MOE_ROWS
    block_e = jnp.minimum(jnp.sum((blk_row[:, None] >= pad_end[None, :]).astype(jnp.int32), axis=1),
                          MOE_EXPERTS - 1)
    n_used = (pad_end[-1:] // MOE_ROWS).astype(jnp.int32)
    e_kt = selt[0:MOE_TOP_K].astype(jnp.int32)
    ids = jnp.arange(MOE_EXPERTS, dtype=jnp.int32)[None, :, None]
    start_kt = jnp.sum(jnp.where(e_kt[:, None, :] == ids, pad_start[None, :, None], 0), axis=1)
    pos_kt = start_kt + selt[4:4 + MOE_TOP_K].astype(jnp.int32)
    pos_t = pos_kt.reshape(MOE_TOP_K, nt, t).transpose(1, 0, 2).reshape(nt, 1, MOE_TOP_K * t)
    xin = _dispatch(pos_t, hs, shift, scale, n_blocks * MOE_ROWS)
    y = _experts(xin, block_e, n_used, wg, wu, wd, layer)
    return _combine(y, pos_t, hs, sel, gate2, lg, lb)


def _mixer(h, mod, lw, ssd_init, tables, *, latent, need_mix):
    n = h.shape[0]
    p, dt_raw = _inproj(h, mod[0], mod[1], lw["w_in"], lw["layer"])
    xbc = _conv3(p, PX, SSD_XBC, lw["ssd_conv_w"], lw["ssd_conv_b"], silu=True, out_dtype=BF16)
    ydir, finals = _ssd_scan(xbc, dt_raw, lw["ssd_dt_bias"], lw["ssd_a_log"], ssd_init)
    if not need_mix:
        return None, finals
    conf_w = (lw["conf_dw_w"], lw["conf_dw_b"], lw["conf_ln_g"], lw["conf_ln_b"])
    ya = _conformer_grid(p, *conf_w) if latent else _conformer(p, *conf_w, dil=1)
    q = _conv3(p, PC, 3 * HY_D, lw["hy_short_w"], lw["hy_short_b"], silu=False, out_dtype=F32)
    k2, nrm = _hyena_filters(lw["feat_lat" if latent else "feat_ctx"], lw["hy_filter"], n)
    if latent:
        yc = _hyena_long(q, k2, nrm, lw["hy_bias"], tables)
    else:
        yc = _hyena_ctx(q, k2, nrm, lw["hy_bias"])
    yd = _gated_conv(p, lw["sc_conv_w"])
    h = _merge(ya, xbc, ydir, p, yc, yd, h, lw["ssd_dvec"], lw["ssd_norm_g"], mod[2], lw["ln_g0"], lw["ln_b0"],
               lw["w_branch_a"], lw["w_branch_b"], lw["w_branch_c"], lw["w_branch_d"], lw["w_out"])
    return h, finals


def _positional_features(n):
    t01 = jnp.linspace(0.0, 1.0, n, dtype=F32)[:, None]
    omega = (2.0 * math.pi / n) * jnp.arange(n, dtype=F32)[:, None]
    bands = jnp.linspace(1e-4, HY_BANDS - 1, HY_BANDS, dtype=F32)
    feat = jnp.concatenate([t01, jnp.cos(bands * omega), -jnp.sin(bands * omega)], axis=-1)
    featx = jnp.concatenate([feat, jnp.zeros((1, HY_EMB), F32), jnp.flip(feat[1:], axis=0)], axis=0)
    return _pad_lanes(featx)


def _relayout_w_in(w_in):
    seg = lambda a, b: w_in[:, :, a:b]
    ob = OFF_B
    parts = [seg(OFF_G, OFF_G + N_BRANCH * D_MODEL),
             seg(OFF_A, OFF_B),
             seg(OFF_C, OFF_D),
             seg(OFF_D, OFF_G),
             seg(ob + SSD_D, ob + SSD_D + SSD_XBC),
             seg(ob, ob + SSD_D),
             seg(ob + SSD_D + SSD_XBC, OFF_C),
             jnp.zeros(w_in.shape[:2] + (NP - PDT - 2 * SSD_HEADS,), w_in.dtype)]
    return jnp.concatenate(parts, axis=-1).astype(BF16)


def _pad_lanes(v):
    return jnp.pad(v, ((0, 0), (0, LANE - v.shape[-1])))


def kernel(x, c, ctx, c_ctx, w_mod, b_mod, ln_g, ln_b, w_in, conf_dw_w, conf_dw_b, conf_ln_g, conf_ln_b,
           ssd_conv_w, ssd_conv_b, ssd_a_log, ssd_dt_bias, ssd_d, ssd_norm_g, hy_short_w, hy_short_b,
           hy_w1, hy_b1, hy_w2, hy_b2, hy_freq, hy_w3, hy_bias, sc_conv_w, w_branch_a, w_branch_b,
           w_branch_c, w_branch_d, w_out, rt_group_w, rt_group_b, rt_expert_w, rt_expert_b,
           ex_w_gate, ex_w_up, ex_w_down):
    assert x.shape[0] == 1 and ctx.shape[0] == 1
    n_lat, n_ctx = x.shape[1], ctx.shape[1]
    depth = w_in.shape[0]

    cv = jnp.concatenate([c, c_ctx[None, :], jnp.zeros((SUBLANE - 2, D_MODEL), F32)], axis=0)
    mods = _mod_vectors(cv, w_mod, b_mod)
    w_in_p = _relayout_w_in(w_in)
    tables = _dft_tables(n_lat)
    feat_lat = _positional_features(n_lat)
    feat_ctx = _positional_features(n_ctx)
    deltas = jnp.abs(jnp.linspace(HY_MIN_DECAY, HY_MAX_DECAY, HY_N_FILT, dtype=F32))
    deltas_d = deltas.reshape(HY_ORDER, 2, HY_D).transpose(1, 0, 2).reshape(2, 1, HY_ORDER * HY_D)
    router_w = jnp.concatenate([rt_group_w, rt_expert_w,
                                jnp.zeros((depth, D_MODEL, LANE - MOE_GROUPS - MOE_EXPERTS), F32)], axis=-1)
    router_b = jnp.concatenate([rt_group_b, rt_expert_b,
                                jnp.zeros((depth, LANE - MOE_GROUPS - MOE_EXPERTS), F32)], axis=-1)
    ssd_zero = jnp.zeros((2,) + SSD_STATE_SHAPE, F32)

    h_lat, h_ctx = x[0], ctx[0]
    for l in range(depth):
        row = lambda v: v[None, :]
        lw = dict(
            w_in=w_in_p, layer=l, conf_dw_w=conf_dw_w[l], conf_dw_b=row(conf_dw_b[l]), conf_ln_g=row(conf_ln_g[l]),
            conf_ln_b=row(conf_ln_b[l]), ssd_conv_w=ssd_conv_w[l], ssd_conv_b=row(ssd_conv_b[l]),
            ssd_a_log=_pad_lanes(ssd_a_log[l].reshape(1, -1)), ssd_dt_bias=_pad_lanes(ssd_dt_bias[l].reshape(1, -1)),
            ssd_dvec=row(jnp.repeat(ssd_d[l], SSD_HEAD_DIM)), ssd_norm_g=row(ssd_norm_g[l]),
            hy_short_w=hy_short_w[l], hy_short_b=row(hy_short_b[l]),
            hy_filter=_filter_weights(hy_w1[l], hy_b1[l], hy_w2[l], hy_b2[l], hy_freq[l], hy_w3[l], deltas_d),
            hy_bias=hy_bias[l], feat_lat=feat_lat, feat_ctx=feat_ctx,
            sc_conv_w=sc_conv_w[l], ln_g0=row(ln_g[l, 0]), ln_b0=row(ln_b[l, 0]),
            w_branch_a=w_branch_a[l].astype(BF16), w_branch_b=w_branch_b[l].astype(BF16),
            w_branch_c=w_branch_c[l].astype(BF16), w_branch_d=w_branch_d[l].astype(BF16),
            w_out=w_out[l].astype(BF16))
        moe_w = (router_w[l], row(router_b[l]), ex_w_gate, ex_w_up, ex_w_down, l)
        last = l == depth - 1
        d = D_MODEL
        mod_lat = [mods[l, 0:1, k * d:(k + 1) * d] for k in range(6)]
        mod_ctx = [mods[l, 1:2, k * d:(k + 1) * d] for k in range(6)]

        mix_ctx, ctx_states = _mixer(h_ctx, mod_ctx, lw, ssd_zero, None, latent=False, need_mix=not last)
        h_lat, _ = _mixer(h_lat, mod_lat, lw, ctx_states, tables, latent=True, need_mix=True)
        streams = [(h_lat, mod_lat)] if last else [(h_lat, mod_lat), (mix_ctx, mod_ctx)]
        vecs = [jnp.stack([m[k] for _, m in streams]) for k in (3, 4, 5)]
        outs = _moe([h for h, _ in streams], *vecs, row(ln_g[l, 1]), row(ln_b[l, 1]), *moe_w)
        h_lat = outs[0]
        if not last:
            h_ctx = outs[1]
    return h_lat[None]
```

```python
import functools
import math

import jax
import jax.numpy as jnp
from jax import lax
from jax.experimental import pallas as pl
from jax.experimental.pallas import tpu as pltpu

F32 = jnp.float32
BF16 = jnp.bfloat16
HIGHEST = lax.Precision.HIGHEST

D_MODEL = 1024
DEPTH = 4
GRID_W = 64
CONF_D = 512
CONF_K = 31
SSD_D = 768
SSD_HEADS = 12
SSD_HEAD_DIM = 64
SSD_GROUPS = 4
SSD_HPG = SSD_HEADS // SSD_GROUPS
SSD_STATE = 128
SSD_CHUNK = 128
SSD_BC = SSD_GROUPS * SSD_STATE
SSD_XBC = SSD_D + 2 * SSD_BC
SSD_PROJ = SSD_D + SSD_XBC + 2 * SSD_HEADS
HY_D = 512
HY_ORDER = 2
HY_EMB = 33
HY_BANDS = (HY_EMB - 1) // 2
HY_HID = 64
HY_N_FILT = HY_ORDER * 2 * HY_D
HY_MIN_DECAY = math.log(1e-2) / 1.5
HY_MAX_DECAY = math.log(1e-2) / 0.3
SC_D = 512
N_BRANCH = 4
OFF_A = 0
OFF_B = OFF_A + 2 * CONF_D
OFF_C = OFF_B + SSD_PROJ
OFF_D = OFF_C + 3 * HY_D
OFF_G = OFF_D + 3 * SC_D
MOE_GROUPS = 4
MOE_EPG = 8
MOE_EXPERTS = MOE_GROUPS * MOE_EPG
MOE_TOP_K = 2
MOE_FF = 512
DN_ALPHA = (2 * DEPTH) ** 0.25
LN_EPS = 1e-5

PG = 0
PA = PG + N_BRANCH * D_MODEL
PC = PA + 2 * CONF_D
PD = PC + 3 * HY_D
PX = PD + 3 * SC_D
PZ = PX + SSD_XBC
PDT = PZ + SSD_D
INPROJ_TN = 1024
NP = -(-(PDT + 128) // INPROJ_TN) * INPROJ_TN

LANE = 128
SUBLANE = 8
FFT_N2 = 256
MOE_ROWS = 256
VMEM_LIMIT = 48 * 1024 * 1024


def _params(*sem):
    return pltpu.CompilerParams(dimension_semantics=sem, vmem_limit_bytes=VMEM_LIMIT)


def _sigmoid(x):
    return 0.5 * jnp.tanh(0.5 * x) + 0.5


def _silu(x):
    return x * _sigmoid(x)


def _layer_norm(x, g, b):
    mu = jnp.mean(x, -1, keepdims=True)
    xc = x - mu
    var = jnp.mean(xc * xc, -1, keepdims=True)
    return xc * lax.rsqrt(var + LN_EPS) * g + b


def _dot(a, b):
    return jnp.dot(a, b, preferred_element_type=F32)


def _mod_kernel(cv_ref, w_ref, b_ref, o_ref):
    o_ref[...] = jnp.dot(_silu(cv_ref[...]), w_ref[...], precision=HIGHEST,
                         preferred_element_type=F32) + b_ref[...]


def _mod_vectors(cv, w_mod, b_mod):
    tn = 1536
    return pl.pallas_call(
        _mod_kernel,
        grid=(DEPTH, 6 * D_MODEL // tn),
        in_specs=[pl.BlockSpec((SUBLANE, D_MODEL), lambda l, j: (0, 0)),
                  pl.BlockSpec((None, D_MODEL, tn), lambda l, j: (l, 0, j)),
                  pl.BlockSpec((None, 1, tn), lambda l, j: (l, 0, j))],
        out_specs=pl.BlockSpec((None, SUBLANE, tn), lambda l, j: (l, 0, j)),
        out_shape=jax.ShapeDtypeStruct((DEPTH, SUBLANE, 6 * D_MODEL), F32),
        compiler_params=_params("parallel", "parallel"),
        name="mod_vectors",
    )(cv, w_mod, b_mod.reshape(DEPTH, 1, 6 * D_MODEL))


def _inproj_kernel(x_ref, sh_ref, sc_ref, w_ref, o_ref, dt_ref, xb_ref, *, nj):
    j = pl.program_id(1)

    @pl.when(j == 0)
    def _():
        xb_ref[...] = (x_ref[...] * (1.0 + sc_ref[...]) + sh_ref[...]).astype(BF16)

    res = _dot(xb_ref[...], w_ref[...])
    o_ref[...] = res.astype(BF16)

    @pl.when(j == nj - 1)
    def _():
        off = PDT - (nj - 1) * INPROJ_TN
        dt_ref[...] = res[:, off:off + LANE]


def _inproj(h, shift, scale, w, layer):
    n = h.shape[0]
    tm = min(n, 2048)
    tn = INPROJ_TN
    nj = NP // tn
    assert PDT >= (nj - 1) * tn
    return pl.pallas_call(
        functools.partial(_inproj_kernel, nj=nj),
        grid=(n // tm, nj),
        in_specs=[pl.BlockSpec((tm, D_MODEL), lambda i, j: (i, 0)),
                  pl.BlockSpec((1, D_MODEL), lambda i, j: (0, 0)),
                  pl.BlockSpec((1, D_MODEL), lambda i, j: (0, 0)),
                  pl.BlockSpec((None, D_MODEL, tn), lambda i, j: (layer, 0, j))],
        out_specs=[pl.BlockSpec((tm, tn), lambda i, j: (i, j)),
                   pl.BlockSpec((tm, LANE), lambda i, j: (i, 0))],
        out_shape=[jax.ShapeDtypeStruct((n, NP), BF16), jax.ShapeDtypeStruct((n, LANE), F32)],
        scratch_shapes=[pltpu.VMEM((tm, D_MODEL), BF16)],
        compiler_params=_params("parallel", "arbitrary"),
        name="inproj",
    )(h, shift, scale, w)


def _shifted(x, prev_row, next_row):
    t = x.shape[0]
    row = lax.broadcasted_iota(jnp.int32, x.shape, 0)
    xm = jnp.where(row == 0, prev_row, pltpu.roll(x, 1, 0))
    xp = jnp.where(row == t - 1, next_row, pltpu.roll(x, t - 1, 0))
    return xm, xp


HALO_ROWS = 16


def _conv3_kernel(cur_ref, prev_ref, next_ref, w_ref, b_ref, o_ref, *, silu, nt):
    i = pl.program_id(0)
    x = cur_ref[...].astype(F32)
    pv = jnp.where(i > 0, prev_ref[HALO_ROWS - 1:HALO_ROWS, :].astype(F32), 0.0)
    nx = jnp.where(i < nt - 1, next_ref[0:1, :].astype(F32), 0.0)
    xm, xp = _shifted(x, pv, nx)
    y = w_ref[0:1, :] * xm + w_ref[1:2, :] * x + w_ref[2:3, :] * xp + b_ref[...]
    o_ref[...] = (_silu(y) if silu else y).astype(o_ref.dtype)


def _halo_specs(t, ct, n, col0):
    rb = t // HALO_ROWS
    last = n // HALO_ROWS - 1
    return [pl.BlockSpec((t, ct), lambda i, j: (i, col0 + j)),
            pl.BlockSpec((HALO_ROWS, ct), lambda i, j: (jnp.maximum(i * rb - 1, 0), col0 + j)),
            pl.BlockSpec((HALO_ROWS, ct), lambda i, j: (jnp.minimum((i + 1) * rb, last), col0 + j))]


def _conv3(p, col, width, w, b, *, silu, out_dtype):
    n = p.shape[0]
    t = min(n, 1024)
    ct = 256
    nt = n // t
    return pl.pallas_call(
        functools.partial(_conv3_kernel, silu=silu, nt=nt),
        grid=(nt, width // ct),
        in_specs=_halo_specs(t, ct, n, col // ct) + [
            pl.BlockSpec((3, ct), lambda i, j: (0, j)),
            pl.BlockSpec((1, ct), lambda i, j: (0, j))],
        out_specs=pl.BlockSpec((t, ct), lambda i, j: (i, j)),
        out_shape=jax.ShapeDtypeStruct((n, width), out_dtype),
        compiler_params=_params("parallel", "parallel"),
        name="conv3",
    )(p, p, p, w, b)


def _gconv_kernel(bg_ref, cc_ref, cp_ref, cn_ref, xc_ref, xp_ref, xn_ref, w_ref, o_ref, *, nt):
    i = pl.program_id(0)
    f = lambda v: v.astype(F32)
    last = slice(HALO_ROWS - 1, HALO_ROWS)
    x = f(cc_ref[...]) * f(xc_ref[...])
    pv = jnp.where(i > 0, f(cp_ref[last, :]) * f(xp_ref[last, :]), 0.0)
    nx = jnp.where(i < nt - 1, f(cn_ref[0:1, :]) * f(xn_ref[0:1, :]), 0.0)
    xm, xp = _shifted(x, pv, nx)
    o_ref[...] = f(bg_ref[...]) * (w_ref[0:1, :] * xm + w_ref[1:2, :] * x + w_ref[2:3, :] * xp)


def _gated_conv(p, w):
    n = p.shape[0]
    t = min(n, 1024)
    ct = 256
    nt = n // t
    nb = SC_D // ct
    return pl.pallas_call(
        functools.partial(_gconv_kernel, nt=nt),
        grid=(nt, nb),
        in_specs=([pl.BlockSpec((t, ct), lambda i, j: (i, PD // ct + j))]
                  + _halo_specs(t, ct, n, PD // ct + nb)
                  + _halo_specs(t, ct, n, PD // ct + 2 * nb)
                  + [pl.BlockSpec((3, ct), lambda i, j: (0, j))]),
        out_specs=pl.BlockSpec((t, ct), lambda i, j: (i, j)),
        out_shape=jax.ShapeDtypeStruct((n, SC_D), F32),
        compiler_params=_params("parallel", "parallel"),
        name="gated_conv",
    )(p, p, p, p, p, p, p, w)


CONF_RB = 64


def _conf_kernel(vc, gc, vp, gp, vn, gn, w_ref, b_ref, lg_ref, lb_ref, o_ref, buf, *, t, halo, dil, nt):
    i = pl.program_id(0)
    glu = lambda v, g: v.astype(F32) * _sigmoid(g.astype(F32))
    buf[halo:halo + t, :] = glu(vc[...], gc[...])
    buf[0:halo, :] = jnp.where(i > 0, glu(vp[t - halo:t, :], gp[t - halo:t, :]), 0.0)
    buf[halo + t:halo + t + halo, :] = jnp.where(i < nt - 1, glu(vn[0:halo, :], gn[0:halo, :]), 0.0)

    def block(r0):
        acc = jnp.zeros((CONF_RB, CONF_D), F32)
        for j in range(CONF_K):
            off = halo + (j - CONF_K // 2) * dil
            acc = acc + w_ref[j:j + 1, :] * buf[pl.ds(r0 + off, CONF_RB), :]
        v = _layer_norm(acc + b_ref[...], lg_ref[...], lb_ref[...])
        o_ref[pl.ds(r0, CONF_RB), :] = _silu(v)

    if dil % CONF_RB == 0:
        def body(rb, carry):
            block(pl.multiple_of(rb * CONF_RB, CONF_RB))
            return carry
        lax.fori_loop(0, t // CONF_RB, body, 0)
    else:
        for rb in range(t // CONF_RB):
            block(rb * CONF_RB)


def _conformer(p, w, b, lg, lb, *, dil):
    n = p.shape[0]
    t = min(n, 1024)
    nt = n // t
    halo = -(-(CONF_K // 2) * dil // SUBLANE) * SUBLANE
    assert halo <= t
    cb = PA // CONF_D

    def spec(col, shift):
        return pl.BlockSpec((t, CONF_D), lambda i: (jnp.clip(i + shift, 0, nt - 1), col))

    vec = pl.BlockSpec((1, CONF_D), lambda i: (0, 0))
    return pl.pallas_call(
        functools.partial(_conf_kernel, t=t, halo=halo, dil=dil, nt=nt),
        grid=(nt,),
        in_specs=[spec(cb, 0), spec(cb + 1, 0), spec(cb, -1), spec(cb + 1, -1), spec(cb, 1), spec(cb + 1, 1),
                  pl.BlockSpec((CONF_K, CONF_D), lambda i: (0, 0)), vec, vec, vec],
        out_specs=pl.BlockSpec((t, CONF_D), lambda i: (i, 0)),
        out_shape=jax.ShapeDtypeStruct((n, CONF_D), F32),
        scratch_shapes=[pltpu.VMEM((t + 2 * halo, CONF_D), F32)],
        compiler_params=_params("parallel"),
        name="conformer",
    )(p, p, p, p, p, p, w, b, lg, lb)


CONF_COLS = 16
CONF_ROWS_PER_ITER = 4


def _conf_grid_kernel(v_ref, g_ref, w_ref, b_ref, lg_ref, lb_ref, o_ref, buf, *, rows):
    half = CONF_K // 2
    zeros = jnp.zeros((half,) + buf.shape[1:], F32)
    buf[0:half] = zeros
    buf[half + rows:half + rows + half] = zeros
    buf[half:half + rows] = v_ref[...].astype(F32) * _sigmoid(g_ref[...].astype(F32))

    rb = CONF_ROWS_PER_ITER

    def body(it, carry):
        r0 = it * rb
        accs = [None] * rb
        for j in range(CONF_K):
            wj = w_ref[j:j + 1, :]
            for s in range(rb):
                term = wj * buf[r0 + s + j]
                accs[s] = term if j == 0 else accs[s] + term
        for s in range(rb):
            v = _layer_norm(accs[s] + b_ref[...], lg_ref[...], lb_ref[...])
            o_ref[r0 + s] = _silu(v).astype(o_ref.dtype)
        return carry

    lax.fori_loop(0, rows // rb, body, 0)


def _conformer_grid(p, w, b, lg, lb):
    n = p.shape[0]
    rows = n // GRID_W
    p3 = p.reshape(rows, GRID_W, p.shape[1])
    cb = PA // CONF_D
    vec = pl.BlockSpec((1, CONF_D), lambda j: (0, 0))
    blk = lambda col: pl.BlockSpec((rows, CONF_COLS, CONF_D), lambda j: (0, j, col))
    out = pl.pallas_call(
        functools.partial(_conf_grid_kernel, rows=rows),
        grid=(GRID_W // CONF_COLS,),
        in_specs=[blk(cb), blk(cb + 1), pl.BlockSpec((CONF_K, CONF_D), lambda j: (0, 0)), vec, vec, vec],
        out_specs=blk(0),
        out_shape=jax.ShapeDtypeStruct((rows, GRID_W, CONF_D), BF16),
        scratch_shapes=[pltpu.VMEM((rows + 2 * (CONF_K // 2), CONF_COLS, CONF_D), F32)],
        compiler_params=_params("parallel"),
        name="conformer_grid",
    )(p3, p3, w, b, lg, lb)
    return out.reshape(n, CONF_D)


SSD_STATE_SHAPE = (SSD_HEADS, SSD_STATE, SSD_HEAD_DIM)
SSD_CHUNKS_PER_STEP = 2


def _ssd_kernel(xbc_ref, dt_ref, dtb_ref, alog_ref, init_ref, y_ref, fin_ref, h_ref, *, ns, cps):
    d = pl.program_id(0)
    c = pl.program_id(1)
    q = SSD_CHUNK
    hd = SSD_HEAD_DIM

    @pl.when(c == 0)
    def _():
        h_ref[...] = init_ref[...]

    lane = lax.broadcasted_iota(jnp.int32, (q, LANE), 1)
    head = lane < SSD_HEADS
    li = lax.broadcasted_iota(jnp.int32, (q, q), 0)
    si = lax.broadcasted_iota(jnp.int32, (q, q), 1)
    mask = (li - si) * (1 - 2 * d) >= 0
    tri = mask.astype(F32)
    a_rate = -jnp.exp(alog_ref[...])

    def one_chunk(r0):
        rows = pl.ds(r0, q)
        raw = dt_ref[rows, :] + dtb_ref[...]
        dt_all = jnp.maximum(raw, 0.0) + jnp.log(1.0 + jnp.exp(-jnp.abs(raw)))
        ld_all = dt_all * a_rate
        dt_d = jnp.where(head, jnp.where(d == 0, dt_all, pltpu.roll(dt_all, LANE - SSD_HEADS, 1)), 0.0)
        ld_d = jnp.where(head, jnp.where(d == 0, ld_all, pltpu.roll(ld_all, LANE - SSD_HEADS, 1)), 0.0)
        cum = jnp.dot(tri, ld_d, precision=HIGHEST, preferred_element_type=F32)
        tot = jnp.sum(ld_d, axis=0, keepdims=True)
        cum_t = cum.T
        dt_t = dt_d.T
        w_t = (jnp.exp(tot - cum) * dt_d).T
        a_out = jnp.exp(cum)
        e_tot = jnp.exp(tot)

        for g in range(SSD_GROUPS):
            bg = xbc_ref[rows, SSD_D + g * SSD_STATE:SSD_D + (g + 1) * SSD_STATE].astype(F32)
            cg = xbc_ref[rows, SSD_D + SSD_BC + g * SSD_STATE:SSD_D + SSD_BC + (g + 1) * SSD_STATE].astype(F32)
            bg_t = bg.T
            cb = _dot(cg.astype(BF16), bg_t.astype(BF16))
            for e in range(SSD_HPG):
                hh = g * SSD_HPG + e
                diff = cum[:, hh:hh + 1] - cum_t[hh:hh + 1, :]
                dec = jnp.exp(jnp.where(mask, diff, -1e30))
                m = (cb * dec * dt_t[hh:hh + 1, :]).astype(BF16)
                xe = xbc_ref[rows, hh * hd:(hh + 1) * hd].astype(BF16)
                cs = (cg * a_out[:, hh:hh + 1]).astype(BF16)
                h_in = h_ref[hh]
                y_ref[rows, hh * hd:(hh + 1) * hd] = _dot(m, xe) + _dot(cs, h_in.astype(BF16))
                s_new = _dot((bg_t * w_t[hh:hh + 1, :]).astype(BF16), xe)
                h_ref[hh] = e_tot[:, hh:hh + 1] * h_in + s_new

    for k in range(cps):
        one_chunk(pl.multiple_of(jnp.where(d == 0, k, cps - 1 - k) * q, q))

    @pl.when(c == ns - 1)
    def _():
        fin_ref[...] = h_ref[...]


def _ssd_scan(xbc, p, dt_bias, a_log, init):
    n = xbc.shape[0]
    cps = SSD_CHUNKS_PER_STEP
    q = SSD_CHUNK * cps
    ns = n // q

    def chunk(d, c):
        return jnp.where(d == 0, c, ns - 1 - c)

    st = SSD_STATE_SHAPE
    vec = pl.BlockSpec((1, LANE), lambda d, c: (0, 0))
    return pl.pallas_call(
        functools.partial(_ssd_kernel, ns=ns, cps=cps),
        grid=(2, ns),
        in_specs=[pl.BlockSpec((q, SSD_XBC), lambda d, c: (chunk(d, c), 0)),
                  pl.BlockSpec((q, LANE), lambda d, c: (chunk(d, c), 0)),
                  vec, vec,
                  pl.BlockSpec((None,) + st, lambda d, c: (d, 0, 0, 0))],
        out_specs=[pl.BlockSpec((None, q, SSD_D), lambda d, c: (d, chunk(d, c), 0)),
                   pl.BlockSpec((None,) + st, lambda d, c: (d, 0, 0, 0))],
        out_shape=[jax.ShapeDtypeStruct((2, n, SSD_D), F32),
                   jax.ShapeDtypeStruct((2,) + st, F32)],
        scratch_shapes=[pltpu.VMEM(st, F32)],
        compiler_params=_params("arbitrary", "arbitrary"),
        name="ssd_scan",
    )(xbc, p, dt_bias, a_log, init)


def _filt_kernel(feat_ref, w1_ref, b1_ref, w2_ref, b2_ref, fr_ref, w3h_ref, w3l_ref, dl_ref, k_ref, nrm_ref, *,
                 n, t):
    i = pl.program_id(0)
    hf = t // 2
    feat = feat_ref[...]
    x = jnp.concatenate([feat[0:hf], feat[hf:t]], axis=1)
    hid = jnp.sin(fr_ref[0:1, :] * (jnp.dot(x, w1_ref[...], precision=HIGHEST,
                                            preferred_element_type=F32) + b1_ref[...]))
    hid = jnp.sin(fr_ref[1:2, :] * (jnp.dot(hid, w2_ref[...], precision=HIGHEST,
                                            preferred_element_type=F32) + b2_ref[...]))
    hi = hid.astype(BF16)
    lo = (hid - hi.astype(F32)).astype(BF16)

    @pl.when(i == 0)
    def _():
        nrm_ref[...] = jnp.zeros_like(nrm_ref)

    for half in range(2):
        wh, wl = w3h_ref[half], w3l_ref[half]
        filt = _dot(hi, wh) + _dot(lo, wh) + _dot(hi, wl)
        filt = filt * jnp.exp(-feat[half * hf:(half + 1) * hf, 0:1] * dl_ref[...])
        row = i * t + half * hf + lax.broadcasted_iota(jnp.int32, filt.shape, 0)
        filt = jnp.where(row == n, 0.0, filt)
        k_ref[half * hf:(half + 1) * hf, :] = filt
        nrm_ref[...] += jnp.sum(jnp.abs(filt), axis=0, keepdims=True)


def _hyena_filters(featx, fw, n):
    t = min(n, 512)
    half = n // t
    oc = HY_ORDER * HY_D
    full = lambda shape: pl.BlockSpec(shape, lambda i: tuple(0 for _ in shape))
    w3spec = pl.BlockSpec((None, 2, LANE, oc), lambda i: (i // half, 0, 0, 0))
    return pl.pallas_call(
        functools.partial(_filt_kernel, n=n, t=t),
        grid=(2 * n // t,),
        in_specs=[pl.BlockSpec((t, LANE), lambda i: (i, 0)),
                  full((2 * LANE, LANE)), full((1, LANE)), full((LANE, LANE)), full((1, LANE)),
                  full((2, LANE)), w3spec, w3spec,
                  pl.BlockSpec((None, 1, oc), lambda i: (i // half, 0, 0))],
        out_specs=[pl.BlockSpec((t, oc), lambda i: (i, 0)),
                   pl.BlockSpec((1, oc), lambda i: (0, 0))],
        out_shape=[jax.ShapeDtypeStruct((2 * n, oc), F32),
                   jax.ShapeDtypeStruct((1, oc), F32)],
        compiler_params=_params("arbitrary"),
        name="hyena_filters",
    )(featx, fw["w1"], fw["b1"], fw["w2"], fw["b2"], fw["freq"], fw["w3h"], fw["w3l"], fw["deltas"])


def _filter_weights(w1, b1, w2, b2, freq, w3, deltas_d):
    hh = HY_HID
    z = lambda r, c: jnp.zeros((r, c), F32)
    w1p = jnp.pad(w1, ((0, LANE - HY_EMB), (0, 0)))
    w1b = jnp.concatenate([jnp.concatenate([w1p, z(LANE, hh)], 1),
                           jnp.concatenate([z(LANE, hh), w1p], 1)], 0)
    w2b = jnp.concatenate([jnp.concatenate([w2, z(hh, hh)], 1),
                           jnp.concatenate([z(hh, hh), w2], 1)], 0)
    two = lambda v: jnp.concatenate([v, v], axis=-1)
    w3d = w3.reshape(hh, HY_ORDER, 2, HY_D).transpose(2, 0, 1, 3).reshape(2, hh, HY_ORDER * HY_D)
    zz = jnp.zeros_like(w3d)
    w3x = jnp.stack([jnp.concatenate([w3d, zz], 1), jnp.concatenate([zz, w3d], 1)], axis=1)
    w3h = w3x.astype(BF16)
    w3l = (w3x - w3h.astype(F32)).astype(BF16)
    return dict(w1=w1b, b1=two(b1[None, :]), w2=w2b, b2=two(b2[None, :]), freq=two(freq), w3h=w3h, w3l=w3l,
                deltas=deltas_d)


DFT_LANES = 8192


def _dft_rows_kernel(f_ref, x_ref, o_ref, *, nj):
    x = jnp.concatenate([x_ref[:, jj, :] for jj in range(nj)], axis=1)
    o_ref[...] = _dot(f_ref[...], x.astype(BF16)).astype(o_ref.dtype)


def _dft_rows(fmat, x3, col, width):
    m, k = fmat.shape
    n2 = x3.shape[1]
    nj = min(DFT_LANES // width, n2)
    return pl.pallas_call(
        functools.partial(_dft_rows_kernel, nj=nj),
        grid=(n2 // nj,),
        in_specs=[pl.BlockSpec((m, k), lambda j: (0, 0)),
                  pl.BlockSpec((k, nj, width), lambda j: (0, j, col))],
        out_specs=pl.BlockSpec((m, nj * width), lambda j: (0, j)),
        out_shape=jax.ShapeDtypeStruct((m, n2 * width), BF16),
        compiler_params=_params("parallel"),
        name="dft_rows",
    )(fmat, x3)


def _spec_kernel(ar_ref, ai_ref, gr_ref, gi_ref, kr_ref, ki_ref):
    ar, ai, gr, gi = ar_ref[...], ai_ref[...], gr_ref[...], gi_ref[...]
    kr_ref[...] = (_dot(gr, ar) - _dot(gi, ai)).astype(BF16)
    ki_ref[...] = (_dot(gr, ai) + _dot(gi, ar)).astype(BF16)


def _filter_spectrum(a4, gr, gi):
    _, _, n2, ch = a4.shape
    nh = gr.shape[0]
    ct = 512
    blk = lambda ri: pl.BlockSpec((None, None, n2, ct), lambda f, j: (ri, f, 0, j))
    gspec = pl.BlockSpec((None, n2, n2), lambda f, j: (f, 0, 0))
    ospec = pl.BlockSpec((None, n2, ct), lambda f, j: (f, 0, j))
    return pl.pallas_call(
        _spec_kernel,
        grid=(nh, ch // ct),
        in_specs=[blk(0), blk(1), gspec, gspec],
        out_specs=[ospec, ospec],
        out_shape=[jax.ShapeDtypeStruct((nh, n2, ch), BF16)] * 2,
        compiler_params=_params("parallel", "parallel"),
        name="filter_spectrum",
    )(a4, a4, gr, gi)


def _mid_kernel(ar_ref, ai_ref, gr_ref, gi_ref, grt_ref, git_ref, kr_ref, ki_ref, br_ref, bi_ref, *, nh):
    f = pl.program_id(0)

    @pl.when(f < nh)
    def _():
        ar, ai, gr, gi = ar_ref[...], ai_ref[...], gr_ref[...], gi_ref[...]
        xr = _dot(gr, ar) - _dot(gi, ai)
        xi = _dot(gr, ai) + _dot(gi, ar)
        kr, ki = kr_ref[...].astype(F32), ki_ref[...].astype(F32)
        yr = (xr * kr - xi * ki).astype(BF16)
        yi = (xr * ki + xi * kr).astype(BF16)
        grt, git = grt_ref[...], git_ref[...]
        br_ref[...] = (_dot(grt, yr) + _dot(git, yi)).astype(BF16)
        bi_ref[...] = (_dot(grt, yi) - _dot(git, yr)).astype(BF16)

    @pl.when(f >= nh)
    def _():
        br_ref[...] = jnp.zeros_like(br_ref)
        bi_ref[...] = jnp.zeros_like(bi_ref)


def _hyena_mid(a4, tabs, kf_r, kf_i, order):
    _, nf, n2, ch = a4.shape
    gr, gi, grt, git = tabs
    nh = gr.shape[0]
    fi = lambda f: jnp.minimum(f, nh - 1)
    blk = lambda ri: pl.BlockSpec((None, None, n2, ch), lambda f: (ri, fi(f), 0, 0))
    kspec = pl.BlockSpec((None, n2, ch), lambda f: (fi(f), 0, order))
    gspec = pl.BlockSpec((None, n2, n2), lambda f: (fi(f), 0, 0))
    ospec = pl.BlockSpec((None, n2, ch), lambda f: (f, 0, 0))
    return pl.pallas_call(
        functools.partial(_mid_kernel, nh=nh),
        grid=(nf,),
        in_specs=[blk(0), blk(1), gspec, gspec, gspec, gspec, kspec, kspec],
        out_specs=[ospec, ospec],
        out_shape=[jax.ShapeDtypeStruct((nf, n2, ch), BF16)] * 2,
        compiler_params=_params("parallel"),
        name="hyena_mid",
    )(a4, a4, gr, gi, grt, git, kf_r, kf_i)


def _inv_kernel(f_ref, br_ref, bi_ref, s_ref, bias_ref, z_ref, g_ref, o_ref, *, nf, nj, ch):
    acc = _dot(f_ref[:, 0:nf], br_ref[...]) + _dot(f_ref[:, nf:2 * nf], bi_ref[...])
    for jj in range(nj):
        y = acc[:, jj * ch:(jj + 1) * ch] * s_ref[...]
        o_ref[:, jj, :] = g_ref[:, jj, :] * (y + bias_ref[...] * z_ref[:, jj, :])


def _hyena_inverse(finv, b_r, b_i, scale, bias, z3, zcol, g3, gcol):
    t1, k2 = finv.shape
    nf = k2 // 2
    n2 = z3.shape[1]
    ch = HY_D
    nj = min(DFT_LANES // ch, n2)
    col = pl.BlockSpec((nf, nj * ch), lambda j: (0, j))
    row = pl.BlockSpec((1, ch), lambda j: (0, 0))
    return pl.pallas_call(
        functools.partial(_inv_kernel, nf=nf, nj=nj, ch=ch),
        grid=(n2 // nj,),
        in_specs=[pl.BlockSpec((t1, k2), lambda j: (0, 0)), col, col, row, row,
                  pl.BlockSpec((t1, nj, ch), lambda j: (0, j, zcol)),
                  pl.BlockSpec((t1, nj, ch), lambda j: (0, j, gcol))],
        out_specs=pl.BlockSpec((t1, nj, ch), lambda j: (0, j, 0)),
        out_shape=jax.ShapeDtypeStruct((t1, n2, ch), F32),
        compiler_params=_params("parallel"),
        name="hyena_inverse",
    )(finv, b_r, b_i, scale, bias, z3, g3)


def _hyena_nf(n):
    nh = (2 * n // FFT_N2) // 2 + 1
    return -(-nh // 16) * 16


def _dft_tables(n):
    n2 = FFT_N2
    n1 = 2 * n // n2
    tot = 2 * n
    two_pi = 2.0 * math.pi

    def cs(num, den):
        ang = (two_pi / den) * (num % den).astype(F32)
        return jnp.cos(ang), jnp.sin(ang)

    nh = n1 // 2 + 1
    nf = _hyena_nf(n)
    f1 = jnp.arange(nh, dtype=jnp.int32)
    t1 = jnp.arange(n1, dtype=jnp.int32)
    c1, s1 = cs(f1[:, None] * t1[None, :], n1)
    zrow = jnp.zeros((nf - nh, n1), F32)
    fwd_full = jnp.concatenate([c1, zrow, -s1, zrow], axis=0).astype(BF16)
    fwd_half = fwd_full[:, :n1 // 2]
    wgt = jnp.where((f1 == 0) | (f1 == n1 // 2), 1.0, 2.0)[:, None]
    zcol = jnp.zeros((n1 // 2, nf - nh), F32)
    inv = jnp.concatenate([(wgt * c1[:, :n1 // 2]).T, zcol, -(wgt * s1[:, :n1 // 2]).T, zcol],
                          axis=1).astype(BF16)
    t2 = jnp.arange(n2, dtype=jnp.int32)
    twr, twi = cs(f1[:, None] * t2[None, :], tot)
    fr, fi = cs(t2[:, None] * t2[None, :], n2)
    twi, fi = -twi, -fi
    gr = twr[:, None, :] * fr[None] - twi[:, None, :] * fi[None]
    gi = twr[:, None, :] * fi[None] + twi[:, None, :] * fr[None]
    tabs = (gr.astype(BF16), gi.astype(BF16),
            gr.transpose(0, 2, 1).astype(BF16), gi.transpose(0, 2, 1).astype(BF16))
    return fwd_full, fwd_half, inv, tabs


def _hyena_long(q, k2, nrm, hy_bias, tables):
    n = q.shape[0]
    n2 = FFT_N2
    n1 = 2 * n // n2
    fwd_full, fwd_half, inv, tabs = tables
    nf = _hyena_nf(n)
    oc = HY_ORDER * HY_D
    ak = _dft_rows(fwd_full, k2.reshape(n1, n2, oc), 0, oc).reshape(2, nf, n2, oc)
    kf_r, kf_i = _filter_spectrum(ak, tabs[0], tabs[1])
    q3 = q.reshape(n1 // 2, n2, 3 * HY_D)
    z3, zcol = q3, 0
    for o in range(HY_ORDER):
        a4 = _dft_rows(fwd_half, z3, zcol, HY_D).reshape(2, nf, n2, HY_D)
        b_r, b_i = _hyena_mid(a4, tabs, kf_r, kf_i, o)
        scale = 1.0 / (2.0 * n * nrm[:, o * HY_D:(o + 1) * HY_D])
        z3 = _hyena_inverse(inv, b_r.reshape(nf, n2 * HY_D), b_i.reshape(nf, n2 * HY_D), scale,
                            hy_bias[o][None, :], z3, zcol, q3, o + 1)
        zcol = 0
    return z3.reshape(n, HY_D)


def _hy_ctx_kernel(v_ref, x1_ref, x2_ref, k0_ref, k1_ref, n0_ref, n1_ref, bias_ref, o_ref, kf, zs, *, n):
    zs[...] = v_ref[...]
    for o, (k_ref, nr_ref, x_ref) in enumerate(((k0_ref, n0_ref, x1_ref), (k1_ref, n1_ref, x2_ref))):
        kf[0:n, :] = k_ref[n:2 * n, :]
        kf[n:2 * n, :] = k_ref[0:n, :]

        def body(s, acc):
            return acc + kf[pl.ds(n - s, n), :] * zs[pl.ds(s, 1), :]

        acc = lax.fori_loop(0, n, body, jnp.zeros((n, LANE), F32))
        z = zs[...]
        zs[...] = x_ref[...] * (acc / nr_ref[...] + bias_ref[o:o + 1, :] * z)
    o_ref[...] = zs[...]


def _hyena_ctx(q, k2, nrm, hy_bias):
    n = q.shape[0]
    nb = HY_D // LANE
    col = lambda c0: pl.BlockSpec((n, LANE), lambda j: (0, c0 + j))
    kcol = lambda c0: pl.BlockSpec((2 * n, LANE), lambda j: (0, c0 + j))
    ncol = lambda c0: pl.BlockSpec((1, LANE), lambda j: (0, c0 + j))
    return pl.pallas_call(
        functools.partial(_hy_ctx_kernel, n=n),
        grid=(nb,),
        in_specs=[col(0), col(nb), col(2 * nb), kcol(0), kcol(nb), ncol(0), ncol(nb),
                  pl.BlockSpec((HY_ORDER, LANE), lambda j: (0, j))],
        out_specs=pl.BlockSpec((n, LANE), lambda j: (0, j)),
        out_shape=jax.ShapeDtypeStruct((n, HY_D), F32),
        scratch_shapes=[pltpu.VMEM((2 * n, LANE), F32), pltpu.VMEM((n, LANE), F32)],
        compiler_params=_params("parallel"),
        name="hyena_ctx",
    )(q, q, q, k2, k2, nrm, nrm, hy_bias)


def _merge_kernel(ya_ref, xs_ref, yf_ref, yb_ref, z_ref, yc_ref, yd_ref, g_ref, h_ref,
                  dv_ref, ng_ref, g1_ref, lg_ref, lb_ref,
                  wa_ref, wb_ref, wc_ref, wd_ref, wo_ref, o_ref):
    y = xs_ref[...].astype(F32) * dv_ref[...] + yf_ref[...] + yb_ref[...]
    gz = y * _silu(z_ref[...].astype(F32))
    ssd = gz * lax.rsqrt(jnp.mean(gz * gz, -1, keepdims=True) + LN_EPS) * ng_ref[...]
    d = D_MODEL
    gate = lambda k: jnp.tanh(g_ref[:, k * d:(k + 1) * d].astype(F32)) + 1.0
    m = gate(0) * _dot(ya_ref[...].astype(BF16), wa_ref[...])
    m = m + gate(1) * _dot(ssd.astype(BF16), wb_ref[...])
    m = m + gate(2) * _dot(yc_ref[...].astype(BF16), wc_ref[...])
    m = m + gate(3) * _dot(yd_ref[...].astype(BF16), wd_ref[...])
    mix = _dot(m.astype(BF16), wo_ref[...])
    o_ref[...] = _layer_norm(DN_ALPHA * h_ref[...] + g1_ref[...] * mix, lg_ref[...], lb_ref[...])


def _merge(ya, xbc, ydir, p, yc, yd, h, dvec, ng, gate1, lg, lb, wa, wb, wc, wd, wo):
    n = h.shape[0]
    t = 256
    tok = lambda w, col=0: pl.BlockSpec((t, w), lambda i: (i, col))
    vec = lambda w: pl.BlockSpec((1, w), lambda i: (0, 0))
    mat = lambda r: pl.BlockSpec((r, D_MODEL), lambda i: (0, 0))
    return pl.pallas_call(
        _merge_kernel,
        grid=(n // t,),
        in_specs=[tok(CONF_D), tok(SSD_D),
                  pl.BlockSpec((None, t, SSD_D), lambda i: (0, i, 0)),
                  pl.BlockSpec((None, t, SSD_D), lambda i: (1, i, 0)),
                  tok(SSD_D, PZ // SSD_D), tok(HY_D), tok(SC_D), tok(N_BRANCH * D_MODEL, 0), tok(D_MODEL),
                  vec(SSD_D), vec(SSD_D), vec(D_MODEL), vec(D_MODEL), vec(D_MODEL),
                  mat(CONF_D), mat(SSD_D), mat(HY_D), mat(SC_D), mat(D_MODEL)],
        out_specs=tok(D_MODEL),
        out_shape=jax.ShapeDtypeStruct((n, D_MODEL), F32),
        compiler_params=_params("parallel"),
        name="merge",
    )(ya, xbc, ydir, ydir, p, yc, yd, p, h, dvec, ng, gate1, lg, lb, wa, wb, wc, wd, wo)


MOE_T = 256


def _stream_tiles(hs):
    tiles = [h.shape[0] // MOE_T for h in hs]
    first = [sum(tiles[:s]) for s in range(len(hs))]
    return tiles, first


def _stream_specs(tiles, first, width):
    return [pl.BlockSpec((MOE_T, width), lambda i, nt=nt, f=f: (jnp.clip(i - f, 0, nt - 1), 0))
            for nt, f in zip(tiles, first)]


def _stream_vec_spec(first):
    def index(i):
        s = 0
        for f in first[1:]:
            s = s + (i >= f).astype(jnp.int32)
        return (s, 0, 0)
    return pl.BlockSpec((None, 1, D_MODEL), index)


def _stream_tile(i, refs, first):
    x = refs[0][...]
    for r, f in zip(refs[1:], first[1:]):
        x = jnp.where(i >= f, r[...], x)
    return x


def _router_kernel(*refs, first):
    ns = len(first)
    h_refs = refs[:ns]
    sh_ref, sc_ref, wh_ref, wl_ref, b_ref, sel_ref, cnt_ref, selt_ref = refs[ns:]
    i = pl.program_id(0)

    @pl.when(i == 0)
    def _():
        cnt_ref[...] = jnp.zeros_like(cnt_ref)

    u = _stream_tile(i, h_refs, first) * (1.0 + sc_ref[...]) + sh_ref[...]
    u_hi = u.astype(BF16)
    u_lo = (u - u_hi.astype(F32)).astype(BF16)
    lg = _dot(u_hi, wh_ref[...]) + _dot(u_lo, wh_ref[...]) + _dot(u_hi, wl_ref[...]) + b_ref[...]
    lane = lax.broadcasted_iota(jnp.int32, lg.shape, 1).astype(F32)
    neg = -1e30
    big = 1e9
    gl = jnp.where(lane < MOE_GROUPS, lg, neg)
    gmax = jnp.max(gl, -1, keepdims=True)
    gsel = jnp.min(jnp.where(gl == gmax, lane, big), -1, keepdims=True)
    gprob = 1.0 / jnp.sum(jnp.where(lane < MOE_GROUPS, jnp.exp(lg - gmax), 0.0), -1, keepdims=True)
    lo = MOE_GROUPS + gsel * MOE_EPG
    el = jnp.where(jnp.abs(lane - lo - (MOE_EPG - 1) / 2.0) < MOE_EPG / 2.0, lg, neg)
    m1 = jnp.max(el, -1, keepdims=True)
    i1 = jnp.min(jnp.where(el == m1, lane, big), -1, keepdims=True)
    el2 = jnp.where(lane == i1, neg, el)
    m2 = jnp.max(el2, -1, keepdims=True)
    i2 = jnp.min(jnp.where(el2 == m2, lane, big), -1, keepdims=True)
    t = jnp.exp(m2 - m1)
    w1 = gprob / (1.0 + t)
    w2 = gprob * t / (1.0 + t)
    oh1 = jnp.where(lane == i1, 1.0, 0.0)
    oh2 = jnp.where(lane == i2, 1.0, 0.0)
    oh = oh1 + oh2
    tt = lg.shape[0]
    li = lax.broadcasted_iota(jnp.int32, (tt, tt), 0)
    si = lax.broadcasted_iota(jnp.int32, (tt, tt), 1)
    before = _dot(jnp.where(li > si, 1.0, 0.0).astype(BF16), oh.astype(BF16)) + cnt_ref[...]
    r1 = jnp.sum(oh1 * before, -1, keepdims=True)
    r2 = jnp.sum(oh2 * before, -1, keepdims=True)
    cnt_ref[...] += jnp.sum(oh, axis=0, keepdims=True)
    cols = (i1 - MOE_GROUPS, i2 - MOE_GROUPS, w1, w2, r1, r2)
    sel = jnp.zeros_like(lg)
    for k, v in enumerate(cols):
        sel = jnp.where(lane == k, v, sel)
    sel_ref[...] = sel
    selt_ref[...] = sel.T[0:SUBLANE, :]


def _router(hs, shift, scale, wr, br):
    tiles, first = _stream_tiles(hs)
    t = MOE_T
    n = t * sum(tiles)
    vec = lambda w: pl.BlockSpec((1, w), lambda i: (0, 0))
    wr_hi = wr.astype(BF16)
    wr_lo = (wr - wr_hi.astype(F32)).astype(BF16)
    wspec = pl.BlockSpec((D_MODEL, LANE), lambda i: (0, 0))
    return pl.pallas_call(
        functools.partial(_router_kernel, first=first),
        grid=(sum(tiles),),
        in_specs=_stream_specs(tiles, first, D_MODEL) + [
            _stream_vec_spec(first), _stream_vec_spec(first), wspec, wspec, vec(LANE)],
        out_specs=[pl.BlockSpec((t, LANE), lambda i: (i, 0)), vec(LANE),
                   pl.BlockSpec((SUBLANE, t), lambda i: (0, i))],
        out_shape=[jax.ShapeDtypeStruct((n, LANE), F32), jax.ShapeDtypeStruct((1, LANE), F32),
                   jax.ShapeDtypeStruct((SUBLANE, n), F32)],
        compiler_params=_params("arbitrary"),
        name="router",
    )(*hs, shift, scale, wr_hi, wr_lo, br)


ROW_WORDS = D_MODEL // 2


def _pack_rows(x):
    c = x.shape[1] // 2
    bits = lambda v: lax.bitcast_convert_type(v.astype(BF16).astype(F32), jnp.uint32)
    return bits(x[:, :c]) | (bits(x[:, c:]) >> 16)


def _unpack_rows(w):
    hi = lax.bitcast_convert_type(w & jnp.uint32(0xFFFF0000), F32)
    lo = lax.bitcast_convert_type(w << 16, F32)
    return jnp.concatenate([hi, lo], axis=1)


def _dispatch_kernel(dst_ref, *refs, first, nt):
    ns = len(first)
    h_refs = refs[:ns]
    sh_ref, sc_ref, zero_hbm, xin_hbm, ubuf, sem = refs[ns:]
    del zero_hbm
    t = MOE_T
    i = pl.program_id(0)
    slot = i % 2

    def wait_slot(s):
        for _ in range(MOE_TOP_K):
            pltpu.make_async_copy(ubuf.at[s], xin_hbm.at[pl.ds(0, t), :], sem.at[s]).wait()

    @pl.when(i >= 2)
    def _():
        wait_slot(slot)

    ubuf[slot] = _pack_rows(_stream_tile(i, h_refs, first) * (1.0 + sc_ref[...]) + sh_ref[...])

    def issue(r, carry):
        for k in range(MOE_TOP_K):
            pltpu.make_async_copy(ubuf.at[slot, pl.ds(r, 1), :],
                                  xin_hbm.at[pl.ds(dst_ref[0, 0, k * t + r], 1), :], sem.at[slot]).start()
        return carry

    lax.fori_loop(0, t, issue, 0, unroll=16)

    @pl.when(i == nt - 1)
    def _():
        wait_slot(slot)
        if nt > 1:
            wait_slot(1 - slot)


def _dispatch(pos_t, hs, shift, scale, n_rows):
    tiles, first = _stream_tiles(hs)
    t = MOE_T
    nt = sum(tiles)
    return pl.pallas_call(
        functools.partial(_dispatch_kernel, first=first, nt=nt),
        grid=(nt,),
        in_specs=([pl.BlockSpec((1, 1, MOE_TOP_K * t), lambda i: (i, 0, 0), memory_space=pltpu.SMEM)]
                  + _stream_specs(tiles, first, D_MODEL)
                  + [_stream_vec_spec(first), _stream_vec_spec(first), pl.BlockSpec(memory_space=pl.ANY)]),
        out_specs=pl.BlockSpec(memory_space=pl.ANY),
        out_shape=jax.ShapeDtypeStruct((n_rows, ROW_WORDS), jnp.uint32),
        scratch_shapes=[pltpu.VMEM((2, t, ROW_WORDS), jnp.uint32), pltpu.SemaphoreType.DMA((2,))],
        input_output_aliases={3 + len(hs): 0},
        compiler_params=_params("arbitrary"),
        name="dispatch",
    )(pos_t, *hs, shift, scale, jnp.zeros((n_rows, ROW_WORDS), jnp.uint32))


def _expert_kernel(be_ref, nu_ref, x_ref, wg_ref, wu_ref, wd_ref, o_ref, wgb, wub, wdb):
    b = pl.program_id(0)

    @pl.when((b == 0) | (be_ref[b] != be_ref[jnp.maximum(b - 1, 0)]))
    def _():
        wgb[...] = wg_ref[...].astype(BF16)
        wub[...] = wu_ref[...].astype(BF16)
        wdb[...] = wd_ref[...].astype(BF16)

    @pl.when(b < nu_ref[0])
    def _():
        x = _unpack_rows(x_ref[...]).astype(BF16)
        hid = _silu(_dot(x, wgb[...])) * _dot(x, wub[...])
        o_ref[...] = _pack_rows(_dot(hid.astype(BF16), wdb[...]))

    @pl.when(b >= nu_ref[0])
    def _():
        o_ref[...] = jnp.zeros_like(o_ref)


def _experts(xin, block_e, n_used, wg, wu, wd, layer):
    n_blocks = block_e.shape[0]
    gs = pltpu.PrefetchScalarGridSpec(
        num_scalar_prefetch=2,
        grid=(n_blocks,),
        in_specs=[pl.BlockSpec((MOE_ROWS, ROW_WORDS), lambda b, be, nu: (b, 0)),
                  pl.BlockSpec((None, None, D_MODEL, MOE_FF), lambda b, be, nu: (layer, be[b], 0, 0)),
                  pl.BlockSpec((None, None, D_MODEL, MOE_FF), lambda b, be, nu: (layer, be[b], 0, 0)),
                  pl.BlockSpec((None, None, MOE_FF, D_MODEL), lambda b, be, nu: (layer, be[b], 0, 0))],
        out_specs=pl.BlockSpec((MOE_ROWS, ROW_WORDS), lambda b, be, nu: (b, 0)),
        scratch_shapes=[pltpu.VMEM((D_MODEL, MOE_FF), BF16), pltpu.VMEM((D_MODEL, MOE_FF), BF16),
                        pltpu.VMEM((MOE_FF, D_MODEL), BF16)],
    )
    return pl.pallas_call(
        _expert_kernel,
        grid_spec=gs,
        out_shape=jax.ShapeDtypeStruct((n_blocks * MOE_ROWS, ROW_WORDS), jnp.uint32),
        compiler_params=_params("arbitrary"),
        name="experts",
    )(block_e, n_used, xin, wg, wu, wd)


def _combine_kernel(pos_ref, posn_ref, y_hbm, *refs, first, tiles):
    ns = len(first)
    nt = sum(tiles)
    h_refs = refs[:ns]
    sel_ref, g2_ref, lg_ref, lb_ref = refs[ns:ns + 4]
    o_refs = refs[ns + 4:2 * ns + 4]
    ybuf, sem = refs[2 * ns + 4:]
    t = MOE_T
    i = pl.program_id(0)
    slot = i % 2

    def gather(p_ref, s):
        def issue(r, carry):
            pltpu.make_async_copy(y_hbm.at[pl.ds(p_ref[0, 0, r], 1), :], ybuf.at[s, pl.ds(r, 1), :],
                                  sem.at[s]).start()
            return carry
        lax.fori_loop(0, MOE_TOP_K * t, issue, 0, unroll=32)

    @pl.when(i == 0)
    def _():
        gather(pos_ref, 0)

    @pl.when(i + 1 < nt)
    def _():
        gather(posn_ref, 1 - slot)

    pltpu.make_async_copy(y_hbm.at[pl.ds(0, MOE_TOP_K * t), :], ybuf.at[slot], sem.at[slot]).wait()
    ffn = (sel_ref[:, 2:3] * _unpack_rows(ybuf[slot, 0:t, :])
           + sel_ref[:, 3:4] * _unpack_rows(ybuf[slot, t:2 * t, :]))
    out = _layer_norm(DN_ALPHA * _stream_tile(i, h_refs, first) + g2_ref[...] * ffn, lg_ref[...], lb_ref[...])
    for s in range(ns):
        @pl.when((i >= first[s]) & (i < first[s] + tiles[s]))
        def _(s=s):
            o_refs[s][...] = out


def _combine(y, pos_t, hs, sel, gate2, lg, lb):
    tiles, first = _stream_tiles(hs)
    t = MOE_T
    nt = sum(tiles)
    vec = pl.BlockSpec((1, D_MODEL), lambda i: (0, 0))
    return pl.pallas_call(
        functools.partial(_combine_kernel, first=first, tiles=tiles),
        grid=(nt,),
        in_specs=([pl.BlockSpec((1, 1, MOE_TOP_K * t), lambda i: (i, 0, 0), memory_space=pltpu.SMEM),
                   pl.BlockSpec((1, 1, MOE_TOP_K * t), lambda i: (jnp.minimum(i + 1, nt - 1), 0, 0),
                                memory_space=pltpu.SMEM),
                   pl.BlockSpec(memory_space=pl.ANY)]
                  + _stream_specs(tiles, first, D_MODEL)
                  + [pl.BlockSpec((t, LANE), lambda i: (i, 0)), _stream_vec_spec(first), vec, vec]),
        out_specs=_stream_specs(tiles, first, D_MODEL),
        out_shape=[jax.ShapeDtypeStruct(h.shape, F32) for h in hs],
        scratch_shapes=[pltpu.VMEM((2, MOE_TOP_K * t, ROW_WORDS), jnp.uint32), pltpu.SemaphoreType.DMA((2,))],
        compiler_params=_params("arbitrary"),
        name="combine",
    )(pos_t, pos_t, y, *hs, sel, gate2, lg, lb)


def _moe(hs, shift, scale, gate2, lg, lb, wr, br, wg, wu, wd, layer):
    t = MOE_T
    n = sum(h.shape[0] for h in hs)
    nt = n // t
    sel, cnt, selt = _router(hs, shift, scale, wr, br)
    counts = cnt[0, MOE_GROUPS:MOE_GROUPS + MOE_EXPERTS].astype(jnp.int32)
    padded = (counts + MOE_ROWS - 1) // MOE_ROWS * MOE_ROWS
    pad_end = jnp.cumsum(padded)
    pad_start = pad_end - padded
    n_blocks = (n * MOE_TOP_K + MOE_EXPERTS * (MOE_ROWS - 1) + MOE_ROWS - 1) // MOE_ROWS
    blk_row = jnp.arange(n_blocks, dtype=jnp.int32) * MOE_ROWS
    block_e = jnp.minimum(jnp.sum((blk_row[:, None] >= pad_end[None, :]).astype(jnp.int32), axis=1),
                          MOE_EXPERTS - 1)
    n_used = (pad_end[-1:] // MOE_ROWS).astype(jnp.int32)
    e_kt = selt[0:MOE_TOP_K].astype(jnp.int32)
    ids = jnp.arange(MOE_EXPERTS, dtype=jnp.int32)[None, :, None]
    start_kt = jnp.sum(jnp.where(e_kt[:, None, :] == ids, pad_start[None, :, None], 0), axis=1)
    pos_kt = start_kt + selt[4:4 + MOE_TOP_K].astype(jnp.int32)
    pos_t = pos_kt.reshape(MOE_TOP_K, nt, t).transpose(1, 0, 2).reshape(nt, 1, MOE_TOP_K * t)
    xin = _dispatch(pos_t, hs, shift, scale, n_blocks * MOE_ROWS)
    y = _experts(xin, block_e, n_used, wg, wu, wd, layer)
    return _combine(y, pos_t, hs, sel, gate2, lg, lb)


def _mixer(h, mod, lw, ssd_init, tables, *, latent, need_mix):
    n = h.shape[0]
    p, dt_raw = _inproj(h, mod[0], mod[1], lw["w_in"], lw["layer"])
    xbc = _conv3(p, PX, SSD_XBC, lw["ssd_conv_w"], lw["ssd_conv_b"], silu=True, out_dtype=BF16)
    ydir, finals = _ssd_scan(xbc, dt_raw, lw["ssd_dt_bias"], lw["ssd_a_log"], ssd_init)
    if not need_mix:
        return None, finals
    conf_w = (lw["conf_dw_w"], lw["conf_dw_b"], lw["conf_ln_g"], lw["conf_ln_b"])
    ya = _conformer_grid(p, *conf_w) if latent else _conformer(p, *conf_w, dil=1)
    q = _conv3(p, PC, 3 * HY_D, lw["hy_short_w"], lw["hy_short_b"], silu=False, out_dtype=F32)
    k2, nrm = _hyena_filters(lw["feat_lat" if latent else "feat_ctx"], lw["hy_filter"], n)
    if latent:
        yc = _hyena_long(q, k2, nrm, lw["hy_bias"], tables)
    else:
        yc = _hyena_ctx(q, k2, nrm, lw["hy_bias"])
    yd = _gated_conv(p, lw["sc_conv_w"])
    h = _merge(ya, xbc, ydir, p, yc, yd, h, lw["ssd_dvec"], lw["ssd_norm_g"], mod[2], lw["ln_g0"], lw["ln_b0"],
               lw["w_branch_a"], lw["w_branch_b"], lw["w_branch_c"], lw["w_branch_d"], lw["w_out"])
    return h, finals


def _positional_features(n):
    tau = jnp.arange(2 * n, dtype=jnp.int32)
    lag = jnp.where(tau < n, tau, 2 * n - tau).astype(F32)[:, None]
    t01 = lag * (1.0 / (n - 1))
    omega = (2.0 * math.pi / n) * lag
    bands = jnp.linspace(1e-4, HY_BANDS - 1, HY_BANDS, dtype=F32)
    featx = jnp.concatenate([t01, jnp.cos(bands * omega), -jnp.sin(bands * omega)], axis=-1)
    return _pad_lanes(featx)


def _relayout_w_in(w_in):
    w16 = w_in.astype(BF16)
    seg = lambda a, b: w16[:, :, a:b]
    ob = OFF_B
    parts = [0.5 * seg(OFF_G, OFF_G + N_BRANCH * D_MODEL),
             seg(OFF_A, OFF_B),
             seg(OFF_C, OFF_D),
             seg(OFF_D, OFF_G),
             seg(ob + SSD_D, ob + SSD_D + SSD_XBC),
             seg(ob, ob + SSD_D),
             seg(ob + SSD_D + SSD_XBC, OFF_C),
             jnp.zeros(w_in.shape[:2] + (NP - PDT - 2 * SSD_HEADS,), BF16)]
    return jnp.concatenate(parts, axis=-1)


def _pad_lanes(v):
    return jnp.pad(v, ((0, 0), (0, LANE - v.shape[-1])))


def kernel(x, c, ctx, c_ctx, w_mod, b_mod, ln_g, ln_b, w_in, conf_dw_w, conf_dw_b, conf_ln_g, conf_ln_b,
           ssd_conv_w, ssd_conv_b, ssd_a_log, ssd_dt_bias, ssd_d, ssd_norm_g, hy_short_w, hy_short_b,
           hy_w1, hy_b1, hy_w2, hy_b2, hy_freq, hy_w3, hy_bias, sc_conv_w, w_branch_a, w_branch_b,
           w_branch_c, w_branch_d, w_out, rt_group_w, rt_group_b, rt_expert_w, rt_expert_b,
           ex_w_gate, ex_w_up, ex_w_down):
    assert x.shape[0] == 1 and ctx.shape[0] == 1
    n_lat, n_ctx = x.shape[1], ctx.shape[1]
    depth = w_in.shape[0]

    cv = jnp.concatenate([c, c_ctx[None, :], jnp.zeros((SUBLANE - 2, D_MODEL), F32)], axis=0)
    mods = _mod_vectors(cv, w_mod, b_mod)
    w_in_p = _relayout_w_in(w_in)
    tables = _dft_tables(n_lat)
    feat_lat = _positional_features(n_lat)
    feat_ctx = _positional_features(n_ctx)
    deltas = jnp.abs(jnp.linspace(HY_MIN_DECAY, HY_MAX_DECAY, HY_N_FILT, dtype=F32))
    deltas_d = deltas.reshape(HY_ORDER, 2, HY_D).transpose(1, 0, 2).reshape(2, 1, HY_ORDER * HY_D)
    router_w = jnp.concatenate([rt_group_w, rt_expert_w,
                                jnp.zeros((depth, D_MODEL, LANE - MOE_GROUPS - MOE_EXPERTS), F32)], axis=-1)
    router_b = jnp.concatenate([rt_group_b, rt_expert_b,
                                jnp.zeros((depth, LANE - MOE_GROUPS - MOE_EXPERTS), F32)], axis=-1)
    ssd_zero = jnp.zeros((2,) + SSD_STATE_SHAPE, F32)

    h_lat, h_ctx = x[0], ctx[0]
    for l in range(depth):
        row = lambda v: v[None, :]
        lw = dict(
            w_in=w_in_p, layer=l, conf_dw_w=conf_dw_w[l], conf_dw_b=row(conf_dw_b[l]), conf_ln_g=row(conf_ln_g[l]),
            conf_ln_b=row(conf_ln_b[l]), ssd_conv_w=ssd_conv_w[l], ssd_conv_b=row(ssd_conv_b[l]),
            ssd_a_log=_pad_lanes(ssd_a_log[l].reshape(1, -1)), ssd_dt_bias=_pad_lanes(ssd_dt_bias[l].reshape(1, -1)),
            ssd_dvec=row(jnp.repeat(ssd_d[l], SSD_HEAD_DIM)), ssd_norm_g=row(ssd_norm_g[l]),
            hy_short_w=hy_short_w[l], hy_short_b=row(hy_short_b[l]),
            hy_filter=_filter_weights(hy_w1[l], hy_b1[l], hy_w2[l], hy_b2[l], hy_freq[l], hy_w3[l], deltas_d),
            hy_bias=hy_bias[l], feat_lat=feat_lat, feat_ctx=feat_ctx,
            sc_conv_w=sc_conv_w[l], ln_g0=row(ln_g[l, 0]), ln_b0=row(ln_b[l, 0]),
            w_branch_a=w_branch_a[l].astype(BF16), w_branch_b=w_branch_b[l].astype(BF16),
            w_branch_c=w_branch_c[l].astype(BF16), w_branch_d=w_branch_d[l].astype(BF16),
            w_out=(0.5 * w_out[l]).astype(BF16))
        moe_w = (router_w[l], row(router_b[l]), ex_w_gate, ex_w_up, ex_w_down, l)
        last = l == depth - 1
        d = D_MODEL
        mod_lat = [mods[l, 0:1, k * d:(k + 1) * d] for k in range(6)]
        mod_ctx = [mods[l, 1:2, k * d:(k + 1) * d] for k in range(6)]

        mix_ctx, ctx_states = _mixer(h_ctx, mod_ctx, lw, ssd_zero, None, latent=False, need_mix=not last)
        h_lat, _ = _mixer(h_lat, mod_lat, lw, ctx_states, tables, latent=True, need_mix=True)
        streams = [(h_lat, mod_lat)] if last else [(h_lat, mod_lat), (mix_ctx, mod_ctx)]
        vecs = [jnp.stack([m[k] for _, m in streams]) for k in (3, 4, 5)]
        outs = _moe([h for h, _ in streams], *vecs, row(ln_g[l, 1]), row(ln_b[l, 1]), *moe_w)
        h_lat = outs[0]
        if not last:
            h_ctx = outs[1]
    return h_lat[None]
```

```python
import functools
import math

import jax
import jax.numpy as jnp
from jax import lax
from jax.experimental import pallas as pl
from jax.experimental.pallas import tpu as pltpu

F32 = jnp.float32
BF16 = jnp.bfloat16
HIGHEST = lax.Precision.HIGHEST

D_MODEL = 1024
DEPTH = 4
GRID_W = 64
CONF_D = 512
CONF_K = 31
SSD_D = 768
SSD_HEADS = 12
SSD_HEAD_DIM = 64
SSD_GROUPS = 4
SSD_HPG = SSD_HEADS // SSD_GROUPS
SSD_STATE = 128
SSD_CHUNK = 128
SSD_BC = SSD_GROUPS * SSD_STATE
SSD_XBC = SSD_D + 2 * SSD_BC
SSD_PROJ = SSD_D + SSD_XBC + 2 * SSD_HEADS
HY_D = 512
HY_ORDER = 2
HY_EMB = 33
HY_BANDS = (HY_EMB - 1) // 2
HY_HID = 64
HY_N_FILT = HY_ORDER * 2 * HY_D
HY_MIN_DECAY = math.log(1e-2) / 1.5
HY_MAX_DECAY = math.log(1e-2) / 0.3
SC_D = 512
N_BRANCH = 4
OFF_A = 0
OFF_B = OFF_A + 2 * CONF_D
OFF_C = OFF_B + SSD_PROJ
OFF_D = OFF_C + 3 * HY_D
OFF_G = OFF_D + 3 * SC_D
MOE_GROUPS = 4
MOE_EPG = 8
MOE_EXPERTS = MOE_GROUPS * MOE_EPG
MOE_TOP_K = 2
MOE_FF = 512
DN_ALPHA = (2 * DEPTH) ** 0.25
LN_EPS = 1e-5

PG = 0
PA = PG + N_BRANCH * D_MODEL
PC = PA + 2 * CONF_D
PD = PC + 3 * HY_D
PX = PD + 3 * SC_D
PZ = PX + SSD_XBC
PDT = PZ + SSD_D
INPROJ_TN = 1024
NP = -(-(PDT + 128) // INPROJ_TN) * INPROJ_TN

LANE = 128
SUBLANE = 8
FFT_N2 = 256
MOE_ROWS = 256
VMEM_LIMIT = 48 * 1024 * 1024


def _params(*sem):
    return pltpu.CompilerParams(dimension_semantics=sem, vmem_limit_bytes=VMEM_LIMIT)


def _sigmoid(x):
    return 0.5 * jnp.tanh(0.5 * x) + 0.5


def _silu(x):
    return x * _sigmoid(x)


def _layer_norm(x, g, b):
    mu = jnp.mean(x, -1, keepdims=True)
    xc = x - mu
    var = jnp.mean(xc * xc, -1, keepdims=True)
    return xc * lax.rsqrt(var + LN_EPS) * g + b


def _dot(a, b):
    return jnp.dot(a, b, preferred_element_type=F32)


def _mod_kernel(cv_ref, w_ref, b_ref, o_ref):
    o_ref[...] = jnp.dot(_silu(cv_ref[...]), w_ref[...], precision=HIGHEST,
                         preferred_element_type=F32) + b_ref[...]


def _mod_vectors(cv, w_mod, b_mod):
    tn = 1536
    return pl.pallas_call(
        _mod_kernel,
        grid=(DEPTH, 6 * D_MODEL // tn),
        in_specs=[pl.BlockSpec((SUBLANE, D_MODEL), lambda l, j: (0, 0)),
                  pl.BlockSpec((None, D_MODEL, tn), lambda l, j: (l, 0, j)),
                  pl.BlockSpec((None, 1, tn), lambda l, j: (l, 0, j))],
        out_specs=pl.BlockSpec((None, SUBLANE, tn), lambda l, j: (l, 0, j)),
        out_shape=jax.ShapeDtypeStruct((DEPTH, SUBLANE, 6 * D_MODEL), F32),
        compiler_params=_params("parallel", "parallel"),
        name="mod_vectors",
    )(cv, w_mod, b_mod.reshape(DEPTH, 1, 6 * D_MODEL))


def _inproj_kernel(x_ref, sh_ref, sc_ref, w_ref, o_ref, dt_ref, xb_ref, *, nj):
    j = pl.program_id(1)

    @pl.when(j == 0)
    def _():
        xb_ref[...] = (x_ref[...] * (1.0 + sc_ref[...]) + sh_ref[...]).astype(BF16)

    res = _dot(xb_ref[...], w_ref[...])
    o_ref[...] = res.astype(BF16)

    @pl.when(j == nj - 1)
    def _():
        off = PDT - (nj - 1) * INPROJ_TN
        dt_ref[...] = res[:, off:off + LANE]


def _inproj(h, shift, scale, w, layer):
    n = h.shape[0]
    tm = min(n, 2048)
    tn = INPROJ_TN
    nj = NP // tn
    assert PDT >= (nj - 1) * tn
    return pl.pallas_call(
        functools.partial(_inproj_kernel, nj=nj),
        grid=(n // tm, nj),
        in_specs=[pl.BlockSpec((tm, D_MODEL), lambda i, j: (i, 0)),
                  pl.BlockSpec((1, D_MODEL), lambda i, j: (0, 0)),
                  pl.BlockSpec((1, D_MODEL), lambda i, j: (0, 0)),
                  pl.BlockSpec((None, D_MODEL, tn), lambda i, j: (layer, 0, j))],
        out_specs=[pl.BlockSpec((tm, tn), lambda i, j: (i, j)),
                   pl.BlockSpec((tm, LANE), lambda i, j: (i, 0))],
        out_shape=[jax.ShapeDtypeStruct((n, NP), BF16), jax.ShapeDtypeStruct((n, LANE), F32)],
        scratch_shapes=[pltpu.VMEM((tm, D_MODEL), BF16)],
        compiler_params=_params("parallel", "arbitrary"),
        name="inproj",
    )(h, shift, scale, w)


def _shifted(x, prev_row, next_row):
    t = x.shape[0]
    row = lax.broadcasted_iota(jnp.int32, x.shape, 0)
    xm = jnp.where(row == 0, prev_row, pltpu.roll(x, 1, 0))
    xp = jnp.where(row == t - 1, next_row, pltpu.roll(x, t - 1, 0))
    return xm, xp


HALO_ROWS = 16


def _conv3_kernel(cur_ref, prev_ref, next_ref, w_ref, b_ref, o_ref, *, silu, nt):
    i = pl.program_id(0)
    x = cur_ref[...].astype(F32)
    pv = jnp.where(i > 0, prev_ref[HALO_ROWS - 1:HALO_ROWS, :].astype(F32), 0.0)
    nx = jnp.where(i < nt - 1, next_ref[0:1, :].astype(F32), 0.0)
    xm, xp = _shifted(x, pv, nx)
    y = w_ref[0:1, :] * xm + w_ref[1:2, :] * x + w_ref[2:3, :] * xp + b_ref[...]
    o_ref[...] = (_silu(y) if silu else y).astype(o_ref.dtype)


def _halo_specs(t, ct, n, col0):
    rb = t // HALO_ROWS
    last = n // HALO_ROWS - 1
    return [pl.BlockSpec((t, ct), lambda i, j: (i, col0 + j)),
            pl.BlockSpec((HALO_ROWS, ct), lambda i, j: (jnp.maximum(i * rb - 1, 0), col0 + j)),
            pl.BlockSpec((HALO_ROWS, ct), lambda i, j: (jnp.minimum((i + 1) * rb, last), col0 + j))]


def _conv3(p, col, width, w, b, *, silu, out_dtype):
    n = p.shape[0]
    t = min(n, 1024)
    ct = 256
    nt = n // t
    return pl.pallas_call(
        functools.partial(_conv3_kernel, silu=silu, nt=nt),
        grid=(nt, width // ct),
        in_specs=_halo_specs(t, ct, n, col // ct) + [
            pl.BlockSpec((3, ct), lambda i, j: (0, j)),
            pl.BlockSpec((1, ct), lambda i, j: (0, j))],
        out_specs=pl.BlockSpec((t, ct), lambda i, j: (i, j)),
        out_shape=jax.ShapeDtypeStruct((n, width), out_dtype),
        compiler_params=_params("parallel", "parallel"),
        name="conv3",
    )(p, p, p, w, b)


def _gconv_kernel(bg_ref, cc_ref, cp_ref, cn_ref, xc_ref, xp_ref, xn_ref, w_ref, o_ref, *, nt):
    i = pl.program_id(0)
    f = lambda v: v.astype(F32)
    last = slice(HALO_ROWS - 1, HALO_ROWS)
    x = f(cc_ref[...]) * f(xc_ref[...])
    pv = jnp.where(i > 0, f(cp_ref[last, :]) * f(xp_ref[last, :]), 0.0)
    nx = jnp.where(i < nt - 1, f(cn_ref[0:1, :]) * f(xn_ref[0:1, :]), 0.0)
    xm, xp = _shifted(x, pv, nx)
    o_ref[...] = f(bg_ref[...]) * (w_ref[0:1, :] * xm + w_ref[1:2, :] * x + w_ref[2:3, :] * xp)


def _gated_conv(p, w):
    n = p.shape[0]
    t = min(n, 1024)
    ct = 256
    nt = n // t
    nb = SC_D // ct
    return pl.pallas_call(
        functools.partial(_gconv_kernel, nt=nt),
        grid=(nt, nb),
        in_specs=([pl.BlockSpec((t, ct), lambda i, j: (i, PD // ct + j))]
                  + _halo_specs(t, ct, n, PD // ct + nb)
                  + _halo_specs(t, ct, n, PD // ct + 2 * nb)
                  + [pl.BlockSpec((3, ct), lambda i, j: (0, j))]),
        out_specs=pl.BlockSpec((t, ct), lambda i, j: (i, j)),
        out_shape=jax.ShapeDtypeStruct((n, SC_D), F32),
        compiler_params=_params("parallel", "parallel"),
        name="gated_conv",
    )(p, p, p, p, p, p, p, w)


CONF_RB = 64


def _conf_kernel(vc, gc, vp, gp, vn, gn, w_ref, b_ref, lg_ref, lb_ref, o_ref, buf, *, t, halo, dil, nt):
    i = pl.program_id(0)
    glu = lambda v, g: v.astype(F32) * _sigmoid(g.astype(F32))
    buf[halo:halo + t, :] = glu(vc[...], gc[...])
    buf[0:halo, :] = jnp.where(i > 0, glu(vp[t - halo:t, :], gp[t - halo:t, :]), 0.0)
    buf[halo + t:halo + t + halo, :] = jnp.where(i < nt - 1, glu(vn[0:halo, :], gn[0:halo, :]), 0.0)

    def block(r0):
        acc = jnp.zeros((CONF_RB, CONF_D), F32)
        for j in range(CONF_K):
            off = halo + (j - CONF_K // 2) * dil
            acc = acc + w_ref[j:j + 1, :] * buf[pl.ds(r0 + off, CONF_RB), :]
        v = _layer_norm(acc + b_ref[...], lg_ref[...], lb_ref[...])
        o_ref[pl.ds(r0, CONF_RB), :] = _silu(v)

    if dil % CONF_RB == 0:
        def body(rb, carry):
            block(pl.multiple_of(rb * CONF_RB, CONF_RB))
            return carry
        lax.fori_loop(0, t // CONF_RB, body, 0)
    else:
        for rb in range(t // CONF_RB):
            block(rb * CONF_RB)


def _conformer(p, w, b, lg, lb, *, dil):
    n = p.shape[0]
    t = min(n, 1024)
    nt = n // t
    halo = -(-(CONF_K // 2) * dil // SUBLANE) * SUBLANE
    assert halo <= t
    cb = PA // CONF_D

    def spec(col, shift):
        return pl.BlockSpec((t, CONF_D), lambda i: (jnp.clip(i + shift, 0, nt - 1), col))

    vec = pl.BlockSpec((1, CONF_D), lambda i: (0, 0))
    return pl.pallas_call(
        functools.partial(_conf_kernel, t=t, halo=halo, dil=dil, nt=nt),
        grid=(nt,),
        in_specs=[spec(cb, 0), spec(cb + 1, 0), spec(cb, -1), spec(cb + 1, -1), spec(cb, 1), spec(cb + 1, 1),
                  pl.BlockSpec((CONF_K, CONF_D), lambda i: (0, 0)), vec, vec, vec],
        out_specs=pl.BlockSpec((t, CONF_D), lambda i: (i, 0)),
        out_shape=jax.ShapeDtypeStruct((n, CONF_D), F32),
        scratch_shapes=[pltpu.VMEM((t + 2 * halo, CONF_D), F32)],
        compiler_params=_params("parallel"),
        name="conformer",
    )(p, p, p, p, p, p, w, b, lg, lb)


CONF_COLS = 16
CONF_ROWS_PER_ITER = 4


def _conf_grid_kernel(v_ref, g_ref, w_ref, b_ref, lg_ref, lb_ref, o_ref, buf, *, rows):
    half = CONF_K // 2
    zeros = jnp.zeros((half,) + buf.shape[1:], F32)
    buf[0:half] = zeros
    buf[half + rows:half + rows + half] = zeros
    buf[half:half + rows] = v_ref[...].astype(F32) * _sigmoid(g_ref[...].astype(F32))

    rb = CONF_ROWS_PER_ITER

    def body(it, carry):
        r0 = it * rb
        accs = [None] * rb
        for j in range(CONF_K):
            wj = w_ref[j:j + 1, :]
            for s in range(rb):
                term = wj * buf[r0 + s + j]
                accs[s] = term if j == 0 else accs[s] + term
        for s in range(rb):
            v = _layer_norm(accs[s] + b_ref[...], lg_ref[...], lb_ref[...])
            o_ref[r0 + s] = _silu(v).astype(o_ref.dtype)
        return carry

    lax.fori_loop(0, rows // rb, body, 0)


def _conformer_grid(p, w, b, lg, lb):
    n = p.shape[0]
    rows = n // GRID_W
    p3 = p.reshape(rows, GRID_W, p.shape[1])
    cb = PA // CONF_D
    vec = pl.BlockSpec((1, CONF_D), lambda j: (0, 0))
    blk = lambda col: pl.BlockSpec((rows, CONF_COLS, CONF_D), lambda j: (0, j, col))
    out = pl.pallas_call(
        functools.partial(_conf_grid_kernel, rows=rows),
        grid=(GRID_W // CONF_COLS,),
        in_specs=[blk(cb), blk(cb + 1), pl.BlockSpec((CONF_K, CONF_D), lambda j: (0, 0)), vec, vec, vec],
        out_specs=blk(0),
        out_shape=jax.ShapeDtypeStruct((rows, GRID_W, CONF_D), BF16),
        scratch_shapes=[pltpu.VMEM((rows + 2 * (CONF_K // 2), CONF_COLS, CONF_D), F32)],
        compiler_params=_params("parallel"),
        name="conformer_grid",
    )(p3, p3, w, b, lg, lb)
    return out.reshape(n, CONF_D)


SSD_STATE_SHAPE = (SSD_HEADS // 2, SSD_STATE, 2 * SSD_HEAD_DIM)
SSD_CHUNKS_PER_STEP = 2


def _ssd_kernel(xbc_ref, dt_ref, dtb_ref, alog_ref, init_ref, y_ref, fin_ref, h_ref, *, ns, cps):
    d = pl.program_id(0)
    c = pl.program_id(1)
    q = SSD_CHUNK
    hd = SSD_HEAD_DIM

    @pl.when(c == 0)
    def _():
        h_ref[...] = init_ref[...]

    lane = lax.broadcasted_iota(jnp.int32, (q, LANE), 1)
    head = lane < SSD_HEADS
    first_half = lane < hd
    li = lax.broadcasted_iota(jnp.int32, (q, q), 0)
    si = lax.broadcasted_iota(jnp.int32, (q, q), 1)
    mask = (li - si) * (1 - 2 * d) >= 0
    tri = mask.astype(F32)
    a_rate = -jnp.exp(alog_ref[...])

    def one_chunk(r0):
        rows = pl.ds(r0, q)
        raw = dt_ref[rows, :] + dtb_ref[...]
        dt_all = jnp.maximum(raw, 0.0) + jnp.log(1.0 + jnp.exp(-jnp.abs(raw)))
        ld_all = dt_all * a_rate
        dt_d = jnp.where(head, jnp.where(d == 0, dt_all, pltpu.roll(dt_all, LANE - SSD_HEADS, 1)), 0.0)
        ld_d = jnp.where(head, jnp.where(d == 0, ld_all, pltpu.roll(ld_all, LANE - SSD_HEADS, 1)), 0.0)
        cum = jnp.dot(tri, ld_d, precision=HIGHEST, preferred_element_type=F32)
        tot = jnp.sum(ld_d, axis=0, keepdims=True)
        cum_t = cum.T
        dt_t = dt_d.T
        w_t = (jnp.exp(tot - cum) * dt_d).T
        a_out = jnp.exp(cum)
        e_tot = jnp.exp(tot)

        groups = {}

        def group(g):
            if g not in groups:
                bg = xbc_ref[rows, SSD_D + g * SSD_STATE:SSD_D + (g + 1) * SSD_STATE].astype(F32)
                cg = xbc_ref[rows, SSD_D + SSD_BC + g * SSD_STATE:SSD_D + SSD_BC + (g + 1) * SSD_STATE].astype(F32)
                bg_t = bg.T
                groups[g] = (bg_t, cg, _dot(cg.astype(BF16), bg_t.astype(BF16)))
            return groups[g]

        def head_terms(hh):
            bg_t, cg, cb = group(hh // SSD_HPG)
            diff = cum[:, hh:hh + 1] - cum_t[hh:hh + 1, :]
            dec = jnp.exp(jnp.where(mask, diff, -1e30))
            m = (cb * dec * dt_t[hh:hh + 1, :]).astype(BF16)
            cs = (cg * a_out[:, hh:hh + 1]).astype(BF16)
            bw = (bg_t * w_t[hh:hh + 1, :]).astype(BF16)
            return m, cs, bw

        zero = jnp.zeros((q, LANE), BF16)
        for j in range(SSD_HEADS // 2):
            ma, ca, wa = head_terms(2 * j)
            mb, cb_, wb = head_terms(2 * j + 1)
            xp = xbc_ref[rows, j * LANE:(j + 1) * LANE].astype(BF16)
            hp = h_ref[j]
            hp_b = hp.astype(BF16)
            x_bd = jnp.concatenate([jnp.where(first_half, xp, zero), jnp.where(first_half, zero, xp)], axis=0)
            h_bd = jnp.concatenate([jnp.where(first_half, hp_b, zero), jnp.where(first_half, zero, hp_b)], axis=0)
            y_ref[rows, j * LANE:(j + 1) * LANE] = _dot(jnp.concatenate([ma, mb, ca, cb_], axis=1),
                                                        jnp.concatenate([x_bd, h_bd], axis=0))
            s_new = _dot(jnp.concatenate([wa, wb], axis=1), x_bd)
            e_pair = jnp.where(first_half[0:1, :], e_tot[:, 2 * j:2 * j + 1], e_tot[:, 2 * j + 1:2 * j + 2])
            h_ref[j] = e_pair * hp + s_new

    for k in range(cps):
        one_chunk(pl.multiple_of(jnp.where(d == 0, k, cps - 1 - k) * q, q))

    @pl.when(c == ns - 1)
    def _():
        fin_ref[...] = h_ref[...]


def _ssd_scan(xbc, p, dt_bias, a_log, init):
    n = xbc.shape[0]
    cps = SSD_CHUNKS_PER_STEP
    q = SSD_CHUNK * cps
    ns = n // q

    def chunk(d, c):
        return jnp.where(d == 0, c, ns - 1 - c)

    st = SSD_STATE_SHAPE
    vec = pl.BlockSpec((1, LANE), lambda d, c: (0, 0))
    return pl.pallas_call(
        functools.partial(_ssd_kernel, ns=ns, cps=cps),
        grid=(2, ns),
        in_specs=[pl.BlockSpec((q, SSD_XBC), lambda d, c: (chunk(d, c), 0)),
                  pl.BlockSpec((q, LANE), lambda d, c: (chunk(d, c), 0)),
                  vec, vec,
                  pl.BlockSpec((None,) + st, lambda d, c: (d, 0, 0, 0))],
        out_specs=[pl.BlockSpec((None, q, SSD_D), lambda d, c: (d, chunk(d, c), 0)),
                   pl.BlockSpec((None,) + st, lambda d, c: (d, 0, 0, 0))],
        out_shape=[jax.ShapeDtypeStruct((2, n, SSD_D), F32),
                   jax.ShapeDtypeStruct((2,) + st, F32)],
        scratch_shapes=[pltpu.VMEM(st, F32)],
        compiler_params=_params("arbitrary", "arbitrary"),
        name="ssd_scan",
    )(xbc, p, dt_bias, a_log, init)


def _filt_kernel(feat_ref, w1_ref, b1_ref, w2_ref, b2_ref, fr_ref, w3h_ref, w3l_ref, dl_ref, k_ref, nrm_ref, *,
                 n, t):
    i = pl.program_id(0)
    hf = t // 2
    feat = feat_ref[...]
    x = jnp.concatenate([feat[0:hf], feat[hf:t]], axis=1)
    hid = jnp.sin(fr_ref[0:1, :] * (jnp.dot(x, w1_ref[...], precision=HIGHEST,
                                            preferred_element_type=F32) + b1_ref[...]))
    hid = jnp.sin(fr_ref[1:2, :] * (jnp.dot(hid, w2_ref[...], precision=HIGHEST,
                                            preferred_element_type=F32) + b2_ref[...]))
    hi = hid.astype(BF16)
    lo = (hid - hi.astype(F32)).astype(BF16)

    @pl.when(i == 0)
    def _():
        nrm_ref[...] = jnp.zeros_like(nrm_ref)

    for half in range(2):
        wh, wl = w3h_ref[half], w3l_ref[half]
        filt = _dot(hi, wh) + _dot(lo, wh) + _dot(hi, wl)
        filt = filt * jnp.exp(-feat[half * hf:(half + 1) * hf, 0:1] * dl_ref[...])
        row = i * t + half * hf + lax.broadcasted_iota(jnp.int32, filt.shape, 0)
        filt = jnp.where(row == n, 0.0, filt)
        k_ref[half * hf:(half + 1) * hf, :] = filt
        nrm_ref[...] += jnp.sum(jnp.abs(filt), axis=0, keepdims=True)


def _hyena_filters(featx, fw, n):
    t = min(n, 512)
    half = n // t
    oc = HY_ORDER * HY_D
    full = lambda shape: pl.BlockSpec(shape, lambda i: tuple(0 for _ in shape))
    w3spec = pl.BlockSpec((None, 2, LANE, oc), lambda i: (i // half, 0, 0, 0))
    return pl.pallas_call(
        functools.partial(_filt_kernel, n=n, t=t),
        grid=(2 * n // t,),
        in_specs=[pl.BlockSpec((t, LANE), lambda i: (i, 0)),
                  full((2 * LANE, LANE)), full((1, LANE)), full((LANE, LANE)), full((1, LANE)),
                  full((2, LANE)), w3spec, w3spec,
                  pl.BlockSpec((None, 1, oc), lambda i: (i // half, 0, 0))],
        out_specs=[pl.BlockSpec((t, oc), lambda i: (i, 0)),
                   pl.BlockSpec((1, oc), lambda i: (0, 0))],
        out_shape=[jax.ShapeDtypeStruct((2 * n, oc), F32),
                   jax.ShapeDtypeStruct((1, oc), F32)],
        compiler_params=_params("arbitrary"),
        name="hyena_filters",
    )(featx, fw["w1"], fw["b1"], fw["w2"], fw["b2"], fw["freq"], fw["w3h"], fw["w3l"], fw["deltas"])


def _filter_weights(w1, b1, w2, b2, freq, w3, deltas_d):
    hh = HY_HID
    z = lambda r, c: jnp.zeros((r, c), F32)
    w1p = jnp.pad(w1, ((0, LANE - HY_EMB), (0, 0)))
    w1b = jnp.concatenate([jnp.concatenate([w1p, z(LANE, hh)], 1),
                           jnp.concatenate([z(LANE, hh), w1p], 1)], 0)
    w2b = jnp.concatenate([jnp.concatenate([w2, z(hh, hh)], 1),
                           jnp.concatenate([z(hh, hh), w2], 1)], 0)
    two = lambda v: jnp.concatenate([v, v], axis=-1)
    w3d = w3.reshape(hh, HY_ORDER, 2, HY_D).transpose(2, 0, 1, 3).reshape(2, hh, HY_ORDER * HY_D)
    zz = jnp.zeros_like(w3d)
    w3x = jnp.stack([jnp.concatenate([w3d, zz], 1), jnp.concatenate([zz, w3d], 1)], axis=1)
    w3h = w3x.astype(BF16)
    w3l = (w3x - w3h.astype(F32)).astype(BF16)
    return dict(w1=w1b, b1=two(b1[None, :]), w2=w2b, b2=two(b2[None, :]), freq=two(freq), w3h=w3h, w3l=w3l,
                deltas=deltas_d)


DFT_LANES = 8192


def _dft_rows_kernel(f_ref, x_ref, o_ref, *, nj):
    x = jnp.concatenate([x_ref[:, jj, :] for jj in range(nj)], axis=1)
    o_ref[...] = _dot(f_ref[...], x.astype(BF16)).astype(o_ref.dtype)


def _dft_rows(fmat, x3, col, width):
    m, k = fmat.shape
    n2 = x3.shape[1]
    nj = min(DFT_LANES // width, n2)
    return pl.pallas_call(
        functools.partial(_dft_rows_kernel, nj=nj),
        grid=(n2 // nj,),
        in_specs=[pl.BlockSpec((m, k), lambda j: (0, 0)),
                  pl.BlockSpec((k, nj, width), lambda j: (0, j, col))],
        out_specs=pl.BlockSpec((m, nj * width), lambda j: (0, j)),
        out_shape=jax.ShapeDtypeStruct((m, n2 * width), BF16),
        compiler_params=_params("parallel"),
        name="dft_rows",
    )(fmat, x3)


def _spec_kernel(ar_ref, ai_ref, gr_ref, gi_ref, kr_ref, ki_ref):
    ar, ai, gr, gi = ar_ref[...], ai_ref[...], gr_ref[...], gi_ref[...]
    kr_ref[...] = (_dot(gr, ar) - _dot(gi, ai)).astype(BF16)
    ki_ref[...] = (_dot(gr, ai) + _dot(gi, ar)).astype(BF16)


def _filter_spectrum(a4, gr, gi):
    _, _, n2, ch = a4.shape
    nh = gr.shape[0]
    ct = 512
    blk = lambda ri: pl.BlockSpec((None, None, n2, ct), lambda f, j: (ri, f, 0, j))
    gspec = pl.BlockSpec((None, n2, n2), lambda f, j: (f, 0, 0))
    ospec = pl.BlockSpec((None, n2, ct), lambda f, j: (f, 0, j))
    return pl.pallas_call(
        _spec_kernel,
        grid=(nh, ch // ct),
        in_specs=[blk(0), blk(1), gspec, gspec],
        out_specs=[ospec, ospec],
        out_shape=[jax.ShapeDtypeStruct((nh, n2, ch), BF16)] * 2,
        compiler_params=_params("parallel", "parallel"),
        name="filter_spectrum",
    )(a4, a4, gr, gi)


def _mid_kernel(ar_ref, ai_ref, gr_ref, gi_ref, grt_ref, git_ref, kr_ref, ki_ref, br_ref, bi_ref, *, nh):
    f = pl.program_id(0)

    @pl.when(f < nh)
    def _():
        ar, ai, gr, gi = ar_ref[...], ai_ref[...], gr_ref[...], gi_ref[...]
        xr = _dot(gr, ar) - _dot(gi, ai)
        xi = _dot(gr, ai) + _dot(gi, ar)
        kr, ki = kr_ref[...].astype(F32), ki_ref[...].astype(F32)
        yr = (xr * kr - xi * ki).astype(BF16)
        yi = (xr * ki + xi * kr).astype(BF16)
        grt, git = grt_ref[...], git_ref[...]
        br_ref[...] = (_dot(grt, yr) + _dot(git, yi)).astype(BF16)
        bi_ref[...] = (_dot(grt, yi) - _dot(git, yr)).astype(BF16)

    @pl.when(f >= nh)
    def _():
        br_ref[...] = jnp.zeros_like(br_ref)
        bi_ref[...] = jnp.zeros_like(bi_ref)


def _hyena_mid(a4, tabs, kf_r, kf_i, order):
    _, nf, n2, ch = a4.shape
    gr, gi, grt, git = tabs
    nh = gr.shape[0]
    fi = lambda f: jnp.minimum(f, nh - 1)
    blk = lambda ri: pl.BlockSpec((None, None, n2, ch), lambda f: (ri, fi(f), 0, 0))
    kspec = pl.BlockSpec((None, n2, ch), lambda f: (fi(f), 0, order))
    gspec = pl.BlockSpec((None, n2, n2), lambda f: (fi(f), 0, 0))
    ospec = pl.BlockSpec((None, n2, ch), lambda f: (f, 0, 0))
    return pl.pallas_call(
        functools.partial(_mid_kernel, nh=nh),
        grid=(nf,),
        in_specs=[blk(0), blk(1), gspec, gspec, gspec, gspec, kspec, kspec],
        out_specs=[ospec, ospec],
        out_shape=[jax.ShapeDtypeStruct((nf, n2, ch), BF16)] * 2,
        compiler_params=_params("parallel"),
        name="hyena_mid",
    )(a4, a4, gr, gi, grt, git, kf_r, kf_i)


def _inv_kernel(f_ref, br_ref, bi_ref, s_ref, bias_ref, z_ref, g_ref, o_ref, *, nf, nj, ch):
    acc = _dot(f_ref[:, 0:nf], br_ref[...]) + _dot(f_ref[:, nf:2 * nf], bi_ref[...])
    for jj in range(nj):
        y = acc[:, jj * ch:(jj + 1) * ch] * s_ref[...]
        o_ref[:, jj, :] = g_ref[:, jj, :] * (y + bias_ref[...] * z_ref[:, jj, :])


def _hyena_inverse(finv, b_r, b_i, scale, bias, z3, zcol, g3, gcol):
    t1, k2 = finv.shape
    nf = k2 // 2
    n2 = z3.shape[1]
    ch = HY_D
    nj = min(DFT_LANES // ch, n2)
    col = pl.BlockSpec((nf, nj * ch), lambda j: (0, j))
    row = pl.BlockSpec((1, ch), lambda j: (0, 0))
    return pl.pallas_call(
        functools.partial(_inv_kernel, nf=nf, nj=nj, ch=ch),
        grid=(n2 // nj,),
        in_specs=[pl.BlockSpec((t1, k2), lambda j: (0, 0)), col, col, row, row,
                  pl.BlockSpec((t1, nj, ch), lambda j: (0, j, zcol)),
                  pl.BlockSpec((t1, nj, ch), lambda j: (0, j, gcol))],
        out_specs=pl.BlockSpec((t1, nj, ch), lambda j: (0, j, 0)),
        out_shape=jax.ShapeDtypeStruct((t1, n2, ch), F32),
        compiler_params=_params("parallel"),
        name="hyena_inverse",
    )(finv, b_r, b_i, scale, bias, z3, g3)


def _hyena_nf(n):
    nh = (2 * n // FFT_N2) // 2 + 1
    return -(-nh // 16) * 16


def _dft_tables(n):
    n2 = FFT_N2
    n1 = 2 * n // n2
    tot = 2 * n
    two_pi = 2.0 * math.pi

    def cs(num, den):
        ang = (two_pi / den) * (num % den).astype(F32)
        return jnp.cos(ang), jnp.sin(ang)

    nh = n1 // 2 + 1
    nf = _hyena_nf(n)
    f1 = jnp.arange(nh, dtype=jnp.int32)
    t1 = jnp.arange(n1, dtype=jnp.int32)
    c1, s1 = cs(f1[:, None] * t1[None, :], n1)
    zrow = jnp.zeros((nf - nh, n1), F32)
    fwd_full = jnp.concatenate([c1, zrow, -s1, zrow], axis=0).astype(BF16)
    fwd_half = fwd_full[:, :n1 // 2]
    wgt = jnp.where((f1 == 0) | (f1 == n1 // 2), 1.0, 2.0)[:, None]
    zcol = jnp.zeros((n1 // 2, nf - nh), F32)
    inv = jnp.concatenate([(wgt * c1[:, :n1 // 2]).T, zcol, -(wgt * s1[:, :n1 // 2]).T, zcol],
                          axis=1).astype(BF16)
    t2 = jnp.arange(n2, dtype=jnp.int32)
    twr, twi = cs(f1[:, None] * t2[None, :], tot)
    fr, fi = cs(t2[:, None] * t2[None, :], n2)
    twi, fi = -twi, -fi
    gr = twr[:, None, :] * fr[None] - twi[:, None, :] * fi[None]
    gi = twr[:, None, :] * fi[None] + twi[:, None, :] * fr[None]
    tabs = (gr.astype(BF16), gi.astype(BF16),
            gr.transpose(0, 2, 1).astype(BF16), gi.transpose(0, 2, 1).astype(BF16))
    return fwd_full, fwd_half, inv, tabs


def _hyena_long(q, k2, nrm, hy_bias, tables):
    n = q.shape[0]
    n2 = FFT_N2
    n1 = 2 * n // n2
    fwd_full, fwd_half, inv, tabs = tables
    nf = _hyena_nf(n)
    oc = HY_ORDER * HY_D
    ak = _dft_rows(fwd_full, k2.reshape(n1, n2, oc), 0, oc).reshape(2, nf, n2, oc)
    kf_r, kf_i = _filter_spectrum(ak, tabs[0], tabs[1])
    q3 = q.reshape(n1 // 2, n2, 3 * HY_D)
    z3, zcol = q3, 0
    for o in range(HY_ORDER):
        a4 = _dft_rows(fwd_half, z3, zcol, HY_D).reshape(2, nf, n2, HY_D)
        b_r, b_i = _hyena_mid(a4, tabs, kf_r, kf_i, o)
        scale = 1.0 / (2.0 * n * nrm[:, o * HY_D:(o + 1) * HY_D])
        z3 = _hyena_inverse(inv, b_r.reshape(nf, n2 * HY_D), b_i.reshape(nf, n2 * HY_D), scale,
                            hy_bias[o][None, :], z3, zcol, q3, o + 1)
        zcol = 0
    return z3.reshape(n, HY_D)


def _hy_ctx_kernel(v_ref, x1_ref, x2_ref, k0_ref, k1_ref, n0_ref, n1_ref, bias_ref, o_ref, kf, zs, *, n):
    zs[...] = v_ref[...]
    for o, (k_ref, nr_ref, x_ref) in enumerate(((k0_ref, n0_ref, x1_ref), (k1_ref, n1_ref, x2_ref))):
        kf[0:n, :] = k_ref[n:2 * n, :]
        kf[n:2 * n, :] = k_ref[0:n, :]

        def body(s, acc):
            return acc + kf[pl.ds(n - s, n), :] * zs[pl.ds(s, 1), :]

        acc = lax.fori_loop(0, n, body, jnp.zeros((n, LANE), F32))
        z = zs[...]
        zs[...] = x_ref[...] * (acc / nr_ref[...] + bias_ref[o:o + 1, :] * z)
    o_ref[...] = zs[...]


def _hyena_ctx(q, k2, nrm, hy_bias):
    n = q.shape[0]
    nb = HY_D // LANE
    col = lambda c0: pl.BlockSpec((n, LANE), lambda j: (0, c0 + j))
    kcol = lambda c0: pl.BlockSpec((2 * n, LANE), lambda j: (0, c0 + j))
    ncol = lambda c0: pl.BlockSpec((1, LANE), lambda j: (0, c0 + j))
    return pl.pallas_call(
        functools.partial(_hy_ctx_kernel, n=n),
        grid=(nb,),
        in_specs=[col(0), col(nb), col(2 * nb), kcol(0), kcol(nb), ncol(0), ncol(nb),
                  pl.BlockSpec((HY_ORDER, LANE), lambda j: (0, j))],
        out_specs=pl.BlockSpec((n, LANE), lambda j: (0, j)),
        out_shape=jax.ShapeDtypeStruct((n, HY_D), F32),
        scratch_shapes=[pltpu.VMEM((2 * n, LANE), F32), pltpu.VMEM((n, LANE), F32)],
        compiler_params=_params("parallel"),
        name="hyena_ctx",
    )(q, q, q, k2, k2, nrm, nrm, hy_bias)


def _merge_kernel(ya_ref, xs_ref, yf_ref, yb_ref, z_ref, yc_ref, yd_ref, g_ref, h_ref,
                  dv_ref, ng_ref, g1_ref, lg_ref, lb_ref,
                  wa_ref, wb_ref, wc_ref, wd_ref, wo_ref, o_ref):
    y = xs_ref[...].astype(F32) * dv_ref[...] + yf_ref[...] + yb_ref[...]
    gz = y * _silu(z_ref[...].astype(F32))
    ssd = gz * lax.rsqrt(jnp.mean(gz * gz, -1, keepdims=True) + LN_EPS) * ng_ref[...]
    d = D_MODEL
    gate = lambda k: jnp.tanh(g_ref[:, k * d:(k + 1) * d].astype(F32)) + 1.0
    m = gate(0) * _dot(ya_ref[...].astype(BF16), wa_ref[...])
    m = m + gate(1) * _dot(ssd.astype(BF16), wb_ref[...])
    m = m + gate(2) * _dot(yc_ref[...].astype(BF16), wc_ref[...])
    m = m + gate(3) * _dot(yd_ref[...].astype(BF16), wd_ref[...])
    mix = _dot(m.astype(BF16), wo_ref[...])
    o_ref[...] = _layer_norm(DN_ALPHA * h_ref[...] + g1_ref[...] * mix, lg_ref[...], lb_ref[...])


def _merge(ya, xbc, ydir, p, yc, yd, h, dvec, ng, gate1, lg, lb, wa, wb, wc, wd, wo):
    n = h.shape[0]
    t = 256
    tok = lambda w, col=0: pl.BlockSpec((t, w), lambda i: (i, col))
    vec = lambda w: pl.BlockSpec((1, w), lambda i: (0, 0))
    mat = lambda r: pl.BlockSpec((r, D_MODEL), lambda i: (0, 0))
    return pl.pallas_call(
        _merge_kernel,
        grid=(n // t,),
        in_specs=[tok(CONF_D), tok(SSD_D),
                  pl.BlockSpec((None, t, SSD_D), lambda i: (0, i, 0)),
                  pl.BlockSpec((None, t, SSD_D), lambda i: (1, i, 0)),
                  tok(SSD_D, PZ // SSD_D), tok(HY_D), tok(SC_D), tok(N_BRANCH * D_MODEL, 0), tok(D_MODEL),
                  vec(SSD_D), vec(SSD_D), vec(D_MODEL), vec(D_MODEL), vec(D_MODEL),
                  mat(CONF_D), mat(SSD_D), mat(HY_D), mat(SC_D), mat(D_MODEL)],
        out_specs=tok(D_MODEL),
        out_shape=jax.ShapeDtypeStruct((n, D_MODEL), F32),
        compiler_params=_params("parallel"),
        name="merge",
    )(ya, xbc, ydir, ydir, p, yc, yd, p, h, dvec, ng, gate1, lg, lb, wa, wb, wc, wd, wo)


MOE_T = 256


def _stream_tiles(hs):
    tiles = [h.shape[0] // MOE_T for h in hs]
    first = [sum(tiles[:s]) for s in range(len(hs))]
    return tiles, first


def _stream_specs(tiles, first, width):
    return [pl.BlockSpec((MOE_T, width), lambda i, nt=nt, f=f: (jnp.clip(i - f, 0, nt - 1), 0))
            for nt, f in zip(tiles, first)]


def _stream_vec_spec(first):
    def index(i):
        s = 0
        for f in first[1:]:
            s = s + (i >= f).astype(jnp.int32)
        return (s, 0, 0)
    return pl.BlockSpec((None, 1, D_MODEL), index)


def _stream_tile(i, refs, first):
    x = refs[0][...]
    for r, f in zip(refs[1:], first[1:]):
        x = jnp.where(i >= f, r[...], x)
    return x


def _router_kernel(*refs, first):
    ns = len(first)
    h_refs = refs[:ns]
    sh_ref, sc_ref, wh_ref, wl_ref, b_ref, sel_ref, cnt_ref, selt_ref = refs[ns:]
    i = pl.program_id(0)

    @pl.when(i == 0)
    def _():
        cnt_ref[...] = jnp.zeros_like(cnt_ref)

    u = _stream_tile(i, h_refs, first) * (1.0 + sc_ref[...]) + sh_ref[...]
    u_hi = u.astype(BF16)
    u_lo = (u - u_hi.astype(F32)).astype(BF16)
    lg = _dot(u_hi, wh_ref[...]) + _dot(u_lo, wh_ref[...]) + _dot(u_hi, wl_ref[...]) + b_ref[...]
    lane = lax.broadcasted_iota(jnp.int32, lg.shape, 1).astype(F32)
    neg = -1e30
    big = 1e9
    gl = jnp.where(lane < MOE_GROUPS, lg, neg)
    gmax = jnp.max(gl, -1, keepdims=True)
    gsel = jnp.min(jnp.where(gl == gmax, lane, big), -1, keepdims=True)
    gprob = 1.0 / jnp.sum(jnp.where(lane < MOE_GROUPS, jnp.exp(lg - gmax), 0.0), -1, keepdims=True)
    lo = MOE_GROUPS + gsel * MOE_EPG
    el = jnp.where(jnp.abs(lane - lo - (MOE_EPG - 1) / 2.0) < MOE_EPG / 2.0, lg, neg)
    m1 = jnp.max(el, -1, keepdims=True)
    i1 = jnp.min(jnp.where(el == m1, lane, big), -1, keepdims=True)
    el2 = jnp.where(lane == i1, neg, el)
    m2 = jnp.max(el2, -1, keepdims=True)
    i2 = jnp.min(jnp.where(el2 == m2, lane, big), -1, keepdims=True)
    t = jnp.exp(m2 - m1)
    w1 = gprob / (1.0 + t)
    w2 = gprob * t / (1.0 + t)
    oh1 = jnp.where(lane == i1, 1.0, 0.0)
    oh2 = jnp.where(lane == i2, 1.0, 0.0)
    oh = oh1 + oh2
    tt = lg.shape[0]
    li = lax.broadcasted_iota(jnp.int32, (tt, tt), 0)
    si = lax.broadcasted_iota(jnp.int32, (tt, tt), 1)
    before = _dot(jnp.where(li > si, 1.0, 0.0).astype(BF16), oh.astype(BF16)) + cnt_ref[...]
    r1 = jnp.sum(oh1 * before, -1, keepdims=True)
    r2 = jnp.sum(oh2 * before, -1, keepdims=True)
    cnt_ref[...] += jnp.sum(oh, axis=0, keepdims=True)
    cols = (i1 - MOE_GROUPS, i2 - MOE_GROUPS, w1, w2, r1, r2)
    sel = jnp.zeros_like(lg)
    for k, v in enumerate(cols):
        sel = jnp.where(lane == k, v, sel)
    sel_ref[...] = sel
    selt_ref[...] = sel.T[0:SUBLANE, :]


def _router(hs, shift, scale, wr, br):
    tiles, first = _stream_tiles(hs)
    t = MOE_T
    n = t * sum(tiles)
    vec = lambda w: pl.BlockSpec((1, w), lambda i: (0, 0))
    wr_hi = wr.astype(BF16)
    wr_lo = (wr - wr_hi.astype(F32)).astype(BF16)
    wspec = pl.BlockSpec((D_MODEL, LANE), lambda i: (0, 0))
    return pl.pallas_call(
        functools.partial(_router_kernel, first=first),
        grid=(sum(tiles),),
        in_specs=_stream_specs(tiles, first, D_MODEL) + [
            _stream_vec_spec(first), _stream_vec_spec(first), wspec, wspec, vec(LANE)],
        out_specs=[pl.BlockSpec((t, LANE), lambda i: (i, 0)), vec(LANE),
                   pl.BlockSpec((SUBLANE, t), lambda i: (0, i))],
        out_shape=[jax.ShapeDtypeStruct((n, LANE), F32), jax.ShapeDtypeStruct((1, LANE), F32),
                   jax.ShapeDtypeStruct((SUBLANE, n), F32)],
        compiler_params=_params("arbitrary"),
        name="router",
    )(*hs, shift, scale, wr_hi, wr_lo, br)


ROW_WORDS = D_MODEL // 2


def _pack_rows(x):
    c = x.shape[1] // 2
    bits = lambda v: lax.bitcast_convert_type(v.astype(BF16).astype(F32), jnp.uint32)
    return bits(x[:, :c]) | (bits(x[:, c:]) >> 16)


def _unpack_rows(w):
    hi = lax.bitcast_convert_type(w & jnp.uint32(0xFFFF0000), F32)
    lo = lax.bitcast_convert_type(w << 16, F32)
    return jnp.concatenate([hi, lo], axis=1)


def _dispatch_kernel(dst_ref, *refs, first, nt):
    ns = len(first)
    h_refs = refs[:ns]
    sh_ref, sc_ref, zero_hbm, xin_hbm, ubuf, sem = refs[ns:]
    del zero_hbm
    t = MOE_T
    i = pl.program_id(0)
    slot = i % 2

    def wait_slot(s):
        for _ in range(MOE_TOP_K):
            pltpu.make_async_copy(ubuf.at[s], xin_hbm.at[pl.ds(0, t), :], sem.at[s]).wait()

    @pl.when(i >= 2)
    def _():
        wait_slot(slot)

    ubuf[slot] = _pack_rows(_stream_tile(i, h_refs, first) * (1.0 + sc_ref[...]) + sh_ref[...])

    def issue(r, carry):
        for k in range(MOE_TOP_K):
            pltpu.make_async_copy(ubuf.at[slot, pl.ds(r, 1), :],
                                  xin_hbm.at[pl.ds(dst_ref[0, 0, k * t + r], 1), :], sem.at[slot]).start()
        return carry

    lax.fori_loop(0, t, issue, 0, unroll=16)

    @pl.when(i == nt - 1)
    def _():
        wait_slot(slot)
        if nt > 1:
            wait_slot(1 - slot)


def _dispatch(pos_t, hs, shift, scale, n_rows):
    tiles, first = _stream_tiles(hs)
    t = MOE_T
    nt = sum(tiles)
    return pl.pallas_call(
        functools.partial(_dispatch_kernel, first=first, nt=nt),
        grid=(nt,),
        in_specs=([pl.BlockSpec((1, 1, MOE_TOP_K * t), lambda i: (i, 0, 0), memory_space=pltpu.SMEM)]
                  + _stream_specs(tiles, first, D_MODEL)
                  + [_stream_vec_spec(first), _stream_vec_spec(first), pl.BlockSpec(memory_space=pl.ANY)]),
        out_specs=pl.BlockSpec(memory_space=pl.ANY),
        out_shape=jax.ShapeDtypeStruct((n_rows, ROW_WORDS), jnp.uint32),
        scratch_shapes=[pltpu.VMEM((2, t, ROW_WORDS), jnp.uint32), pltpu.SemaphoreType.DMA((2,))],
        input_output_aliases={3 + len(hs): 0},
        compiler_params=_params("arbitrary"),
        name="dispatch",
    )(pos_t, *hs, shift, scale, jnp.zeros((n_rows, ROW_WORDS), jnp.uint32))


def _expert_kernel(be_ref, nu_ref, x_ref, wg_ref, wu_ref, wd_ref, o_ref, wgb, wub, wdb):
    b = pl.program_id(0)

    @pl.when((b == 0) | (be_ref[b] != be_ref[jnp.maximum(b - 1, 0)]))
    def _():
        wgb[...] = wg_ref[...].astype(BF16)
        wub[...] = wu_ref[...].astype(BF16)
        wdb[...] = wd_ref[...].astype(BF16)

    @pl.when(b < nu_ref[0])
    def _():
        x = _unpack_rows(x_ref[...]).astype(BF16)
        hid = _silu(_dot(x, wgb[...])) * _dot(x, wub[...])
        o_ref[...] = _pack_rows(_dot(hid.astype(BF16), wdb[...]))

    @pl.when(b >= nu_ref[0])
    def _():
        o_ref[...] = jnp.zeros_like(o_ref)


def _experts(xin, block_e, n_used, wg, wu, wd, layer):
    n_blocks = block_e.shape[0]
    gs = pltpu.PrefetchScalarGridSpec(
        num_scalar_prefetch=2,
        grid=(n_blocks,),
        in_specs=[pl.BlockSpec((MOE_ROWS, ROW_WORDS), lambda b, be, nu: (b, 0)),
                  pl.BlockSpec((None, None, D_MODEL, MOE_FF), lambda b, be, nu: (layer, be[b], 0, 0)),
                  pl.BlockSpec((None, None, D_MODEL, MOE_FF), lambda b, be, nu: (layer, be[b], 0, 0)),
                  pl.BlockSpec((None, None, MOE_FF, D_MODEL), lambda b, be, nu: (layer, be[b], 0, 0))],
        out_specs=pl.BlockSpec((MOE_ROWS, ROW_WORDS), lambda b, be, nu: (b, 0)),
        scratch_shapes=[pltpu.VMEM((D_MODEL, MOE_FF), BF16), pltpu.VMEM((D_MODEL, MOE_FF), BF16),
                        pltpu.VMEM((MOE_FF, D_MODEL), BF16)],
    )
    return pl.pallas_call(
        _expert_kernel,
        grid_spec=gs,
        out_shape=jax.ShapeDtypeStruct((n_blocks * MOE_ROWS, ROW_WORDS), jnp.uint32),
        compiler_params=_params("arbitrary"),
        name="experts",
    )(block_e, n_used, xin, wg, wu, wd)


def _combine_kernel(pos_ref, posn_ref, y_hbm, *refs, first, tiles):
    ns = len(first)
    nt = sum(tiles)
    h_refs = refs[:ns]
    sel_ref, g2_ref, lg_ref, lb_ref = refs[ns:ns + 4]
    o_refs = refs[ns + 4:2 * ns + 4]
    ybuf, sem = refs[2 * ns + 4:]
    t = MOE_T
    i = pl.program_id(0)
    slot = i % 2

    def gather(p_ref, s):
        def issue(r, carry):
            pltpu.make_async_copy(y_hbm.at[pl.ds(p_ref[0, 0, r], 1), :], ybuf.at[s, pl.ds(r, 1), :],
                                  sem.at[s]).start()
            return carry
        lax.fori_loop(0, MOE_TOP_K * t, issue, 0, unroll=32)

    @pl.when(i == 0)
    def _():
        gather(pos_ref, 0)

    @pl.when(i + 1 < nt)
    def _():
        gather(posn_ref, 1 - slot)

    pltpu.make_async_copy(y_hbm.at[pl.ds(0, MOE_TOP_K * t), :], ybuf.at[slot], sem.at[slot]).wait()
    ffn = (sel_ref[:, 2:3] * _unpack_rows(ybuf[slot, 0:t, :])
           + sel_ref[:, 3:4] * _unpack_rows(ybuf[slot, t:2 * t, :]))
    out = _layer_norm(DN_ALPHA * _stream_tile(i, h_refs, first) + g2_ref[...] * ffn, lg_ref[...], lb_ref[...])
    for s in range(ns):
        @pl.when((i >= first[s]) & (i < first[s] + tiles[s]))
        def _(s=s):
            o_refs[s][...] = out


def _combine(y, pos_t, hs, sel, gate2, lg, lb):
    tiles, first = _stream_tiles(hs)
    t = MOE_T
    nt = sum(tiles)
    vec = pl.BlockSpec((1, D_MODEL), lambda i: (0, 0))
    return pl.pallas_call(
        functools.partial(_combine_kernel, first=first, tiles=tiles),
        grid=(nt,),
        in_specs=([pl.BlockSpec((1, 1, MOE_TOP_K * t), lambda i: (i, 0, 0), memory_space=pltpu.SMEM),
                   pl.BlockSpec((1, 1, MOE_TOP_K * t), lambda i: (jnp.minimum(i + 1, nt - 1), 0, 0),
                                memory_space=pltpu.SMEM),
                   pl.BlockSpec(memory_space=pl.ANY)]
                  + _stream_specs(tiles, first, D_MODEL)
                  + [pl.BlockSpec((t, LANE), lambda i: (i, 0)), _stream_vec_spec(first), vec, vec]),
        out_specs=_stream_specs(tiles, first, D_MODEL),
        out_shape=[jax.ShapeDtypeStruct(h.shape, F32) for h in hs],
        scratch_shapes=[pltpu.VMEM((2, MOE_TOP_K * t, ROW_WORDS), jnp.uint32), pltpu.SemaphoreType.DMA((2,))],
        compiler_params=_params("arbitrary"),
        name="combine",
    )(pos_t, pos_t, y, *hs, sel, gate2, lg, lb)


def _moe(hs, shift, scale, gate2, lg, lb, wr, br, wg, wu, wd, layer):
    t = MOE_T
    n = sum(h.shape[0] for h in hs)
    nt = n // t
    sel, cnt, selt = _router(hs, shift, scale, wr, br)
    counts = cnt[0, MOE_GROUPS:MOE_GROUPS + MOE_EXPERTS].astype(jnp.int32)
    padded = (counts + MOE_ROWS - 1) // MOE_ROWS * MOE_ROWS
    pad_end = jnp.cumsum(padded)
    pad_start = pad_end - padded
    n_blocks = (n * MOE_TOP_K + MOE_EXPERTS * (MOE_ROWS - 1) + MOE_ROWS - 1) // MOE_ROWS
    blk_row = jnp.arange(n_blocks, dtype=jnp.int32) * MOE_ROWS
    block_e = jnp.minimum(jnp.sum((blk_row[:, None] >= pad_end[None, :]).astype(jnp.int32), axis=1),
                          MOE_EXPERTS - 1)
    n_used = (pad_end[-1:] // MOE_ROWS).astype(jnp.int32)
    e_kt = selt[0:MOE_TOP_K].astype(jnp.int32)
    ids = jnp.arange(MOE_EXPERTS, dtype=jnp.int32)[None, :, None]
    start_kt = jnp.sum(jnp.where(e_kt[:, None, :] == ids, pad_start[None, :, None], 0), axis=1)
    pos_kt = start_kt + selt[4:4 + MOE_TOP_K].astype(jnp.int32)
    pos_t = pos_kt.reshape(MOE_TOP_K, nt, t).transpose(1, 0, 2).reshape(nt, 1, MOE_TOP_K * t)
    xin = _dispatch(pos_t, hs, shift, scale, n_blocks * MOE_ROWS)
    y = _experts(xin, block_e, n_used, wg, wu, wd, layer)
    return _combine(y, pos_t, hs, sel, gate2, lg, lb)


def _mixer(h, mod, lw, ssd_init, tables, *, latent, need_mix):
    n = h.shape[0]
    p, dt_raw = _inproj(h, mod[0], mod[1], lw["w_in"], lw["layer"])
    xbc = _conv3(p, PX, SSD_XBC, lw["ssd_conv_w"], lw["ssd_conv_b"], silu=True, out_dtype=BF16)
    ydir, finals = _ssd_scan(xbc, dt_raw, lw["ssd_dt_bias"], lw["ssd_a_log"], ssd_init)
    if not need_mix:
        return None, finals
    conf_w = (lw["conf_dw_w"], lw["conf_dw_b"], lw["conf_ln_g"], lw["conf_ln_b"])
    ya = _conformer_grid(p, *conf_w) if latent else _conformer(p, *conf_w, dil=1)
    q = _conv3(p, PC, 3 * HY_D, lw["hy_short_w"], lw["hy_short_b"], silu=False, out_dtype=F32)
    k2, nrm = _hyena_filters(lw["feat_lat" if latent else "feat_ctx"], lw["hy_filter"], n)
    if latent:
        yc = _hyena_long(q, k2, nrm, lw["hy_bias"], tables)
    else:
        yc = _hyena_ctx(q, k2, nrm, lw["hy_bias"])
    yd = _gated_conv(p, lw["sc_conv_w"])
    h = _merge(ya, xbc, ydir, p, yc, yd, h, lw["ssd_dvec"], lw["ssd_norm_g"], mod[2], lw["ln_g0"], lw["ln_b0"],
               lw["w_branch_a"], lw["w_branch_b"], lw["w_branch_c"], lw["w_branch_d"], lw["w_out"])
    return h, finals


def _positional_features(n):
    tau = jnp.arange(2 * n, dtype=jnp.int32)
    lag = jnp.where(tau < n, tau, 2 * n - tau).astype(F32)[:, None]
    t01 = lag * (1.0 / (n - 1))
    omega = (2.0 * math.pi / n) * lag
    bands = jnp.linspace(1e-4, HY_BANDS - 1, HY_BANDS, dtype=F32)
    featx = jnp.concatenate([t01, jnp.cos(bands * omega), -jnp.sin(bands * omega)], axis=-1)
    return _pad_lanes(featx)


def _relayout_w_in(w_in):
    w16 = w_in.astype(BF16)
    seg = lambda a, b: w16[:, :, a:b]
    ob = OFF_B
    parts = [0.5 * seg(OFF_G, OFF_G + N_BRANCH * D_MODEL),
             seg(OFF_A, OFF_B),
             seg(OFF_C, OFF_D),
             seg(OFF_D, OFF_G),
             seg(ob + SSD_D, ob + SSD_D + SSD_XBC),
             seg(ob, ob + SSD_D),
             seg(ob + SSD_D + SSD_XBC, OFF_C),
             jnp.zeros(w_in.shape[:2] + (NP - PDT - 2 * SSD_HEADS,), BF16)]
    return jnp.concatenate(parts, axis=-1)


def _pad_lanes(v):
    return jnp.pad(v, ((0, 0), (0, LANE - v.shape[-1])))


def kernel(x, c, ctx, c_ctx, w_mod, b_mod, ln_g, ln_b, w_in, conf_dw_w, conf_dw_b, conf_ln_g, conf_ln_b,
           ssd_conv_w, ssd_conv_b, ssd_a_log, ssd_dt_bias, ssd_d, ssd_norm_g, hy_short_w, hy_short_b,
           hy_w1, hy_b1, hy_w2, hy_b2, hy_freq, hy_w3, hy_bias, sc_conv_w, w_branch_a, w_branch_b,
           w_branch_c, w_branch_d, w_out, rt_group_w, rt_group_b, rt_expert_w, rt_expert_b,
           ex_w_gate, ex_w_up, ex_w_down):
    assert x.shape[0] == 1 and ctx.shape[0] == 1
    n_lat, n_ctx = x.shape[1], ctx.shape[1]
    depth = w_in.shape[0]

    cv = jnp.concatenate([c, c_ctx[None, :], jnp.zeros((SUBLANE - 2, D_MODEL), F32)], axis=0)
    mods = _mod_vectors(cv, w_mod, b_mod)
    w_in_p = _relayout_w_in(w_in)
    tables = _dft_tables(n_lat)
    feat_lat = _positional_features(n_lat)
    feat_ctx = _positional_features(n_ctx)
    deltas = jnp.abs(jnp.linspace(HY_MIN_DECAY, HY_MAX_DECAY, HY_N_FILT, dtype=F32))
    deltas_d = deltas.reshape(HY_ORDER, 2, HY_D).transpose(1, 0, 2).reshape(2, 1, HY_ORDER * HY_D)
    router_w = jnp.concatenate([rt_group_w, rt_expert_w,
                                jnp.zeros((depth, D_MODEL, LANE - MOE_GROUPS - MOE_EXPERTS), F32)], axis=-1)
    router_b = jnp.concatenate([rt_group_b, rt_expert_b,
                                jnp.zeros((depth, LANE - MOE_GROUPS - MOE_EXPERTS), F32)], axis=-1)
    ssd_zero = jnp.zeros((2,) + SSD_STATE_SHAPE, F32)

    h_lat, h_ctx = x[0], ctx[0]
    for l in range(depth):
        row = lambda v: v[None, :]
        lw = dict(
            w_in=w_in_p, layer=l, conf_dw_w=conf_dw_w[l], conf_dw_b=row(conf_dw_b[l]), conf_ln_g=row(conf_ln_g[l]),
            conf_ln_b=row(conf_ln_b[l]), ssd_conv_w=ssd_conv_w[l], ssd_conv_b=row(ssd_conv_b[l]),
            ssd_a_log=_pad_lanes(ssd_a_log[l].reshape(1, -1)), ssd_dt_bias=_pad_lanes(ssd_dt_bias[l].reshape(1, -1)),
            ssd_dvec=row(jnp.repeat(ssd_d[l], SSD_HEAD_DIM)), ssd_norm_g=row(ssd_norm_g[l]),
            hy_short_w=hy_short_w[l], hy_short_b=row(hy_short_b[l]),
            hy_filter=_filter_weights(hy_w1[l], hy_b1[l], hy_w2[l], hy_b2[l], hy_freq[l], hy_w3[l], deltas_d),
            hy_bias=hy_bias[l], feat_lat=feat_lat, feat_ctx=feat_ctx,
            sc_conv_w=sc_conv_w[l], ln_g0=row(ln_g[l, 0]), ln_b0=row(ln_b[l, 0]),
            w_branch_a=w_branch_a[l].astype(BF16), w_branch_b=w_branch_b[l].astype(BF16),
            w_branch_c=w_branch_c[l].astype(BF16), w_branch_d=w_branch_d[l].astype(BF16),
            w_out=(0.5 * w_out[l]).astype(BF16))
        moe_w = (router_w[l], row(router_b[l]), ex_w_gate, ex_w_up, ex_w_down, l)
        last = l == depth - 1
        d = D_MODEL
        mod_lat = [mods[l, 0:1, k * d:(k + 1) * d] for k in range(6)]
        mod_ctx = [mods[l, 1:2, k * d:(k + 1) * d] for k in range(6)]

        mix_ctx, ctx_states = _mixer(h_ctx, mod_ctx, lw, ssd_zero, None, latent=False, need_mix=not last)
        h_lat, _ = _mixer(h_lat, mod_lat, lw, ctx_states, tables, latent=True, need_mix=True)
        streams = [(h_lat, mod_lat)] if last else [(h_lat, mod_lat), (mix_ctx, mod_ctx)]
        vecs = [jnp.stack([m[k] for _, m in streams]) for k in (3, 4, 5)]
        outs = _moe([h for h, _ in streams], *vecs, row(ln_g[l, 1]), row(ln_b[l, 1]), *moe_w)
        h_lat = outs[0]
        if not last:
            h_ctx = outs[1]
    return h_lat[None]
```

```python
import functools
import math

import jax
import jax.numpy as jnp
from jax import lax
from jax.experimental import pallas as pl
from jax.experimental.pallas import tpu as pltpu

F32 = jnp.float32
BF16 = jnp.bfloat16
HIGHEST = lax.Precision.HIGHEST

D_MODEL = 1024
DEPTH = 4
GRID_W = 64
CONF_D = 512
CONF_K = 31
SSD_D = 768
SSD_HEADS = 12
SSD_HEAD_DIM = 64
SSD_GROUPS = 4
SSD_HPG = SSD_HEADS // SSD_GROUPS
SSD_STATE = 128
SSD_CHUNK = 128
SSD_BC = SSD_GROUPS * SSD_STATE
SSD_XBC = SSD_D + 2 * SSD_BC
SSD_PROJ = SSD_D + SSD_XBC + 2 * SSD_HEADS
HY_D = 512
HY_ORDER = 2
HY_EMB = 33
HY_BANDS = (HY_EMB - 1) // 2
HY_HID = 64
HY_N_FILT = HY_ORDER * 2 * HY_D
HY_MIN_DECAY = math.log(1e-2) / 1.5
HY_MAX_DECAY = math.log(1e-2) / 0.3
SC_D = 512
N_BRANCH = 4
OFF_A = 0
OFF_B = OFF_A + 2 * CONF_D
OFF_C = OFF_B + SSD_PROJ
OFF_D = OFF_C + 3 * HY_D
OFF_G = OFF_D + 3 * SC_D
MOE_GROUPS = 4
MOE_EPG = 8
MOE_EXPERTS = MOE_GROUPS * MOE_EPG
MOE_TOP_K = 2
MOE_FF = 512
DN_ALPHA = (2 * DEPTH) ** 0.25
LN_EPS = 1e-5

PG = 0
PA = PG + N_BRANCH * D_MODEL
PC = PA + 2 * CONF_D
PD = PC + 3 * HY_D
PX = PD + 3 * SC_D
PZ = PX + SSD_XBC
PDT = PZ + SSD_D
INPROJ_TN = 1024
NP = -(-(PDT + 128) // INPROJ_TN) * INPROJ_TN

LANE = 128
SUBLANE = 8
FFT_N2 = 256
MOE_ROWS = 256
VMEM_LIMIT = 48 * 1024 * 1024


def _params(*sem):
    return pltpu.CompilerParams(dimension_semantics=sem, vmem_limit_bytes=VMEM_LIMIT)


def _sigmoid(x):
    return 0.5 * jnp.tanh(0.5 * x) + 0.5


def _silu(x):
    h = 0.5 * x
    return h * jnp.tanh(h) + h


def _layer_norm(x, g, b):
    mu = jnp.mean(x, -1, keepdims=True)
    xc = x - mu
    var = jnp.mean(xc * xc, -1, keepdims=True)
    return xc * lax.rsqrt(var + LN_EPS) * g + b


def _dot(a, b):
    return jnp.dot(a, b, preferred_element_type=F32)


def _mod_kernel(cv_ref, w_ref, b_ref, o_ref):
    o_ref[...] = jnp.dot(_silu(cv_ref[...]), w_ref[...], precision=HIGHEST,
                         preferred_element_type=F32) + b_ref[...]


def _mod_vectors(cv, w_mod, b_mod):
    tn = 1536
    return pl.pallas_call(
        _mod_kernel,
        grid=(DEPTH, 6 * D_MODEL // tn),
        in_specs=[pl.BlockSpec((SUBLANE, D_MODEL), lambda l, j: (0, 0)),
                  pl.BlockSpec((None, D_MODEL, tn), lambda l, j: (l, 0, j)),
                  pl.BlockSpec((None, 1, tn), lambda l, j: (l, 0, j))],
        out_specs=pl.BlockSpec((None, SUBLANE, tn), lambda l, j: (l, 0, j)),
        out_shape=jax.ShapeDtypeStruct((DEPTH, SUBLANE, 6 * D_MODEL), F32),
        compiler_params=_params("parallel", "parallel"),
        name="mod_vectors",
    )(cv, w_mod, b_mod.reshape(DEPTH, 1, 6 * D_MODEL))


def _inproj_kernel(x_ref, sh_ref, sc_ref, w_ref, o_ref, dt_ref, xb_ref, *, nj):
    j = pl.program_id(1)

    @pl.when(j == 0)
    def _():
        xb_ref[...] = (x_ref[...] * (1.0 + sc_ref[...]) + sh_ref[...]).astype(BF16)

    res = _dot(xb_ref[...], w_ref[...])
    o_ref[...] = res.astype(BF16)

    @pl.when(j == nj - 1)
    def _():
        off = PDT - (nj - 1) * INPROJ_TN
        dt_ref[...] = res[:, off:off + LANE]


def _inproj(h, shift, scale, w, layer):
    n = h.shape[0]
    tm = min(n, 2048)
    tn = INPROJ_TN
    nj = NP // tn
    assert PDT >= (nj - 1) * tn
    return pl.pallas_call(
        functools.partial(_inproj_kernel, nj=nj),
        grid=(n // tm, nj),
        in_specs=[pl.BlockSpec((tm, D_MODEL), lambda i, j: (i, 0)),
                  pl.BlockSpec((1, D_MODEL), lambda i, j: (0, 0)),
                  pl.BlockSpec((1, D_MODEL), lambda i, j: (0, 0)),
                  pl.BlockSpec((None, D_MODEL, tn), lambda i, j: (layer, 0, j))],
        out_specs=[pl.BlockSpec((tm, tn), lambda i, j: (i, j)),
                   pl.BlockSpec((tm, LANE), lambda i, j: (i, 0))],
        out_shape=[jax.ShapeDtypeStruct((n, NP), BF16), jax.ShapeDtypeStruct((n, LANE), F32)],
        scratch_shapes=[pltpu.VMEM((tm, D_MODEL), BF16)],
        compiler_params=_params("parallel", "arbitrary"),
        name="inproj",
    )(h, shift, scale, w)


HALO_ROWS = 16


def _conv3_tile(x, prev_row, next_row, w_ref, store):
    t = x.shape[0]
    e = HALO_ROWS
    w0, w1, w2 = w_ref[0:1, :], w_ref[1:2, :], w_ref[2:3, :]
    y = w0 * pltpu.roll(x, 1, 0) + w1 * x + w2 * pltpu.roll(x, t - 1, 0)
    store(slice(0, t), y)
    row = lax.broadcasted_iota(jnp.int32, (e, x.shape[1]), 0)
    store(slice(0, e), y[0:e] + jnp.where(row == 0, w0 * (prev_row - x[t - 1:t]), 0.0))
    store(slice(t - e, t), y[t - e:t] + jnp.where(row == e - 1, w2 * (next_row - x[0:1]), 0.0))


def _conv3_kernel(cur_ref, prev_ref, next_ref, w_ref, b_ref, o_ref, *, silu, nt):
    i = pl.program_id(0)
    x = cur_ref[...].astype(F32)
    pv = jnp.where(i > 0, prev_ref[HALO_ROWS - 1:HALO_ROWS, :].astype(F32), 0.0)
    nx = jnp.where(i < nt - 1, next_ref[0:1, :].astype(F32), 0.0)

    def store(rows, y):
        y = y + b_ref[...]
        o_ref[rows, :] = (_silu(y) if silu else y).astype(o_ref.dtype)

    _conv3_tile(x, pv, nx, w_ref, store)


def _halo_specs(t, ct, n, col0):
    rb = t // HALO_ROWS
    last = n // HALO_ROWS - 1
    return [pl.BlockSpec((t, ct), lambda i, j: (i, col0 + j)),
            pl.BlockSpec((HALO_ROWS, ct), lambda i, j: (jnp.maximum(i * rb - 1, 0), col0 + j)),
            pl.BlockSpec((HALO_ROWS, ct), lambda i, j: (jnp.minimum((i + 1) * rb, last), col0 + j))]


def _conv3(p, col, width, w, b, *, silu, out_dtype):
    n = p.shape[0]
    t = min(n, 1024)
    ct = 256
    nt = n // t
    return pl.pallas_call(
        functools.partial(_conv3_kernel, silu=silu, nt=nt),
        grid=(nt, width // ct),
        in_specs=_halo_specs(t, ct, n, col // ct) + [
            pl.BlockSpec((3, ct), lambda i, j: (0, j)),
            pl.BlockSpec((1, ct), lambda i, j: (0, j))],
        out_specs=pl.BlockSpec((t, ct), lambda i, j: (i, j)),
        out_shape=jax.ShapeDtypeStruct((n, width), out_dtype),
        compiler_params=_params("parallel", "parallel"),
        name="conv3",
    )(p, p, p, w, b)


def _gconv_kernel(bg_ref, cc_ref, cp_ref, cn_ref, xc_ref, xp_ref, xn_ref, w_ref, o_ref, *, nt):
    i = pl.program_id(0)
    f = lambda v: v.astype(F32)
    last = slice(HALO_ROWS - 1, HALO_ROWS)
    x = f(cc_ref[...]) * f(xc_ref[...])
    pv = jnp.where(i > 0, f(cp_ref[last, :]) * f(xp_ref[last, :]), 0.0)
    nx = jnp.where(i < nt - 1, f(cn_ref[0:1, :]) * f(xn_ref[0:1, :]), 0.0)

    def store(rows, y):
        o_ref[rows, :] = f(bg_ref[rows, :]) * y

    _conv3_tile(x, pv, nx, w_ref, store)


def _gated_conv(p, w):
    n = p.shape[0]
    t = min(n, 1024)
    ct = 256
    nt = n // t
    nb = SC_D // ct
    return pl.pallas_call(
        functools.partial(_gconv_kernel, nt=nt),
        grid=(nt, nb),
        in_specs=([pl.BlockSpec((t, ct), lambda i, j: (i, PD // ct + j))]
                  + _halo_specs(t, ct, n, PD // ct + nb)
                  + _halo_specs(t, ct, n, PD // ct + 2 * nb)
                  + [pl.BlockSpec((3, ct), lambda i, j: (0, j))]),
        out_specs=pl.BlockSpec((t, ct), lambda i, j: (i, j)),
        out_shape=jax.ShapeDtypeStruct((n, SC_D), F32),
        compiler_params=_params("parallel", "parallel"),
        name="gated_conv",
    )(p, p, p, p, p, p, p, w)


CONF_RB = 64


def _conf_kernel(vc, gc, vp, gp, vn, gn, w_ref, b_ref, lg_ref, lb_ref, o_ref, buf, *, t, halo, dil, nt):
    i = pl.program_id(0)
    glu = lambda v, g: v.astype(F32) * _sigmoid(g.astype(F32))
    buf[halo:halo + t, :] = glu(vc[...], gc[...])
    buf[0:halo, :] = jnp.where(i > 0, glu(vp[t - halo:t, :], gp[t - halo:t, :]), 0.0)
    buf[halo + t:halo + t + halo, :] = jnp.where(i < nt - 1, glu(vn[0:halo, :], gn[0:halo, :]), 0.0)

    def block(r0):
        acc = jnp.zeros((CONF_RB, CONF_D), F32)
        for j in range(CONF_K):
            off = halo + (j - CONF_K // 2) * dil
            acc = acc + w_ref[j:j + 1, :] * buf[pl.ds(r0 + off, CONF_RB), :]
        v = _layer_norm(acc + b_ref[...], lg_ref[...], lb_ref[...])
        o_ref[pl.ds(r0, CONF_RB), :] = _silu(v)

    if dil % CONF_RB == 0:
        def body(rb, carry):
            block(pl.multiple_of(rb * CONF_RB, CONF_RB))
            return carry
        lax.fori_loop(0, t // CONF_RB, body, 0)
    else:
        for rb in range(t // CONF_RB):
            block(rb * CONF_RB)


def _conformer(p, w, b, lg, lb, *, dil):
    n = p.shape[0]
    t = min(n, 1024)
    nt = n // t
    halo = -(-(CONF_K // 2) * dil // SUBLANE) * SUBLANE
    assert halo <= t
    cb = PA // CONF_D

    def spec(col, shift):
        return pl.BlockSpec((t, CONF_D), lambda i: (jnp.clip(i + shift, 0, nt - 1), col))

    vec = pl.BlockSpec((1, CONF_D), lambda i: (0, 0))
    return pl.pallas_call(
        functools.partial(_conf_kernel, t=t, halo=halo, dil=dil, nt=nt),
        grid=(nt,),
        in_specs=[spec(cb, 0), spec(cb + 1, 0), spec(cb, -1), spec(cb + 1, -1), spec(cb, 1), spec(cb + 1, 1),
                  pl.BlockSpec((CONF_K, CONF_D), lambda i: (0, 0)), vec, vec, vec],
        out_specs=pl.BlockSpec((t, CONF_D), lambda i: (i, 0)),
        out_shape=jax.ShapeDtypeStruct((n, CONF_D), F32),
        scratch_shapes=[pltpu.VMEM((t + 2 * halo, CONF_D), F32)],
        compiler_params=_params("parallel"),
        name="conformer",
    )(p, p, p, p, p, p, w, b, lg, lb)


CONF_COLS = 16
CONF_ROWS_PER_ITER = 4


def _conf_grid_kernel(v_ref, g_ref, w_ref, b_ref, lg_ref, lb_ref, o_ref, buf, *, rows):
    half = CONF_K // 2
    zeros = jnp.zeros((half,) + buf.shape[1:], F32)
    buf[0:half] = zeros
    buf[half + rows:half + rows + half] = zeros
    buf[half:half + rows] = v_ref[...].astype(F32) * _sigmoid(g_ref[...].astype(F32))

    rb = CONF_ROWS_PER_ITER

    def body(it, carry):
        r0 = it * rb
        accs = [None] * rb
        for j in range(CONF_K):
            wj = w_ref[j:j + 1, :]
            for s in range(rb):
                term = wj * buf[r0 + s + j]
                accs[s] = term if j == 0 else accs[s] + term
        for s in range(rb):
            v = _layer_norm(accs[s] + b_ref[...], lg_ref[...], lb_ref[...])
            o_ref[r0 + s] = _silu(v).astype(o_ref.dtype)
        return carry

    lax.fori_loop(0, rows // rb, body, 0)


def _conformer_grid(p, w, b, lg, lb):
    n = p.shape[0]
    rows = n // GRID_W
    p3 = p.reshape(rows, GRID_W, p.shape[1])
    cb = PA // CONF_D
    vec = pl.BlockSpec((1, CONF_D), lambda j: (0, 0))
    blk = lambda col: pl.BlockSpec((rows, CONF_COLS, CONF_D), lambda j: (0, j, col))
    out = pl.pallas_call(
        functools.partial(_conf_grid_kernel, rows=rows),
        grid=(GRID_W // CONF_COLS,),
        in_specs=[blk(cb), blk(cb + 1), pl.BlockSpec((CONF_K, CONF_D), lambda j: (0, 0)), vec, vec, vec],
        out_specs=blk(0),
        out_shape=jax.ShapeDtypeStruct((rows, GRID_W, CONF_D), BF16),
        scratch_shapes=[pltpu.VMEM((rows + 2 * (CONF_K // 2), CONF_COLS, CONF_D), F32)],
        compiler_params=_params("parallel"),
        name="conformer_grid",
    )(p3, p3, w, b, lg, lb)
    return out.reshape(n, CONF_D)


SSD_STATE_SHAPE = (SSD_HEADS // 2, SSD_STATE, 2 * SSD_HEAD_DIM)
SSD_CHUNKS_PER_STEP = 2


def _ssd_kernel(xbc_ref, dt_ref, dtb_ref, alog_ref, init_ref, y_ref, fin_ref, h_ref, *, ns, cps):
    d = pl.program_id(0)
    c = pl.program_id(1)
    q = SSD_CHUNK
    hd = SSD_HEAD_DIM

    @pl.when(c == 0)
    def _():
        h_ref[...] = init_ref[...]

    lane = lax.broadcasted_iota(jnp.int32, (q, LANE), 1)
    head = lane < SSD_HEADS
    first_half = lane < hd
    li = lax.broadcasted_iota(jnp.int32, (q, q), 0)
    si = lax.broadcasted_iota(jnp.int32, (q, q), 1)
    mask = (li - si) * (1 - 2 * d) >= 0
    tri = mask.astype(F32)
    a_rate = -jnp.exp(alog_ref[...])

    def one_chunk(r0):
        rows = pl.ds(r0, q)
        raw = dt_ref[rows, :] + dtb_ref[...]
        dt_all = jnp.maximum(raw, 0.0) + jnp.log(1.0 + jnp.exp(-jnp.abs(raw)))
        ld_all = dt_all * a_rate
        dt_d = jnp.where(head, jnp.where(d == 0, dt_all, pltpu.roll(dt_all, LANE - SSD_HEADS, 1)), 0.0)
        ld_d = jnp.where(head, jnp.where(d == 0, ld_all, pltpu.roll(ld_all, LANE - SSD_HEADS, 1)), 0.0)
        cum = jnp.dot(tri, ld_d, precision=HIGHEST, preferred_element_type=F32)
        tot = jnp.sum(ld_d, axis=0, keepdims=True)
        cum_t = cum.T
        dt_t = dt_d.T
        w_t = (jnp.exp(tot - cum) * dt_d).T
        a_out = jnp.exp(cum)
        e_tot = jnp.exp(tot)

        groups = {}

        def group(g):
            if g not in groups:
                bg = xbc_ref[rows, SSD_D + g * SSD_STATE:SSD_D + (g + 1) * SSD_STATE].astype(F32)
                cg = xbc_ref[rows, SSD_D + SSD_BC + g * SSD_STATE:SSD_D + SSD_BC + (g + 1) * SSD_STATE].astype(F32)
                bg_t = bg.T
                groups[g] = (bg_t, cg, _dot(cg.astype(BF16), bg_t.astype(BF16)))
            return groups[g]

        def head_terms(hh):
            bg_t, cg, cb = group(hh // SSD_HPG)
            diff = cum[:, hh:hh + 1] - cum_t[hh:hh + 1, :]
            dec = jnp.exp(jnp.where(mask, diff, -1e30))
            m = (cb * dec * dt_t[hh:hh + 1, :]).astype(BF16)
            cs = (cg * a_out[:, hh:hh + 1]).astype(BF16)
            bw = (bg_t * w_t[hh:hh + 1, :]).astype(BF16)
            return m, cs, bw

        zero = jnp.zeros((q, LANE), BF16)
        for j in range(SSD_HEADS // 2):
            ma, ca, wa = head_terms(2 * j)
            mb, cb_, wb = head_terms(2 * j + 1)
            xp = xbc_ref[rows, j * LANE:(j + 1) * LANE].astype(BF16)
            hp = h_ref[j]
            hp_b = hp.astype(BF16)
            x_bd = jnp.concatenate([jnp.where(first_half, xp, zero), jnp.where(first_half, zero, xp)], axis=0)
            h_bd = jnp.concatenate([jnp.where(first_half, hp_b, zero), jnp.where(first_half, zero, hp_b)], axis=0)
            y_ref[rows, j * LANE:(j + 1) * LANE] = _dot(jnp.concatenate([ma, mb, ca, cb_], axis=1),
                                                        jnp.concatenate([x_bd, h_bd], axis=0))
            s_new = _dot(jnp.concatenate([wa, wb], axis=1), x_bd)
            e_pair = jnp.where(first_half[0:1, :], e_tot[:, 2 * j:2 * j + 1], e_tot[:, 2 * j + 1:2 * j + 2])
            h_ref[j] = e_pair * hp + s_new

    for k in range(cps):
        one_chunk(pl.multiple_of(jnp.where(d == 0, k, cps - 1 - k) * q, q))

    @pl.when(c == ns - 1)
    def _():
        fin_ref[...] = h_ref[...]


def _ssd_scan(xbc, p, dt_bias, a_log, init):
    n = xbc.shape[0]
    cps = SSD_CHUNKS_PER_STEP
    q = SSD_CHUNK * cps
    ns = n // q

    def chunk(d, c):
        return jnp.where(d == 0, c, ns - 1 - c)

    st = SSD_STATE_SHAPE
    vec = pl.BlockSpec((1, LANE), lambda d, c: (0, 0))
    return pl.pallas_call(
        functools.partial(_ssd_kernel, ns=ns, cps=cps),
        grid=(2, ns),
        in_specs=[pl.BlockSpec((q, SSD_XBC), lambda d, c: (chunk(d, c), 0)),
                  pl.BlockSpec((q, LANE), lambda d, c: (chunk(d, c), 0)),
                  vec, vec,
                  pl.BlockSpec((None,) + st, lambda d, c: (d, 0, 0, 0))],
        out_specs=[pl.BlockSpec((None, q, SSD_D), lambda d, c: (d, chunk(d, c), 0)),
                   pl.BlockSpec((None,) + st, lambda d, c: (d, 0, 0, 0))],
        out_shape=[jax.ShapeDtypeStruct((2, n, SSD_D), F32),
                   jax.ShapeDtypeStruct((2,) + st, F32)],
        scratch_shapes=[pltpu.VMEM(st, F32)],
        compiler_params=_params("arbitrary", "arbitrary"),
        name="ssd_scan",
    )(xbc, p, dt_bias, a_log, init)


def _filt_kernel(feat_ref, w1_ref, b1_ref, w2_ref, b2_ref, fr_ref, w3h_ref, w3l_ref, dl_ref, k_ref, nrm_ref, *,
                 n, t):
    i = pl.program_id(0)
    hf = t // 2
    feat = feat_ref[...]
    x = jnp.concatenate([feat[0:hf], feat[hf:t]], axis=1)
    hid = jnp.sin(fr_ref[0:1, :] * (jnp.dot(x, w1_ref[...], precision=HIGHEST,
                                            preferred_element_type=F32) + b1_ref[...]))
    hid = jnp.sin(fr_ref[1:2, :] * (jnp.dot(hid, w2_ref[...], precision=HIGHEST,
                                            preferred_element_type=F32) + b2_ref[...]))
    hi = hid.astype(BF16)
    lo = (hid - hi.astype(F32)).astype(BF16)

    @pl.when(i == 0)
    def _():
        nrm_ref[...] = jnp.zeros_like(nrm_ref)

    for half in range(2):
        wh, wl = w3h_ref[half], w3l_ref[half]
        filt = _dot(hi, wh) + _dot(lo, wh) + _dot(hi, wl)
        filt = filt * jnp.exp(-feat[half * hf:(half + 1) * hf, 0:1] * dl_ref[...])
        row = i * t + half * hf + lax.broadcasted_iota(jnp.int32, filt.shape, 0)
        filt = jnp.where(row == n, 0.0, filt)
        k_ref[half * hf:(half + 1) * hf, :] = filt
        nrm_ref[...] += jnp.sum(jnp.abs(filt), axis=0, keepdims=True)


def _hyena_filters(featx, fw, n):
    t = min(n, 512)
    half = n // t
    oc = HY_ORDER * HY_D
    full = lambda shape: pl.BlockSpec(shape, lambda i: tuple(0 for _ in shape))
    w3spec = pl.BlockSpec((None, 2, LANE, oc), lambda i: (i // half, 0, 0, 0))
    return pl.pallas_call(
        functools.partial(_filt_kernel, n=n, t=t),
        grid=(2 * n // t,),
        in_specs=[pl.BlockSpec((t, LANE), lambda i: (i, 0)),
                  full((2 * LANE, LANE)), full((1, LANE)), full((LANE, LANE)), full((1, LANE)),
                  full((2, LANE)), w3spec, w3spec,
                  pl.BlockSpec((None, 1, oc), lambda i: (i // half, 0, 0))],
        out_specs=[pl.BlockSpec((t, oc), lambda i: (i, 0)),
                   pl.BlockSpec((1, oc), lambda i: (0, 0))],
        out_shape=[jax.ShapeDtypeStruct((2 * n, oc), F32),
                   jax.ShapeDtypeStruct((1, oc), F32)],
        compiler_params=_params("arbitrary"),
        name="hyena_filters",
    )(featx, fw["w1"], fw["b1"], fw["w2"], fw["b2"], fw["freq"], fw["w3h"], fw["w3l"], fw["deltas"])


def _filter_weights(w1, b1, w2, b2, freq, w3, deltas_d):
    hh = HY_HID
    z = lambda r, c: jnp.zeros((r, c), F32)
    w1p = jnp.pad(w1, ((0, LANE - HY_EMB), (0, 0)))
    w1b = jnp.concatenate([jnp.concatenate([w1p, z(LANE, hh)], 1),
                           jnp.concatenate([z(LANE, hh), w1p], 1)], 0)
    w2b = jnp.concatenate([jnp.concatenate([w2, z(hh, hh)], 1),
                           jnp.concatenate([z(hh, hh), w2], 1)], 0)
    two = lambda v: jnp.concatenate([v, v], axis=-1)
    w3d = w3.reshape(hh, HY_ORDER, 2, HY_D).transpose(2, 0, 1, 3).reshape(2, hh, HY_ORDER * HY_D)
    zz = jnp.zeros_like(w3d)
    w3x = jnp.stack([jnp.concatenate([w3d, zz], 1), jnp.concatenate([zz, w3d], 1)], axis=1)
    w3h = w3x.astype(BF16)
    w3l = (w3x - w3h.astype(F32)).astype(BF16)
    return dict(w1=w1b, b1=two(b1[None, :]), w2=w2b, b2=two(b2[None, :]), freq=two(freq), w3h=w3h, w3l=w3l,
                deltas=deltas_d)


DFT_LANES = 8192


def _dft_rows_kernel(f_ref, x_ref, o_ref, *, nj):
    x = jnp.concatenate([x_ref[:, jj, :] for jj in range(nj)], axis=1)
    o_ref[...] = _dot(f_ref[...], x.astype(BF16)).astype(o_ref.dtype)


def _dft_rows(fmat, x3, col, width):
    m, k = fmat.shape
    n2 = x3.shape[1]
    nj = min(DFT_LANES // width, n2)
    return pl.pallas_call(
        functools.partial(_dft_rows_kernel, nj=nj),
        grid=(n2 // nj,),
        in_specs=[pl.BlockSpec((m, k), lambda j: (0, 0)),
                  pl.BlockSpec((k, nj, width), lambda j: (0, j, col))],
        out_specs=pl.BlockSpec((m, nj * width), lambda j: (0, j)),
        out_shape=jax.ShapeDtypeStruct((m, n2 * width), BF16),
        compiler_params=_params("parallel"),
        name="dft_rows",
    )(fmat, x3)


def _spec_kernel(ar_ref, ai_ref, gr_ref, gi_ref, kr_ref, ki_ref):
    ar, ai, gr, gi = ar_ref[...], ai_ref[...], gr_ref[...], gi_ref[...]
    kr_ref[...] = (_dot(gr, ar) - _dot(gi, ai)).astype(BF16)
    ki_ref[...] = (_dot(gr, ai) + _dot(gi, ar)).astype(BF16)


def _filter_spectrum(a4, gr, gi):
    _, _, n2, ch = a4.shape
    nh = gr.shape[0]
    ct = 512
    blk = lambda ri: pl.BlockSpec((None, None, n2, ct), lambda f, j: (ri, f, 0, j))
    gspec = pl.BlockSpec((None, n2, n2), lambda f, j: (f, 0, 0))
    ospec = pl.BlockSpec((None, n2, ct), lambda f, j: (f, 0, j))
    return pl.pallas_call(
        _spec_kernel,
        grid=(nh, ch // ct),
        in_specs=[blk(0), blk(1), gspec, gspec],
        out_specs=[ospec, ospec],
        out_shape=[jax.ShapeDtypeStruct((nh, n2, ch), BF16)] * 2,
        compiler_params=_params("parallel", "parallel"),
        name="filter_spectrum",
    )(a4, a4, gr, gi)


def _mid_kernel(ar_ref, ai_ref, gr_ref, gi_ref, grt_ref, git_ref, kr_ref, ki_ref, br_ref, bi_ref, *, nh):
    f = pl.program_id(0)

    @pl.when(f < nh)
    def _():
        ar, ai, gr, gi = ar_ref[...], ai_ref[...], gr_ref[...], gi_ref[...]
        xr = _dot(gr, ar) - _dot(gi, ai)
        xi = _dot(gr, ai) + _dot(gi, ar)
        kr, ki = kr_ref[...].astype(F32), ki_ref[...].astype(F32)
        yr = (xr * kr - xi * ki).astype(BF16)
        yi = (xr * ki + xi * kr).astype(BF16)
        grt, git = grt_ref[...], git_ref[...]
        br_ref[...] = (_dot(grt, yr) + _dot(git, yi)).astype(BF16)
        bi_ref[...] = (_dot(grt, yi) - _dot(git, yr)).astype(BF16)

    @pl.when(f >= nh)
    def _():
        br_ref[...] = jnp.zeros_like(br_ref)
        bi_ref[...] = jnp.zeros_like(bi_ref)


def _hyena_mid(a4, tabs, kf_r, kf_i, order):
    _, nf, n2, ch = a4.shape
    gr, gi, grt, git = tabs
    nh = gr.shape[0]
    fi = lambda f: jnp.minimum(f, nh - 1)
    blk = lambda ri: pl.BlockSpec((None, None, n2, ch), lambda f: (ri, fi(f), 0, 0))
    kspec = pl.BlockSpec((None, n2, ch), lambda f: (fi(f), 0, order))
    gspec = pl.BlockSpec((None, n2, n2), lambda f: (fi(f), 0, 0))
    ospec = pl.BlockSpec((None, n2, ch), lambda f: (f, 0, 0))
    return pl.pallas_call(
        functools.partial(_mid_kernel, nh=nh),
        grid=(nf,),
        in_specs=[blk(0), blk(1), gspec, gspec, gspec, gspec, kspec, kspec],
        out_specs=[ospec, ospec],
        out_shape=[jax.ShapeDtypeStruct((nf, n2, ch), BF16)] * 2,
        compiler_params=_params("parallel"),
        name="hyena_mid",
    )(a4, a4, gr, gi, grt, git, kf_r, kf_i)


def _inv_kernel(f_ref, br_ref, bi_ref, s_ref, bias_ref, z_ref, g_ref, o_ref, *, nf, nj, ch):
    acc = _dot(f_ref[:, 0:nf], br_ref[...]) + _dot(f_ref[:, nf:2 * nf], bi_ref[...])
    for jj in range(nj):
        y = acc[:, jj * ch:(jj + 1) * ch] * s_ref[...]
        o_ref[:, jj, :] = g_ref[:, jj, :] * (y + bias_ref[...] * z_ref[:, jj, :])


def _hyena_inverse(finv, b_r, b_i, scale, bias, z3, zcol, g3, gcol):
    t1, k2 = finv.shape
    nf = k2 // 2
    n2 = z3.shape[1]
    ch = HY_D
    nj = min(DFT_LANES // ch, n2)
    col = pl.BlockSpec((nf, nj * ch), lambda j: (0, j))
    row = pl.BlockSpec((1, ch), lambda j: (0, 0))
    return pl.pallas_call(
        functools.partial(_inv_kernel, nf=nf, nj=nj, ch=ch),
        grid=(n2 // nj,),
        in_specs=[pl.BlockSpec((t1, k2), lambda j: (0, 0)), col, col, row, row,
                  pl.BlockSpec((t1, nj, ch), lambda j: (0, j, zcol)),
                  pl.BlockSpec((t1, nj, ch), lambda j: (0, j, gcol))],
        out_specs=pl.BlockSpec((t1, nj, ch), lambda j: (0, j, 0)),
        out_shape=jax.ShapeDtypeStruct((t1, n2, ch), F32),
        compiler_params=_params("parallel"),
        name="hyena_inverse",
    )(finv, b_r, b_i, scale, bias, z3, g3)


def _hyena_nf(n):
    nh = (2 * n // FFT_N2) // 2 + 1
    return -(-nh // 16) * 16


def _dft_tables(n):
    n2 = FFT_N2
    n1 = 2 * n // n2
    tot = 2 * n
    two_pi = 2.0 * math.pi

    def cs(num, den):
        ang = (two_pi / den) * (num % den).astype(F32)
        return jnp.cos(ang), jnp.sin(ang)

    nh = n1 // 2 + 1
    nf = _hyena_nf(n)
    f1 = jnp.arange(nh, dtype=jnp.int32)
    t1 = jnp.arange(n1, dtype=jnp.int32)
    c1, s1 = cs(f1[:, None] * t1[None, :], n1)
    zrow = jnp.zeros((nf - nh, n1), F32)
    fwd_full = jnp.concatenate([c1, zrow, -s1, zrow], axis=0).astype(BF16)
    fwd_half = fwd_full[:, :n1 // 2]
    wgt = jnp.where((f1 == 0) | (f1 == n1 // 2), 1.0, 2.0)[:, None]
    zcol = jnp.zeros((n1 // 2, nf - nh), F32)
    inv = jnp.concatenate([(wgt * c1[:, :n1 // 2]).T, zcol, -(wgt * s1[:, :n1 // 2]).T, zcol],
                          axis=1).astype(BF16)
    t2 = jnp.arange(n2, dtype=jnp.int32)
    twr, twi = cs(f1[:, None] * t2[None, :], tot)
    fr, fi = cs(t2[:, None] * t2[None, :], n2)
    twi, fi = -twi, -fi
    gr = twr[:, None, :] * fr[None] - twi[:, None, :] * fi[None]
    gi = twr[:, None, :] * fi[None] + twi[:, None, :] * fr[None]
    tabs = (gr.astype(BF16), gi.astype(BF16),
            gr.transpose(0, 2, 1).astype(BF16), gi.transpose(0, 2, 1).astype(BF16))
    return fwd_full, fwd_half, inv, tabs


def _hyena_long(q, k2, nrm, hy_bias, tables):
    n = q.shape[0]
    n2 = FFT_N2
    n1 = 2 * n // n2
    fwd_full, fwd_half, inv, tabs = tables
    nf = _hyena_nf(n)
    oc = HY_ORDER * HY_D
    ak = _dft_rows(fwd_full, k2.reshape(n1, n2, oc), 0, oc).reshape(2, nf, n2, oc)
    kf_r, kf_i = _filter_spectrum(ak, tabs[0], tabs[1])
    q3 = q.reshape(n1 // 2, n2, 3 * HY_D)
    z3, zcol = q3, 0
    for o in range(HY_ORDER):
        a4 = _dft_rows(fwd_half, z3, zcol, HY_D).reshape(2, nf, n2, HY_D)
        b_r, b_i = _hyena_mid(a4, tabs, kf_r, kf_i, o)
        scale = 1.0 / (2.0 * n * nrm[:, o * HY_D:(o + 1) * HY_D])
        z3 = _hyena_inverse(inv, b_r.reshape(nf, n2 * HY_D), b_i.reshape(nf, n2 * HY_D), scale,
                            hy_bias[o][None, :], z3, zcol, q3, o + 1)
        zcol = 0
    return z3.reshape(n, HY_D)


def _hy_ctx_kernel(v_ref, x1_ref, x2_ref, k0_ref, k1_ref, n0_ref, n1_ref, bias_ref, o_ref, kf, zs, *, n):
    zs[...] = v_ref[...]
    for o, (k_ref, nr_ref, x_ref) in enumerate(((k0_ref, n0_ref, x1_ref), (k1_ref, n1_ref, x2_ref))):
        kf[0:n, :] = k_ref[n:2 * n, :]
        kf[n:2 * n, :] = k_ref[0:n, :]

        def body(s, acc):
            return acc + kf[pl.ds(n - s, n), :] * zs[pl.ds(s, 1), :]

        acc = lax.fori_loop(0, n, body, jnp.zeros((n, LANE), F32))
        z = zs[...]
        zs[...] = x_ref[...] * (acc / nr_ref[...] + bias_ref[o:o + 1, :] * z)
    o_ref[...] = zs[...]


def _hyena_ctx(q, k2, nrm, hy_bias):
    n = q.shape[0]
    nb = HY_D // LANE
    col = lambda c0: pl.BlockSpec((n, LANE), lambda j: (0, c0 + j))
    kcol = lambda c0: pl.BlockSpec((2 * n, LANE), lambda j: (0, c0 + j))
    ncol = lambda c0: pl.BlockSpec((1, LANE), lambda j: (0, c0 + j))
    return pl.pallas_call(
        functools.partial(_hy_ctx_kernel, n=n),
        grid=(nb,),
        in_specs=[col(0), col(nb), col(2 * nb), kcol(0), kcol(nb), ncol(0), ncol(nb),
                  pl.BlockSpec((HY_ORDER, LANE), lambda j: (0, j))],
        out_specs=pl.BlockSpec((n, LANE), lambda j: (0, j)),
        out_shape=jax.ShapeDtypeStruct((n, HY_D), F32),
        scratch_shapes=[pltpu.VMEM((2 * n, LANE), F32), pltpu.VMEM((n, LANE), F32)],
        compiler_params=_params("parallel"),
        name="hyena_ctx",
    )(q, q, q, k2, k2, nrm, nrm, hy_bias)


def _merge_kernel(ya_ref, xs_ref, yf_ref, yb_ref, z_ref, yc_ref, yd_ref, g_ref, h_ref,
                  dv_ref, ng_ref, g1_ref, lg_ref, lb_ref,
                  wa_ref, wb_ref, wc_ref, wd_ref, wo_ref, o_ref):
    y = xs_ref[...].astype(F32) * dv_ref[...] + yf_ref[...] + yb_ref[...]
    gz = y * _silu(z_ref[...].astype(F32))
    ssd = gz * lax.rsqrt(jnp.mean(gz * gz, -1, keepdims=True) + LN_EPS) * ng_ref[...]
    d = D_MODEL
    gate = lambda k: jnp.tanh(g_ref[:, k * d:(k + 1) * d].astype(F32)) + 1.0
    m = gate(0) * _dot(ya_ref[...].astype(BF16), wa_ref[...])
    m = m + gate(1) * _dot(ssd.astype(BF16), wb_ref[...])
    m = m + gate(2) * _dot(yc_ref[...].astype(BF16), wc_ref[...])
    m = m + gate(3) * _dot(yd_ref[...].astype(BF16), wd_ref[...])
    mix = _dot(m.astype(BF16), wo_ref[...])
    o_ref[...] = _layer_norm(DN_ALPHA * h_ref[...] + g1_ref[...] * mix, lg_ref[...], lb_ref[...])


def _merge(ya, xbc, ydir, p, yc, yd, h, dvec, ng, gate1, lg, lb, wa, wb, wc, wd, wo):
    n = h.shape[0]
    t = 256
    tok = lambda w, col=0: pl.BlockSpec((t, w), lambda i: (i, col))
    vec = lambda w: pl.BlockSpec((1, w), lambda i: (0, 0))
    mat = lambda r: pl.BlockSpec((r, D_MODEL), lambda i: (0, 0))
    return pl.pallas_call(
        _merge_kernel,
        grid=(n // t,),
        in_specs=[tok(CONF_D), tok(SSD_D),
                  pl.BlockSpec((None, t, SSD_D), lambda i: (0, i, 0)),
                  pl.BlockSpec((None, t, SSD_D), lambda i: (1, i, 0)),
                  tok(SSD_D, PZ // SSD_D), tok(HY_D), tok(SC_D), tok(N_BRANCH * D_MODEL, 0), tok(D_MODEL),
                  vec(SSD_D), vec(SSD_D), vec(D_MODEL), vec(D_MODEL), vec(D_MODEL),
                  mat(CONF_D), mat(SSD_D), mat(HY_D), mat(SC_D), mat(D_MODEL)],
        out_specs=tok(D_MODEL),
        out_shape=jax.ShapeDtypeStruct((n, D_MODEL), F32),
        compiler_params=_params("parallel"),
        name="merge",
    )(ya, xbc, ydir, ydir, p, yc, yd, p, h, dvec, ng, gate1, lg, lb, wa, wb, wc, wd, wo)


MOE_T = 256


def _stream_tiles(hs):
    tiles = [h.shape[0] // MOE_T for h in hs]
    first = [sum(tiles[:s]) for s in range(len(hs))]
    return tiles, first


def _stream_specs(tiles, first, width):
    return [pl.BlockSpec((MOE_T, width), lambda i, nt=nt, f=f: (jnp.clip(i - f, 0, nt - 1), 0))
            for nt, f in zip(tiles, first)]


def _stream_vec_spec(first):
    def index(i):
        s = 0
        for f in first[1:]:
            s = s + (i >= f).astype(jnp.int32)
        return (s, 0, 0)
    return pl.BlockSpec((None, 1, D_MODEL), index)


def _stream_tile(i, refs, first):
    x = refs[0][...]
    for r, f in zip(refs[1:], first[1:]):
        x = jnp.where(i >= f, r[...], x)
    return x


def _router_kernel(*refs, first):
    ns = len(first)
    h_refs = refs[:ns]
    sh_ref, sc_ref, wh_ref, wl_ref, b_ref, sel_ref, cnt_ref, selt_ref = refs[ns:]
    i = pl.program_id(0)

    @pl.when(i == 0)
    def _():
        cnt_ref[...] = jnp.zeros_like(cnt_ref)

    u = _stream_tile(i, h_refs, first) * (1.0 + sc_ref[...]) + sh_ref[...]
    u_hi = u.astype(BF16)
    u_lo = (u - u_hi.astype(F32)).astype(BF16)
    lg = _dot(u_hi, wh_ref[...]) + _dot(u_lo, wh_ref[...]) + _dot(u_hi, wl_ref[...]) + b_ref[...]
    lane = lax.broadcasted_iota(jnp.int32, lg.shape, 1).astype(F32)
    neg = -1e30
    big = 1e9
    gl = jnp.where(lane < MOE_GROUPS, lg, neg)
    gmax = jnp.max(gl, -1, keepdims=True)
    gsel = jnp.min(jnp.where(gl == gmax, lane, big), -1, keepdims=True)
    gprob = 1.0 / jnp.sum(jnp.where(lane < MOE_GROUPS, jnp.exp(lg - gmax), 0.0), -1, keepdims=True)
    lo = MOE_GROUPS + gsel * MOE_EPG
    el = jnp.where(jnp.abs(lane - lo - (MOE_EPG - 1) / 2.0) < MOE_EPG / 2.0, lg, neg)
    m1 = jnp.max(el, -1, keepdims=True)
    i1 = jnp.min(jnp.where(el == m1, lane, big), -1, keepdims=True)
    el2 = jnp.where(lane == i1, neg, el)
    m2 = jnp.max(el2, -1, keepdims=True)
    i2 = jnp.min(jnp.where(el2 == m2, lane, big), -1, keepdims=True)
    t = jnp.exp(m2 - m1)
    w1 = gprob / (1.0 + t)
    w2 = gprob * t / (1.0 + t)
    oh1 = jnp.where(lane == i1, 1.0, 0.0)
    oh2 = jnp.where(lane == i2, 1.0, 0.0)
    oh = oh1 + oh2
    tt = lg.shape[0]
    li = lax.broadcasted_iota(jnp.int32, (tt, tt), 0)
    si = lax.broadcasted_iota(jnp.int32, (tt, tt), 1)
    before = _dot(jnp.where(li > si, 1.0, 0.0).astype(BF16), oh.astype(BF16)) + cnt_ref[...]
    r1 = jnp.sum(oh1 * before, -1, keepdims=True)
    r2 = jnp.sum(oh2 * before, -1, keepdims=True)
    cnt_ref[...] += jnp.sum(oh, axis=0, keepdims=True)
    cols = (i1 - MOE_GROUPS, i2 - MOE_GROUPS, w1, w2, r1, r2)
    sel = jnp.zeros_like(lg)
    for k, v in enumerate(cols):
        sel = jnp.where(lane == k, v, sel)
    sel_ref[...] = sel
    selt_ref[...] = sel.T[0:SUBLANE, :]


def _router(hs, shift, scale, wr, br):
    tiles, first = _stream_tiles(hs)
    t = MOE_T
    n = t * sum(tiles)
    vec = lambda w: pl.BlockSpec((1, w), lambda i: (0, 0))
    wr_hi = wr.astype(BF16)
    wr_lo = (wr - wr_hi.astype(F32)).astype(BF16)
    wspec = pl.BlockSpec((D_MODEL, LANE), lambda i: (0, 0))
    return pl.pallas_call(
        functools.partial(_router_kernel, first=first),
        grid=(sum(tiles),),
        in_specs=_stream_specs(tiles, first, D_MODEL) + [
            _stream_vec_spec(first), _stream_vec_spec(first), wspec, wspec, vec(LANE)],
        out_specs=[pl.BlockSpec((t, LANE), lambda i: (i, 0)), vec(LANE),
                   pl.BlockSpec((SUBLANE, t), lambda i: (0, i))],
        out_shape=[jax.ShapeDtypeStruct((n, LANE), F32), jax.ShapeDtypeStruct((1, LANE), F32),
                   jax.ShapeDtypeStruct((SUBLANE, n), F32)],
        compiler_params=_params("arbitrary"),
        name="router",
    )(*hs, shift, scale, wr_hi, wr_lo, br)


ROW_WORDS = D_MODEL // 2


def _pack_rows(x):
    c = x.shape[1] // 2
    bits = lambda v: lax.bitcast_convert_type(v.astype(BF16).astype(F32), jnp.uint32)
    return bits(x[:, :c]) | (bits(x[:, c:]) >> 16)


def _unpack_rows(w):
    hi = lax.bitcast_convert_type(w & jnp.uint32(0xFFFF0000), F32)
    lo = lax.bitcast_convert_type(w << 16, F32)
    return jnp.concatenate([hi, lo], axis=1)


def _dispatch_kernel(dst_ref, *refs, first, nt):
    ns = len(first)
    h_refs = refs[:ns]
    sh_ref, sc_ref, zero_hbm, xin_hbm, ubuf, sem = refs[ns:]
    del zero_hbm
    t = MOE_T
    i = pl.program_id(0)
    slot = i % 2

    def wait_slot(s):
        for _ in range(MOE_TOP_K):
            pltpu.make_async_copy(ubuf.at[s], xin_hbm.at[pl.ds(0, t), :], sem.at[s]).wait()

    @pl.when(i >= 2)
    def _():
        wait_slot(slot)

    ubuf[slot] = _pack_rows(_stream_tile(i, h_refs, first) * (1.0 + sc_ref[...]) + sh_ref[...])

    def issue(r, carry):
        for k in range(MOE_TOP_K):
            pltpu.make_async_copy(ubuf.at[slot, pl.ds(r, 1), :],
                                  xin_hbm.at[pl.ds(dst_ref[0, 0, k * t + r], 1), :], sem.at[slot]).start()
        return carry

    lax.fori_loop(0, t, issue, 0, unroll=16)

    @pl.when(i == nt - 1)
    def _():
        wait_slot(slot)
        if nt > 1:
            wait_slot(1 - slot)


def _dispatch(pos_t, hs, shift, scale, n_rows):
    tiles, first = _stream_tiles(hs)
    t = MOE_T
    nt = sum(tiles)
    return pl.pallas_call(
        functools.partial(_dispatch_kernel, first=first, nt=nt),
        grid=(nt,),
        in_specs=([pl.BlockSpec((1, 1, MOE_TOP_K * t), lambda i: (i, 0, 0), memory_space=pltpu.SMEM)]
                  + _stream_specs(tiles, first, D_MODEL)
                  + [_stream_vec_spec(first), _stream_vec_spec(first), pl.BlockSpec(memory_space=pl.ANY)]),
        out_specs=pl.BlockSpec(memory_space=pl.ANY),
        out_shape=jax.ShapeDtypeStruct((n_rows, ROW_WORDS), jnp.uint32),
        scratch_shapes=[pltpu.VMEM((2, t, ROW_WORDS), jnp.uint32), pltpu.SemaphoreType.DMA((2,))],
        input_output_aliases={3 + len(hs): 0},
        compiler_params=_params("arbitrary"),
        name="dispatch",
    )(pos_t, *hs, shift, scale, jnp.zeros((n_rows, ROW_WORDS), jnp.uint32))


def _expert_kernel(be_ref, nu_ref, x_ref, wg_ref, wu_ref, wd_ref, o_ref, wgb, wub, wdb):
    b = pl.program_id(0)

    @pl.when((b == 0) | (be_ref[b] != be_ref[jnp.maximum(b - 1, 0)]))
    def _():
        wgb[...] = wg_ref[...].astype(BF16)
        wub[...] = wu_ref[...].astype(BF16)
        wdb[...] = wd_ref[...].astype(BF16)

    @pl.when(b < nu_ref[0])
    def _():
        x = _unpack_rows(x_ref[...]).astype(BF16)
        hid = _silu(_dot(x, wgb[...])) * _dot(x, wub[...])
        o_ref[...] = _pack_rows(_dot(hid.astype(BF16), wdb[...]))

    @pl.when(b >= nu_ref[0])
    def _():
        o_ref[...] = jnp.zeros_like(o_ref)


def _experts(xin, block_e, n_used, wg, wu, wd, layer):
    n_blocks = block_e.shape[0]
    gs = pltpu.PrefetchScalarGridSpec(
        num_scalar_prefetch=2,
        grid=(n_blocks,),
        in_specs=[pl.BlockSpec((MOE_ROWS, ROW_WORDS), lambda b, be, nu: (b, 0)),
                  pl.BlockSpec((None, None, D_MODEL, MOE_FF), lambda b, be, nu: (layer, be[b], 0, 0)),
                  pl.BlockSpec((None, None, D_MODEL, MOE_FF), lambda b, be, nu: (layer, be[b], 0, 0)),
                  pl.BlockSpec((None, None, MOE_FF, D_MODEL), lambda b, be, nu: (layer, be[b], 0, 0))],
        out_specs=pl.BlockSpec((MOE_ROWS, ROW_WORDS), lambda b, be, nu: (b, 0)),
        scratch_shapes=[pltpu.VMEM((D_MODEL, MOE_FF), BF16), pltpu.VMEM((D_MODEL, MOE_FF), BF16),
                        pltpu.VMEM((MOE_FF, D_MODEL), BF16)],
    )
    return pl.pallas_call(
        _expert_kernel,
        grid_spec=gs,
        out_shape=jax.ShapeDtypeStruct((n_blocks * MOE_ROWS, ROW_WORDS), jnp.uint32),
        compiler_params=_params("arbitrary"),
        name="experts",
    )(block_e, n_used, xin, wg, wu, wd)


def _combine_kernel(pos_ref, posn_ref, y_hbm, *refs, first, tiles):
    ns = len(first)
    nt = sum(tiles)
    h_refs = refs[:ns]
    sel_ref, g2_ref, lg_ref, lb_ref = refs[ns:ns + 4]
    o_refs = refs[ns + 4:2 * ns + 4]
    ybuf, sem = refs[2 * ns + 4:]
    t = MOE_T
    i = pl.program_id(0)
    slot = i % 2

    def gather(p_ref, s):
        def issue(r, carry):
            pltpu.make_async_copy(y_hbm.at[pl.ds(p_ref[0, 0, r], 1), :], ybuf.at[s, pl.ds(r, 1), :],
                                  sem.at[s]).start()
            return carry
        lax.fori_loop(0, MOE_TOP_K * t, issue, 0, unroll=32)

    @pl.when(i == 0)
    def _():
        gather(pos_ref, 0)

    @pl.when(i + 1 < nt)
    def _():
        gather(posn_ref, 1 - slot)

    pltpu.make_async_copy(y_hbm.at[pl.ds(0, MOE_TOP_K * t), :], ybuf.at[slot], sem.at[slot]).wait()
    ffn = (sel_ref[:, 2:3] * _unpack_rows(ybuf[slot, 0:t, :])
           + sel_ref[:, 3:4] * _unpack_rows(ybuf[slot, t:2 * t, :]))
    out = _layer_norm(DN_ALPHA * _stream_tile(i, h_refs, first) + g2_ref[...] * ffn, lg_ref[...], lb_ref[...])
    for s in range(ns):
        @pl.when((i >= first[s]) & (i < first[s] + tiles[s]))
        def _(s=s):
            o_refs[s][...] = out


def _combine(y, pos_t, hs, sel, gate2, lg, lb):
    tiles, first = _stream_tiles(hs)
    t = MOE_T
    nt = sum(tiles)
    vec = pl.BlockSpec((1, D_MODEL), lambda i: (0, 0))
    return pl.pallas_call(
        functools.partial(_combine_kernel, first=first, tiles=tiles),
        grid=(nt,),
        in_specs=([pl.BlockSpec((1, 1, MOE_TOP_K * t), lambda i: (i, 0, 0), memory_space=pltpu.SMEM),
                   pl.BlockSpec((1, 1, MOE_TOP_K * t), lambda i: (jnp.minimum(i + 1, nt - 1), 0, 0),
                                memory_space=pltpu.SMEM),
                   pl.BlockSpec(memory_space=pl.ANY)]
                  + _stream_specs(tiles, first, D_MODEL)
                  + [pl.BlockSpec((t, LANE), lambda i: (i, 0)), _stream_vec_spec(first), vec, vec]),
        out_specs=_stream_specs(tiles, first, D_MODEL),
        out_shape=[jax.ShapeDtypeStruct(h.shape, F32) for h in hs],
        scratch_shapes=[pltpu.VMEM((2, MOE_TOP_K * t, ROW_WORDS), jnp.uint32), pltpu.SemaphoreType.DMA((2,))],
        compiler_params=_params("arbitrary"),
        name="combine",
    )(pos_t, pos_t, y, *hs, sel, gate2, lg, lb)


def _moe(hs, shift, scale, gate2, lg, lb, wr, br, wg, wu, wd, layer):
    t = MOE_T
    n = sum(h.shape[0] for h in hs)
    nt = n // t
    sel, cnt, selt = _router(hs, shift, scale, wr, br)
    counts = cnt[0, MOE_GROUPS:MOE_GROUPS + MOE_EXPERTS].astype(jnp.int32)
    padded = (counts + MOE_ROWS - 1) // MOE_ROWS * MOE_ROWS
    pad_end = jnp.cumsum(padded)
    pad_start = pad_end - padded
    n_blocks = (n * MOE_TOP_K + MOE_EXPERTS * (MOE_ROWS - 1) + MOE_ROWS - 1) // MOE_ROWS
    blk_row = jnp.arange(n_blocks, dtype=jnp.int32) * MOE_ROWS
    block_e = jnp.minimum(jnp.sum((blk_row[:, None] >= pad_end[None, :]).astype(jnp.int32), axis=1),
                          MOE_EXPERTS - 1)
    n_used = (pad_end[-1:] // MOE_ROWS).astype(jnp.int32)
    e_kt = selt[0:MOE_TOP_K].astype(jnp.int32)
    ids = jnp.arange(MOE_EXPERTS, dtype=jnp.int32)[None, :, None]
    start_kt = jnp.sum(jnp.where(e_kt[:, None, :] == ids, pad_start[None, :, None], 0), axis=1)
    pos_kt = start_kt + selt[4:4 + MOE_TOP_K].astype(jnp.int32)
    pos_t = pos_kt.reshape(MOE_TOP_K, nt, t).transpose(1, 0, 2).reshape(nt, 1, MOE_TOP_K * t)
    xin = _dispatch(pos_t, hs, shift, scale, n_blocks * MOE_ROWS)
    y = _experts(xin, block_e, n_used, wg, wu, wd, layer)
    return _combine(y, pos_t, hs, sel, gate2, lg, lb)


def _mixer(h, mod, lw, ssd_init, tables, *, latent, need_mix):
    n = h.shape[0]
    p, dt_raw = _inproj(h, mod[0], mod[1], lw["w_in"], lw["layer"])
    xbc = _conv3(p, PX, SSD_XBC, lw["ssd_conv_w"], lw["ssd_conv_b"], silu=True, out_dtype=BF16)
    ydir, finals = _ssd_scan(xbc, dt_raw, lw["ssd_dt_bias"], lw["ssd_a_log"], ssd_init)
    if not need_mix:
        return None, finals
    conf_w = (lw["conf_dw_w"], lw["conf_dw_b"], lw["conf_ln_g"], lw["conf_ln_b"])
    ya = _conformer_grid(p, *conf_w) if latent else _conformer(p, *conf_w, dil=1)
    q = _conv3(p, PC, 3 * HY_D, lw["hy_short_w"], lw["hy_short_b"], silu=False, out_dtype=F32)
    k2, nrm = _hyena_filters(lw["feat_lat" if latent else "feat_ctx"], lw["hy_filter"], n)
    if latent:
        yc = _hyena_long(q, k2, nrm, lw["hy_bias"], tables)
    else:
        yc = _hyena_ctx(q, k2, nrm, lw["hy_bias"])
    yd = _gated_conv(p, lw["sc_conv_w"])
    h = _merge(ya, xbc, ydir, p, yc, yd, h, lw["ssd_dvec"], lw["ssd_norm_g"], mod[2], lw["ln_g0"], lw["ln_b0"],
               lw["w_branch_a"], lw["w_branch_b"], lw["w_branch_c"], lw["w_branch_d"], lw["w_out"])
    return h, finals


def _positional_features(n):
    tau = jnp.arange(2 * n, dtype=jnp.int32)
    lag = jnp.where(tau < n, tau, 2 * n - tau).astype(F32)[:, None]
    t01 = lag * (1.0 / (n - 1))
    omega = (2.0 * math.pi / n) * lag
    bands = jnp.linspace(1e-4, HY_BANDS - 1, HY_BANDS, dtype=F32)
    featx = jnp.concatenate([t01, jnp.cos(bands * omega), -jnp.sin(bands * omega)], axis=-1)
    return _pad_lanes(featx)


RELAYOUT_W = 256


def _relayout_kernel(blk_ref, sh_ref, valid_ref, half_ref, a_ref, b_ref, o_ref):
    j = pl.program_id(1)
    sh = sh_ref[j]
    rot = (LANE - sh) % LANE
    scale = jnp.where(half_ref[j] == 1, 0.5, 1.0)
    lane = lax.broadcasted_iota(jnp.int32, (a_ref.shape[0], LANE), 1)
    nk = RELAYOUT_W // LANE
    chunk = lambda k: (a_ref if k < nk else b_ref)[:, (k % nk) * LANE:(k % nk + 1) * LANE]
    for k in range(nk):
        x = jnp.where(lane < LANE - sh, pltpu.roll(chunk(k), rot, 1), pltpu.roll(chunk(k + 1), rot, 1))
        o_ref[:, k * LANE:(k + 1) * LANE] = jnp.where(lane + k * LANE < valid_ref[j], x * scale, 0.0).astype(BF16)


def _relayout_w_in(w_in):
    w = RELAYOUT_W
    depth, d, total = w_in.shape
    ob = OFF_B
    segs = [(PG, OFF_G, N_BRANCH * D_MODEL, 1), (PA, OFF_A, 2 * CONF_D, 0), (PC, OFF_C, 3 * HY_D, 0),
            (PD, OFF_D, 3 * SC_D, 0), (PX, ob + SSD_D, SSD_XBC, 0), (PZ, ob, SSD_D, 0),
            (PDT, ob + SSD_D + SSD_XBC, 2 * SSD_HEADS, 0)]
    ntile = NP // w
    blk, sh, valid, half = ([0] * ntile for _ in range(4))
    for p0, s0, width, hv in segs:
        assert p0 % w == 0
        for j in range(p0 // w, -(-(p0 + width) // w)):
            src = s0 + j * w - p0
            blk[j], sh[j], valid[j], half[j] = src // w, src % w, min(w, p0 + width - j * w), hv
            assert sh[j] < LANE
    last = (total - 1) // w
    tables = [jnp.asarray(v, jnp.int32) for v in (blk, sh, valid, half)]
    gs = pltpu.PrefetchScalarGridSpec(
        num_scalar_prefetch=4,
        grid=(depth, ntile),
        in_specs=[pl.BlockSpec((None, d, w), lambda l, j, blk, *_: (l, 0, blk[j])),
                  pl.BlockSpec((None, d, w), lambda l, j, blk, *_: (l, 0, jnp.minimum(blk[j] + 1, last)))],
        out_specs=pl.BlockSpec((None, d, w), lambda l, j, *_: (l, 0, j)),
    )
    return pl.pallas_call(
        _relayout_kernel,
        grid_spec=gs,
        out_shape=jax.ShapeDtypeStruct((depth, d, NP), BF16),
        compiler_params=_params("parallel", "parallel"),
        name="relayout_w_in",
    )(*tables, w_in, w_in)


def _pad_lanes(v):
    return jnp.pad(v, ((0, 0), (0, LANE - v.shape[-1])))


def kernel(x, c, ctx, c_ctx, w_mod, b_mod, ln_g, ln_b, w_in, conf_dw_w, conf_dw_b, conf_ln_g, conf_ln_b,
           ssd_conv_w, ssd_conv_b, ssd_a_log, ssd_dt_bias, ssd_d, ssd_norm_g, hy_short_w, hy_short_b,
           hy_w1, hy_b1, hy_w2, hy_b2, hy_freq, hy_w3, hy_bias, sc_conv_w, w_branch_a, w_branch_b,
           w_branch_c, w_branch_d, w_out, rt_group_w, rt_group_b, rt_expert_w, rt_expert_b,
           ex_w_gate, ex_w_up, ex_w_down):
    assert x.shape[0] == 1 and ctx.shape[0] == 1
    n_lat, n_ctx = x.shape[1], ctx.shape[1]
    depth = w_in.shape[0]

    cv = jnp.concatenate([c, c_ctx[None, :], jnp.zeros((SUBLANE - 2, D_MODEL), F32)], axis=0)
    mods = _mod_vectors(cv, w_mod, b_mod)
    w_in_p = _relayout_w_in(w_in)
    tables = _dft_tables(n_lat)
    feat_lat = _positional_features(n_lat)
    feat_ctx = _positional_features(n_ctx)
    deltas = jnp.abs(jnp.linspace(HY_MIN_DECAY, HY_MAX_DECAY, HY_N_FILT, dtype=F32))
    deltas_d = deltas.reshape(HY_ORDER, 2, HY_D).transpose(1, 0, 2).reshape(2, 1, HY_ORDER * HY_D)
    router_w = jnp.concatenate([rt_group_w, rt_expert_w,
                                jnp.zeros((depth, D_MODEL, LANE - MOE_GROUPS - MOE_EXPERTS), F32)], axis=-1)
    router_b = jnp.concatenate([rt_group_b, rt_expert_b,
                                jnp.zeros((depth, LANE - MOE_GROUPS - MOE_EXPERTS), F32)], axis=-1)
    ssd_zero = jnp.zeros((2,) + SSD_STATE_SHAPE, F32)

    h_lat, h_ctx = x[0], ctx[0]
    for l in range(depth):
        row = lambda v: v[None, :]
        lw = dict(
            w_in=w_in_p, layer=l, conf_dw_w=conf_dw_w[l], conf_dw_b=row(conf_dw_b[l]), conf_ln_g=row(conf_ln_g[l]),
            conf_ln_b=row(conf_ln_b[l]), ssd_conv_w=ssd_conv_w[l], ssd_conv_b=row(ssd_conv_b[l]),
            ssd_a_log=_pad_lanes(ssd_a_log[l].reshape(1, -1)), ssd_dt_bias=_pad_lanes(ssd_dt_bias[l].reshape(1, -1)),
            ssd_dvec=row(jnp.repeat(ssd_d[l], SSD_HEAD_DIM)), ssd_norm_g=row(ssd_norm_g[l]),
            hy_short_w=hy_short_w[l], hy_short_b=row(hy_short_b[l]),
            hy_filter=_filter_weights(hy_w1[l], hy_b1[l], hy_w2[l], hy_b2[l], hy_freq[l], hy_w3[l], deltas_d),
            hy_bias=hy_bias[l], feat_lat=feat_lat, feat_ctx=feat_ctx,
            sc_conv_w=sc_conv_w[l], ln_g0=row(ln_g[l, 0]), ln_b0=row(ln_b[l, 0]),
            w_branch_a=w_branch_a[l].astype(BF16), w_branch_b=w_branch_b[l].astype(BF16),
            w_branch_c=w_branch_c[l].astype(BF16), w_branch_d=w_branch_d[l].astype(BF16),
            w_out=(0.5 * w_out[l]).astype(BF16))
        moe_w = (router_w[l], row(router_b[l]), ex_w_gate, ex_w_up, ex_w_down, l)
        last = l == depth - 1
        d = D_MODEL
        mod_lat = [mods[l, 0:1, k * d:(k + 1) * d] for k in range(6)]
        mod_ctx = [mods[l, 1:2, k * d:(k + 1) * d] for k in range(6)]

        mix_ctx, ctx_states = _mixer(h_ctx, mod_ctx, lw, ssd_zero, None, latent=False, need_mix=not last)
        h_lat, _ = _mixer(h_lat, mod_lat, lw, ctx_states, tables, latent=True, need_mix=True)
        streams = [(h_lat, mod_lat)] if last else [(h_lat, mod_lat), (mix_ctx, mod_ctx)]
        vecs = [jnp.stack([m[k] for _, m in streams]) for k in (3, 4, 5)]
        outs = _moe([h for h, _ in streams], *vecs, row(ln_g[l, 1]), row(ln_b[l, 1]), *moe_w)
        h_lat = outs[0]
        if not last:
            h_ctx = outs[1]
    return h_lat[None]
```

```python
import functools
import math

import jax
import jax.numpy as jnp
from jax import lax
from jax.experimental import pallas as pl
from jax.experimental.pallas import tpu as pltpu

F32 = jnp.float32
BF16 = jnp.bfloat16
HIGHEST = lax.Precision.HIGHEST

D_MODEL = 1024
DEPTH = 4
GRID_W = 64
CONF_D = 512
CONF_K = 31
SSD_D = 768
SSD_HEADS = 12
SSD_HEAD_DIM = 64
SSD_GROUPS = 4
SSD_HPG = SSD_HEADS // SSD_GROUPS
SSD_STATE = 128
SSD_CHUNK = 128
SSD_BC = SSD_GROUPS * SSD_STATE
SSD_XBC = SSD_D + 2 * SSD_BC
SSD_PROJ = SSD_D + SSD_XBC + 2 * SSD_HEADS
HY_D = 512
HY_ORDER = 2
HY_EMB = 33
HY_BANDS = (HY_EMB - 1) // 2
HY_HID = 64
HY_N_FILT = HY_ORDER * 2 * HY_D
HY_MIN_DECAY = math.log(1e-2) / 1.5
HY_MAX_DECAY = math.log(1e-2) / 0.3
SC_D = 512
N_BRANCH = 4
OFF_A = 0
OFF_B = OFF_A + 2 * CONF_D
OFF_C = OFF_B + SSD_PROJ
OFF_D = OFF_C + 3 * HY_D
OFF_G = OFF_D + 3 * SC_D
MOE_GROUPS = 4
MOE_EPG = 8
MOE_EXPERTS = MOE_GROUPS * MOE_EPG
MOE_TOP_K = 2
MOE_FF = 512
DN_ALPHA = (2 * DEPTH) ** 0.25
LN_EPS = 1e-5

PG = 0
PA = PG + N_BRANCH * D_MODEL
PC = PA + 2 * CONF_D
PD = PC + 3 * HY_D
PX = PD + 3 * SC_D
PZ = PX + SSD_XBC
PDT = PZ + SSD_D
INPROJ_TN = 1024
NP = -(-(PDT + 128) // INPROJ_TN) * INPROJ_TN

LANE = 128
SUBLANE = 8
FFT_N2 = 256
MOE_ROWS = 256
VMEM_LIMIT = 48 * 1024 * 1024


def _params(*sem):
    return pltpu.CompilerParams(dimension_semantics=sem, vmem_limit_bytes=VMEM_LIMIT)


def _sigmoid(x):
    return 0.5 * jnp.tanh(0.5 * x) + 0.5


def _silu(x):
    h = 0.5 * x
    return h * jnp.tanh(h) + h


def _layer_norm(x, g, b):
    mu = jnp.mean(x, -1, keepdims=True)
    xc = x - mu
    var = jnp.mean(xc * xc, -1, keepdims=True)
    return xc * lax.rsqrt(var + LN_EPS) * g + b


def _dot(a, b):
    return jnp.dot(a, b, preferred_element_type=F32)


def _mod_kernel(cv_ref, w_ref, b_ref, o_ref):
    o_ref[...] = jnp.dot(_silu(cv_ref[...]), w_ref[...], precision=HIGHEST,
                         preferred_element_type=F32) + b_ref[...]


def _mod_vectors(cv, w_mod, b_mod):
    tn = 1536
    return pl.pallas_call(
        _mod_kernel,
        grid=(DEPTH, 6 * D_MODEL // tn),
        in_specs=[pl.BlockSpec((SUBLANE, D_MODEL), lambda l, j: (0, 0)),
                  pl.BlockSpec((None, D_MODEL, tn), lambda l, j: (l, 0, j)),
                  pl.BlockSpec((None, 1, tn), lambda l, j: (l, 0, j))],
        out_specs=pl.BlockSpec((None, SUBLANE, tn), lambda l, j: (l, 0, j)),
        out_shape=jax.ShapeDtypeStruct((DEPTH, SUBLANE, 6 * D_MODEL), F32),
        compiler_params=_params("parallel", "parallel"),
        name="mod_vectors",
    )(cv, w_mod, b_mod.reshape(DEPTH, 1, 6 * D_MODEL))


def _inproj_kernel(x_ref, sh_ref, sc_ref, w_ref, o_ref, dt_ref, xb_ref, *, nj):
    j = pl.program_id(1)

    @pl.when(j == 0)
    def _():
        xb_ref[...] = (x_ref[...] * (1.0 + sc_ref[...]) + sh_ref[...]).astype(BF16)

    res = _dot(xb_ref[...], w_ref[...])
    o_ref[...] = res.astype(BF16)

    @pl.when(j == nj - 1)
    def _():
        off = PDT - (nj - 1) * INPROJ_TN
        dt_ref[...] = res[:, off:off + LANE]


def _inproj(h, shift, scale, w, layer):
    n = h.shape[0]
    tm = min(n, 2048)
    tn = INPROJ_TN
    nj = NP // tn
    assert PDT >= (nj - 1) * tn
    return pl.pallas_call(
        functools.partial(_inproj_kernel, nj=nj),
        grid=(n // tm, nj),
        in_specs=[pl.BlockSpec((tm, D_MODEL), lambda i, j: (i, 0)),
                  pl.BlockSpec((1, D_MODEL), lambda i, j: (0, 0)),
                  pl.BlockSpec((1, D_MODEL), lambda i, j: (0, 0)),
                  pl.BlockSpec((None, D_MODEL, tn), lambda i, j: (layer, 0, j))],
        out_specs=[pl.BlockSpec((tm, tn), lambda i, j: (i, j)),
                   pl.BlockSpec((tm, LANE), lambda i, j: (i, 0))],
        out_shape=[jax.ShapeDtypeStruct((n, NP), BF16), jax.ShapeDtypeStruct((n, LANE), F32)],
        scratch_shapes=[pltpu.VMEM((tm, D_MODEL), BF16)],
        compiler_params=_params("parallel", "arbitrary"),
        name="inproj",
    )(h, shift, scale, w)


HALO_ROWS = 16


def _conv3_tile(x, prev_row, next_row, w_ref, store):
    t = x.shape[0]
    e = HALO_ROWS
    w0, w1, w2 = w_ref[0:1, :], w_ref[1:2, :], w_ref[2:3, :]
    y = w0 * pltpu.roll(x, 1, 0) + w1 * x + w2 * pltpu.roll(x, t - 1, 0)
    store(slice(0, t), y)
    row = lax.broadcasted_iota(jnp.int32, (e, x.shape[1]), 0)
    store(slice(0, e), y[0:e] + jnp.where(row == 0, w0 * (prev_row - x[t - 1:t]), 0.0))
    store(slice(t - e, t), y[t - e:t] + jnp.where(row == e - 1, w2 * (next_row - x[0:1]), 0.0))


def _conv3_kernel(cur_ref, prev_ref, next_ref, w_ref, b_ref, o_ref, *, silu, nt):
    i = pl.program_id(0)
    x = cur_ref[...].astype(F32)
    pv = jnp.where(i > 0, prev_ref[HALO_ROWS - 1:HALO_ROWS, :].astype(F32), 0.0)
    nx = jnp.where(i < nt - 1, next_ref[0:1, :].astype(F32), 0.0)

    def store(rows, y):
        y = y + b_ref[...]
        o_ref[rows, :] = (_silu(y) if silu else y).astype(o_ref.dtype)

    _conv3_tile(x, pv, nx, w_ref, store)


def _halo_specs(t, ct, n, col0):
    rb = t // HALO_ROWS
    last = n // HALO_ROWS - 1
    return [pl.BlockSpec((t, ct), lambda i, j: (i, col0 + j)),
            pl.BlockSpec((HALO_ROWS, ct), lambda i, j: (jnp.maximum(i * rb - 1, 0), col0 + j)),
            pl.BlockSpec((HALO_ROWS, ct), lambda i, j: (jnp.minimum((i + 1) * rb, last), col0 + j))]


def _conv3(p, col, width, w, b, *, silu, out_dtype):
    n = p.shape[0]
    t = min(n, 1024)
    ct = 256
    nt = n // t
    return pl.pallas_call(
        functools.partial(_conv3_kernel, silu=silu, nt=nt),
        grid=(nt, width // ct),
        in_specs=_halo_specs(t, ct, n, col // ct) + [
            pl.BlockSpec((3, ct), lambda i, j: (0, j)),
            pl.BlockSpec((1, ct), lambda i, j: (0, j))],
        out_specs=pl.BlockSpec((t, ct), lambda i, j: (i, j)),
        out_shape=jax.ShapeDtypeStruct((n, width), out_dtype),
        compiler_params=_params("parallel", "parallel"),
        name="conv3",
    )(p, p, p, w, b)


def _gconv_kernel(bg_ref, cc_ref, cp_ref, cn_ref, xc_ref, xp_ref, xn_ref, w_ref, o_ref, *, nt):
    i = pl.program_id(0)
    f = lambda v: v.astype(F32)
    last = slice(HALO_ROWS - 1, HALO_ROWS)
    x = f(cc_ref[...]) * f(xc_ref[...])
    pv = jnp.where(i > 0, f(cp_ref[last, :]) * f(xp_ref[last, :]), 0.0)
    nx = jnp.where(i < nt - 1, f(cn_ref[0:1, :]) * f(xn_ref[0:1, :]), 0.0)

    def store(rows, y):
        o_ref[rows, :] = f(bg_ref[rows, :]) * y

    _conv3_tile(x, pv, nx, w_ref, store)


def _gated_conv(p, w):
    n = p.shape[0]
    t = min(n, 1024)
    ct = 256
    nt = n // t
    nb = SC_D // ct
    return pl.pallas_call(
        functools.partial(_gconv_kernel, nt=nt),
        grid=(nt, nb),
        in_specs=([pl.BlockSpec((t, ct), lambda i, j: (i, PD // ct + j))]
                  + _halo_specs(t, ct, n, PD // ct + nb)
                  + _halo_specs(t, ct, n, PD // ct + 2 * nb)
                  + [pl.BlockSpec((3, ct), lambda i, j: (0, j))]),
        out_specs=pl.BlockSpec((t, ct), lambda i, j: (i, j)),
        out_shape=jax.ShapeDtypeStruct((n, SC_D), F32),
        compiler_params=_params("parallel", "parallel"),
        name="gated_conv",
    )(p, p, p, p, p, p, p, w)


CONF_RB = 64


def _conf_kernel(vc, gc, vp, gp, vn, gn, w_ref, b_ref, lg_ref, lb_ref, o_ref, buf, *, t, halo, dil, nt):
    i = pl.program_id(0)
    glu = lambda v, g: v.astype(F32) * _sigmoid(g.astype(F32))
    buf[halo:halo + t, :] = glu(vc[...], gc[...])
    buf[0:halo, :] = jnp.where(i > 0, glu(vp[t - halo:t, :], gp[t - halo:t, :]), 0.0)
    buf[halo + t:halo + t + halo, :] = jnp.where(i < nt - 1, glu(vn[0:halo, :], gn[0:halo, :]), 0.0)

    def block(r0):
        acc = jnp.zeros((CONF_RB, CONF_D), F32)
        for j in range(CONF_K):
            off = halo + (j - CONF_K // 2) * dil
            acc = acc + w_ref[j:j + 1, :] * buf[pl.ds(r0 + off, CONF_RB), :]
        v = _layer_norm(acc + b_ref[...], lg_ref[...], lb_ref[...])
        o_ref[pl.ds(r0, CONF_RB), :] = _silu(v)

    if dil % CONF_RB == 0:
        def body(rb, carry):
            block(pl.multiple_of(rb * CONF_RB, CONF_RB))
            return carry
        lax.fori_loop(0, t // CONF_RB, body, 0)
    else:
        for rb in range(t // CONF_RB):
            block(rb * CONF_RB)


def _conformer(p, w, b, lg, lb, *, dil):
    n = p.shape[0]
    t = min(n, 1024)
    nt = n // t
    halo = -(-(CONF_K // 2) * dil // SUBLANE) * SUBLANE
    assert halo <= t
    cb = PA // CONF_D

    def spec(col, shift):
        return pl.BlockSpec((t, CONF_D), lambda i: (jnp.clip(i + shift, 0, nt - 1), col))

    vec = pl.BlockSpec((1, CONF_D), lambda i: (0, 0))
    return pl.pallas_call(
        functools.partial(_conf_kernel, t=t, halo=halo, dil=dil, nt=nt),
        grid=(nt,),
        in_specs=[spec(cb, 0), spec(cb + 1, 0), spec(cb, -1), spec(cb + 1, -1), spec(cb, 1), spec(cb + 1, 1),
                  pl.BlockSpec((CONF_K, CONF_D), lambda i: (0, 0)), vec, vec, vec],
        out_specs=pl.BlockSpec((t, CONF_D), lambda i: (i, 0)),
        out_shape=jax.ShapeDtypeStruct((n, CONF_D), F32),
        scratch_shapes=[pltpu.VMEM((t + 2 * halo, CONF_D), F32)],
        compiler_params=_params("parallel"),
        name="conformer",
    )(p, p, p, p, p, p, w, b, lg, lb)


CONF_COLS = 16
CONF_ROWS_PER_ITER = 4


def _conf_grid_kernel(v_ref, g_ref, w_ref, b_ref, lg_ref, lb_ref, o_ref, buf, *, rows):
    half = CONF_K // 2
    zeros = jnp.zeros((half,) + buf.shape[1:], F32)
    buf[0:half] = zeros
    buf[half + rows:half + rows + half] = zeros
    buf[half:half + rows] = v_ref[...].astype(F32) * _sigmoid(g_ref[...].astype(F32))

    rb = CONF_ROWS_PER_ITER

    def body(it, carry):
        r0 = it * rb
        accs = [None] * rb
        for j in range(CONF_K):
            wj = w_ref[j:j + 1, :]
            for s in range(rb):
                term = wj * buf[r0 + s + j]
                accs[s] = term if j == 0 else accs[s] + term
        for s in range(rb):
            v = _layer_norm(accs[s] + b_ref[...], lg_ref[...], lb_ref[...])
            o_ref[r0 + s] = _silu(v).astype(o_ref.dtype)
        return carry

    lax.fori_loop(0, rows // rb, body, 0)


def _conformer_grid(p, w, b, lg, lb):
    n = p.shape[0]
    rows = n // GRID_W
    p3 = p.reshape(rows, GRID_W, p.shape[1])
    cb = PA // CONF_D
    vec = pl.BlockSpec((1, CONF_D), lambda j: (0, 0))
    blk = lambda col: pl.BlockSpec((rows, CONF_COLS, CONF_D), lambda j: (0, j, col))
    out = pl.pallas_call(
        functools.partial(_conf_grid_kernel, rows=rows),
        grid=(GRID_W // CONF_COLS,),
        in_specs=[blk(cb), blk(cb + 1), pl.BlockSpec((CONF_K, CONF_D), lambda j: (0, 0)), vec, vec, vec],
        out_specs=blk(0),
        out_shape=jax.ShapeDtypeStruct((rows, GRID_W, CONF_D), BF16),
        scratch_shapes=[pltpu.VMEM((rows + 2 * (CONF_K // 2), CONF_COLS, CONF_D), F32)],
        compiler_params=_params("parallel"),
        name="conformer_grid",
    )(p3, p3, w, b, lg, lb)
    return out.reshape(n, CONF_D)


SSD_STATE_SHAPE = (SSD_HEADS // 2, SSD_STATE, 2 * SSD_HEAD_DIM)
SSD_CHUNKS_PER_STEP = 2


def _ssd_kernel(xbc_ref, dt_ref, dtb_ref, alog_ref, init_ref, y_ref, fin_ref, h_ref, *, ns, cps):
    d = pl.program_id(0)
    c = pl.program_id(1)
    q = SSD_CHUNK
    hd = SSD_HEAD_DIM

    @pl.when(c == 0)
    def _():
        h_ref[...] = init_ref[...]

    lane = lax.broadcasted_iota(jnp.int32, (q, LANE), 1)
    head = lane < SSD_HEADS
    first_half = lane < hd
    li = lax.broadcasted_iota(jnp.int32, (q, q), 0)
    si = lax.broadcasted_iota(jnp.int32, (q, q), 1)
    mask = (li - si) * (1 - 2 * d) >= 0
    tri = mask.astype(F32)
    a_rate = -jnp.exp(alog_ref[...])

    def one_chunk(r0):
        rows = pl.ds(r0, q)
        raw = dt_ref[rows, :] + dtb_ref[...]
        dt_all = jnp.maximum(raw, 0.0) + jnp.log(1.0 + jnp.exp(-jnp.abs(raw)))
        ld_all = dt_all * a_rate
        dt_d = jnp.where(head, jnp.where(d == 0, dt_all, pltpu.roll(dt_all, LANE - SSD_HEADS, 1)), 0.0)
        ld_d = jnp.where(head, jnp.where(d == 0, ld_all, pltpu.roll(ld_all, LANE - SSD_HEADS, 1)), 0.0)
        cum = jnp.dot(tri, ld_d, precision=HIGHEST, preferred_element_type=F32)
        tot = jnp.sum(ld_d, axis=0, keepdims=True)
        cum_t = cum.T
        dt_t = dt_d.T
        w_t = (jnp.exp(tot - cum) * dt_d).T
        a_out = jnp.exp(cum)
        e_tot = jnp.exp(tot)

        groups = {}

        def group(g):
            if g not in groups:
                bg = xbc_ref[rows, SSD_D + g * SSD_STATE:SSD_D + (g + 1) * SSD_STATE].astype(F32)
                cg = xbc_ref[rows, SSD_D + SSD_BC + g * SSD_STATE:SSD_D + SSD_BC + (g + 1) * SSD_STATE].astype(F32)
                bg_t = bg.T
                groups[g] = (bg_t, cg, _dot(cg.astype(BF16), bg_t.astype(BF16)))
            return groups[g]

        def head_terms(hh):
            bg_t, cg, cb = group(hh // SSD_HPG)
            diff = cum[:, hh:hh + 1] - cum_t[hh:hh + 1, :]
            dec = jnp.exp(jnp.where(mask, diff, -1e30))
            m = (cb * dec * dt_t[hh:hh + 1, :]).astype(BF16)
            cs = (cg * a_out[:, hh:hh + 1]).astype(BF16)
            bw = (bg_t * w_t[hh:hh + 1, :]).astype(BF16)
            return m, cs, bw

        zero = jnp.zeros((q, LANE), BF16)
        for j in range(SSD_HEADS // 2):
            ma, ca, wa = head_terms(2 * j)
            mb, cb_, wb = head_terms(2 * j + 1)
            xp = xbc_ref[rows, j * LANE:(j + 1) * LANE].astype(BF16)
            hp = h_ref[j]
            hp_b = hp.astype(BF16)
            x_bd = jnp.concatenate([jnp.where(first_half, xp, zero), jnp.where(first_half, zero, xp)], axis=0)
            h_bd = jnp.concatenate([jnp.where(first_half, hp_b, zero), jnp.where(first_half, zero, hp_b)], axis=0)
            y_ref[rows, j * LANE:(j + 1) * LANE] = _dot(jnp.concatenate([ma, mb, ca, cb_], axis=1),
                                                        jnp.concatenate([x_bd, h_bd], axis=0))
            s_new = _dot(jnp.concatenate([wa, wb], axis=1), x_bd)
            e_pair = jnp.where(first_half[0:1, :], e_tot[:, 2 * j:2 * j + 1], e_tot[:, 2 * j + 1:2 * j + 2])
            h_ref[j] = e_pair * hp + s_new

    for k in range(cps):
        one_chunk(pl.multiple_of(jnp.where(d == 0, k, cps - 1 - k) * q, q))

    @pl.when(c == ns - 1)
    def _():
        fin_ref[...] = h_ref[...]


def _ssd_scan(xbc, p, dt_bias, a_log, init):
    n = xbc.shape[0]
    cps = SSD_CHUNKS_PER_STEP
    q = SSD_CHUNK * cps
    ns = n // q

    def chunk(d, c):
        return jnp.where(d == 0, c, ns - 1 - c)

    st = SSD_STATE_SHAPE
    vec = pl.BlockSpec((1, LANE), lambda d, c: (0, 0))
    return pl.pallas_call(
        functools.partial(_ssd_kernel, ns=ns, cps=cps),
        grid=(2, ns),
        in_specs=[pl.BlockSpec((q, SSD_XBC), lambda d, c: (chunk(d, c), 0)),
                  pl.BlockSpec((q, LANE), lambda d, c: (chunk(d, c), 0)),
                  vec, vec,
                  pl.BlockSpec((None,) + st, lambda d, c: (d, 0, 0, 0))],
        out_specs=[pl.BlockSpec((None, q, SSD_D), lambda d, c: (d, chunk(d, c), 0)),
                   pl.BlockSpec((None,) + st, lambda d, c: (d, 0, 0, 0))],
        out_shape=[jax.ShapeDtypeStruct((2, n, SSD_D), F32),
                   jax.ShapeDtypeStruct((2,) + st, F32)],
        scratch_shapes=[pltpu.VMEM(st, F32)],
        compiler_params=_params("arbitrary", "arbitrary"),
        name="ssd_scan",
    )(xbc, p, dt_bias, a_log, init)


def _filt_kernel(feat_ref, w1_ref, b1_ref, w2_ref, b2_ref, fr_ref, w3h_ref, w3l_ref, dl_ref, k_ref, nrm_ref, *,
                 n, t):
    i = pl.program_id(0)
    hf = t // 2
    feat = feat_ref[...]
    x = jnp.concatenate([feat[0:hf], feat[hf:t]], axis=1)
    hid = jnp.sin(fr_ref[0:1, :] * (jnp.dot(x, w1_ref[...], precision=HIGHEST,
                                            preferred_element_type=F32) + b1_ref[...]))
    hid = jnp.sin(fr_ref[1:2, :] * (jnp.dot(hid, w2_ref[...], precision=HIGHEST,
                                            preferred_element_type=F32) + b2_ref[...]))
    hi = hid.astype(BF16)
    lo = (hid - hi.astype(F32)).astype(BF16)

    @pl.when(i == 0)
    def _():
        nrm_ref[...] = jnp.zeros_like(nrm_ref)

    for half in range(2):
        wh, wl = w3h_ref[half], w3l_ref[half]
        filt = _dot(hi, wh) + _dot(lo, wh) + _dot(hi, wl)
        filt = filt * jnp.exp(-feat[half * hf:(half + 1) * hf, 0:1] * dl_ref[...])
        row = i * t + half * hf + lax.broadcasted_iota(jnp.int32, filt.shape, 0)
        filt = jnp.where(row == n, 0.0, filt)
        k_ref[half * hf:(half + 1) * hf, :] = filt
        nrm_ref[...] += jnp.sum(jnp.abs(filt), axis=0, keepdims=True)


def _hyena_filters(featx, fw, n):
    t = min(n, 512)
    half = n // t
    oc = HY_ORDER * HY_D
    full = lambda shape: pl.BlockSpec(shape, lambda i: tuple(0 for _ in shape))
    w3spec = pl.BlockSpec((None, 2, LANE, oc), lambda i: (i // half, 0, 0, 0))
    return pl.pallas_call(
        functools.partial(_filt_kernel, n=n, t=t),
        grid=(2 * n // t,),
        in_specs=[pl.BlockSpec((t, LANE), lambda i: (i, 0)),
                  full((2 * LANE, LANE)), full((1, LANE)), full((LANE, LANE)), full((1, LANE)),
                  full((2, LANE)), w3spec, w3spec,
                  pl.BlockSpec((None, 1, oc), lambda i: (i // half, 0, 0))],
        out_specs=[pl.BlockSpec((t, oc), lambda i: (i, 0)),
                   pl.BlockSpec((1, oc), lambda i: (0, 0))],
        out_shape=[jax.ShapeDtypeStruct((2 * n, oc), F32),
                   jax.ShapeDtypeStruct((1, oc), F32)],
        compiler_params=_params("arbitrary"),
        name="hyena_filters",
    )(featx, fw["w1"], fw["b1"], fw["w2"], fw["b2"], fw["freq"], fw["w3h"], fw["w3l"], fw["deltas"])


def _filter_weights(w1, b1, w2, b2, freq, w3, deltas_d):
    hh = HY_HID
    z = lambda r, c: jnp.zeros((r, c), F32)
    w1p = jnp.pad(w1, ((0, LANE - HY_EMB), (0, 0)))
    w1b = jnp.concatenate([jnp.concatenate([w1p, z(LANE, hh)], 1),
                           jnp.concatenate([z(LANE, hh), w1p], 1)], 0)
    w2b = jnp.concatenate([jnp.concatenate([w2, z(hh, hh)], 1),
                           jnp.concatenate([z(hh, hh), w2], 1)], 0)
    two = lambda v: jnp.concatenate([v, v], axis=-1)
    w3d = w3.reshape(hh, HY_ORDER, 2, HY_D).transpose(2, 0, 1, 3).reshape(2, hh, HY_ORDER * HY_D)
    zz = jnp.zeros_like(w3d)
    w3x = jnp.stack([jnp.concatenate([w3d, zz], 1), jnp.concatenate([zz, w3d], 1)], axis=1)
    w3h = w3x.astype(BF16)
    w3l = (w3x - w3h.astype(F32)).astype(BF16)
    return dict(w1=w1b, b1=two(b1[None, :]), w2=w2b, b2=two(b2[None, :]), freq=two(freq), w3h=w3h, w3l=w3l,
                deltas=deltas_d)


DFT_LANES = 8192


def _dft_rows_kernel(f_ref, x_ref, o_ref, *, nj):
    x = jnp.concatenate([x_ref[:, jj, :] for jj in range(nj)], axis=1)
    o_ref[...] = _dot(f_ref[...], x.astype(BF16)).astype(o_ref.dtype)


def _dft_rows(fmat, x3, col, width):
    m, k = fmat.shape
    n2 = x3.shape[1]
    nj = min(DFT_LANES // width, n2)
    return pl.pallas_call(
        functools.partial(_dft_rows_kernel, nj=nj),
        grid=(n2 // nj,),
        in_specs=[pl.BlockSpec((m, k), lambda j: (0, 0)),
                  pl.BlockSpec((k, nj, width), lambda j: (0, j, col))],
        out_specs=pl.BlockSpec((m, nj * width), lambda j: (0, j)),
        out_shape=jax.ShapeDtypeStruct((m, n2 * width), BF16),
        compiler_params=_params("parallel"),
        name="dft_rows",
    )(fmat, x3)


def _spec_kernel(ar_ref, ai_ref, gr_ref, gi_ref, kr_ref, ki_ref):
    ar, ai, gr, gi = ar_ref[...], ai_ref[...], gr_ref[...], gi_ref[...]
    kr_ref[...] = (_dot(gr, ar) - _dot(gi, ai)).astype(BF16)
    ki_ref[...] = (_dot(gr, ai) + _dot(gi, ar)).astype(BF16)


def _filter_spectrum(a4, gr, gi):
    _, _, n2, ch = a4.shape
    nh = gr.shape[0]
    ct = ch
    blk = lambda ri: pl.BlockSpec((None, None, n2, ct), lambda f, j: (ri, f, 0, j))
    gspec = pl.BlockSpec((None, n2, n2), lambda f, j: (f, 0, 0))
    ospec = pl.BlockSpec((None, n2, ct), lambda f, j: (f, 0, j))
    return pl.pallas_call(
        _spec_kernel,
        grid=(nh, ch // ct),
        in_specs=[blk(0), blk(1), gspec, gspec],
        out_specs=[ospec, ospec],
        out_shape=[jax.ShapeDtypeStruct((nh, n2, ch), BF16)] * 2,
        compiler_params=_params("parallel", "parallel"),
        name="filter_spectrum",
    )(a4, a4, gr, gi)


MID_FREQS = 2


def _mid_kernel(ar_ref, ai_ref, gr_ref, gi_ref, grt_ref, git_ref, kr_ref, ki_ref, br_ref, bi_ref, *, nh):
    for k in range(MID_FREQS):
        f = pl.program_id(0) * MID_FREQS + k

        @pl.when(f < nh)
        def _(k=k):
            ar, ai, gr, gi = ar_ref[k], ai_ref[k], gr_ref[k], gi_ref[k]
            xr = _dot(gr, ar) - _dot(gi, ai)
            xi = _dot(gr, ai) + _dot(gi, ar)
            kr, ki = kr_ref[k].astype(F32), ki_ref[k].astype(F32)
            yr = (xr * kr - xi * ki).astype(BF16)
            yi = (xr * ki + xi * kr).astype(BF16)
            grt, git = grt_ref[k], git_ref[k]
            br_ref[k] = (_dot(grt, yr) + _dot(git, yi)).astype(BF16)
            bi_ref[k] = (_dot(grt, yi) - _dot(git, yr)).astype(BF16)

        @pl.when(f >= nh)
        def _(k=k):
            br_ref[k] = jnp.zeros(br_ref.shape[1:], BF16)
            bi_ref[k] = jnp.zeros(bi_ref.shape[1:], BF16)


def _hyena_mid(a4, tabs, kf_r, kf_i, order):
    _, nf, n2, ch = a4.shape
    gr, gi, grt, git = tabs
    nh = gr.shape[0]
    mf = MID_FREQS
    assert nf % mf == 0
    fi = lambda s: jnp.minimum(s, (nh - 1) // mf)
    blk = lambda ri: pl.BlockSpec((None, mf, n2, ch), lambda s: (ri, fi(s), 0, 0))
    kspec = pl.BlockSpec((mf, n2, ch), lambda s: (fi(s), 0, order))
    gspec = pl.BlockSpec((mf, n2, n2), lambda s: (fi(s), 0, 0))
    ospec = pl.BlockSpec((mf, n2, ch), lambda s: (s, 0, 0))
    return pl.pallas_call(
        functools.partial(_mid_kernel, nh=nh),
        grid=(nf // mf,),
        in_specs=[blk(0), blk(1), gspec, gspec, gspec, gspec, kspec, kspec],
        out_specs=[ospec, ospec],
        out_shape=[jax.ShapeDtypeStruct((nf, n2, ch), BF16)] * 2,
        compiler_params=_params("parallel"),
        name="hyena_mid",
    )(a4, a4, gr, gi, grt, git, kf_r, kf_i)


def _inv_kernel(f_ref, br_ref, bi_ref, s_ref, bias_ref, z_ref, g_ref, o_ref, *, nf, nj, ch):
    acc = _dot(f_ref[:, 0:nf], br_ref[...]) + _dot(f_ref[:, nf:2 * nf], bi_ref[...])
    for jj in range(nj):
        y = acc[:, jj * ch:(jj + 1) * ch] * s_ref[...]
        o_ref[:, jj, :] = g_ref[:, jj, :] * (y + bias_ref[...] * z_ref[:, jj, :])


def _hyena_inverse(finv, b_r, b_i, scale, bias, z3, zcol, g3, gcol):
    t1, k2 = finv.shape
    nf = k2 // 2
    n2 = z3.shape[1]
    ch = HY_D
    nj = min(DFT_LANES // ch, n2)
    col = pl.BlockSpec((nf, nj * ch), lambda j: (0, j))
    row = pl.BlockSpec((1, ch), lambda j: (0, 0))
    return pl.pallas_call(
        functools.partial(_inv_kernel, nf=nf, nj=nj, ch=ch),
        grid=(n2 // nj,),
        in_specs=[pl.BlockSpec((t1, k2), lambda j: (0, 0)), col, col, row, row,
                  pl.BlockSpec((t1, nj, ch), lambda j: (0, j, zcol)),
                  pl.BlockSpec((t1, nj, ch), lambda j: (0, j, gcol))],
        out_specs=pl.BlockSpec((t1, nj, ch), lambda j: (0, j, 0)),
        out_shape=jax.ShapeDtypeStruct((t1, n2, ch), F32),
        compiler_params=_params("parallel"),
        name="hyena_inverse",
    )(finv, b_r, b_i, scale, bias, z3, g3)


def _hyena_nf(n):
    nh = (2 * n // FFT_N2) // 2 + 1
    return -(-nh // 16) * 16


def _dft_tables(n):
    n2 = FFT_N2
    n1 = 2 * n // n2
    tot = 2 * n
    two_pi = 2.0 * math.pi

    def cs(num, den):
        ang = (two_pi / den) * (num % den).astype(F32)
        return jnp.cos(ang), jnp.sin(ang)

    nh = n1 // 2 + 1
    nf = _hyena_nf(n)
    f1 = jnp.arange(nh, dtype=jnp.int32)
    t1 = jnp.arange(n1, dtype=jnp.int32)
    c1, s1 = cs(f1[:, None] * t1[None, :], n1)
    zrow = jnp.zeros((nf - nh, n1), F32)
    fwd_full = jnp.concatenate([c1, zrow, -s1, zrow], axis=0).astype(BF16)
    fwd_half = fwd_full[:, :n1 // 2]
    wgt = jnp.where((f1 == 0) | (f1 == n1 // 2), 1.0, 2.0)[:, None]
    zcol = jnp.zeros((n1 // 2, nf - nh), F32)
    inv = jnp.concatenate([(wgt * c1[:, :n1 // 2]).T, zcol, -(wgt * s1[:, :n1 // 2]).T, zcol],
                          axis=1).astype(BF16)
    t2 = jnp.arange(n2, dtype=jnp.int32)
    twr, twi = cs(f1[:, None] * t2[None, :], tot)
    fr, fi = cs(t2[:, None] * t2[None, :], n2)
    twi, fi = -twi, -fi
    gr = twr[:, None, :] * fr[None] - twi[:, None, :] * fi[None]
    gi = twr[:, None, :] * fi[None] + twi[:, None, :] * fr[None]
    tabs = (gr.astype(BF16), gi.astype(BF16),
            gr.transpose(0, 2, 1).astype(BF16), gi.transpose(0, 2, 1).astype(BF16))
    return fwd_full, fwd_half, inv, tabs


def _hyena_long(q, k2, nrm, hy_bias, tables):
    n = q.shape[0]
    n2 = FFT_N2
    n1 = 2 * n // n2
    fwd_full, fwd_half, inv, tabs = tables
    nf = _hyena_nf(n)
    oc = HY_ORDER * HY_D
    ak = _dft_rows(fwd_full, k2.reshape(n1, n2, oc), 0, oc).reshape(2, nf, n2, oc)
    kf_r, kf_i = _filter_spectrum(ak, tabs[0], tabs[1])
    q3 = q.reshape(n1 // 2, n2, 3 * HY_D)
    z3, zcol = q3, 0
    for o in range(HY_ORDER):
        a4 = _dft_rows(fwd_half, z3, zcol, HY_D).reshape(2, nf, n2, HY_D)
        b_r, b_i = _hyena_mid(a4, tabs, kf_r, kf_i, o)
        scale = 1.0 / (2.0 * n * nrm[:, o * HY_D:(o + 1) * HY_D])
        z3 = _hyena_inverse(inv, b_r.reshape(nf, n2 * HY_D), b_i.reshape(nf, n2 * HY_D), scale,
                            hy_bias[o][None, :], z3, zcol, q3, o + 1)
        zcol = 0
    return z3.reshape(n, HY_D)


def _hy_ctx_kernel(v_ref, x1_ref, x2_ref, k0_ref, k1_ref, n0_ref, n1_ref, bias_ref, o_ref, kf, zs, *, n):
    zs[...] = v_ref[...]
    for o, (k_ref, nr_ref, x_ref) in enumerate(((k0_ref, n0_ref, x1_ref), (k1_ref, n1_ref, x2_ref))):
        kf[0:n, :] = k_ref[n:2 * n, :]
        kf[n:2 * n, :] = k_ref[0:n, :]

        def body(s, acc):
            return acc + kf[pl.ds(n - s, n), :] * zs[pl.ds(s, 1), :]

        acc = lax.fori_loop(0, n, body, jnp.zeros((n, LANE), F32))
        z = zs[...]
        zs[...] = x_ref[...] * (acc / nr_ref[...] + bias_ref[o:o + 1, :] * z)
    o_ref[...] = zs[...]


def _hyena_ctx(q, k2, nrm, hy_bias):
    n = q.shape[0]
    nb = HY_D // LANE
    col = lambda c0: pl.BlockSpec((n, LANE), lambda j: (0, c0 + j))
    kcol = lambda c0: pl.BlockSpec((2 * n, LANE), lambda j: (0, c0 + j))
    ncol = lambda c0: pl.BlockSpec((1, LANE), lambda j: (0, c0 + j))
    return pl.pallas_call(
        functools.partial(_hy_ctx_kernel, n=n),
        grid=(nb,),
        in_specs=[col(0), col(nb), col(2 * nb), kcol(0), kcol(nb), ncol(0), ncol(nb),
                  pl.BlockSpec((HY_ORDER, LANE), lambda j: (0, j))],
        out_specs=pl.BlockSpec((n, LANE), lambda j: (0, j)),
        out_shape=jax.ShapeDtypeStruct((n, HY_D), F32),
        scratch_shapes=[pltpu.VMEM((2 * n, LANE), F32), pltpu.VMEM((n, LANE), F32)],
        compiler_params=_params("parallel"),
        name="hyena_ctx",
    )(q, q, q, k2, k2, nrm, nrm, hy_bias)


def _merge_kernel(ya_ref, xs_ref, yf_ref, yb_ref, z_ref, yc_ref, yd_ref, g_ref, h_ref,
                  dv_ref, ng_ref, g1_ref, lg_ref, lb_ref,
                  wa_ref, wb_ref, wc_ref, wd_ref, wo_ref, o_ref):
    y = xs_ref[...].astype(F32) * dv_ref[...] + yf_ref[...] + yb_ref[...]
    gz = y * _silu(z_ref[...].astype(F32))
    ssd = gz * lax.rsqrt(jnp.mean(gz * gz, -1, keepdims=True) + LN_EPS) * ng_ref[...]
    d = D_MODEL
    gate = lambda k: jnp.tanh(g_ref[:, k * d:(k + 1) * d].astype(F32)) + 1.0
    m = gate(0) * _dot(ya_ref[...].astype(BF16), wa_ref[...])
    m = m + gate(1) * _dot(ssd.astype(BF16), wb_ref[...])
    m = m + gate(2) * _dot(yc_ref[...].astype(BF16), wc_ref[...])
    m = m + gate(3) * _dot(yd_ref[...].astype(BF16), wd_ref[...])
    mix = _dot(m.astype(BF16), wo_ref[...])
    o_ref[...] = _layer_norm(DN_ALPHA * h_ref[...] + g1_ref[...] * mix, lg_ref[...], lb_ref[...])


def _merge(ya, xbc, ydir, p, yc, yd, h, dvec, ng, gate1, lg, lb, wa, wb, wc, wd, wo):
    n = h.shape[0]
    t = 256
    tok = lambda w, col=0: pl.BlockSpec((t, w), lambda i: (i, col))
    vec = lambda w: pl.BlockSpec((1, w), lambda i: (0, 0))
    mat = lambda r: pl.BlockSpec((r, D_MODEL), lambda i: (0, 0))
    return pl.pallas_call(
        _merge_kernel,
        grid=(n // t,),
        in_specs=[tok(CONF_D), tok(SSD_D),
                  pl.BlockSpec((None, t, SSD_D), lambda i: (0, i, 0)),
                  pl.BlockSpec((None, t, SSD_D), lambda i: (1, i, 0)),
                  tok(SSD_D, PZ // SSD_D), tok(HY_D), tok(SC_D), tok(N_BRANCH * D_MODEL, 0), tok(D_MODEL),
                  vec(SSD_D), vec(SSD_D), vec(D_MODEL), vec(D_MODEL), vec(D_MODEL),
                  mat(CONF_D), mat(SSD_D), mat(HY_D), mat(SC_D), mat(D_MODEL)],
        out_specs=tok(D_MODEL),
        out_shape=jax.ShapeDtypeStruct((n, D_MODEL), F32),
        compiler_params=_params("parallel"),
        name="merge",
    )(ya, xbc, ydir, ydir, p, yc, yd, p, h, dvec, ng, gate1, lg, lb, wa, wb, wc, wd, wo)


MOE_T = 256


def _stream_tiles(hs):
    tiles = [h.shape[0] // MOE_T for h in hs]
    first = [sum(tiles[:s]) for s in range(len(hs))]
    return tiles, first


def _stream_specs(tiles, first, width):
    return [pl.BlockSpec((MOE_T, width), lambda i, nt=nt, f=f: (jnp.clip(i - f, 0, nt - 1), 0))
            for nt, f in zip(tiles, first)]


def _stream_vec_spec(first):
    def index(i):
        s = 0
        for f in first[1:]:
            s = s + (i >= f).astype(jnp.int32)
        return (s, 0, 0)
    return pl.BlockSpec((None, 1, D_MODEL), index)


def _stream_tile(i, refs, first):
    x = refs[0][...]
    for r, f in zip(refs[1:], first[1:]):
        x = jnp.where(i >= f, r[...], x)
    return x


def _router_kernel(*refs, first):
    ns = len(first)
    h_refs = refs[:ns]
    sh_ref, sc_ref, wh_ref, wl_ref, b_ref, sel_ref, cnt_ref, selt_ref = refs[ns:]
    i = pl.program_id(0)

    @pl.when(i == 0)
    def _():
        cnt_ref[...] = jnp.zeros_like(cnt_ref)

    u = _stream_tile(i, h_refs, first) * (1.0 + sc_ref[...]) + sh_ref[...]
    u_hi = u.astype(BF16)
    u_lo = (u - u_hi.astype(F32)).astype(BF16)
    lg = _dot(u_hi, wh_ref[...]) + _dot(u_lo, wh_ref[...]) + _dot(u_hi, wl_ref[...]) + b_ref[...]
    lane = lax.broadcasted_iota(jnp.int32, lg.shape, 1).astype(F32)
    neg = -1e30
    big = 1e9
    gl = jnp.where(lane < MOE_GROUPS, lg, neg)
    gmax = jnp.max(gl, -1, keepdims=True)
    gsel = jnp.min(jnp.where(gl == gmax, lane, big), -1, keepdims=True)
    gprob = 1.0 / jnp.sum(jnp.where(lane < MOE_GROUPS, jnp.exp(lg - gmax), 0.0), -1, keepdims=True)
    lo = MOE_GROUPS + gsel * MOE_EPG
    el = jnp.where(jnp.abs(lane - lo - (MOE_EPG - 1) / 2.0) < MOE_EPG / 2.0, lg, neg)
    m1 = jnp.max(el, -1, keepdims=True)
    i1 = jnp.min(jnp.where(el == m1, lane, big), -1, keepdims=True)
    el2 = jnp.where(lane == i1, neg, el)
    m2 = jnp.max(el2, -1, keepdims=True)
    i2 = jnp.min(jnp.where(el2 == m2, lane, big), -1, keepdims=True)
    t = jnp.exp(m2 - m1)
    w1 = gprob / (1.0 + t)
    w2 = gprob * t / (1.0 + t)
    oh1 = jnp.where(lane == i1, 1.0, 0.0)
    oh2 = jnp.where(lane == i2, 1.0, 0.0)
    oh = oh1 + oh2
    tt = lg.shape[0]
    li = lax.broadcasted_iota(jnp.int32, (tt, tt), 0)
    si = lax.broadcasted_iota(jnp.int32, (tt, tt), 1)
    before = _dot(jnp.where(li > si, 1.0, 0.0).astype(BF16), oh.astype(BF16)) + cnt_ref[...]
    r1 = jnp.sum(oh1 * before, -1, keepdims=True)
    r2 = jnp.sum(oh2 * before, -1, keepdims=True)
    cnt_ref[...] += jnp.sum(oh, axis=0, keepdims=True)
    cols = (i1 - MOE_GROUPS, i2 - MOE_GROUPS, w1, w2, r1, r2)
    sel = jnp.zeros_like(lg)
    for k, v in enumerate(cols):
        sel = jnp.where(lane == k, v, sel)
    sel_ref[...] = sel
    selt_ref[...] = sel.T[0:SUBLANE, :]


def _router(hs, shift, scale, wr, br):
    tiles, first = _stream_tiles(hs)
    t = MOE_T
    n = t * sum(tiles)
    vec = lambda w: pl.BlockSpec((1, w), lambda i: (0, 0))
    wr_hi = wr.astype(BF16)
    wr_lo = (wr - wr_hi.astype(F32)).astype(BF16)
    wspec = pl.BlockSpec((D_MODEL, LANE), lambda i: (0, 0))
    return pl.pallas_call(
        functools.partial(_router_kernel, first=first),
        grid=(sum(tiles),),
        in_specs=_stream_specs(tiles, first, D_MODEL) + [
            _stream_vec_spec(first), _stream_vec_spec(first), wspec, wspec, vec(LANE)],
        out_specs=[pl.BlockSpec((t, LANE), lambda i: (i, 0)), vec(LANE),
                   pl.BlockSpec((SUBLANE, t), lambda i: (0, i))],
        out_shape=[jax.ShapeDtypeStruct((n, LANE), F32), jax.ShapeDtypeStruct((1, LANE), F32),
                   jax.ShapeDtypeStruct((SUBLANE, n), F32)],
        compiler_params=_params("arbitrary"),
        name="router",
    )(*hs, shift, scale, wr_hi, wr_lo, br)


ROW_WORDS = D_MODEL // 2


def _pack_rows(x):
    c = x.shape[1] // 2
    bits = lambda v: lax.bitcast_convert_type(v.astype(BF16).astype(F32), jnp.uint32)
    return bits(x[:, :c]) | (bits(x[:, c:]) >> 16)


def _unpack_rows(w):
    hi = lax.bitcast_convert_type(w & jnp.uint32(0xFFFF0000), F32)
    lo = lax.bitcast_convert_type(w << 16, F32)
    return jnp.concatenate([hi, lo], axis=1)


def _dispatch_kernel(dst_ref, *refs, first, nt):
    ns = len(first)
    h_refs = refs[:ns]
    sh_ref, sc_ref, zero_hbm, xin_hbm, ubuf, sem = refs[ns:]
    del zero_hbm
    t = MOE_T
    i = pl.program_id(0)
    slot = i % 2

    def wait_slot(s):
        for _ in range(MOE_TOP_K):
            pltpu.make_async_copy(ubuf.at[s], xin_hbm.at[pl.ds(0, t), :], sem.at[s]).wait()

    @pl.when(i >= 2)
    def _():
        wait_slot(slot)

    ubuf[slot] = _pack_rows(_stream_tile(i, h_refs, first) * (1.0 + sc_ref[...]) + sh_ref[...])

    def issue(r, carry):
        for k in range(MOE_TOP_K):
            pltpu.make_async_copy(ubuf.at[slot, pl.ds(r, 1), :],
                                  xin_hbm.at[pl.ds(dst_ref[0, 0, k * t + r], 1), :], sem.at[slot]).start()
        return carry

    lax.fori_loop(0, t, issue, 0, unroll=16)

    @pl.when(i == nt - 1)
    def _():
        wait_slot(slot)
        if nt > 1:
            wait_slot(1 - slot)


def _dispatch(pos_t, hs, shift, scale, n_rows):
    tiles, first = _stream_tiles(hs)
    t = MOE_T
    nt = sum(tiles)
    return pl.pallas_call(
        functools.partial(_dispatch_kernel, first=first, nt=nt),
        grid=(nt,),
        in_specs=([pl.BlockSpec((1, 1, MOE_TOP_K * t), lambda i: (i, 0, 0), memory_space=pltpu.SMEM)]
                  + _stream_specs(tiles, first, D_MODEL)
                  + [_stream_vec_spec(first), _stream_vec_spec(first), pl.BlockSpec(memory_space=pl.ANY)]),
        out_specs=pl.BlockSpec(memory_space=pl.ANY),
        out_shape=jax.ShapeDtypeStruct((n_rows, ROW_WORDS), jnp.uint32),
        scratch_shapes=[pltpu.VMEM((2, t, ROW_WORDS), jnp.uint32), pltpu.SemaphoreType.DMA((2,))],
        input_output_aliases={3 + len(hs): 0},
        compiler_params=_params("arbitrary"),
        name="dispatch",
    )(pos_t, *hs, shift, scale, jnp.zeros((n_rows, ROW_WORDS), jnp.uint32))


def _expert_kernel(be_ref, nu_ref, x_ref, wg_ref, wu_ref, wd_ref, o_ref, wgb, wub, wdb):
    b = pl.program_id(0)

    @pl.when((b == 0) | (be_ref[b] != be_ref[jnp.maximum(b - 1, 0)]))
    def _():
        wgb[...] = wg_ref[...].astype(BF16)
        wub[...] = wu_ref[...].astype(BF16)
        wdb[...] = wd_ref[...].astype(BF16)

    @pl.when(b < nu_ref[0])
    def _():
        x = _unpack_rows(x_ref[...]).astype(BF16)
        hid = _silu(_dot(x, wgb[...])) * _dot(x, wub[...])
        o_ref[...] = _pack_rows(_dot(hid.astype(BF16), wdb[...]))

    @pl.when(b >= nu_ref[0])
    def _():
        o_ref[...] = jnp.zeros_like(o_ref)


def _experts(xin, block_e, n_used, wg, wu, wd, layer):
    n_blocks = block_e.shape[0]
    gs = pltpu.PrefetchScalarGridSpec(
        num_scalar_prefetch=2,
        grid=(n_blocks,),
        in_specs=[pl.BlockSpec((MOE_ROWS, ROW_WORDS), lambda b, be, nu: (b, 0)),
                  pl.BlockSpec((None, None, D_MODEL, MOE_FF), lambda b, be, nu: (layer, be[b], 0, 0)),
                  pl.BlockSpec((None, None, D_MODEL, MOE_FF), lambda b, be, nu: (layer, be[b], 0, 0)),
                  pl.BlockSpec((None, None, MOE_FF, D_MODEL), lambda b, be, nu: (layer, be[b], 0, 0))],
        out_specs=pl.BlockSpec((MOE_ROWS, ROW_WORDS), lambda b, be, nu: (b, 0)),
        scratch_shapes=[pltpu.VMEM((D_MODEL, MOE_FF), BF16), pltpu.VMEM((D_MODEL, MOE_FF), BF16),
                        pltpu.VMEM((MOE_FF, D_MODEL), BF16)],
    )
    return pl.pallas_call(
        _expert_kernel,
        grid_spec=gs,
        out_shape=jax.ShapeDtypeStruct((n_blocks * MOE_ROWS, ROW_WORDS), jnp.uint32),
        compiler_params=_params("arbitrary"),
        name="experts",
    )(block_e, n_used, xin, wg, wu, wd)


def _combine_kernel(pos_ref, posn_ref, y_hbm, *refs, first, tiles):
    ns = len(first)
    nt = sum(tiles)
    h_refs = refs[:ns]
    sel_ref, g2_ref, lg_ref, lb_ref = refs[ns:ns + 4]
    o_refs = refs[ns + 4:2 * ns + 4]
    ybuf, sem = refs[2 * ns + 4:]
    t = MOE_T
    i = pl.program_id(0)
    slot = i % 2

    def gather(p_ref, s):
        def issue(r, carry):
            pltpu.make_async_copy(y_hbm.at[pl.ds(p_ref[0, 0, r], 1), :], ybuf.at[s, pl.ds(r, 1), :],
                                  sem.at[s]).start()
            return carry
        lax.fori_loop(0, MOE_TOP_K * t, issue, 0, unroll=32)

    @pl.when(i == 0)
    def _():
        gather(pos_ref, 0)

    @pl.when(i + 1 < nt)
    def _():
        gather(posn_ref, 1 - slot)

    pltpu.make_async_copy(y_hbm.at[pl.ds(0, MOE_TOP_K * t), :], ybuf.at[slot], sem.at[slot]).wait()
    ffn = (sel_ref[:, 2:3] * _unpack_rows(ybuf[slot, 0:t, :])
           + sel_ref[:, 3:4] * _unpack_rows(ybuf[slot, t:2 * t, :]))
    out = _layer_norm(DN_ALPHA * _stream_tile(i, h_refs, first) + g2_ref[...] * ffn, lg_ref[...], lb_ref[...])
    for s in range(ns):
        @pl.when((i >= first[s]) & (i < first[s] + tiles[s]))
        def _(s=s):
            o_refs[s][...] = out


def _combine(y, pos_t, hs, sel, gate2, lg, lb):
    tiles, first = _stream_tiles(hs)
    t = MOE_T
    nt = sum(tiles)
    vec = pl.BlockSpec((1, D_MODEL), lambda i: (0, 0))
    return pl.pallas_call(
        functools.partial(_combine_kernel, first=first, tiles=tiles),
        grid=(nt,),
        in_specs=([pl.BlockSpec((1, 1, MOE_TOP_K * t), lambda i: (i, 0, 0), memory_space=pltpu.SMEM),
                   pl.BlockSpec((1, 1, MOE_TOP_K * t), lambda i: (jnp.minimum(i + 1, nt - 1), 0, 0),
                                memory_space=pltpu.SMEM),
                   pl.BlockSpec(memory_space=pl.ANY)]
                  + _stream_specs(tiles, first, D_MODEL)
                  + [pl.BlockSpec((t, LANE), lambda i: (i, 0)), _stream_vec_spec(first), vec, vec]),
        out_specs=_stream_specs(tiles, first, D_MODEL),
        out_shape=[jax.ShapeDtypeStruct(h.shape, F32) for h in hs],
        scratch_shapes=[pltpu.VMEM((2, MOE_TOP_K * t, ROW_WORDS), jnp.uint32), pltpu.SemaphoreType.DMA((2,))],
        compiler_params=_params("arbitrary"),
        name="combine",
    )(pos_t, pos_t, y, *hs, sel, gate2, lg, lb)


def _moe(hs, shift, scale, gate2, lg, lb, wr, br, wg, wu, wd, layer):
    t = MOE_T
    n = sum(h.shape[0] for h in hs)
    nt = n // t
    sel, cnt, selt = _router(hs, shift, scale, wr, br)
    counts = cnt[0, MOE_GROUPS:MOE_GROUPS + MOE_EXPERTS].astype(jnp.int32)
    padded = (counts + MOE_ROWS - 1) // MOE_ROWS * MOE_ROWS
    pad_end = jnp.cumsum(padded)
    pad_start = pad_end - padded
    n_blocks = (n * MOE_TOP_K + MOE_EXPERTS * (MOE_ROWS - 1) + MOE_ROWS - 1) // MOE_ROWS
    blk_row = jnp.arange(n_blocks, dtype=jnp.int32) * MOE_ROWS
    block_e = jnp.minimum(jnp.sum((blk_row[:, None] >= pad_end[None, :]).astype(jnp.int32), axis=1),
                          MOE_EXPERTS - 1)
    n_used = (pad_end[-1:] // MOE_ROWS).astype(jnp.int32)
    e_kt = selt[0:MOE_TOP_K].astype(jnp.int32)
    ids = jnp.arange(MOE_EXPERTS, dtype=jnp.int32)[None, :, None]
    start_kt = jnp.sum(jnp.where(e_kt[:, None, :] == ids, pad_start[None, :, None], 0), axis=1)
    pos_kt = start_kt + selt[4:4 + MOE_TOP_K].astype(jnp.int32)
    pos_t = pos_kt.reshape(MOE_TOP_K, nt, t).transpose(1, 0, 2).reshape(nt, 1, MOE_TOP_K * t)
    xin = _dispatch(pos_t, hs, shift, scale, n_blocks * MOE_ROWS)
    y = _experts(xin, block_e, n_used, wg, wu, wd, layer)
    return _combine(y, pos_t, hs, sel, gate2, lg, lb)


def _mixer(h, mod, lw, ssd_init, tables, *, latent, need_mix):
    n = h.shape[0]
    p, dt_raw = _inproj(h, mod[0], mod[1], lw["w_in"], lw["layer"])
    xbc = _conv3(p, PX, SSD_XBC, lw["ssd_conv_w"], lw["ssd_conv_b"], silu=True, out_dtype=BF16)
    ydir, finals = _ssd_scan(xbc, dt_raw, lw["ssd_dt_bias"], lw["ssd_a_log"], ssd_init)
    if not need_mix:
        return None, finals
    conf_w = (lw["conf_dw_w"], lw["conf_dw_b"], lw["conf_ln_g"], lw["conf_ln_b"])
    ya = _conformer_grid(p, *conf_w) if latent else _conformer(p, *conf_w, dil=1)
    q = _conv3(p, PC, 3 * HY_D, lw["hy_short_w"], lw["hy_short_b"], silu=False, out_dtype=F32)
    k2, nrm = _hyena_filters(lw["feat_lat" if latent else "feat_ctx"], lw["hy_filter"], n)
    if latent:
        yc = _hyena_long(q, k2, nrm, lw["hy_bias"], tables)
    else:
        yc = _hyena_ctx(q, k2, nrm, lw["hy_bias"])
    yd = _gated_conv(p, lw["sc_conv_w"])
    h = _merge(ya, xbc, ydir, p, yc, yd, h, lw["ssd_dvec"], lw["ssd_norm_g"], mod[2], lw["ln_g0"], lw["ln_b0"],
               lw["w_branch_a"], lw["w_branch_b"], lw["w_branch_c"], lw["w_branch_d"], lw["w_out"])
    return h, finals


def _positional_features(n):
    tau = jnp.arange(2 * n, dtype=jnp.int32)
    lag = jnp.where(tau < n, tau, 2 * n - tau).astype(F32)[:, None]
    t01 = lag * (1.0 / (n - 1))
    omega = (2.0 * math.pi / n) * lag
    bands = jnp.linspace(1e-4, HY_BANDS - 1, HY_BANDS, dtype=F32)
    featx = jnp.concatenate([t01, jnp.cos(bands * omega), -jnp.sin(bands * omega)], axis=-1)
    return _pad_lanes(featx)


RELAYOUT_W = 256


def _relayout_kernel(blk_ref, sh_ref, valid_ref, half_ref, a_ref, b_ref, o_ref):
    j = pl.program_id(1)
    sh = sh_ref[j]
    rot = (LANE - sh) % LANE
    scale = jnp.where(half_ref[j] == 1, 0.5, 1.0)
    lane = lax.broadcasted_iota(jnp.int32, (a_ref.shape[0], LANE), 1)
    nk = RELAYOUT_W // LANE
    chunk = lambda k: (a_ref if k < nk else b_ref)[:, (k % nk) * LANE:(k % nk + 1) * LANE]
    for k in range(nk):
        x = jnp.where(lane < LANE - sh, pltpu.roll(chunk(k), rot, 1), pltpu.roll(chunk(k + 1), rot, 1))
        o_ref[:, k * LANE:(k + 1) * LANE] = jnp.where(lane + k * LANE < valid_ref[j], x * scale, 0.0).astype(BF16)


def _relayout_w_in(w_in):
    w = RELAYOUT_W
    depth, d, total = w_in.shape
    ob = OFF_B
    segs = [(PG, OFF_G, N_BRANCH * D_MODEL, 1), (PA, OFF_A, 2 * CONF_D, 0), (PC, OFF_C, 3 * HY_D, 0),
            (PD, OFF_D, 3 * SC_D, 0), (PX, ob + SSD_D, SSD_XBC, 0), (PZ, ob, SSD_D, 0),
            (PDT, ob + SSD_D + SSD_XBC, 2 * SSD_HEADS, 0)]
    ntile = NP // w
    blk, sh, valid, half = ([0] * ntile for _ in range(4))
    for p0, s0, width, hv in segs:
        assert p0 % w == 0
        for j in range(p0 // w, -(-(p0 + width) // w)):
            src = s0 + j * w - p0
            blk[j], sh[j], valid[j], half[j] = src // w, src % w, min(w, p0 + width - j * w), hv
            assert sh[j] < LANE
    last = (total - 1) // w
    tables = [jnp.asarray(v, jnp.int32) for v in (blk, sh, valid, half)]
    gs = pltpu.PrefetchScalarGridSpec(
        num_scalar_prefetch=4,
        grid=(depth, ntile),
        in_specs=[pl.BlockSpec((None, d, w), lambda l, j, blk, *_: (l, 0, blk[j])),
                  pl.BlockSpec((None, d, w), lambda l, j, blk, *_: (l, 0, jnp.minimum(blk[j] + 1, last)))],
        out_specs=pl.BlockSpec((None, d, w), lambda l, j, *_: (l, 0, j)),
    )
    return pl.pallas_call(
        _relayout_kernel,
        grid_spec=gs,
        out_shape=jax.ShapeDtypeStruct((depth, d, NP), BF16),
        compiler_params=_params("parallel", "parallel"),
        name="relayout_w_in",
    )(*tables, w_in, w_in)


def _pad_lanes(v):
    return jnp.pad(v, ((0, 0), (0, LANE - v.shape[-1])))


def kernel(x, c, ctx, c_ctx, w_mod, b_mod, ln_g, ln_b, w_in, conf_dw_w, conf_dw_b, conf_ln_g, conf_ln_b,
           ssd_conv_w, ssd_conv_b, ssd_a_log, ssd_dt_bias, ssd_d, ssd_norm_g, hy_short_w, hy_short_b,
           hy_w1, hy_b1, hy_w2, hy_b2, hy_freq, hy_w3, hy_bias, sc_conv_w, w_branch_a, w_branch_b,
           w_branch_c, w_branch_d, w_out, rt_group_w, rt_group_b, rt_expert_w, rt_expert_b,
           ex_w_gate, ex_w_up, ex_w_down):
    assert x.shape[0] == 1 and ctx.shape[0] == 1
    n_lat, n_ctx = x.shape[1], ctx.shape[1]
    depth = w_in.shape[0]

    cv = jnp.concatenate([c, c_ctx[None, :], jnp.zeros((SUBLANE - 2, D_MODEL), F32)], axis=0)
    mods = _mod_vectors(cv, w_mod, b_mod)
    w_in_p = _relayout_w_in(w_in)
    tables = _dft_tables(n_lat)
    feat_lat = _positional_features(n_lat)
    feat_ctx = _positional_features(n_ctx)
    deltas = jnp.abs(jnp.linspace(HY_MIN_DECAY, HY_MAX_DECAY, HY_N_FILT, dtype=F32))
    deltas_d = deltas.reshape(HY_ORDER, 2, HY_D).transpose(1, 0, 2).reshape(2, 1, HY_ORDER * HY_D)
    router_w = jnp.concatenate([rt_group_w, rt_expert_w,
                                jnp.zeros((depth, D_MODEL, LANE - MOE_GROUPS - MOE_EXPERTS), F32)], axis=-1)
    router_b = jnp.concatenate([rt_group_b, rt_expert_b,
                                jnp.zeros((depth, LANE - MOE_GROUPS - MOE_EXPERTS), F32)], axis=-1)
    ssd_zero = jnp.zeros((2,) + SSD_STATE_SHAPE, F32)

    h_lat, h_ctx = x[0], ctx[0]
    for l in range(depth):
        row = lambda v: v[None, :]
        lw = dict(
            w_in=w_in_p, layer=l, conf_dw_w=conf_dw_w[l], conf_dw_b=row(conf_dw_b[l]), conf_ln_g=row(conf_ln_g[l]),
            conf_ln_b=row(conf_ln_b[l]), ssd_conv_w=ssd_conv_w[l], ssd_conv_b=row(ssd_conv_b[l]),
            ssd_a_log=_pad_lanes(ssd_a_log[l].reshape(1, -1)), ssd_dt_bias=_pad_lanes(ssd_dt_bias[l].reshape(1, -1)),
            ssd_dvec=row(jnp.repeat(ssd_d[l], SSD_HEAD_DIM)), ssd_norm_g=row(ssd_norm_g[l]),
            hy_short_w=hy_short_w[l], hy_short_b=row(hy_short_b[l]),
            hy_filter=_filter_weights(hy_w1[l], hy_b1[l], hy_w2[l], hy_b2[l], hy_freq[l], hy_w3[l], deltas_d),
            hy_bias=hy_bias[l], feat_lat=feat_lat, feat_ctx=feat_ctx,
            sc_conv_w=sc_conv_w[l], ln_g0=row(ln_g[l, 0]), ln_b0=row(ln_b[l, 0]),
            w_branch_a=w_branch_a[l].astype(BF16), w_branch_b=w_branch_b[l].astype(BF16),
            w_branch_c=w_branch_c[l].astype(BF16), w_branch_d=w_branch_d[l].astype(BF16),
            w_out=(0.5 * w_out[l]).astype(BF16))
        moe_w = (router_w[l], row(router_b[l]), ex_w_gate, ex_w_up, ex_w_down, l)
        last = l == depth - 1
        d = D_MODEL
        mod_lat = [mods[l, 0:1, k * d:(k + 1) * d] for k in range(6)]
        mod_ctx = [mods[l, 1:2, k * d:(k + 1) * d] for k in range(6)]

        mix_ctx, ctx_states = _mixer(h_ctx, mod_ctx, lw, ssd_zero, None, latent=False, need_mix=not last)
        h_lat, _ = _mixer(h_lat, mod_lat, lw, ctx_states, tables, latent=True, need_mix=True)
        streams = [(h_lat, mod_lat)] if last else [(h_lat, mod_lat), (mix_ctx, mod_ctx)]
        vecs = [jnp.stack([m[k] for _, m in streams]) for k in (3, 4, 5)]
        outs = _moe([h for h, _ in streams], *vecs, row(ln_g[l, 1]), row(ln_b[l, 1]), *moe_w)
        h_lat = outs[0]
        if not last:
            h_ctx = outs[1]
    return h_lat[None]
```

```python
import functools
import math

import jax
import jax.numpy as jnp
from jax import lax
from jax.experimental import pallas as pl
from jax.experimental.pallas import tpu as pltpu

F32 = jnp.float32
BF16 = jnp.bfloat16
HIGHEST = lax.Precision.HIGHEST

D_MODEL = 1024
DEPTH = 4
GRID_W = 64
CONF_D = 512
CONF_K = 31
SSD_D = 768
SSD_HEADS = 12
SSD_HEAD_DIM = 64
SSD_GROUPS = 4
SSD_HPG = SSD_HEADS // SSD_GROUPS
SSD_STATE = 128
SSD_CHUNK = 128
SSD_BC = SSD_GROUPS * SSD_STATE
SSD_XBC = SSD_D + 2 * SSD_BC
SSD_PROJ = SSD_D + SSD_XBC + 2 * SSD_HEADS
HY_D = 512
HY_ORDER = 2
HY_EMB = 33
HY_BANDS = (HY_EMB - 1) // 2
HY_HID = 64
HY_N_FILT = HY_ORDER * 2 * HY_D
HY_MIN_DECAY = math.log(1e-2) / 1.5
HY_MAX_DECAY = math.log(1e-2) / 0.3
SC_D = 512
N_BRANCH = 4
OFF_A = 0
OFF_B = OFF_A + 2 * CONF_D
OFF_C = OFF_B + SSD_PROJ
OFF_D = OFF_C + 3 * HY_D
OFF_G = OFF_D + 3 * SC_D
MOE_GROUPS = 4
MOE_EPG = 8
MOE_EXPERTS = MOE_GROUPS * MOE_EPG
MOE_TOP_K = 2
MOE_FF = 512
DN_ALPHA = (2 * DEPTH) ** 0.25
LN_EPS = 1e-5

PG = 0
PA = PG + N_BRANCH * D_MODEL
PC = PA + 2 * CONF_D
PD = PC + 3 * HY_D
PX = PD + 3 * SC_D
PZ = PX + SSD_XBC
PDT = PZ + SSD_D
INPROJ_TN = 1024
NP = -(-(PDT + 128) // INPROJ_TN) * INPROJ_TN

LANE = 128
SUBLANE = 8
MXU_DIM = 256
FFT_N2 = MXU_DIM
MOE_ROWS = MXU_DIM
INPROJ_TM = 2048
CONV_TOKENS = 1024
CONV_COLS = 256
MERGE_T = 256
VMEM_PHYSICAL = 64 * 1024 * 1024
VMEM_LIMIT = VMEM_PHYSICAL * 3 // 4


def _params(*sem):
    return pltpu.CompilerParams(dimension_semantics=sem, vmem_limit_bytes=VMEM_LIMIT)


def _sigmoid(x):
    return 0.5 * jnp.tanh(0.5 * x) + 0.5


def _silu(x):
    h = 0.5 * x
    return h * jnp.tanh(h) + h


def _layer_norm(x, g, b):
    mu = jnp.mean(x, -1, keepdims=True)
    xc = x - mu
    var = jnp.mean(xc * xc, -1, keepdims=True)
    return xc * lax.rsqrt(var + LN_EPS) * g + b


def _dot(a, b):
    return jnp.dot(a, b, preferred_element_type=F32)


def _mod_kernel(cv_ref, w_ref, b_ref, o_ref):
    o_ref[...] = jnp.dot(_silu(cv_ref[...]), w_ref[...], precision=HIGHEST,
                         preferred_element_type=F32) + b_ref[...]


def _mod_vectors(cv, w_mod, b_mod):
    tn = 1536
    return pl.pallas_call(
        _mod_kernel,
        grid=(DEPTH, 6 * D_MODEL // tn),
        in_specs=[pl.BlockSpec((SUBLANE, D_MODEL), lambda l, j: (0, 0)),
                  pl.BlockSpec((None, D_MODEL, tn), lambda l, j: (l, 0, j)),
                  pl.BlockSpec((None, 1, tn), lambda l, j: (l, 0, j))],
        out_specs=pl.BlockSpec((None, SUBLANE, tn), lambda l, j: (l, 0, j)),
        out_shape=jax.ShapeDtypeStruct((DEPTH, SUBLANE, 6 * D_MODEL), F32),
        compiler_params=_params("parallel", "parallel"),
        name="mod_vectors",
    )(cv, w_mod, b_mod.reshape(DEPTH, 1, 6 * D_MODEL))


def _inproj_kernel(x_ref, sh_ref, sc_ref, w_ref, o_ref, dt_ref, xb_ref, *, nj):
    j = pl.program_id(1)

    @pl.when(j == 0)
    def _():
        xb_ref[...] = (x_ref[...] * (1.0 + sc_ref[...]) + sh_ref[...]).astype(BF16)

    res = _dot(xb_ref[...], w_ref[...])
    o_ref[...] = res.astype(BF16)

    @pl.when(j == nj - 1)
    def _():
        off = PDT - (nj - 1) * INPROJ_TN
        dt_ref[...] = res[:, off:off + LANE]


def _inproj(h, shift, scale, w, layer):
    n = h.shape[0]
    tm = min(n, INPROJ_TM)
    tn = INPROJ_TN
    nj = NP // tn
    assert PDT >= (nj - 1) * tn
    return pl.pallas_call(
        functools.partial(_inproj_kernel, nj=nj),
        grid=(n // tm, nj),
        in_specs=[pl.BlockSpec((tm, D_MODEL), lambda i, j: (i, 0)),
                  pl.BlockSpec((1, D_MODEL), lambda i, j: (0, 0)),
                  pl.BlockSpec((1, D_MODEL), lambda i, j: (0, 0)),
                  pl.BlockSpec((None, D_MODEL, tn), lambda i, j: (layer, 0, j))],
        out_specs=[pl.BlockSpec((tm, tn), lambda i, j: (i, j)),
                   pl.BlockSpec((tm, LANE), lambda i, j: (i, 0))],
        out_shape=[jax.ShapeDtypeStruct((n, NP), BF16), jax.ShapeDtypeStruct((n, LANE), F32)],
        scratch_shapes=[pltpu.VMEM((tm, D_MODEL), BF16)],
        compiler_params=_params("parallel", "arbitrary"),
        name="inproj",
    )(h, shift, scale, w)


HALO_ROWS = 16


def _conv3_tile(x, prev_row, next_row, w_ref, store):
    t = x.shape[0]
    e = HALO_ROWS
    w0, w1, w2 = w_ref[0:1, :], w_ref[1:2, :], w_ref[2:3, :]
    y = w0 * pltpu.roll(x, 1, 0) + w1 * x + w2 * pltpu.roll(x, t - 1, 0)
    store(slice(0, t), y)
    row = lax.broadcasted_iota(jnp.int32, (e, x.shape[1]), 0)
    store(slice(0, e), y[0:e] + jnp.where(row == 0, w0 * (prev_row - x[t - 1:t]), 0.0))
    store(slice(t - e, t), y[t - e:t] + jnp.where(row == e - 1, w2 * (next_row - x[0:1]), 0.0))


def _conv3_kernel(cur_ref, prev_ref, next_ref, w_ref, b_ref, o_ref, *, silu, nt):
    i = pl.program_id(0)
    x = cur_ref[...].astype(F32)
    pv = jnp.where(i > 0, prev_ref[HALO_ROWS - 1:HALO_ROWS, :].astype(F32), 0.0)
    nx = jnp.where(i < nt - 1, next_ref[0:1, :].astype(F32), 0.0)

    def store(rows, y):
        y = y + b_ref[...]
        o_ref[rows, :] = (_silu(y) if silu else y).astype(o_ref.dtype)

    _conv3_tile(x, pv, nx, w_ref, store)


def _halo_specs(t, ct, n, col0):
    rb = t // HALO_ROWS
    last = n // HALO_ROWS - 1
    return [pl.BlockSpec((t, ct), lambda i, j: (i, col0 + j)),
            pl.BlockSpec((HALO_ROWS, ct), lambda i, j: (jnp.maximum(i * rb - 1, 0), col0 + j)),
            pl.BlockSpec((HALO_ROWS, ct), lambda i, j: (jnp.minimum((i + 1) * rb, last), col0 + j))]


def _conv3(p, col, width, w, b, *, silu, out_dtype):
    n = p.shape[0]
    t = min(n, CONV_TOKENS)
    ct = CONV_COLS
    nt = n // t
    return pl.pallas_call(
        functools.partial(_conv3_kernel, silu=silu, nt=nt),
        grid=(nt, width // ct),
        in_specs=_halo_specs(t, ct, n, col // ct) + [
            pl.BlockSpec((3, ct), lambda i, j: (0, j)),
            pl.BlockSpec((1, ct), lambda i, j: (0, j))],
        out_specs=pl.BlockSpec((t, ct), lambda i, j: (i, j)),
        out_shape=jax.ShapeDtypeStruct((n, width), out_dtype),
        compiler_params=_params("parallel", "parallel"),
        name="conv3",
    )(p, p, p, w, b)


def _gconv_kernel(bg_ref, cc_ref, cp_ref, cn_ref, xc_ref, xp_ref, xn_ref, w_ref, o_ref, *, nt):
    i = pl.program_id(0)
    f = lambda v: v.astype(F32)
    last = slice(HALO_ROWS - 1, HALO_ROWS)
    x = f(cc_ref[...]) * f(xc_ref[...])
    pv = jnp.where(i > 0, f(cp_ref[last, :]) * f(xp_ref[last, :]), 0.0)
    nx = jnp.where(i < nt - 1, f(cn_ref[0:1, :]) * f(xn_ref[0:1, :]), 0.0)

    def store(rows, y):
        o_ref[rows, :] = f(bg_ref[rows, :]) * y

    _conv3_tile(x, pv, nx, w_ref, store)


def _gated_conv(p, w):
    n = p.shape[0]
    t = min(n, CONV_TOKENS)
    ct = CONV_COLS
    nt = n // t
    nb = SC_D // ct
    return pl.pallas_call(
        functools.partial(_gconv_kernel, nt=nt),
        grid=(nt, nb),
        in_specs=([pl.BlockSpec((t, ct), lambda i, j: (i, PD // ct + j))]
                  + _halo_specs(t, ct, n, PD // ct + nb)
                  + _halo_specs(t, ct, n, PD // ct + 2 * nb)
                  + [pl.BlockSpec((3, ct), lambda i, j: (0, j))]),
        out_specs=pl.BlockSpec((t, ct), lambda i, j: (i, j)),
        out_shape=jax.ShapeDtypeStruct((n, SC_D), F32),
        compiler_params=_params("parallel", "parallel"),
        name="gated_conv",
    )(p, p, p, p, p, p, p, w)


CONF_RB = 64


def _conf_kernel(vc, gc, vp, gp, vn, gn, w_ref, b_ref, lg_ref, lb_ref, o_ref, buf, *, t, halo, dil, nt):
    i = pl.program_id(0)
    glu = lambda v, g: v.astype(F32) * _sigmoid(g.astype(F32))
    buf[halo:halo + t, :] = glu(vc[...], gc[...])
    buf[0:halo, :] = jnp.where(i > 0, glu(vp[t - halo:t, :], gp[t - halo:t, :]), 0.0)
    buf[halo + t:halo + t + halo, :] = jnp.where(i < nt - 1, glu(vn[0:halo, :], gn[0:halo, :]), 0.0)

    def block(r0):
        acc = jnp.zeros((CONF_RB, CONF_D), F32)
        for j in range(CONF_K):
            off = halo + (j - CONF_K // 2) * dil
            acc = acc + w_ref[j:j + 1, :] * buf[pl.ds(r0 + off, CONF_RB), :]
        v = _layer_norm(acc + b_ref[...], lg_ref[...], lb_ref[...])
        o_ref[pl.ds(r0, CONF_RB), :] = _silu(v)

    if dil % CONF_RB == 0:
        def body(rb, carry):
            block(pl.multiple_of(rb * CONF_RB, CONF_RB))
            return carry
        lax.fori_loop(0, t // CONF_RB, body, 0)
    else:
        for rb in range(t // CONF_RB):
            block(rb * CONF_RB)


def _conformer(p, w, b, lg, lb, *, dil):
    n = p.shape[0]
    t = min(n, CONV_TOKENS)
    nt = n // t
    halo =-(-(CONF_K // 2) * dil // SUBLANE) * SUBLANE
    assert halo <= t
    cb = PA // CONF_D

    def spec(col, shift):
        return pl.BlockSpec((t, CONF_D), lambda i: (jnp.clip(i + shift, 0, nt - 1), col))

    vec = pl.BlockSpec((1, CONF_D), lambda i: (0, 0))
    return pl.pallas_call(
        functools.partial(_conf_kernel, t=t, halo=halo, dil=dil, nt=nt),
        grid=(nt,),
        in_specs=[spec(cb, 0), spec(cb + 1, 0), spec(cb, -1), spec(cb + 1, -1), spec(cb, 1), spec(cb + 1, 1),
                  pl.BlockSpec((CONF_K, CONF_D), lambda i: (0, 0)), vec, vec, vec],
        out_specs=pl.BlockSpec((t, CONF_D), lambda i: (i, 0)),
        out_shape=jax.ShapeDtypeStruct((n, CONF_D), F32),
        scratch_shapes=[pltpu.VMEM((t + 2 * halo, CONF_D), F32)],
        compiler_params=_params("parallel"),
        name="conformer",
    )(p, p, p, p, p, p, w, b, lg, lb)


CONF_COLS = 16
CONF_ROWS_PER_ITER = 4


def _conf_grid_kernel(v_ref, g_ref, w_ref, b_ref, lg_ref, lb_ref, o_ref, buf, *, rows):
    half = CONF_K // 2
    zeros = jnp.zeros((half,) + buf.shape[1:], F32)
    buf[0:half] = zeros
    buf[half + rows:half + rows + half] = zeros
    buf[half:half + rows] = v_ref[...].astype(F32) * _sigmoid(g_ref[...].astype(F32))

    rb = CONF_ROWS_PER_ITER

    def body(it, carry):
        r0 = it * rb
        accs = [None] * rb
        for j in range(CONF_K):
            wj = w_ref[j:j + 1, :]
            for s in range(rb):
                term = wj * buf[r0 + s + j]
                accs[s] = term if j == 0 else accs[s] + term
        for s in range(rb):
            v = _layer_norm(accs[s] + b_ref[...], lg_ref[...], lb_ref[...])
            o_ref[r0 + s] = _silu(v).astype(o_ref.dtype)
        return carry

    lax.fori_loop(0, rows // rb, body, 0)


def _conformer_grid(p, w, b, lg, lb):
    n = p.shape[0]
    rows = n // GRID_W
    p3 = p.reshape(rows, GRID_W, p.shape[1])
    cb = PA // CONF_D
    vec = pl.BlockSpec((1, CONF_D), lambda j: (0, 0))
    blk = lambda col: pl.BlockSpec((rows, CONF_COLS, CONF_D), lambda j: (0, j, col))
    out = pl.pallas_call(
        functools.partial(_conf_grid_kernel, rows=rows),
        grid=(GRID_W // CONF_COLS,),
        in_specs=[blk(cb), blk(cb + 1), pl.BlockSpec((CONF_K, CONF_D), lambda j: (0, 0)), vec, vec, vec],
        out_specs=blk(0),
        out_shape=jax.ShapeDtypeStruct((rows, GRID_W, CONF_D), BF16),
        scratch_shapes=[pltpu.VMEM((rows + 2 * (CONF_K // 2), CONF_COLS, CONF_D), F32)],
        compiler_params=_params("parallel"),
        name="conformer_grid",
    )(p3, p3, w, b, lg, lb)
    return out.reshape(n, CONF_D)


SSD_STATE_SHAPE = (SSD_HEADS // 2, SSD_STATE, 2 * SSD_HEAD_DIM)
SSD_CHUNKS_PER_STEP = 2


def _ssd_kernel(xbc_ref, dt_ref, dtb_ref, alog_ref, init_ref, y_ref, fin_ref, h_ref, *, ns, cps):
    d = pl.program_id(0)
    c = pl.program_id(1)
    q = SSD_CHUNK
    hd = SSD_HEAD_DIM

    @pl.when(c == 0)
    def _():
        h_ref[...] = init_ref[...]

    lane = lax.broadcasted_iota(jnp.int32, (q, LANE), 1)
    head = lane < SSD_HEADS
    first_half = lane < hd
    li = lax.broadcasted_iota(jnp.int32, (q, q), 0)
    si = lax.broadcasted_iota(jnp.int32, (q, q), 1)
    mask = (li - si) * (1 - 2 * d) >= 0
    tri = mask.astype(F32)
    a_rate = -jnp.exp(alog_ref[...])

    def one_chunk(r0):
        rows = pl.ds(r0, q)
        raw = dt_ref[rows, :] + dtb_ref[...]
        dt_all = jnp.maximum(raw, 0.0) + jnp.log(1.0 + jnp.exp(-jnp.abs(raw)))
        ld_all = dt_all * a_rate
        dt_d = jnp.where(head, jnp.where(d == 0, dt_all, pltpu.roll(dt_all, LANE - SSD_HEADS, 1)), 0.0)
        ld_d = jnp.where(head, jnp.where(d == 0, ld_all, pltpu.roll(ld_all, LANE - SSD_HEADS, 1)), 0.0)
        cum = jnp.dot(tri, ld_d, precision=HIGHEST, preferred_element_type=F32)
        tot = jnp.sum(ld_d, axis=0, keepdims=True)
        cum_t = cum.T
        dt_t = dt_d.T
        w_t = (jnp.exp(tot - cum) * dt_d).T
        a_out = jnp.exp(cum)
        e_tot = jnp.exp(tot)

        groups = {}

        def group(g):
            if g not in groups:
                bg = xbc_ref[rows, SSD_D + g * SSD_STATE:SSD_D + (g + 1) * SSD_STATE].astype(F32)
                cg = xbc_ref[rows, SSD_D + SSD_BC + g * SSD_STATE:SSD_D + SSD_BC + (g + 1) * SSD_STATE].astype(F32)
                bg_t = bg.T
                groups[g] = (bg_t, cg, _dot(cg.astype(BF16), bg_t.astype(BF16)))
            return groups[g]

        def head_terms(hh):
            bg_t, cg, cb = group(hh // SSD_HPG)
            diff = cum[:, hh:hh + 1] - cum_t[hh:hh + 1, :]
            dec = jnp.exp(jnp.where(mask, diff, -1e30))
            m = (cb * dec * dt_t[hh:hh + 1, :]).astype(BF16)
            cs = (cg * a_out[:, hh:hh + 1]).astype(BF16)
            bw = (bg_t * w_t[hh:hh + 1, :]).astype(BF16)
            return m, cs, bw

        zero = jnp.zeros((q, LANE), BF16)
        for j in range(SSD_HEADS // 2):
            ma, ca, wa = head_terms(2 * j)
            mb, cb_, wb = head_terms(2 * j + 1)
            xp = xbc_ref[rows, j * LANE:(j + 1) * LANE].astype(BF16)
            hp = h_ref[j]
            hp_b = hp.astype(BF16)
            x_bd = jnp.concatenate([jnp.where(first_half, xp, zero), jnp.where(first_half, zero, xp)], axis=0)
            h_bd = jnp.concatenate([jnp.where(first_half, hp_b, zero), jnp.where(first_half, zero, hp_b)], axis=0)
            y_ref[rows, j * LANE:(j + 1) * LANE] = _dot(jnp.concatenate([ma, mb, ca, cb_], axis=1),
                                                        jnp.concatenate([x_bd, h_bd], axis=0))
            s_new = _dot(jnp.concatenate([wa, wb], axis=1), x_bd)
            e_pair = jnp.where(first_half[0:1, :], e_tot[:, 2 * j:2 * j + 1], e_tot[:, 2 * j + 1:2 * j + 2])
            h_ref[j] = e_pair * hp + s_new

    for k in range(cps):
        one_chunk(pl.multiple_of(jnp.where(d == 0, k, cps - 1 - k) * q, q))

    @pl.when(c == ns - 1)
    def _():
        fin_ref[...] = h_ref[...]


def _ssd_scan(xbc, p, dt_bias, a_log, init):
    n = xbc.shape[0]
    cps = SSD_CHUNKS_PER_STEP
    q = SSD_CHUNK * cps
    ns = n // q

    def chunk(d, c):
        return jnp.where(d == 0, c, ns - 1 - c)

    st = SSD_STATE_SHAPE
    vec = pl.BlockSpec((1, LANE), lambda d, c: (0, 0))
    return pl.pallas_call(
        functools.partial(_ssd_kernel, ns=ns, cps=cps),
        grid=(2, ns),
        in_specs=[pl.BlockSpec((q, SSD_XBC), lambda d, c: (chunk(d, c), 0)),
                  pl.BlockSpec((q, LANE), lambda d, c: (chunk(d, c), 0)),
                  vec, vec,
                  pl.BlockSpec((None,) + st, lambda d, c: (d, 0, 0, 0))],
        out_specs=[pl.BlockSpec((None, q, SSD_D), lambda d, c: (d, chunk(d, c), 0)),
                   pl.BlockSpec((None,) + st, lambda d, c: (d, 0, 0, 0))],
        out_shape=[jax.ShapeDtypeStruct((2, n, SSD_D), F32),
                   jax.ShapeDtypeStruct((2,) + st, F32)],
        scratch_shapes=[pltpu.VMEM(st, F32)],
        compiler_params=_params("arbitrary", "arbitrary"),
        name="ssd_scan",
    )(xbc, p, dt_bias, a_log, init)


def _filt_kernel(feat_ref, w1_ref, b1_ref, w2_ref, b2_ref, fr_ref, w3h_ref, w3l_ref, dl_ref, k_ref, nrm_ref, *,
                 n, t):
    i = pl.program_id(0)
    hf = t // 2
    feat = feat_ref[...]
    x = jnp.concatenate([feat[0:hf], feat[hf:t]], axis=1)
    hid = jnp.sin(fr_ref[0:1, :] * (jnp.dot(x, w1_ref[...], precision=HIGHEST,
                                            preferred_element_type=F32) + b1_ref[...]))
    hid = jnp.sin(fr_ref[1:2, :] * (jnp.dot(hid, w2_ref[...], precision=HIGHEST,
                                            preferred_element_type=F32) + b2_ref[...]))
    hi = hid.astype(BF16)
    lo = (hid - hi.astype(F32)).astype(BF16)

    @pl.when(i == 0)
    def _():
        nrm_ref[...] = jnp.zeros_like(nrm_ref)

    for half in range(2):
        wh, wl = w3h_ref[half], w3l_ref[half]
        filt = _dot(hi, wh) + _dot(lo, wh) + _dot(hi, wl)
        filt = filt * jnp.exp(-feat[half * hf:(half + 1) * hf, 0:1] * dl_ref[...])
        row = i * t + half * hf + lax.broadcasted_iota(jnp.int32, filt.shape, 0)
        filt = jnp.where(row == n, 0.0, filt)
        k_ref[half * hf:(half + 1) * hf, :] = filt
        nrm_ref[...] += jnp.sum(jnp.abs(filt), axis=0, keepdims=True)


def _hyena_filters(featx, fw, n):
    t = min(n, 512)
    half = n // t
    oc = HY_ORDER * HY_D
    full = lambda shape: pl.BlockSpec(shape, lambda i: tuple(0 for _ in shape))
    w3spec = pl.BlockSpec((None, 2, LANE, oc), lambda i: (i // half, 0, 0, 0))
    return pl.pallas_call(
        functools.partial(_filt_kernel, n=n, t=t),
        grid=(2 * n // t,),
        in_specs=[pl.BlockSpec((t, LANE), lambda i: (i, 0)),
                  full((2 * LANE, LANE)), full((1, LANE)), full((LANE, LANE)), full((1, LANE)),
                  full((2, LANE)), w3spec, w3spec,
                  pl.BlockSpec((None, 1, oc), lambda i: (i // half, 0, 0))],
        out_specs=[pl.BlockSpec((t, oc), lambda i: (i, 0)),
                   pl.BlockSpec((1, oc), lambda i: (0, 0))],
        out_shape=[jax.ShapeDtypeStruct((2 * n, oc), F32),
                   jax.ShapeDtypeStruct((1, oc), F32)],
        compiler_params=_params("arbitrary"),
        name="hyena_filters",
    )(featx, fw["w1"], fw["b1"], fw["w2"], fw["b2"], fw["freq"], fw["w3h"], fw["w3l"], fw["deltas"])


def _filter_weights(w1, b1, w2, b2, freq, w3, deltas_d):
    hh = HY_HID
    z = lambda r, c: jnp.zeros((r, c), F32)
    w1p = jnp.pad(w1, ((0, LANE - HY_EMB), (0, 0)))
    w1b = jnp.concatenate([jnp.concatenate([w1p, z(LANE, hh)], 1),
                           jnp.concatenate([z(LANE, hh), w1p], 1)], 0)
    w2b = jnp.concatenate([jnp.concatenate([w2, z(hh, hh)], 1),
                           jnp.concatenate([z(hh, hh), w2], 1)], 0)
    two = lambda v: jnp.concatenate([v, v], axis=-1)
    w3d = w3.reshape(hh, HY_ORDER, 2, HY_D).transpose(2, 0, 1, 3).reshape(2, hh, HY_ORDER * HY_D)
    zz = jnp.zeros_like(w3d)
    w3x = jnp.stack([jnp.concatenate([w3d, zz], 1), jnp.concatenate([zz, w3d], 1)], axis=1)
    w3h = w3x.astype(BF16)
    w3l = (w3x - w3h.astype(F32)).astype(BF16)
    return dict(w1=w1b, b1=two(b1[None, :]), w2=w2b, b2=two(b2[None, :]), freq=two(freq), w3h=w3h, w3l=w3l,
                deltas=deltas_d)


DFT_LANES = 8192


def _dft_rows_kernel(f_ref, x_ref, o_ref, *, nj):
    x = jnp.concatenate([x_ref[:, jj, :] for jj in range(nj)], axis=1)
    o_ref[...] = _dot(f_ref[...], x.astype(BF16)).astype(o_ref.dtype)


def _dft_rows(fmat, x3, col, width):
    m, k = fmat.shape
    n2 = x3.shape[1]
    nj = min(DFT_LANES // width, n2)
    return pl.pallas_call(
        functools.partial(_dft_rows_kernel, nj=nj),
        grid=(n2 // nj,),
        in_specs=[pl.BlockSpec((m, k), lambda j: (0, 0)),
                  pl.BlockSpec((k, nj, width), lambda j: (0, j, col))],
        out_specs=pl.BlockSpec((m, nj * width), lambda j: (0, j)),
        out_shape=jax.ShapeDtypeStruct((m, n2 * width), BF16),
        compiler_params=_params("parallel"),
        name="dft_rows",
    )(fmat, x3)


def _spec_kernel(ar_ref, ai_ref, gr_ref, gi_ref, kr_ref, ki_ref):
    ar, ai, gr, gi = ar_ref[...], ai_ref[...], gr_ref[...], gi_ref[...]
    kr_ref[...] = (_dot(gr, ar) - _dot(gi, ai)).astype(BF16)
    ki_ref[...] = (_dot(gr, ai) + _dot(gi, ar)).astype(BF16)


def _filter_spectrum(a4, gr, gi):
    _, _, n2, ch = a4.shape
    nh = gr.shape[0]
    ct = ch
    blk = lambda ri: pl.BlockSpec((None, None, n2, ct), lambda f, j: (ri, f, 0, j))
    gspec = pl.BlockSpec((None, n2, n2), lambda f, j: (f, 0, 0))
    ospec = pl.BlockSpec((None, n2, ct), lambda f, j: (f, 0, j))
    return pl.pallas_call(
        _spec_kernel,
        grid=(nh, ch // ct),
        in_specs=[blk(0), blk(1), gspec, gspec],
        out_specs=[ospec, ospec],
        out_shape=[jax.ShapeDtypeStruct((nh, n2, ch), BF16)] * 2,
        compiler_params=_params("parallel", "parallel"),
        name="filter_spectrum",
    )(a4, a4, gr, gi)


MID_FREQS = 4


def _mid_kernel(ar_ref, ai_ref, gr_ref, gi_ref, grt_ref, git_ref, kr_ref, ki_ref, br_ref, bi_ref, *, nh):
    for k in range(MID_FREQS):
        f = pl.program_id(0) * MID_FREQS + k

        @pl.when(f < nh)
        def _(k=k):
            ar, ai, gr, gi = ar_ref[k], ai_ref[k], gr_ref[k], gi_ref[k]
            xr = _dot(gr, ar) - _dot(gi, ai)
            xi = _dot(gr, ai) + _dot(gi, ar)
            kr, ki = kr_ref[k].astype(F32), ki_ref[k].astype(F32)
            yr = (xr * kr - xi * ki).astype(BF16)
            yi = (xr * ki + xi * kr).astype(BF16)
            grt, git = grt_ref[k], git_ref[k]
            br_ref[k] = (_dot(grt, yr) + _dot(git, yi)).astype(BF16)
            bi_ref[k] = (_dot(grt, yi) - _dot(git, yr)).astype(BF16)

        @pl.when(f >= nh)
        def _(k=k):
            br_ref[k] = jnp.zeros(br_ref.shape[1:], BF16)
            bi_ref[k] = jnp.zeros(bi_ref.shape[1:], BF16)


def _hyena_mid(a4, tabs, kf_r, kf_i, order):
    _, nf, n2, ch = a4.shape
    gr, gi, grt, git = tabs
    nh = gr.shape[0]
    mf = MID_FREQS
    assert nf % mf == 0
    fi = lambda s: jnp.minimum(s, (nh - 1) // mf)
    blk = lambda ri: pl.BlockSpec((None, mf, n2, ch), lambda s: (ri, fi(s), 0, 0))
    kspec = pl.BlockSpec((mf, n2, ch), lambda s: (fi(s), 0, order))
    gspec = pl.BlockSpec((mf, n2, n2), lambda s: (fi(s), 0, 0))
    ospec = pl.BlockSpec((mf, n2, ch), lambda s: (s, 0, 0))
    return pl.pallas_call(
        functools.partial(_mid_kernel, nh=nh),
        grid=(nf // mf,),
        in_specs=[blk(0), blk(1), gspec, gspec, gspec, gspec, kspec, kspec],
        out_specs=[ospec, ospec],
        out_shape=[jax.ShapeDtypeStruct((nf, n2, ch), BF16)] * 2,
        compiler_params=_params("parallel"),
        name="hyena_mid",
    )(a4, a4, gr, gi, grt, git, kf_r, kf_i)


def _inv_kernel(f_ref, br_ref, bi_ref, s_ref, bias_ref, z_ref, g_ref, o_ref, *, nf, nj, ch):
    acc = _dot(f_ref[:, 0:nf], br_ref[...]) + _dot(f_ref[:, nf:2 * nf], bi_ref[...])
    for jj in range(nj):
        y = acc[:, jj * ch:(jj + 1) * ch] * s_ref[...]
        o_ref[:, jj, :] = g_ref[:, jj, :] * (y + bias_ref[...] * z_ref[:, jj, :])


def _hyena_inverse(finv, b_r, b_i, scale, bias, z3, zcol, g3, gcol):
    t1, k2 = finv.shape
    nf = k2 // 2
    n2 = z3.shape[1]
    ch = HY_D
    nj = min(DFT_LANES // ch, n2)
    col = pl.BlockSpec((nf, nj * ch), lambda j: (0, j))
    row = pl.BlockSpec((1, ch), lambda j: (0, 0))
    return pl.pallas_call(
        functools.partial(_inv_kernel, nf=nf, nj=nj, ch=ch),
        grid=(n2 // nj,),
        in_specs=[pl.BlockSpec((t1, k2), lambda j: (0, 0)), col, col, row, row,
                  pl.BlockSpec((t1, nj, ch), lambda j: (0, j, zcol)),
                  pl.BlockSpec((t1, nj, ch), lambda j: (0, j, gcol))],
        out_specs=pl.BlockSpec((t1, nj, ch), lambda j: (0, j, 0)),
        out_shape=jax.ShapeDtypeStruct((t1, n2, ch), F32),
        compiler_params=_params("parallel"),
        name="hyena_inverse",
    )(finv, b_r, b_i, scale, bias, z3, g3)


def _hyena_nf(n):
    nh = (2 * n // FFT_N2) // 2 + 1
    return -(-nh // 16) * 16


def _dft_tables(n):
    n2 = FFT_N2
    n1 = 2 * n // n2
    tot = 2 * n
    two_pi = 2.0 * math.pi

    def cs(num, den):
        ang = (two_pi / den) * (num % den).astype(F32)
        return jnp.cos(ang), jnp.sin(ang)

    nh = n1 // 2 + 1
    nf = _hyena_nf(n)
    f1 = jnp.arange(nh, dtype=jnp.int32)
    t1 = jnp.arange(n1, dtype=jnp.int32)
    c1, s1 = cs(f1[:, None] * t1[None, :], n1)
    zrow = jnp.zeros((nf - nh, n1), F32)
    fwd_full = jnp.concatenate([c1, zrow, -s1, zrow], axis=0).astype(BF16)
    fwd_half = fwd_full[:, :n1 // 2]
    wgt = jnp.where((f1 == 0) | (f1 == n1 // 2), 1.0, 2.0)[:, None]
    zcol = jnp.zeros((n1 // 2, nf - nh), F32)
    inv = jnp.concatenate([(wgt * c1[:, :n1 // 2]).T, zcol, -(wgt * s1[:, :n1 // 2]).T, zcol],
                          axis=1).astype(BF16)
    t2 = jnp.arange(n2, dtype=jnp.int32)
    twr, twi = cs(f1[:, None] * t2[None, :], tot)
    fr, fi = cs(t2[:, None] * t2[None, :], n2)
    twi, fi = -twi, -fi
    gr = twr[:, None, :] * fr[None] - twi[:, None, :] * fi[None]
    gi = twr[:, None, :] * fi[None] + twi[:, None, :] * fr[None]
    tabs = (gr.astype(BF16), gi.astype(BF16),
            gr.transpose(0, 2, 1).astype(BF16), gi.transpose(0, 2, 1).astype(BF16))
    return fwd_full, fwd_half, inv, tabs


def _hyena_long(q, k2, nrm, hy_bias, tables):
    n = q.shape[0]
    n2 = FFT_N2
    n1 = 2 * n // n2
    fwd_full, fwd_half, inv, tabs = tables
    nf = _hyena_nf(n)
    oc = HY_ORDER * HY_D
    ak = _dft_rows(fwd_full, k2.reshape(n1, n2, oc), 0, oc).reshape(2, nf, n2, oc)
    kf_r, kf_i = _filter_spectrum(ak, tabs[0], tabs[1])
    q3 = q.reshape(n1 // 2, n2, 3 * HY_D)
    z3, zcol = q3, 0
    for o in range(HY_ORDER):
        a4 = _dft_rows(fwd_half, z3, zcol, HY_D).reshape(2, nf, n2, HY_D)
        b_r, b_i = _hyena_mid(a4, tabs, kf_r, kf_i, o)
        scale = 1.0 / (2.0 * n * nrm[:, o * HY_D:(o + 1) * HY_D])
        z3 = _hyena_inverse(inv, b_r.reshape(nf, n2 * HY_D), b_i.reshape(nf, n2 * HY_D), scale,
                            hy_bias[o][None, :], z3, zcol, q3, o + 1)
        zcol = 0
    return z3.reshape(n, HY_D)


def _hy_ctx_kernel(v_ref, x1_ref, x2_ref, k0_ref, k1_ref, n0_ref, n1_ref, bias_ref, o_ref, kf, zs, *, n):
    zs[...] = v_ref[...]
    for o, (k_ref, nr_ref, x_ref) in enumerate(((k0_ref, n0_ref, x1_ref), (k1_ref, n1_ref, x2_ref))):
        kf[0:n, :] = k_ref[n:2 * n, :]
        kf[n:2 * n, :] = k_ref[0:n, :]

        def body(s, acc):
            return acc + kf[pl.ds(n - s, n), :] * zs[pl.ds(s, 1), :]

        acc = lax.fori_loop(0, n, body, jnp.zeros((n, LANE), F32))
        z = zs[...]
        zs[...] = x_ref[...] * (acc / nr_ref[...] + bias_ref[o:o + 1, :] * z)
    o_ref[...] = zs[...]


def _hyena_ctx(q, k2, nrm, hy_bias):
    n = q.shape[0]
    nb = HY_D // LANE
    col = lambda c0: pl.BlockSpec((n, LANE), lambda j: (0, c0 + j))
    kcol = lambda c0: pl.BlockSpec((2 * n, LANE), lambda j: (0, c0 + j))
    ncol = lambda c0: pl.BlockSpec((1, LANE), lambda j: (0, c0 + j))
    return pl.pallas_call(
        functools.partial(_hy_ctx_kernel, n=n),
        grid=(nb,),
        in_specs=[col(0), col(nb), col(2 * nb), kcol(0), kcol(nb), ncol(0), ncol(nb),
                  pl.BlockSpec((HY_ORDER, LANE), lambda j: (0, j))],
        out_specs=pl.BlockSpec((n, LANE), lambda j: (0, j)),
        out_shape=jax.ShapeDtypeStruct((n, HY_D), F32),
        scratch_shapes=[pltpu.VMEM((2 * n, LANE), F32), pltpu.VMEM((n, LANE), F32)],
        compiler_params=_params("parallel"),
        name="hyena_ctx",
    )(q, q, q, k2, k2, nrm, nrm, hy_bias)


def _merge_kernel(ya_ref, xs_ref, yf_ref, yb_ref, z_ref, yc_ref, yd_ref, g_ref, h_ref,
                  dv_ref, ng_ref, g1_ref, lg_ref, lb_ref,
                  wa_ref, wb_ref, wc_ref, wd_ref, wo_ref, o_ref):
    y = xs_ref[...].astype(F32) * dv_ref[...] + yf_ref[...] + yb_ref[...]
    gz = y * _silu(z_ref[...].astype(F32))
    ssd = gz * lax.rsqrt(jnp.mean(gz * gz, -1, keepdims=True) + LN_EPS) * ng_ref[...]
    d = D_MODEL
    gate = lambda k: jnp.tanh(g_ref[:, k * d:(k + 1) * d].astype(F32)) + 1.0
    m = gate(0) * _dot(ya_ref[...].astype(BF16), wa_ref[...])
    m = m + gate(1) * _dot(ssd.astype(BF16), wb_ref[...])
    m = m + gate(2) * _dot(yc_ref[...].astype(BF16), wc_ref[...])
    m = m + gate(3) * _dot(yd_ref[...].astype(BF16), wd_ref[...])
    mix = _dot(m.astype(BF16), wo_ref[...])
    o_ref[...] = _layer_norm(DN_ALPHA * h_ref[...] + g1_ref[...] * mix, lg_ref[...], lb_ref[...])


def _merge(ya, xbc, ydir, p, yc, yd, h, dvec, ng, gate1, lg, lb, wa, wb, wc, wd, wo):
    n = h.shape[0]
    t = MERGE_T
    tok = lambda w, col=0: pl.BlockSpec((t, w), lambda i: (i, col))
    vec = lambda w: pl.BlockSpec((1, w), lambda i: (0, 0))
    mat = lambda r: pl.BlockSpec((r, D_MODEL), lambda i: (0, 0))
    return pl.pallas_call(
        _merge_kernel,
        grid=(n // t,),
        in_specs=[tok(CONF_D), tok(SSD_D),
                  pl.BlockSpec((None, t, SSD_D), lambda i: (0, i, 0)),
                  pl.BlockSpec((None, t, SSD_D), lambda i: (1, i, 0)),
                  tok(SSD_D, PZ // SSD_D), tok(HY_D), tok(SC_D), tok(N_BRANCH * D_MODEL, 0), tok(D_MODEL),
                  vec(SSD_D), vec(SSD_D), vec(D_MODEL), vec(D_MODEL), vec(D_MODEL),
                  mat(CONF_D), mat(SSD_D), mat(HY_D), mat(SC_D), mat(D_MODEL)],
        out_specs=tok(D_MODEL),
        out_shape=jax.ShapeDtypeStruct((n, D_MODEL), F32),
        compiler_params=_params("parallel"),
        name="merge",
    )(ya, xbc, ydir, ydir, p, yc, yd, p, h, dvec, ng, gate1, lg, lb, wa, wb, wc, wd, wo)


MOE_T = 256


def _stream_tiles(hs):
    tiles = [h.shape[0] // MOE_T for h in hs]
    first = [sum(tiles[:s]) for s in range(len(hs))]
    return tiles, first


def _stream_specs(tiles, first, width):
    return [pl.BlockSpec((MOE_T, width), lambda i, nt=nt, f=f: (jnp.clip(i - f, 0, nt - 1), 0))
            for nt, f in zip(tiles, first)]


def _stream_vec_spec(first):
    def index(i):
        s = 0
        for f in first[1:]:
            s = s + (i >= f).astype(jnp.int32)
        return (s, 0, 0)
    return pl.BlockSpec((None, 1, D_MODEL), index)


def _stream_tile(i, refs, first):
    x = refs[0][...]
    for r, f in zip(refs[1:], first[1:]):
        x = jnp.where(i >= f, r[...], x)
    return x


def _router_kernel(*refs, first):
    ns = len(first)
    h_refs = refs[:ns]
    sh_ref, sc_ref, wh_ref, wl_ref, b_ref, sel_ref, cnt_ref, selt_ref = refs[ns:]
    i = pl.program_id(0)

    @pl.when(i == 0)
    def _():
        cnt_ref[...] = jnp.zeros_like(cnt_ref)

    u = _stream_tile(i, h_refs, first) * (1.0 + sc_ref[...]) + sh_ref[...]
    u_hi = u.astype(BF16)
    u_lo = (u - u_hi.astype(F32)).astype(BF16)
    lg = _dot(u_hi, wh_ref[...]) + _dot(u_lo, wh_ref[...]) + _dot(u_hi, wl_ref[...]) + b_ref[...]
    lane = lax.broadcasted_iota(jnp.int32, lg.shape, 1).astype(F32)
    neg = -1e30
    big = 1e9
    gl = jnp.where(lane < MOE_GROUPS, lg, neg)
    gmax = jnp.max(gl, -1, keepdims=True)
    gsel = jnp.min(jnp.where(gl == gmax, lane, big), -1, keepdims=True)
    gprob = 1.0 / jnp.sum(jnp.where(lane < MOE_GROUPS, jnp.exp(lg - gmax), 0.0), -1, keepdims=True)
    lo = MOE_GROUPS + gsel * MOE_EPG
    el = jnp.where(jnp.abs(lane - lo - (MOE_EPG - 1) / 2.0) < MOE_EPG / 2.0, lg, neg)
    m1 = jnp.max(el, -1, keepdims=True)
    i1 = jnp.min(jnp.where(el == m1, lane, big), -1, keepdims=True)
    el2 = jnp.where(lane == i1, neg, el)
    m2 = jnp.max(el2, -1, keepdims=True)
    i2 = jnp.min(jnp.where(el2 == m2, lane, big), -1, keepdims=True)
    t = jnp.exp(m2 - m1)
    w1 = gprob / (1.0 + t)
    w2 = gprob * t / (1.0 + t)
    oh1 = jnp.where(lane == i1, 1.0, 0.0)
    oh2 = jnp.where(lane == i2, 1.0, 0.0)
    oh = oh1 + oh2
    tt = lg.shape[0]
    li = lax.broadcasted_iota(jnp.int32, (tt, tt), 0)
    si = lax.broadcasted_iota(jnp.int32, (tt, tt), 1)
    before = _dot(jnp.where(li > si, 1.0, 0.0).astype(BF16), oh.astype(BF16)) + cnt_ref[...]
    r1 = jnp.sum(oh1 * before, -1, keepdims=True)
    r2 = jnp.sum(oh2 * before, -1, keepdims=True)
    cnt_ref[...] += jnp.sum(oh, axis=0, keepdims=True)
    cols = (i1 - MOE_GROUPS, i2 - MOE_GROUPS, w1, w2, r1, r2)
    sel = jnp.zeros_like(lg)
    for k, v in enumerate(cols):
        sel = jnp.where(lane == k, v, sel)
    sel_ref[...] = sel
    selt_ref[...] = sel.T[0:SUBLANE, :]


def _router(hs, shift, scale, wr, br):
    tiles, first = _stream_tiles(hs)
    t = MOE_T
    n = t * sum(tiles)
    vec = lambda w: pl.BlockSpec((1, w), lambda i: (0, 0))
    wr_hi = wr.astype(BF16)
    wr_lo = (wr - wr_hi.astype(F32)).astype(BF16)
    wspec = pl.BlockSpec((D_MODEL, LANE), lambda i: (0, 0))
    return pl.pallas_call(
        functools.partial(_router_kernel, first=first),
        grid=(sum(tiles),),
        in_specs=_stream_specs(tiles, first, D_MODEL) + [
            _stream_vec_spec(first), _stream_vec_spec(first), wspec, wspec, vec(LANE)],
        out_specs=[pl.BlockSpec((t, LANE), lambda i: (i, 0)), vec(LANE),
                   pl.BlockSpec((SUBLANE, t), lambda i: (0, i))],
        out_shape=[jax.ShapeDtypeStruct((n, LANE), F32), jax.ShapeDtypeStruct((1, LANE), F32),
                   jax.ShapeDtypeStruct((SUBLANE, n), F32)],
        compiler_params=_params("arbitrary"),
        name="router",
    )(*hs, shift, scale, wr_hi, wr_lo, br)


ROW_WORDS = D_MODEL // 2


def _pack_rows(x):
    c = x.shape[1] // 2
    bits = lambda v: lax.bitcast_convert_type(v.astype(BF16).astype(F32), jnp.uint32)
    return bits(x[:, :c]) | (bits(x[:, c:]) >> 16)


def _unpack_rows(w):
    hi = lax.bitcast_convert_type(w & jnp.uint32(0xFFFF0000), F32)
    lo = lax.bitcast_convert_type(w << 16, F32)
    return jnp.concatenate([hi, lo], axis=1)


def _dispatch_kernel(dst_ref, *refs, first, nt):
    ns = len(first)
    h_refs = refs[:ns]
    sh_ref, sc_ref, zero_hbm, xin_hbm, ubuf, sem = refs[ns:]
    del zero_hbm
    t = MOE_T
    i = pl.program_id(0)
    slot = i % 2

    def wait_slot(s):
        for _ in range(MOE_TOP_K):
            pltpu.make_async_copy(ubuf.at[s], xin_hbm.at[pl.ds(0, t), :], sem.at[s]).wait()

    @pl.when(i >= 2)
    def _():
        wait_slot(slot)

    ubuf[slot] = _pack_rows(_stream_tile(i, h_refs, first) * (1.0 + sc_ref[...]) + sh_ref[...])

    def issue(r, carry):
        for k in range(MOE_TOP_K):
            pltpu.make_async_copy(ubuf.at[slot, pl.ds(r, 1), :],
                                  xin_hbm.at[pl.ds(dst_ref[0, 0, k * t + r], 1), :], sem.at[slot]).start()
        return carry

    lax.fori_loop(0, t, issue, 0, unroll=16)

    @pl.when(i == nt - 1)
    def _():
        wait_slot(slot)
        if nt > 1:
            wait_slot(1 - slot)


def _dispatch(pos_t, hs, shift, scale, n_rows):
    tiles, first = _stream_tiles(hs)
    t = MOE_T
    nt = sum(tiles)
    return pl.pallas_call(
        functools.partial(_dispatch_kernel, first=first, nt=nt),
        grid=(nt,),
        in_specs=([pl.BlockSpec((1, 1, MOE_TOP_K * t), lambda i: (i, 0, 0), memory_space=pltpu.SMEM)]
                  + _stream_specs(tiles, first, D_MODEL)
                  + [_stream_vec_spec(first), _stream_vec_spec(first), pl.BlockSpec(memory_space=pl.ANY)]),
        out_specs=pl.BlockSpec(memory_space=pl.ANY),
        out_shape=jax.ShapeDtypeStruct((n_rows, ROW_WORDS), jnp.uint32),
        scratch_shapes=[pltpu.VMEM((2, t, ROW_WORDS), jnp.uint32), pltpu.SemaphoreType.DMA((2,))],
        input_output_aliases={3 + len(hs): 0},
        compiler_params=_params("arbitrary"),
        name="dispatch",
    )(pos_t, *hs, shift, scale, jnp.zeros((n_rows, ROW_WORDS), jnp.uint32))


def _expert_kernel(be_ref, nu_ref, x_ref, wg_ref, wu_ref, wd_ref, o_ref, wgb, wub, wdb):
    b = pl.program_id(0)

    @pl.when((b == 0) | (be_ref[b] != be_ref[jnp.maximum(b - 1, 0)]))
    def _():
        wgb[...] = wg_ref[...].astype(BF16)
        wub[...] = wu_ref[...].astype(BF16)
        wdb[...] = wd_ref[...].astype(BF16)

    @pl.when(b < nu_ref[0])
    def _():
        x = _unpack_rows(x_ref[...]).astype(BF16)
        hid = _silu(_dot(x, wgb[...])) * _dot(x, wub[...])
        o_ref[...] = _pack_rows(_dot(hid.astype(BF16), wdb[...]))

    @pl.when(b >= nu_ref[0])
    def _():
        o_ref[...] = jnp.zeros_like(o_ref)


def _experts(xin, block_e, n_used, wg, wu, wd, layer):
    n_blocks = block_e.shape[0]
    gs = pltpu.PrefetchScalarGridSpec(
        num_scalar_prefetch=2,
        grid=(n_blocks,),
        in_specs=[pl.BlockSpec((MOE_ROWS, ROW_WORDS), lambda b, be, nu: (b, 0)),
                  pl.BlockSpec((None, None, D_MODEL, MOE_FF), lambda b, be, nu: (layer, be[b], 0, 0)),
                  pl.BlockSpec((None, None, D_MODEL, MOE_FF), lambda b, be, nu: (layer, be[b], 0, 0)),
                  pl.BlockSpec((None, None, MOE_FF, D_MODEL), lambda b, be, nu: (layer, be[b], 0, 0))],
        out_specs=pl.BlockSpec((MOE_ROWS, ROW_WORDS), lambda b, be, nu: (b, 0)),
        scratch_shapes=[pltpu.VMEM((D_MODEL, MOE_FF), BF16), pltpu.VMEM((D_MODEL, MOE_FF), BF16),
                        pltpu.VMEM((MOE_FF, D_MODEL), BF16)],
    )
    return pl.pallas_call(
        _expert_kernel,
        grid_spec=gs,
        out_shape=jax.ShapeDtypeStruct((n_blocks * MOE_ROWS, ROW_WORDS), jnp.uint32),
        compiler_params=_params("arbitrary"),
        name="experts",
    )(block_e, n_used, xin, wg, wu, wd)


def _combine_kernel(pos_ref, posn_ref, y_hbm, *refs, first, tiles):
    ns = len(first)
    nt = sum(tiles)
    h_refs = refs[:ns]
    sel_ref, g2_ref, lg_ref, lb_ref = refs[ns:ns + 4]
    o_refs = refs[ns + 4:2 * ns + 4]
    ybuf, sem = refs[2 * ns + 4:]
    t = MOE_T
    i = pl.program_id(0)
    slot = i % 2

    def gather(p_ref, s):
        def issue(r, carry):
            pltpu.make_async_copy(y_hbm.at[pl.ds(p_ref[0, 0, r], 1), :], ybuf.at[s, pl.ds(r, 1), :],
                                  sem.at[s]).start()
            return carry
        lax.fori_loop(0, MOE_TOP_K * t, issue, 0, unroll=32)

    @pl.when(i == 0)
    def _():
        gather(pos_ref, 0)

    @pl.when(i + 1 < nt)
    def _():
        gather(posn_ref, 1 - slot)

    pltpu.make_async_copy(y_hbm.at[pl.ds(0, MOE_TOP_K * t), :], ybuf.at[slot], sem.at[slot]).wait()
    ffn = (sel_ref[:, 2:3] * _unpack_rows(ybuf[slot, 0:t, :])
           + sel_ref[:, 3:4] * _unpack_rows(ybuf[slot, t:2 * t, :]))
    out = _layer_norm(DN_ALPHA * _stream_tile(i, h_refs, first) + g2_ref[...] * ffn, lg_ref[...], lb_ref[...])
    for s in range(ns):
        @pl.when((i >= first[s]) & (i < first[s] + tiles[s]))
        def _(s=s):
            o_refs[s][...] = out


def _combine(y, pos_t, hs, sel, gate2, lg, lb):
    tiles, first = _stream_tiles(hs)
    t = MOE_T
    nt = sum(tiles)
    vec = pl.BlockSpec((1, D_MODEL), lambda i: (0, 0))
    return pl.pallas_call(
        functools.partial(_combine_kernel, first=first, tiles=tiles),
        grid=(nt,),
        in_specs=([pl.BlockSpec((1, 1, MOE_TOP_K * t), lambda i: (i, 0, 0), memory_space=pltpu.SMEM),
                   pl.BlockSpec((1, 1, MOE_TOP_K * t), lambda i: (jnp.minimum(i + 1, nt - 1), 0, 0),
                                memory_space=pltpu.SMEM),
                   pl.BlockSpec(memory_space=pl.ANY)]
                  + _stream_specs(tiles, first, D_MODEL)
                  + [pl.BlockSpec((t, LANE), lambda i: (i, 0)), _stream_vec_spec(first), vec, vec]),
        out_specs=_stream_specs(tiles, first, D_MODEL),
        out_shape=[jax.ShapeDtypeStruct(h.shape, F32) for h in hs],
        scratch_shapes=[pltpu.VMEM((2, MOE_TOP_K * t, ROW_WORDS), jnp.uint32), pltpu.SemaphoreType.DMA((2,))],
        compiler_params=_params("arbitrary"),
        name="combine",
    )(pos_t, pos_t, y, *hs, sel, gate2, lg, lb)


def _moe(hs, shift, scale, gate2, lg, lb, wr, br, wg, wu, wd, layer):
    t = MOE_T
    n = sum(h.shape[0] for h in hs)
    nt = n // t
    sel, cnt, selt = _router(hs, shift, scale, wr, br)
    counts = cnt[0, MOE_GROUPS:MOE_GROUPS + MOE_EXPERTS].astype(jnp.int32)
    padded = (counts + MOE_ROWS - 1) // MOE_ROWS * MOE_ROWS
    pad_end = jnp.cumsum(padded)
    pad_start = pad_end - padded
    n_blocks = (n * MOE_TOP_K + MOE_EXPERTS * (MOE_ROWS - 1) + MOE_ROWS - 1) // MOE_ROWS
    blk_row = jnp.arange(n_blocks, dtype=jnp.int32) * MOE_ROWS
    block_e = jnp.minimum(jnp.sum((blk_row[:, None] >= pad_end[None, :]).astype(jnp.int32), axis=1),
                          MOE_EXPERTS - 1)
    n_used = (pad_end[-1:] // MOE_ROWS).astype(jnp.int32)
    e_kt = selt[0:MOE_TOP_K].astype(jnp.int32)
    ids = jnp.arange(MOE_EXPERTS, dtype=jnp.int32)[None, :, None]
    start_kt = jnp.sum(jnp.where(e_kt[:, None, :] == ids, pad_start[None, :, None], 0), axis=1)
    pos_kt = start_kt + selt[4:4 + MOE_TOP_K].astype(jnp.int32)
    pos_t = pos_kt.reshape(MOE_TOP_K, nt, t).transpose(1, 0, 2).reshape(nt, 1, MOE_TOP_K * t)
    xin = _dispatch(pos_t, hs, shift, scale, n_blocks * MOE_ROWS)
    y = _experts(xin, block_e, n_used, wg, wu, wd, layer)
    return _combine(y, pos_t, hs, sel, gate2, lg, lb)


def _mixer(h, mod, lw, ssd_init, tables, *, latent, need_mix):
    n = h.shape[0]
    p, dt_raw = _inproj(h, mod[0], mod[1], lw["w_in"], lw["layer"])
    xbc = _conv3(p, PX, SSD_XBC, lw["ssd_conv_w"], lw["ssd_conv_b"], silu=True, out_dtype=BF16)
    ydir, finals = _ssd_scan(xbc, dt_raw, lw["ssd_dt_bias"], lw["ssd_a_log"], ssd_init)
    if not need_mix:
        return None, finals
    conf_w = (lw["conf_dw_w"], lw["conf_dw_b"], lw["conf_ln_g"], lw["conf_ln_b"])
    ya = _conformer_grid(p, *conf_w) if latent else _conformer(p, *conf_w, dil=1)
    q = _conv3(p, PC, 3 * HY_D, lw["hy_short_w"], lw["hy_short_b"], silu=False, out_dtype=F32)
    k2, nrm = _hyena_filters(lw["feat_lat" if latent else "feat_ctx"], lw["hy_filter"], n)
    if latent:
        yc = _hyena_long(q, k2, nrm, lw["hy_bias"], tables)
    else:
        yc = _hyena_ctx(q, k2, nrm, lw["hy_bias"])
    yd = _gated_conv(p, lw["sc_conv_w"])
    h = _merge(ya, xbc, ydir, p, yc, yd, h, lw["ssd_dvec"], lw["ssd_norm_g"], mod[2], lw["ln_g0"], lw["ln_b0"],
               lw["w_branch_a"], lw["w_branch_b"], lw["w_branch_c"], lw["w_branch_d"], lw["w_out"])
    return h, finals


def _positional_features(n):
    tau = jnp.arange(2 * n, dtype=jnp.int32)
    lag = jnp.where(tau < n, tau, 2 * n - tau).astype(F32)[:, None]
    t01 = lag * (1.0 / (n - 1))
    omega = (2.0 * math.pi / n) * lag
    bands = jnp.linspace(1e-4, HY_BANDS - 1, HY_BANDS, dtype=F32)
    featx = jnp.concatenate([t01, jnp.cos(bands * omega), -jnp.sin(bands * omega)], axis=-1)
    return _pad_lanes(featx)


RELAYOUT_W = 256


def _relayout_kernel(blk_ref, sh_ref, valid_ref, half_ref, a_ref, b_ref, o_ref):
    j = pl.program_id(1)
    sh = sh_ref[j]
    rot = (LANE - sh) % LANE
    scale = jnp.where(half_ref[j] == 1, 0.5, 1.0)
    lane = lax.broadcasted_iota(jnp.int32, (a_ref.shape[0], LANE), 1)
    nk = RELAYOUT_W // LANE
    chunk = lambda k: (a_ref if k < nk else b_ref)[:, (k % nk) * LANE:(k % nk + 1) * LANE]
    for k in range(nk):
        x = jnp.where(lane < LANE - sh, pltpu.roll(chunk(k), rot, 1), pltpu.roll(chunk(k + 1), rot, 1))
        o_ref[:, k * LANE:(k + 1) * LANE] = jnp.where(lane + k * LANE < valid_ref[j], x * scale, 0.0).astype(BF16)


def _relayout_w_in(w_in):
    w = RELAYOUT_W
    depth, d, total = w_in.shape
    ob = OFF_B
    segs = [(PG, OFF_G, N_BRANCH * D_MODEL, 1), (PA, OFF_A, 2 * CONF_D, 0), (PC, OFF_C, 3 * HY_D, 0),
            (PD, OFF_D, 3 * SC_D, 0), (PX, ob + SSD_D, SSD_XBC, 0), (PZ, ob, SSD_D, 0),
            (PDT, ob + SSD_D + SSD_XBC, 2 * SSD_HEADS, 0)]
    ntile = NP // w
    blk, sh, valid, half = ([0] * ntile for _ in range(4))
    for p0, s0, width, hv in segs:
        assert p0 % w == 0
        for j in range(p0 // w, -(-(p0 + width) // w)):
            src = s0 + j * w - p0
            blk[j], sh[j], valid[j], half[j] = src // w, src % w, min(w, p0 + width - j * w), hv
            assert sh[j] < LANE
    last = (total - 1) // w
    tables = [jnp.asarray(v, jnp.int32) for v in (blk, sh, valid, half)]
    gs = pltpu.PrefetchScalarGridSpec(
        num_scalar_prefetch=4,
        grid=(depth, ntile),
        in_specs=[pl.BlockSpec((None, d, w), lambda l, j, blk, *_: (l, 0, blk[j])),
                  pl.BlockSpec((None, d, w), lambda l, j, blk, *_: (l, 0, jnp.minimum(blk[j] + 1, last)))],
        out_specs=pl.BlockSpec((None, d, w), lambda l, j, *_: (l, 0, j)),
    )
    return pl.pallas_call(
        _relayout_kernel,
        grid_spec=gs,
        out_shape=jax.ShapeDtypeStruct((depth, d, NP), BF16),
        compiler_params=_params("parallel", "parallel"),
        name="relayout_w_in",
    )(*tables, w_in, w_in)


def _pad_lanes(v):
    return jnp.pad(v, ((0, 0), (0, LANE - v.shape[-1])))


def kernel(x, c, ctx, c_ctx, w_mod, b_mod, ln_g, ln_b, w_in, conf_dw_w, conf_dw_b, conf_ln_g, conf_ln_b,
           ssd_conv_w, ssd_conv_b, ssd_a_log, ssd_dt_bias, ssd_d, ssd_norm_g, hy_short_w, hy_short_b,
           hy_w1, hy_b1, hy_w2, hy_b2, hy_freq, hy_w3, hy_bias, sc_conv_w, w_branch_a, w_branch_b,
           w_branch_c, w_branch_d, w_out, rt_group_w, rt_group_b, rt_expert_w, rt_expert_b,
           ex_w_gate, ex_w_up, ex_w_down):
    assert x.shape[0] == 1 and ctx.shape[0] == 1
    n_lat, n_ctx = x.shape[1], ctx.shape[1]
    depth = w_in.shape[0]

    cv = jnp.concatenate([c, c_ctx[None, :], jnp.zeros((SUBLANE - 2, D_MODEL), F32)], axis=0)
    mods = _mod_vectors(cv, w_mod, b_mod)
    w_in_p = _relayout_w_in(w_in)
    tables = _dft_tables(n_lat)
    feat_lat = _positional_features(n_lat)
    feat_ctx = _positional_features(n_ctx)
    deltas = jnp.abs(jnp.linspace(HY_MIN_DECAY, HY_MAX_DECAY, HY_N_FILT, dtype=F32))
    deltas_d = deltas.reshape(HY_ORDER, 2, HY_D).transpose(1, 0, 2).reshape(2, 1, HY_ORDER * HY_D)
    router_w = jnp.concatenate([rt_group_w, rt_expert_w,
                                jnp.zeros((depth, D_MODEL, LANE - MOE_GROUPS - MOE_EXPERTS), F32)], axis=-1)
    router_b = jnp.concatenate([rt_group_b, rt_expert_b,
                                jnp.zeros((depth, LANE - MOE_GROUPS - MOE_EXPERTS), F32)], axis=-1)
    ssd_zero = jnp.zeros((2,) + SSD_STATE_SHAPE, F32)

    h_lat, h_ctx = x[0], ctx[0]
    for l in range(depth):
        row = lambda v: v[None, :]
        lw = dict(
            w_in=w_in_p, layer=l, conf_dw_w=conf_dw_w[l], conf_dw_b=row(conf_dw_b[l]), conf_ln_g=row(conf_ln_g[l]),
            conf_ln_b=row(conf_ln_b[l]), ssd_conv_w=ssd_conv_w[l], ssd_conv_b=row(ssd_conv_b[l]),
            ssd_a_log=_pad_lanes(ssd_a_log[l].reshape(1, -1)), ssd_dt_bias=_pad_lanes(ssd_dt_bias[l].reshape(1, -1)),
            ssd_dvec=row(jnp.repeat(ssd_d[l], SSD_HEAD_DIM)), ssd_norm_g=row(ssd_norm_g[l]),
            hy_short_w=hy_short_w[l], hy_short_b=row(hy_short_b[l]),
            hy_filter=_filter_weights(hy_w1[l], hy_b1[l], hy_w2[l], hy_b2[l], hy_freq[l], hy_w3[l], deltas_d),
            hy_bias=hy_bias[l], feat_lat=feat_lat, feat_ctx=feat_ctx,
            sc_conv_w=sc_conv_w[l], ln_g0=row(ln_g[l, 0]), ln_b0=row(ln_b[l, 0]),
            w_branch_a=w_branch_a[l].astype(BF16), w_branch_b=w_branch_b[l].astype(BF16),
            w_branch_c=w_branch_c[l].astype(BF16), w_branch_d=w_branch_d[l].astype(BF16),
            w_out=(0.5 * w_out[l]).astype(BF16))
        moe_w = (router_w[l], row(router_b[l]), ex_w_gate, ex_w_up, ex_w_down, l)
        last = l == depth - 1
        d = D_MODEL
        mod_lat = [mods[l, 0:1, k * d:(k + 1) * d] for k in range(6)]
        mod_ctx = [mods[l, 1:2, k * d:(k + 1) * d] for k in range(6)]

        mix_ctx, ctx_states = _mixer(h_ctx, mod_ctx, lw, ssd_zero, None, latent=False, need_mix=not last)
        h_lat, _ = _mixer(h_lat, mod_lat, lw, ctx_states, tables, latent=True, need_mix=True)
        streams = [(h_lat, mod_lat)] if last else [(h_lat, mod_lat), (mix_ctx, mod_ctx)]
        vecs = [jnp.stack([m[k] for _, m in streams]) for k in (3, 4, 5)]
        outs = _moe([h for h, _ in streams], *vecs, row(ln_g[l, 1]), row(ln_b[l, 1]), *moe_w)
        h_lat = outs[0]
        if not last:
            h_ctx = outs[1]
    return h_lat[None]
```

```python
import functools
import math

import jax
import jax.numpy as jnp
from jax import lax
from jax.experimental import pallas as pl
from jax.experimental.pallas import tpu as pltpu

F32 = jnp.float32
BF16 = jnp.bfloat16
HIGHEST = lax.Precision.HIGHEST

D_MODEL = 1024
DEPTH = 4
GRID_W = 64
CONF_D = 512
CONF_K = 31
SSD_D = 768
SSD_HEADS = 12
SSD_HEAD_DIM = 64
SSD_GROUPS = 4
SSD_HPG = SSD_HEADS // SSD_GROUPS
SSD_STATE = 128
SSD_CHUNK = 128
SSD_BC = SSD_GROUPS * SSD_STATE
SSD_XBC = SSD_D + 2 * SSD_BC
SSD_PROJ = SSD_D + SSD_XBC + 2 * SSD_HEADS
HY_D = 512
HY_ORDER = 2
HY_EMB = 33
HY_BANDS = (HY_EMB - 1) // 2
HY_HID = 64
HY_N_FILT = HY_ORDER * 2 * HY_D
HY_MIN_DECAY = math.log(1e-2) / 1.5
HY_MAX_DECAY = math.log(1e-2) / 0.3
SC_D = 512
N_BRANCH = 4
OFF_A = 0
OFF_B = OFF_A + 2 * CONF_D
OFF_C = OFF_B + SSD_PROJ
OFF_D = OFF_C + 3 * HY_D
OFF_G = OFF_D + 3 * SC_D
MOE_GROUPS = 4
MOE_EPG = 8
MOE_EXPERTS = MOE_GROUPS * MOE_EPG
MOE_TOP_K = 2
MOE_FF = 512
DN_ALPHA = (2 * DEPTH) ** 0.25
LN_EPS = 1e-5

PG = 0
PA = PG + N_BRANCH * D_MODEL
PC = PA + 2 * CONF_D
PD = PC + 3 * HY_D
PX = PD + 3 * SC_D
PZ = PX + SSD_XBC
PDT = PZ + SSD_D
INPROJ_TN = 1024
NP = -(-(PDT + 128) // INPROJ_TN) * INPROJ_TN

LANE = 128
SUBLANE = 8
FFT_N2 = 256
MOE_ROWS = 256
VMEM_LIMIT = 48 * 1024 * 1024


def _params(*sem):
    return pltpu.CompilerParams(dimension_semantics=sem, vmem_limit_bytes=VMEM_LIMIT)


def _sigmoid(x):
    return 0.5 * jnp.tanh(0.5 * x) + 0.5


def _silu(x):
    h = 0.5 * x
    return h * jnp.tanh(h) + h


def _layer_norm(x, g, b):
    mu = jnp.mean(x, -1, keepdims=True)
    xc = x - mu
    var = jnp.mean(xc * xc, -1, keepdims=True)
    return xc * lax.rsqrt(var + LN_EPS) * g + b


def _dot(a, b):
    return jnp.dot(a, b, preferred_element_type=F32)


def _mod_kernel(cv_ref, w_ref, b_ref, o_ref):
    o_ref[...] = jnp.dot(_silu(cv_ref[...]), w_ref[...], precision=HIGHEST,
                         preferred_element_type=F32) + b_ref[...]


def _mod_vectors(cv, w_mod, b_mod):
    tn = 1536
    return pl.pallas_call(
        _mod_kernel,
        grid=(DEPTH, 6 * D_MODEL // tn),
        in_specs=[pl.BlockSpec((SUBLANE, D_MODEL), lambda l, j: (0, 0)),
                  pl.BlockSpec((None, D_MODEL, tn), lambda l, j: (l, 0, j)),
                  pl.BlockSpec((None, 1, tn), lambda l, j: (l, 0, j))],
        out_specs=pl.BlockSpec((None, SUBLANE, tn), lambda l, j: (l, 0, j)),
        out_shape=jax.ShapeDtypeStruct((DEPTH, SUBLANE, 6 * D_MODEL), F32),
        compiler_params=_params("parallel", "parallel"),
        name="mod_vectors",
    )(cv, w_mod, b_mod.reshape(DEPTH, 1, 6 * D_MODEL))


def _inproj_kernel(x_ref, sh_ref, sc_ref, w_ref, o_ref, dt_ref, xb_ref, *, nj):
    j = pl.program_id(1)

    @pl.when(j == 0)
    def _():
        xb_ref[...] = (x_ref[...] * (1.0 + sc_ref[...]) + sh_ref[...]).astype(BF16)

    res = _dot(xb_ref[...], w_ref[...])
    o_ref[...] = res.astype(BF16)

    @pl.when(j == nj - 1)
    def _():
        off = PDT - (nj - 1) * INPROJ_TN
        dt_ref[...] = res[:, off:off + LANE]


def _inproj(h, shift, scale, w, layer):
    n = h.shape[0]
    tm = min(n, 2048)
    tn = INPROJ_TN
    nj = NP // tn
    assert PDT >= (nj - 1) * tn
    return pl.pallas_call(
        functools.partial(_inproj_kernel, nj=nj),
        grid=(n // tm, nj),
        in_specs=[pl.BlockSpec((tm, D_MODEL), lambda i, j: (i, 0)),
                  pl.BlockSpec((1, D_MODEL), lambda i, j: (0, 0)),
                  pl.BlockSpec((1, D_MODEL), lambda i, j: (0, 0)),
                  pl.BlockSpec((None, D_MODEL, tn), lambda i, j: (layer, 0, j))],
        out_specs=[pl.BlockSpec((tm, tn), lambda i, j: (i, j)),
                   pl.BlockSpec((tm, LANE), lambda i, j: (i, 0))],
        out_shape=[jax.ShapeDtypeStruct((n, NP), BF16), jax.ShapeDtypeStruct((n, LANE), F32)],
        scratch_shapes=[pltpu.VMEM((tm, D_MODEL), BF16)],
        compiler_params=_params("parallel", "arbitrary"),
        name="inproj",
    )(h, shift, scale, w)


HALO_ROWS = 16


def _conv3_tile(x, prev_row, next_row, w_ref, store):
    t = x.shape[0]
    e = HALO_ROWS
    w0, w1, w2 = w_ref[0:1, :], w_ref[1:2, :], w_ref[2:3, :]
    y = w0 * pltpu.roll(x, 1, 0) + w1 * x + w2 * pltpu.roll(x, t - 1, 0)
    store(slice(0, t), y)
    row = lax.broadcasted_iota(jnp.int32, (e, x.shape[1]), 0)
    store(slice(0, e), y[0:e] + jnp.where(row == 0, w0 * (prev_row - x[t - 1:t]), 0.0))
    store(slice(t - e, t), y[t - e:t] + jnp.where(row == e - 1, w2 * (next_row - x[0:1]), 0.0))


def _conv3_kernel(cur_ref, prev_ref, next_ref, w_ref, b_ref, o_ref, *, silu, nt):
    i = pl.program_id(0)
    x = cur_ref[...].astype(F32)
    pv = jnp.where(i > 0, prev_ref[HALO_ROWS - 1:HALO_ROWS, :].astype(F32), 0.0)
    nx = jnp.where(i < nt - 1, next_ref[0:1, :].astype(F32), 0.0)

    def store(rows, y):
        y = y + b_ref[...]
        o_ref[rows, :] = (_silu(y) if silu else y).astype(o_ref.dtype)

    _conv3_tile(x, pv, nx, w_ref, store)


def _halo_specs(t, ct, n, col0):
    rb = t // HALO_ROWS
    last = n // HALO_ROWS - 1
    return [pl.BlockSpec((t, ct), lambda i, j: (i, col0 + j)),
            pl.BlockSpec((HALO_ROWS, ct), lambda i, j: (jnp.maximum(i * rb - 1, 0), col0 + j)),
            pl.BlockSpec((HALO_ROWS, ct), lambda i, j: (jnp.minimum((i + 1) * rb, last), col0 + j))]


def _conv3(p, col, width, w, b, *, silu, out_dtype):
    n = p.shape[0]
    t = min(n, 1024)
    ct = 256
    nt = n // t
    return pl.pallas_call(
        functools.partial(_conv3_kernel, silu=silu, nt=nt),
        grid=(nt, width // ct),
        in_specs=_halo_specs(t, ct, n, col // ct) + [
            pl.BlockSpec((3, ct), lambda i, j: (0, j)),
            pl.BlockSpec((1, ct), lambda i, j: (0, j))],
        out_specs=pl.BlockSpec((t, ct), lambda i, j: (i, j)),
        out_shape=jax.ShapeDtypeStruct((n, width), out_dtype),
        compiler_params=_params("parallel", "parallel"),
        name="conv3",
    )(p, p, p, w, b)


def _gconv_kernel(bg_ref, cc_ref, cp_ref, cn_ref, xc_ref, xp_ref, xn_ref, w_ref, o_ref, *, nt):
    i = pl.program_id(0)
    f = lambda v: v.astype(F32)
    last = slice(HALO_ROWS - 1, HALO_ROWS)
    x = f(cc_ref[...]) * f(xc_ref[...])
    pv = jnp.where(i > 0, f(cp_ref[last, :]) * f(xp_ref[last, :]), 0.0)
    nx = jnp.where(i < nt - 1, f(cn_ref[0:1, :]) * f(xn_ref[0:1, :]), 0.0)

    def store(rows, y):
        o_ref[rows, :] = f(bg_ref[rows, :]) * y

    _conv3_tile(x, pv, nx, w_ref, store)


def _gated_conv(p, w):
    n = p.shape[0]
    t = min(n, 1024)
    ct = 256
    nt = n // t
    nb = SC_D // ct
    return pl.pallas_call(
        functools.partial(_gconv_kernel, nt=nt),
        grid=(nt, nb),
        in_specs=([pl.BlockSpec((t, ct), lambda i, j: (i, PD // ct + j))]
                  + _halo_specs(t, ct, n, PD // ct + nb)
                  + _halo_specs(t, ct, n, PD // ct + 2 * nb)
                  + [pl.BlockSpec((3, ct), lambda i, j: (0, j))]),
        out_specs=pl.BlockSpec((t, ct), lambda i, j: (i, j)),
        out_shape=jax.ShapeDtypeStruct((n, SC_D), F32),
        compiler_params=_params("parallel", "parallel"),
        name="gated_conv",
    )(p, p, p, p, p, p, p, w)


CONF_RB = 64


def _conf_kernel(vc, gc, vp, gp, vn, gn, w_ref, b_ref, lg_ref, lb_ref, o_ref, buf, *, t, halo, dil, nt):
    i = pl.program_id(0)
    glu = lambda v, g: v.astype(F32) * _sigmoid(g.astype(F32))
    buf[halo:halo + t, :] = glu(vc[...], gc[...])
    buf[0:halo, :] = jnp.where(i > 0, glu(vp[t - halo:t, :], gp[t - halo:t, :]), 0.0)
    buf[halo + t:halo + t + halo, :] = jnp.where(i < nt - 1, glu(vn[0:halo, :], gn[0:halo, :]), 0.0)

    def block(r0):
        acc = jnp.zeros((CONF_RB, CONF_D), F32)
        for j in range(CONF_K):
            off = halo + (j - CONF_K // 2) * dil
            acc = acc + w_ref[j:j + 1, :] * buf[pl.ds(r0 + off, CONF_RB), :]
        v = _layer_norm(acc + b_ref[...], lg_ref[...], lb_ref[...])
        o_ref[pl.ds(r0, CONF_RB), :] = _silu(v)

    if dil % CONF_RB == 0:
        def body(rb, carry):
            block(pl.multiple_of(rb * CONF_RB, CONF_RB))
            return carry
        lax.fori_loop(0, t // CONF_RB, body, 0)
    else:
        for rb in range(t // CONF_RB):
            block(rb * CONF_RB)


def _conformer(p, w, b, lg, lb, *, dil):
    n = p.shape[0]
    t = min(n, 1024)
    nt = n // t
    halo = -(-(CONF_K // 2) * dil // SUBLANE) * SUBLANE
    assert halo <= t
    cb = PA // CONF_D

    def spec(col, shift):
        return pl.BlockSpec((t, CONF_D), lambda i: (jnp.clip(i + shift, 0, nt - 1), col))

    vec = pl.BlockSpec((1, CONF_D), lambda i: (0, 0))
    return pl.pallas_call(
        functools.partial(_conf_kernel, t=t, halo=halo, dil=dil, nt=nt),
        grid=(nt,),
        in_specs=[spec(cb, 0), spec(cb + 1, 0), spec(cb, -1), spec(cb + 1, -1), spec(cb, 1), spec(cb + 1, 1),
                  pl.BlockSpec((CONF_K, CONF_D), lambda i: (0, 0)), vec, vec, vec],
        out_specs=pl.BlockSpec((t, CONF_D), lambda i: (i, 0)),
        out_shape=jax.ShapeDtypeStruct((n, CONF_D), F32),
        scratch_shapes=[pltpu.VMEM((t + 2 * halo, CONF_D), F32)],
        compiler_params=_params("parallel"),
        name="conformer",
    )(p, p, p, p, p, p, w, b, lg, lb)


CONF_COLS = 16
CONF_ROWS_PER_ITER = 4


def _conf_grid_kernel(v_ref, g_ref, w_ref, b_ref, lg_ref, lb_ref, o_ref, buf, *, rows):
    half = CONF_K // 2
    zeros = jnp.zeros((half,) + buf.shape[1:], F32)
    buf[0:half] = zeros
    buf[half + rows:half + rows + half] = zeros
    buf[half:half + rows] = v_ref[...].astype(F32) * _sigmoid(g_ref[...].astype(F32))

    rb = CONF_ROWS_PER_ITER

    def body(it, carry):
        r0 = it * rb
        accs = [None] * rb
        for j in range(CONF_K):
            wj = w_ref[j:j + 1, :]
            for s in range(rb):
                term = wj * buf[r0 + s + j]
                accs[s] = term if j == 0 else accs[s] + term
        for s in range(rb):
            v = _layer_norm(accs[s] + b_ref[...], lg_ref[...], lb_ref[...])
            o_ref[r0 + s] = _silu(v).astype(o_ref.dtype)
        return carry

    lax.fori_loop(0, rows // rb, body, 0)


def _conformer_grid(p, w, b, lg, lb):
    n = p.shape[0]
    rows = n // GRID_W
    p3 = p.reshape(rows, GRID_W, p.shape[1])
    cb = PA // CONF_D
    vec = pl.BlockSpec((1, CONF_D), lambda j: (0, 0))
    blk = lambda col: pl.BlockSpec((rows, CONF_COLS, CONF_D), lambda j: (0, j, col))
    out = pl.pallas_call(
        functools.partial(_conf_grid_kernel, rows=rows),
        grid=(GRID_W // CONF_COLS,),
        in_specs=[blk(cb), blk(cb + 1), pl.BlockSpec((CONF_K, CONF_D), lambda j: (0, 0)), vec, vec, vec],
        out_specs=blk(0),
        out_shape=jax.ShapeDtypeStruct((rows, GRID_W, CONF_D), BF16),
        scratch_shapes=[pltpu.VMEM((rows + 2 * (CONF_K // 2), CONF_COLS, CONF_D), F32)],
        compiler_params=_params("parallel"),
        name="conformer_grid",
    )(p3, p3, w, b, lg, lb)
    return out.reshape(n, CONF_D)


SSD_STATE_SHAPE = (SSD_HEADS // 2, SSD_STATE, 2 * SSD_HEAD_DIM)
SSD_CHUNKS_PER_STEP = 2


def _ssd_kernel(xbc_ref, dt_ref, dtb_ref, alog_ref, init_ref, y_ref, fin_ref, h_ref, *, ns, cps):
    d = pl.program_id(0)
    c = pl.program_id(1)
    q = SSD_CHUNK
    hd = SSD_HEAD_DIM

    @pl.when(c == 0)
    def _():
        h_ref[...] = init_ref[...]

    lane = lax.broadcasted_iota(jnp.int32, (q, LANE), 1)
    head = lane < SSD_HEADS
    first_half = lane < hd
    li = lax.broadcasted_iota(jnp.int32, (q, q), 0)
    si = lax.broadcasted_iota(jnp.int32, (q, q), 1)
    mask = (li - si) * (1 - 2 * d) >= 0
    tri = mask.astype(F32)
    a_rate = -jnp.exp(alog_ref[...])

    def one_chunk(r0):
        rows = pl.ds(r0, q)
        raw = dt_ref[rows, :] + dtb_ref[...]
        dt_all = jnp.maximum(raw, 0.0) + jnp.log(1.0 + jnp.exp(-jnp.abs(raw)))
        ld_all = dt_all * a_rate
        dt_d = jnp.where(head, jnp.where(d == 0, dt_all, pltpu.roll(dt_all, LANE - SSD_HEADS, 1)), 0.0)
        ld_d = jnp.where(head, jnp.where(d == 0, ld_all, pltpu.roll(ld_all, LANE - SSD_HEADS, 1)), 0.0)
        cum = jnp.dot(tri, ld_d, precision=HIGHEST, preferred_element_type=F32)
        tot = jnp.sum(ld_d, axis=0, keepdims=True)
        cum_t = cum.T
        dt_t = dt_d.T
        w_t = (jnp.exp(tot - cum) * dt_d).T
        a_out = jnp.exp(cum)
        e_tot = jnp.exp(tot)

        groups = {}

        def group(g):
            if g not in groups:
                bg = xbc_ref[rows, SSD_D + g * SSD_STATE:SSD_D + (g + 1) * SSD_STATE].astype(F32)
                cg = xbc_ref[rows, SSD_D + SSD_BC + g * SSD_STATE:SSD_D + SSD_BC + (g + 1) * SSD_STATE].astype(F32)
                bg_t = bg.T
                groups[g] = (bg_t, cg, _dot(cg.astype(BF16), bg_t.astype(BF16)))
            return groups[g]

        def head_terms(hh):
            bg_t, cg, cb = group(hh // SSD_HPG)
            diff = cum[:, hh:hh + 1] - cum_t[hh:hh + 1, :]
            dec = jnp.exp(jnp.where(mask, diff, -1e30))
            m = (cb * dec * dt_t[hh:hh + 1, :]).astype(BF16)
            cs = (cg * a_out[:, hh:hh + 1]).astype(BF16)
            bw = (bg_t * w_t[hh:hh + 1, :]).astype(BF16)
            return m, cs, bw

        zero = jnp.zeros((q, LANE), BF16)
        for j in range(SSD_HEADS // 2):
            ma, ca, wa = head_terms(2 * j)
            mb, cb_, wb = head_terms(2 * j + 1)
            xp = xbc_ref[rows, j * LANE:(j + 1) * LANE].astype(BF16)
            hp = h_ref[j]
            hp_b = hp.astype(BF16)
            x_bd = jnp.concatenate([jnp.where(first_half, xp, zero), jnp.where(first_half, zero, xp)], axis=0)
            h_bd = jnp.concatenate([jnp.where(first_half, hp_b, zero), jnp.where(first_half, zero, hp_b)], axis=0)
            y_ref[rows, j * LANE:(j + 1) * LANE] = _dot(jnp.concatenate([ma, mb, ca, cb_], axis=1),
                                                        jnp.concatenate([x_bd, h_bd], axis=0))
            s_new = _dot(jnp.concatenate([wa, wb], axis=1), x_bd)
            e_pair = jnp.where(first_half[0:1, :], e_tot[:, 2 * j:2 * j + 1], e_tot[:, 2 * j + 1:2 * j + 2])
            h_ref[j] = e_pair * hp + s_new

    for k in range(cps):
        one_chunk(pl.multiple_of(jnp.where(d == 0, k, cps - 1 - k) * q, q))

    @pl.when(c == ns - 1)
    def _():
        fin_ref[...] = h_ref[...]


def _ssd_scan(xbc, p, dt_bias, a_log, init):
    n = xbc.shape[0]
    cps = SSD_CHUNKS_PER_STEP
    q = SSD_CHUNK * cps
    ns = n // q

    def chunk(d, c):
        return jnp.where(d == 0, c, ns - 1 - c)

    st = SSD_STATE_SHAPE
    vec = pl.BlockSpec((1, LANE), lambda d, c: (0, 0))
    return pl.pallas_call(
        functools.partial(_ssd_kernel, ns=ns, cps=cps),
        grid=(2, ns),
        in_specs=[pl.BlockSpec((q, SSD_XBC), lambda d, c: (chunk(d, c), 0)),
                  pl.BlockSpec((q, LANE), lambda d, c: (chunk(d, c), 0)),
                  vec, vec,
                  pl.BlockSpec((None,) + st, lambda d, c: (d, 0, 0, 0))],
        out_specs=[pl.BlockSpec((None, q, SSD_D), lambda d, c: (d, chunk(d, c), 0)),
                   pl.BlockSpec((None,) + st, lambda d, c: (d, 0, 0, 0))],
        out_shape=[jax.ShapeDtypeStruct((2, n, SSD_D), F32),
                   jax.ShapeDtypeStruct((2,) + st, F32)],
        scratch_shapes=[pltpu.VMEM(st, F32)],
        compiler_params=_params("arbitrary", "arbitrary"),
        name="ssd_scan",
    )(xbc, p, dt_bias, a_log, init)


def _filt_kernel(feat_ref, w1_ref, b1_ref, w2_ref, b2_ref, fr_ref, w3h_ref, w3l_ref, dl_ref, k_ref, nrm_ref, *,
                 n, t):
    i = pl.program_id(0)
    hf = t // 2
    feat = feat_ref[...]
    x = jnp.concatenate([feat[0:hf], feat[hf:t]], axis=1)
    hid = jnp.sin(fr_ref[0:1, :] * (jnp.dot(x, w1_ref[...], precision=HIGHEST,
                                            preferred_element_type=F32) + b1_ref[...]))
    hid = jnp.sin(fr_ref[1:2, :] * (jnp.dot(hid, w2_ref[...], precision=HIGHEST,
                                            preferred_element_type=F32) + b2_ref[...]))
    hi = hid.astype(BF16)
    lo = (hid - hi.astype(F32)).astype(BF16)

    @pl.when(i == 0)
    def _():
        nrm_ref[...] = jnp.zeros_like(nrm_ref)

    for half in range(2):
        wh, wl = w3h_ref[half], w3l_ref[half]
        filt = _dot(hi, wh) + _dot(lo, wh) + _dot(hi, wl)
        filt = filt * jnp.exp(-feat[half * hf:(half + 1) * hf, 0:1] * dl_ref[...])
        row = i * t + half * hf + lax.broadcasted_iota(jnp.int32, filt.shape, 0)
        filt = jnp.where(row == n, 0.0, filt)
        k_ref[half * hf:(half + 1) * hf, :] = filt
        nrm_ref[...] += jnp.sum(jnp.abs(filt), axis=0, keepdims=True)


def _hyena_filters(featx, fw, n):
    t = min(n, 512)
    half = n // t
    oc = HY_ORDER * HY_D
    full = lambda shape: pl.BlockSpec(shape, lambda i: tuple(0 for _ in shape))
    w3spec = pl.BlockSpec((None, 2, LANE, oc), lambda i: (i // half, 0, 0, 0))
    return pl.pallas_call(
        functools.partial(_filt_kernel, n=n, t=t),
        grid=(2 * n // t,),
        in_specs=[pl.BlockSpec((t, LANE), lambda i: (i, 0)),
                  full((2 * LANE, LANE)), full((1, LANE)), full((LANE, LANE)), full((1, LANE)),
                  full((2, LANE)), w3spec, w3spec,
                  pl.BlockSpec((None, 1, oc), lambda i: (i // half, 0, 0))],
        out_specs=[pl.BlockSpec((t, oc), lambda i: (i, 0)),
                   pl.BlockSpec((1, oc), lambda i: (0, 0))],
        out_shape=[jax.ShapeDtypeStruct((2 * n, oc), F32),
                   jax.ShapeDtypeStruct((1, oc), F32)],
        compiler_params=_params("arbitrary"),
        name="hyena_filters",
    )(featx, fw["w1"], fw["b1"], fw["w2"], fw["b2"], fw["freq"], fw["w3h"], fw["w3l"], fw["deltas"])


def _filter_weights(w1, b1, w2, b2, freq, w3, deltas_d):
    hh = HY_HID
    z = lambda r, c: jnp.zeros((r, c), F32)
    w1p = jnp.pad(w1, ((0, LANE - HY_EMB), (0, 0)))
    w1b = jnp.concatenate([jnp.concatenate([w1p, z(LANE, hh)], 1),
                           jnp.concatenate([z(LANE, hh), w1p], 1)], 0)
    w2b = jnp.concatenate([jnp.concatenate([w2, z(hh, hh)], 1),
                           jnp.concatenate([z(hh, hh), w2], 1)], 0)
    two = lambda v: jnp.concatenate([v, v], axis=-1)
    w3d = w3.reshape(hh, HY_ORDER, 2, HY_D).transpose(2, 0, 1, 3).reshape(2, hh, HY_ORDER * HY_D)
    zz = jnp.zeros_like(w3d)
    w3x = jnp.stack([jnp.concatenate([w3d, zz], 1), jnp.concatenate([zz, w3d], 1)], axis=1)
    w3h = w3x.astype(BF16)
    w3l = (w3x - w3h.astype(F32)).astype(BF16)
    return dict(w1=w1b, b1=two(b1[None, :]), w2=w2b, b2=two(b2[None, :]), freq=two(freq), w3h=w3h, w3l=w3l,
                deltas=deltas_d)


DFT_LANES = 8192


def _dft_rows_kernel(f_ref, x_ref, o_ref, *, nj):
    x = jnp.concatenate([x_ref[:, jj, :] for jj in range(nj)], axis=1)
    o_ref[...] = _dot(f_ref[...], x.astype(BF16)).astype(o_ref.dtype)


def _dft_rows(fmat, x3, col, width):
    m, k = fmat.shape
    n2 = x3.shape[1]
    nj = min(DFT_LANES // width, n2)
    return pl.pallas_call(
        functools.partial(_dft_rows_kernel, nj=nj),
        grid=(n2 // nj,),
        in_specs=[pl.BlockSpec((m, k), lambda j: (0, 0)),
                  pl.BlockSpec((k, nj, width), lambda j: (0, j, col))],
        out_specs=pl.BlockSpec((m, nj * width), lambda j: (0, j)),
        out_shape=jax.ShapeDtypeStruct((m, n2 * width), BF16),
        compiler_params=_params("parallel"),
        name="dft_rows",
    )(fmat, x3)


def _spec_kernel(ar_ref, ai_ref, gr_ref, gi_ref, kr_ref, ki_ref):
    ar, ai, gr, gi = ar_ref[...], ai_ref[...], gr_ref[...], gi_ref[...]
    kr_ref[...] = (_dot(gr, ar) - _dot(gi, ai)).astype(BF16)
    ki_ref[...] = (_dot(gr, ai) + _dot(gi, ar)).astype(BF16)


def _filter_spectrum(a4, gr, gi):
    _, _, n2, ch = a4.shape
    nh = gr.shape[0]
    ct = ch
    blk = lambda ri: pl.BlockSpec((None, None, n2, ct), lambda f, j: (ri, f, 0, j))
    gspec = pl.BlockSpec((None, n2, n2), lambda f, j: (f, 0, 0))
    ospec = pl.BlockSpec((None, n2, ct), lambda f, j: (f, 0, j))
    return pl.pallas_call(
        _spec_kernel,
        grid=(nh, ch // ct),
        in_specs=[blk(0), blk(1), gspec, gspec],
        out_specs=[ospec, ospec],
        out_shape=[jax.ShapeDtypeStruct((nh, n2, ch), BF16)] * 2,
        compiler_params=_params("parallel", "parallel"),
        name="filter_spectrum",
    )(a4, a4, gr, gi)


MID_FREQS = 2


def _mid_kernel(ar_ref, ai_ref, gr_ref, gi_ref, grt_ref, git_ref, kr_ref, ki_ref, br_ref, bi_ref, *, nh):
    for k in range(MID_FREQS):
        f = pl.program_id(0) * MID_FREQS + k

        @pl.when(f < nh)
        def _(k=k):
            ar, ai, gr, gi = ar_ref[k], ai_ref[k], gr_ref[k], gi_ref[k]
            xr = _dot(gr, ar) - _dot(gi, ai)
            xi = _dot(gr, ai) + _dot(gi, ar)
            kr, ki = kr_ref[k].astype(F32), ki_ref[k].astype(F32)
            yr = (xr * kr - xi * ki).astype(BF16)
            yi = (xr * ki + xi * kr).astype(BF16)
            grt, git = grt_ref[k], git_ref[k]
            br_ref[k] = (_dot(grt, yr) + _dot(git, yi)).astype(BF16)
            bi_ref[k] = (_dot(grt, yi) - _dot(git, yr)).astype(BF16)

        @pl.when(f >= nh)
        def _(k=k):
            br_ref[k] = jnp.zeros(br_ref.shape[1:], BF16)
            bi_ref[k] = jnp.zeros(bi_ref.shape[1:], BF16)


def _hyena_mid(a4, tabs, kf_r, kf_i, order):
    _, nf, n2, ch = a4.shape
    gr, gi, grt, git = tabs
    nh = gr.shape[0]
    mf = MID_FREQS
    assert nf % mf == 0
    fi = lambda s: jnp.minimum(s, (nh - 1) // mf)
    blk = lambda ri: pl.BlockSpec((None, mf, n2, ch), lambda s: (ri, fi(s), 0, 0))
    kspec = pl.BlockSpec((mf, n2, ch), lambda s: (fi(s), 0, order))
    gspec = pl.BlockSpec((mf, n2, n2), lambda s: (fi(s), 0, 0))
    ospec = pl.BlockSpec((mf, n2, ch), lambda s: (s, 0, 0))
    return pl.pallas_call(
        functools.partial(_mid_kernel, nh=nh),
        grid=(nf // mf,),
        in_specs=[blk(0), blk(1), gspec, gspec, gspec, gspec, kspec, kspec],
        out_specs=[ospec, ospec],
        out_shape=[jax.ShapeDtypeStruct((nf, n2, ch), BF16)] * 2,
        compiler_params=_params("parallel"),
        name="hyena_mid",
    )(a4, a4, gr, gi, grt, git, kf_r, kf_i)


def _inv_kernel(f_ref, br_ref, bi_ref, s_ref, bias_ref, z_ref, g_ref, o_ref, *, nf, nj, ch):
    acc = _dot(f_ref[:, 0:nf], br_ref[...]) + _dot(f_ref[:, nf:2 * nf], bi_ref[...])
    for jj in range(nj):
        y = acc[:, jj * ch:(jj + 1) * ch] * s_ref[...]
        o_ref[:, jj, :] = g_ref[:, jj, :] * (y + bias_ref[...] * z_ref[:, jj, :])


def _hyena_inverse(finv, b_r, b_i, scale, bias, z3, zcol, g3, gcol):
    t1, k2 = finv.shape
    nf = k2 // 2
    n2 = z3.shape[1]
    ch = HY_D
    nj = min(DFT_LANES // ch, n2)
    col = pl.BlockSpec((nf, nj * ch), lambda j: (0, j))
    row = pl.BlockSpec((1, ch), lambda j: (0, 0))
    return pl.pallas_call(
        functools.partial(_inv_kernel, nf=nf, nj=nj, ch=ch),
        grid=(n2 // nj,),
        in_specs=[pl.BlockSpec((t1, k2), lambda j: (0, 0)), col, col, row, row,
                  pl.BlockSpec((t1, nj, ch), lambda j: (0, j, zcol)),
                  pl.BlockSpec((t1, nj, ch), lambda j: (0, j, gcol))],
        out_specs=pl.BlockSpec((t1, nj, ch), lambda j: (0, j, 0)),
        out_shape=jax.ShapeDtypeStruct((t1, n2, ch), F32),
        compiler_params=_params("parallel"),
        name="hyena_inverse",
    )(finv, b_r, b_i, scale, bias, z3, g3)


def _hyena_nf(n):
    nh = (2 * n // FFT_N2) // 2 + 1
    return -(-nh // 16) * 16


def _dft_tables(n):
    n2 = FFT_N2
    n1 = 2 * n // n2
    tot = 2 * n
    two_pi = 2.0 * math.pi

    def cs(num, den):
        ang = (two_pi / den) * (num % den).astype(F32)
        return jnp.cos(ang), jnp.sin(ang)

    nh = n1 // 2 + 1
    nf = _hyena_nf(n)
    f1 = jnp.arange(nh, dtype=jnp.int32)
    t1 = jnp.arange(n1, dtype=jnp.int32)
    c1, s1 = cs(f1[:, None] * t1[None, :], n1)
    zrow = jnp.zeros((nf - nh, n1), F32)
    fwd_full = jnp.concatenate([c1, zrow, -s1, zrow], axis=0).astype(BF16)
    fwd_half = fwd_full[:, :n1 // 2]
    wgt = jnp.where((f1 == 0) | (f1 == n1 // 2), 1.0, 2.0)[:, None]
    zcol = jnp.zeros((n1 // 2, nf - nh), F32)
    inv = jnp.concatenate([(wgt * c1[:, :n1 // 2]).T, zcol, -(wgt * s1[:, :n1 // 2]).T, zcol],
                          axis=1).astype(BF16)
    t2 = jnp.arange(n2, dtype=jnp.int32)
    twr, twi = cs(f1[:, None] * t2[None, :], tot)
    fr, fi = cs(t2[:, None] * t2[None, :], n2)
    twi, fi = -twi, -fi
    gr = twr[:, None, :] * fr[None] - twi[:, None, :] * fi[None]
    gi = twr[:, None, :] * fi[None] + twi[:, None, :] * fr[None]
    tabs = (gr.astype(BF16), gi.astype(BF16),
            gr.transpose(0, 2, 1).astype(BF16), gi.transpose(0, 2, 1).astype(BF16))
    return fwd_full, fwd_half, inv, tabs


def _hyena_long(q, k2, nrm, hy_bias, tables):
    n = q.shape[0]
    n2 = FFT_N2
    n1 = 2 * n // n2
    fwd_full, fwd_half, inv, tabs = tables
    nf = _hyena_nf(n)
    oc = HY_ORDER * HY_D
    ak = _dft_rows(fwd_full, k2.reshape(n1, n2, oc), 0, oc).reshape(2, nf, n2, oc)
    kf_r, kf_i = _filter_spectrum(ak, tabs[0], tabs[1])
    q3 = q.reshape(n1 // 2, n2, 3 * HY_D)
    z3, zcol = q3, 0
    for o in range(HY_ORDER):
        a4 = _dft_rows(fwd_half, z3, zcol, HY_D).reshape(2, nf, n2, HY_D)
        b_r, b_i = _hyena_mid(a4, tabs, kf_r, kf_i, o)
        scale = 1.0 / (2.0 * n * nrm[:, o * HY_D:(o + 1) * HY_D])
        z3 = _hyena_inverse(inv, b_r.reshape(nf, n2 * HY_D), b_i.reshape(nf, n2 * HY_D), scale,
                            hy_bias[o][None, :], z3, zcol, q3, o + 1)
        zcol = 0
    return z3.reshape(n, HY_D)


def _hy_ctx_kernel(v_ref, x1_ref, x2_ref, k0_ref, k1_ref, n0_ref, n1_ref, bias_ref, o_ref, kf, zs, *, n):
    zs[...] = v_ref[...]
    for o, (k_ref, nr_ref, x_ref) in enumerate(((k0_ref, n0_ref, x1_ref), (k1_ref, n1_ref, x2_ref))):
        kf[0:n, :] = k_ref[n:2 * n, :]
        kf[n:2 * n, :] = k_ref[0:n, :]

        def body(s, acc):
            return acc + kf[pl.ds(n - s, n), :] * zs[pl.ds(s, 1), :]

        acc = lax.fori_loop(0, n, body, jnp.zeros((n, LANE), F32))
        z = zs[...]
        zs[...] = x_ref[...] * (acc / nr_ref[...] + bias_ref[o:o + 1, :] * z)
    o_ref[...] = zs[...]


def _hyena_ctx(q, k2, nrm, hy_bias):
    n = q.shape[0]
    nb = HY_D // LANE
    col = lambda c0: pl.BlockSpec((n, LANE), lambda j: (0, c0 + j))
    kcol = lambda c0: pl.BlockSpec((2 * n, LANE), lambda j: (0, c0 + j))
    ncol = lambda c0: pl.BlockSpec((1, LANE), lambda j: (0, c0 + j))
    return pl.pallas_call(
        functools.partial(_hy_ctx_kernel, n=n),
        grid=(nb,),
        in_specs=[col(0), col(nb), col(2 * nb), kcol(0), kcol(nb), ncol(0), ncol(nb),
                  pl.BlockSpec((HY_ORDER, LANE), lambda j: (0, j))],
        out_specs=pl.BlockSpec((n, LANE), lambda j: (0, j)),
        out_shape=jax.ShapeDtypeStruct((n, HY_D), F32),
        scratch_shapes=[pltpu.VMEM((2 * n, LANE), F32), pltpu.VMEM((n, LANE), F32)],
        compiler_params=_params("parallel"),
        name="hyena_ctx",
    )(q, q, q, k2, k2, nrm, nrm, hy_bias)


def _merge_kernel(ya_ref, xs_ref, yf_ref, yb_ref, z_ref, yc_ref, yd_ref, g_ref, h_ref,
                  dv_ref, ng_ref, g1_ref, lg_ref, lb_ref,
                  wa_ref, wb_ref, wc_ref, wd_ref, wo_ref, o_ref):
    y = xs_ref[...].astype(F32) * dv_ref[...] + yf_ref[...] + yb_ref[...]
    gz = y * _silu(z_ref[...].astype(F32))
    ssd = gz * lax.rsqrt(jnp.mean(gz * gz, -1, keepdims=True) + LN_EPS) * ng_ref[...]
    d = D_MODEL
    gate = lambda k: jnp.tanh(g_ref[:, k * d:(k + 1) * d].astype(F32)) + 1.0
    m = gate(0) * _dot(ya_ref[...].astype(BF16), wa_ref[...])
    m = m + gate(1) * _dot(ssd.astype(BF16), wb_ref[...])
    m = m + gate(2) * _dot(yc_ref[...].astype(BF16), wc_ref[...])
    m = m + gate(3) * _dot(yd_ref[...].astype(BF16), wd_ref[...])
    mix = _dot(m.astype(BF16), wo_ref[...])
    o_ref[...] = _layer_norm(DN_ALPHA * h_ref[...] + g1_ref[...] * mix, lg_ref[...], lb_ref[...])


def _merge(ya, xbc, ydir, p, yc, yd, h, dvec, ng, gate1, lg, lb, wa, wb, wc, wd, wo):
    n = h.shape[0]
    t = 256
    tok = lambda w, col=0: pl.BlockSpec((t, w), lambda i: (i, col))
    vec = lambda w: pl.BlockSpec((1, w), lambda i: (0, 0))
    mat = lambda r: pl.BlockSpec((r, D_MODEL), lambda i: (0, 0))
    return pl.pallas_call(
        _merge_kernel,
        grid=(n // t,),
        in_specs=[tok(CONF_D), tok(SSD_D),
                  pl.BlockSpec((None, t, SSD_D), lambda i: (0, i, 0)),
                  pl.BlockSpec((None, t, SSD_D), lambda i: (1, i, 0)),
                  tok(SSD_D, PZ // SSD_D), tok(HY_D), tok(SC_D), tok(N_BRANCH * D_MODEL, 0), tok(D_MODEL),
                  vec(SSD_D), vec(SSD_D), vec(D_MODEL), vec(D_MODEL), vec(D_MODEL),
                  mat(CONF_D), mat(SSD_D), mat(HY_D), mat(SC_D), mat(D_MODEL)],
        out_specs=tok(D_MODEL),
        out_shape=jax.ShapeDtypeStruct((n, D_MODEL), F32),
        compiler_params=_params("parallel"),
        name="merge",
    )(ya, xbc, ydir, ydir, p, yc, yd, p, h, dvec, ng, gate1, lg, lb, wa, wb, wc, wd, wo)


MOE_T = 256


def _stream_tiles(hs):
    tiles = [h.shape[0] // MOE_T for h in hs]
    first = [sum(tiles[:s]) for s in range(len(hs))]
    return tiles, first


def _stream_specs(tiles, first, width):
    return [pl.BlockSpec((MOE_T, width), lambda i, nt=nt, f=f: (jnp.clip(i - f, 0, nt - 1), 0))
            for nt, f in zip(tiles, first)]


def _stream_vec_spec(first):
    def index(i):
        s = 0
        for f in first[1:]:
            s = s + (i >= f).astype(jnp.int32)
        return (s, 0, 0)
    return pl.BlockSpec((None, 1, D_MODEL), index)


def _stream_tile(i, refs, first):
    x = refs[0][...]
    for r, f in zip(refs[1:], first[1:]):
        x = jnp.where(i >= f, r[...], x)
    return x


def _router_kernel(*refs, first):
    ns = len(first)
    h_refs = refs[:ns]
    sh_ref, sc_ref, wh_ref, wl_ref, b_ref, sel_ref, cnt_ref, selt_ref = refs[ns:]
    i = pl.program_id(0)

    @pl.when(i == 0)
    def _():
        cnt_ref[...] = jnp.zeros_like(cnt_ref)

    u = _stream_tile(i, h_refs, first) * (1.0 + sc_ref[...]) + sh_ref[...]
    u_hi = u.astype(BF16)
    u_lo = (u - u_hi.astype(F32)).astype(BF16)
    lg = _dot(u_hi, wh_ref[...]) + _dot(u_lo, wh_ref[...]) + _dot(u_hi, wl_ref[...]) + b_ref[...]
    lane = lax.broadcasted_iota(jnp.int32, lg.shape, 1).astype(F32)
    neg = -1e30
    big = 1e9
    gl = jnp.where(lane < MOE_GROUPS, lg, neg)
    gmax = jnp.max(gl, -1, keepdims=True)
    gsel = jnp.min(jnp.where(gl == gmax, lane, big), -1, keepdims=True)
    gprob = 1.0 / jnp.sum(jnp.where(lane < MOE_GROUPS, jnp.exp(lg - gmax), 0.0), -1, keepdims=True)
    lo = MOE_GROUPS + gsel * MOE_EPG
    el = jnp.where(jnp.abs(lane - lo - (MOE_EPG - 1) / 2.0) < MOE_EPG / 2.0, lg, neg)
    m1 = jnp.max(el, -1, keepdims=True)
    i1 = jnp.min(jnp.where(el == m1, lane, big), -1, keepdims=True)
    el2 = jnp.where(lane == i1, neg, el)
    m2 = jnp.max(el2, -1, keepdims=True)
    i2 = jnp.min(jnp.where(el2 == m2, lane, big), -1, keepdims=True)
    t = jnp.exp(m2 - m1)
    w1 = gprob / (1.0 + t)
    w2 = gprob * t / (1.0 + t)
    oh1 = jnp.where(lane == i1, 1.0, 0.0)
    oh2 = jnp.where(lane == i2, 1.0, 0.0)
    oh = oh1 + oh2
    tt = lg.shape[0]
    li = lax.broadcasted_iota(jnp.int32, (tt, tt), 0)
    si = lax.broadcasted_iota(jnp.int32, (tt, tt), 1)
    before = _dot(jnp.where(li > si, 1.0, 0.0).astype(BF16), oh.astype(BF16)) + cnt_ref[...]
    r1 = jnp.sum(oh1 * before, -1, keepdims=True)
    r2 = jnp.sum(oh2 * before, -1, keepdims=True)
    cnt_ref[...] += jnp.sum(oh, axis=0, keepdims=True)
    cols = (i1 - MOE_GROUPS, i2 - MOE_GROUPS, w1, w2, r1, r2)
    sel = jnp.zeros_like(lg)
    for k, v in enumerate(cols):
        sel = jnp.where(lane == k, v, sel)
    sel_ref[...] = sel
    selt_ref[...] = sel.T[0:SUBLANE, :]


def _router(hs, shift, scale, wr, br):
    tiles, first = _stream_tiles(hs)
    t = MOE_T
    n = t * sum(tiles)
    vec = lambda w: pl.BlockSpec((1, w), lambda i: (0, 0))
    wr_hi = wr.astype(BF16)
    wr_lo = (wr - wr_hi.astype(F32)).astype(BF16)
    wspec = pl.BlockSpec((D_MODEL, LANE), lambda i: (0, 0))
    return pl.pallas_call(
        functools.partial(_router_kernel, first=first),
        grid=(sum(tiles),),
        in_specs=_stream_specs(tiles, first, D_MODEL) + [
            _stream_vec_spec(first), _stream_vec_spec(first), wspec, wspec, vec(LANE)],
        out_specs=[pl.BlockSpec((t, LANE), lambda i: (i, 0)), vec(LANE),
                   pl.BlockSpec((SUBLANE, t), lambda i: (0, i))],
        out_shape=[jax.ShapeDtypeStruct((n, LANE), F32), jax.ShapeDtypeStruct((1, LANE), F32),
                   jax.ShapeDtypeStruct((SUBLANE, n), F32)],
        compiler_params=_params("arbitrary"),
        name="router",
    )(*hs, shift, scale, wr_hi, wr_lo, br)


ROW_WORDS = D_MODEL // 2


def _pack_rows(x):
    c = x.shape[1] // 2
    bits = lambda v: lax.bitcast_convert_type(v.astype(BF16).astype(F32), jnp.uint32)
    return bits(x[:, :c]) | (bits(x[:, c:]) >> 16)


def _unpack_rows(w):
    hi = lax.bitcast_convert_type(w & jnp.uint32(0xFFFF0000), F32)
    lo = lax.bitcast_convert_type(w << 16, F32)
    return jnp.concatenate([hi, lo], axis=1)


def _dispatch_kernel(dst_ref, *refs, first, nt):
    ns = len(first)
    h_refs = refs[:ns]
    sh_ref, sc_ref, zero_hbm, xin_hbm, ubuf, sem = refs[ns:]
    del zero_hbm
    t = MOE_T
    i = pl.program_id(0)
    slot = i % 2

    def wait_slot(s):
        for _ in range(MOE_TOP_K):
            pltpu.make_async_copy(ubuf.at[s], xin_hbm.at[pl.ds(0, t), :], sem.at[s]).wait()

    @pl.when(i >= 2)
    def _():
        wait_slot(slot)

    ubuf[slot] = _pack_rows(_stream_tile(i, h_refs, first) * (1.0 + sc_ref[...]) + sh_ref[...])

    def issue(r, carry):
        for k in range(MOE_TOP_K):
            pltpu.make_async_copy(ubuf.at[slot, pl.ds(r, 1), :],
                                  xin_hbm.at[pl.ds(dst_ref[0, 0, k * t + r], 1), :],
                                  sem.at[slot]).start(priority=k % 2)
        return carry

    lax.fori_loop(0, t, issue, 0, unroll=16)

    @pl.when(i == nt - 1)
    def _():
        wait_slot(slot)
        if nt > 1:
            wait_slot(1 - slot)


def _dispatch(pos_t, hs, shift, scale, n_rows):
    tiles, first = _stream_tiles(hs)
    t = MOE_T
    nt = sum(tiles)
    return pl.pallas_call(
        functools.partial(_dispatch_kernel, first=first, nt=nt),
        grid=(nt,),
        in_specs=([pl.BlockSpec((1, 1, MOE_TOP_K * t), lambda i: (i, 0, 0), memory_space=pltpu.SMEM)]
                  + _stream_specs(tiles, first, D_MODEL)
                  + [_stream_vec_spec(first), _stream_vec_spec(first), pl.BlockSpec(memory_space=pl.ANY)]),
        out_specs=pl.BlockSpec(memory_space=pl.ANY),
        out_shape=jax.ShapeDtypeStruct((n_rows, ROW_WORDS), jnp.uint32),
        scratch_shapes=[pltpu.VMEM((2, t, ROW_WORDS), jnp.uint32), pltpu.SemaphoreType.DMA((2,))],
        input_output_aliases={3 + len(hs): 0},
        compiler_params=_params("arbitrary"),
        name="dispatch",
    )(pos_t, *hs, shift, scale, jnp.zeros((n_rows, ROW_WORDS), jnp.uint32))


def _expert_kernel(be_ref, nu_ref, x_ref, wg_ref, wu_ref, wd_ref, o_ref, wgb, wub, wdb):
    b = pl.program_id(0)

    @pl.when((b == 0) | (be_ref[b] != be_ref[jnp.maximum(b - 1, 0)]))
    def _():
        wgb[...] = wg_ref[...].astype(BF16)
        wub[...] = wu_ref[...].astype(BF16)
        wdb[...] = wd_ref[...].astype(BF16)

    @pl.when(b < nu_ref[0])
    def _():
        x = _unpack_rows(x_ref[...]).astype(BF16)
        hid = _silu(_dot(x, wgb[...])) * _dot(x, wub[...])
        o_ref[...] = _pack_rows(_dot(hid.astype(BF16), wdb[...]))

    @pl.when(b >= nu_ref[0])
    def _():
        o_ref[...] = jnp.zeros_like(o_ref)


def _experts(xin, block_e, n_used, wg, wu, wd, layer):
    n_blocks = block_e.shape[0]
    gs = pltpu.PrefetchScalarGridSpec(
        num_scalar_prefetch=2,
        grid=(n_blocks,),
        in_specs=[pl.BlockSpec((MOE_ROWS, ROW_WORDS), lambda b, be, nu: (b, 0)),
                  pl.BlockSpec((None, None, D_MODEL, MOE_FF), lambda b, be, nu: (layer, be[b], 0, 0)),
                  pl.BlockSpec((None, None, D_MODEL, MOE_FF), lambda b, be, nu: (layer, be[b], 0, 0)),
                  pl.BlockSpec((None, None, MOE_FF, D_MODEL), lambda b, be, nu: (layer, be[b], 0, 0))],
        out_specs=pl.BlockSpec((MOE_ROWS, ROW_WORDS), lambda b, be, nu: (b, 0)),
        scratch_shapes=[pltpu.VMEM((D_MODEL, MOE_FF), BF16), pltpu.VMEM((D_MODEL, MOE_FF), BF16),
                        pltpu.VMEM((MOE_FF, D_MODEL), BF16)],
    )
    return pl.pallas_call(
        _expert_kernel,
        grid_spec=gs,
        out_shape=jax.ShapeDtypeStruct((n_blocks * MOE_ROWS, ROW_WORDS), jnp.uint32),
        compiler_params=_params("arbitrary"),
        name="experts",
    )(block_e, n_used, xin, wg, wu, wd)


def _combine_kernel(pos_ref, posn_ref, y_hbm, *refs, first, tiles):
    ns = len(first)
    nt = sum(tiles)
    h_refs = refs[:ns]
    sel_ref, g2_ref, lg_ref, lb_ref = refs[ns:ns + 4]
    o_refs = refs[ns + 4:2 * ns + 4]
    ybuf, sem = refs[2 * ns + 4:]
    t = MOE_T
    i = pl.program_id(0)
    slot = i % 2

    def gather(p_ref, s):
        def issue(r, carry):
            for k in range(MOE_TOP_K):
                row = k * t + r
                pltpu.make_async_copy(y_hbm.at[pl.ds(p_ref[0, 0, row], 1), :], ybuf.at[s, pl.ds(row, 1), :],
                                      sem.at[s]).start(priority=k % 2)
            return carry
        lax.fori_loop(0, t, issue, 0, unroll=16)

    @pl.when(i == 0)
    def _():
        gather(pos_ref, 0)

    @pl.when(i + 1 < nt)
    def _():
        gather(posn_ref, 1 - slot)

    pltpu.make_async_copy(y_hbm.at[pl.ds(0, MOE_TOP_K * t), :], ybuf.at[slot], sem.at[slot]).wait()
    ffn = (sel_ref[:, 2:3] * _unpack_rows(ybuf[slot, 0:t, :])
           + sel_ref[:, 3:4] * _unpack_rows(ybuf[slot, t:2 * t, :]))
    out = _layer_norm(DN_ALPHA * _stream_tile(i, h_refs, first) + g2_ref[...] * ffn, lg_ref[...], lb_ref[...])
    for s in range(ns):
        @pl.when((i >= first[s]) & (i < first[s] + tiles[s]))
        def _(s=s):
            o_refs[s][...] = out


def _combine(y, pos_t, hs, sel, gate2, lg, lb):
    tiles, first = _stream_tiles(hs)
    t = MOE_T
    nt = sum(tiles)
    vec = pl.BlockSpec((1, D_MODEL), lambda i: (0, 0))
    return pl.pallas_call(
        functools.partial(_combine_kernel, first=first, tiles=tiles),
        grid=(nt,),
        in_specs=([pl.BlockSpec((1, 1, MOE_TOP_K * t), lambda i: (i, 0, 0), memory_space=pltpu.SMEM),
                   pl.BlockSpec((1, 1, MOE_TOP_K * t), lambda i: (jnp.minimum(i + 1, nt - 1), 0, 0),
                                memory_space=pltpu.SMEM),
                   pl.BlockSpec(memory_space=pl.ANY)]
                  + _stream_specs(tiles, first, D_MODEL)
                  + [pl.BlockSpec((t, LANE), lambda i: (i, 0)), _stream_vec_spec(first), vec, vec]),
        out_specs=_stream_specs(tiles, first, D_MODEL),
        out_shape=[jax.ShapeDtypeStruct(h.shape, F32) for h in hs],
        scratch_shapes=[pltpu.VMEM((2, MOE_TOP_K * t, ROW_WORDS), jnp.uint32), pltpu.SemaphoreType.DMA((2,))],
        compiler_params=_params("arbitrary"),
        name="combine",
    )(pos_t, pos_t, y, *hs, sel, gate2, lg, lb)


def _moe(hs, shift, scale, gate2, lg, lb, wr, br, wg, wu, wd, layer):
    t = MOE_T
    n = sum(h.shape[0] for h in hs)
    nt = n // t
    sel, cnt, selt = _router(hs, shift, scale, wr, br)
    counts = cnt[0, MOE_GROUPS:MOE_GROUPS + MOE_EXPERTS].astype(jnp.int32)
    padded = (counts + MOE_ROWS - 1) // MOE_ROWS * MOE_ROWS
    pad_end = jnp.cumsum(padded)
    pad_start = pad_end - padded
    n_blocks = (n * MOE_TOP_K + MOE_EXPERTS * (MOE_ROWS - 1) + MOE_ROWS - 1) // MOE_ROWS
    blk_row = jnp.arange(n_blocks, dtype=jnp.int32) * MOE_ROWS
    block_e = jnp.minimum(jnp.sum((blk_row[:, None] >= pad_end[None, :]).astype(jnp.int32), axis=1),
                          MOE_EXPERTS - 1)
    n_used = (pad_end[-1:] // MOE_ROWS).astype(jnp.int32)
    e_kt = selt[0:MOE_TOP_K].astype(jnp.int32)
    ids = jnp.arange(MOE_EXPERTS, dtype=jnp.int32)[None, :, None]
    start_kt = jnp.sum(jnp.where(e_kt[:, None, :] == ids, pad_start[None, :, None], 0), axis=1)
    pos_kt = start_kt + selt[4:4 + MOE_TOP_K].astype(jnp.int32)
    pos_t = pos_kt.reshape(MOE_TOP_K, nt, t).transpose(1, 0, 2).reshape(nt, 1, MOE_TOP_K * t)
    xin = _dispatch(pos_t, hs, shift, scale, n_blocks * MOE_ROWS)
    y = _experts(xin, block_e, n_used, wg, wu, wd, layer)
    return _combine(y, pos_t, hs, sel, gate2, lg, lb)


def _mixer(h, mod, lw, ssd_init, tables, *, latent, need_mix):
    n = h.shape[0]
    p, dt_raw = _inproj(h, mod[0], mod[1], lw["w_in"], lw["layer"])
    xbc = _conv3(p, PX, SSD_XBC, lw["ssd_conv_w"], lw["ssd_conv_b"], silu=True, out_dtype=BF16)
    ydir, finals = _ssd_scan(xbc, dt_raw, lw["ssd_dt_bias"], lw["ssd_a_log"], ssd_init)
    if not need_mix:
        return None, finals
    conf_w = (lw["conf_dw_w"], lw["conf_dw_b"], lw["conf_ln_g"], lw["conf_ln_b"])
    ya = _conformer_grid(p, *conf_w) if latent else _conformer(p, *conf_w, dil=1)
    q = _conv3(p, PC, 3 * HY_D, lw["hy_short_w"], lw["hy_short_b"], silu=False, out_dtype=F32)
    k2, nrm = _hyena_filters(lw["feat_lat" if latent else "feat_ctx"], lw["hy_filter"], n)
    if latent:
        yc = _hyena_long(q, k2, nrm, lw["hy_bias"], tables)
    else:
        yc = _hyena_ctx(q, k2, nrm, lw["hy_bias"])
    yd = _gated_conv(p, lw["sc_conv_w"])
    h = _merge(ya, xbc, ydir, p, yc, yd, h, lw["ssd_dvec"], lw["ssd_norm_g"], mod[2], lw["ln_g0"], lw["ln_b0"],
               lw["w_branch_a"], lw["w_branch_b"], lw["w_branch_c"], lw["w_branch_d"], lw["w_out"])
    return h, finals


def _positional_features(n):
    tau = jnp.arange(2 * n, dtype=jnp.int32)
    lag = jnp.where(tau < n, tau, 2 * n - tau).astype(F32)[:, None]
    t01 = lag * (1.0 / (n - 1))
    omega = (2.0 * math.pi / n) * lag
    bands = jnp.linspace(1e-4, HY_BANDS - 1, HY_BANDS, dtype=F32)
    featx = jnp.concatenate([t01, jnp.cos(bands * omega), -jnp.sin(bands * omega)], axis=-1)
    return _pad_lanes(featx)


RELAYOUT_W = 256


def _relayout_kernel(blk_ref, sh_ref, valid_ref, half_ref, a_ref, b_ref, o_ref):
    j = pl.program_id(1)
    sh = sh_ref[j]
    rot = (LANE - sh) % LANE
    scale = jnp.where(half_ref[j] == 1, 0.5, 1.0)
    lane = lax.broadcasted_iota(jnp.int32, (a_ref.shape[0], LANE), 1)
    nk = RELAYOUT_W // LANE
    chunk = lambda k: (a_ref if k < nk else b_ref)[:, (k % nk) * LANE:(k % nk + 1) * LANE]
    for k in range(nk):
        x = jnp.where(lane < LANE - sh, pltpu.roll(chunk(k), rot, 1), pltpu.roll(chunk(k + 1), rot, 1))
        o_ref[:, k * LANE:(k + 1) * LANE] = jnp.where(lane + k * LANE < valid_ref[j], x * scale, 0.0).astype(BF16)


def _relayout_w_in(w_in):
    w = RELAYOUT_W
    depth, d, total = w_in.shape
    ob = OFF_B
    segs = [(PG, OFF_G, N_BRANCH * D_MODEL, 1), (PA, OFF_A, 2 * CONF_D, 0), (PC, OFF_C, 3 * HY_D, 0),
            (PD, OFF_D, 3 * SC_D, 0), (PX, ob + SSD_D, SSD_XBC, 0), (PZ, ob, SSD_D, 0),
            (PDT, ob + SSD_D + SSD_XBC, 2 * SSD_HEADS, 0)]
    ntile = NP // w
    blk, sh, valid, half = ([0] * ntile for _ in range(4))
    for p0, s0, width, hv in segs:
        assert p0 % w == 0
        for j in range(p0 // w, -(-(p0 + width) // w)):
            src = s0 + j * w - p0
            blk[j], sh[j], valid[j], half[j] = src // w, src % w, min(w, p0 + width - j * w), hv
            assert sh[j] < LANE
    last = (total - 1) // w
    tables = [jnp.asarray(v, jnp.int32) for v in (blk, sh, valid, half)]
    gs = pltpu.PrefetchScalarGridSpec(
        num_scalar_prefetch=4,
        grid=(depth, ntile),
        in_specs=[pl.BlockSpec((None, d, w), lambda l, j, blk, *_: (l, 0, blk[j])),
                  pl.BlockSpec((None, d, w), lambda l, j, blk, *_: (l, 0, jnp.minimum(blk[j] + 1, last)))],
        out_specs=pl.BlockSpec((None, d, w), lambda l, j, *_: (l, 0, j)),
    )
    return pl.pallas_call(
        _relayout_kernel,
        grid_spec=gs,
        out_shape=jax.ShapeDtypeStruct((depth, d, NP), BF16),
        compiler_params=_params("parallel", "parallel"),
        name="relayout_w_in",
    )(*tables, w_in, w_in)


def _pad_lanes(v):
    return jnp.pad(v, ((0, 0), (0, LANE - v.shape[-1])))


def kernel(x, c, ctx, c_ctx, w_mod, b_mod, ln_g, ln_b, w_in, conf_dw_w, conf_dw_b, conf_ln_g, conf_ln_b,
           ssd_conv_w, ssd_conv_b, ssd_a_log, ssd_dt_bias, ssd_d, ssd_norm_g, hy_short_w, hy_short_b,
           hy_w1, hy_b1, hy_w2, hy_b2, hy_freq, hy_w3, hy_bias, sc_conv_w, w_branch_a, w_branch_b,
           w_branch_c, w_branch_d, w_out, rt_group_w, rt_group_b, rt_expert_w, rt_expert_b,
           ex_w_gate, ex_w_up, ex_w_down):
    assert x.shape[0] == 1 and ctx.shape[0] == 1
    n_lat, n_ctx = x.shape[1], ctx.shape[1]
    depth = w_in.shape[0]

    cv = jnp.concatenate([c, c_ctx[None, :], jnp.zeros((SUBLANE - 2, D_MODEL), F32)], axis=0)
    mods = _mod_vectors(cv, w_mod, b_mod)
    w_in_p = _relayout_w_in(w_in)
    tables = _dft_tables(n_lat)
    feat_lat = _positional_features(n_lat)
    feat_ctx = _positional_features(n_ctx)
    deltas = jnp.abs(jnp.linspace(HY_MIN_DECAY, HY_MAX_DECAY, HY_N_FILT, dtype=F32))
    deltas_d = deltas.reshape(HY_ORDER, 2, HY_D).transpose(1, 0, 2).reshape(2, 1, HY_ORDER * HY_D)
    router_w = jnp.concatenate([rt_group_w, rt_expert_w,
                                jnp.zeros((depth, D_MODEL, LANE - MOE_GROUPS - MOE_EXPERTS), F32)], axis=-1)
    router_b = jnp.concatenate([rt_group_b, rt_expert_b,
                                jnp.zeros((depth, LANE - MOE_GROUPS - MOE_EXPERTS), F32)], axis=-1)
    ssd_zero = jnp.zeros((2,) + SSD_STATE_SHAPE, F32)

    h_lat, h_ctx = x[0], ctx[0]
    for l in range(depth):
        row = lambda v: v[None, :]
        lw = dict(
            w_in=w_in_p, layer=l, conf_dw_w=conf_dw_w[l], conf_dw_b=row(conf_dw_b[l]), conf_ln_g=row(conf_ln_g[l]),
            conf_ln_b=row(conf_ln_b[l]), ssd_conv_w=ssd_conv_w[l], ssd_conv_b=row(ssd_conv_b[l]),
            ssd_a_log=_pad_lanes(ssd_a_log[l].reshape(1, -1)), ssd_dt_bias=_pad_lanes(ssd_dt_bias[l].reshape(1, -1)),
            ssd_dvec=row(jnp.repeat(ssd_d[l], SSD_HEAD_DIM)), ssd_norm_g=row(ssd_norm_g[l]),
            hy_short_w=hy_short_w[l], hy_short_b=row(hy_short_b[l]),
            hy_filter=_filter_weights(hy_w1[l], hy_b1[l], hy_w2[l], hy_b2[l], hy_freq[l], hy_w3[l], deltas_d),
            hy_bias=hy_bias[l], feat_lat=feat_lat, feat_ctx=feat_ctx,
            sc_conv_w=sc_conv_w[l], ln_g0=row(ln_g[l, 0]), ln_b0=row(ln_b[l, 0]),
            w_branch_a=w_branch_a[l].astype(BF16), w_branch_b=w_branch_b[l].astype(BF16),
            w_branch_c=w_branch_c[l].astype(BF16), w_branch_d=w_branch_d[l].astype(BF16),
            w_out=(0.5 * w_out[l]).astype(BF16))
        moe_w = (router_w[l], row(router_b[l]), ex_w_gate, ex_w_up, ex_w_down, l)
        last = l == depth - 1
        d = D_MODEL
        mod_lat = [mods[l, 0:1, k * d:(k + 1) * d] for k in range(6)]
        mod_ctx = [mods[l, 1:2, k * d:(k + 1) * d] for k in range(6)]

        mix_ctx, ctx_states = _mixer(h_ctx, mod_ctx, lw, ssd_zero, None, latent=False, need_mix=not last)
        h_lat, _ = _mixer(h_lat, mod_lat, lw, ctx_states, tables, latent=True, need_mix=True)
        streams = [(h_lat, mod_lat)] if last else [(h_lat, mod_lat), (mix_ctx, mod_ctx)]
        vecs = [jnp.stack([m[k] for _, m in streams]) for k in (3, 4, 5)]
        outs = _moe([h for h, _ in streams], *vecs, row(ln_g[l, 1]), row(ln_b[l, 1]), *moe_w)
        h_lat = outs[0]
        if not last:
            h_ctx = outs[1]
    return h_lat[None]
```
